```python
import math
import jax, jax.numpy as jnp
from jax import lax
import numpy as np

D_MODEL = 2048
BATCH = 2
SEQ = 4096
DEPTH = 4

N_A_LAYERS = DEPTH // 2
N_B_LAYERS = DEPTH - N_A_LAYERS
SSM_GROUP = 16
SSM_GROUPS = D_MODEL // SSM_GROUP
SSM_STATE = 64
DT_MIN = 0.001
DT_MAX = 0.1
HEAD_DIM = 64
N_HEADS = D_MODEL // (2 * HEAD_DIM)
Q_BLOCK = 128
REL_BUCKETS = 32
REL_MAX_EXACT = REL_BUCKETS // 2
REL_MAX_DISTANCE = 128
N_EXPERT_GROUPS = 4
EXPERTS_PER_GROUP = 4
N_EXPERTS = N_EXPERT_GROUPS * EXPERTS_PER_GROUP
TOP_K = 2
D_EXPERT = D_MODEL // 2
RMS_EPS = 1e-6

kernel_name = 'hybrid_s5_diffattn_hmoe'


def rmsnorm(x, g):
    xf = x.astype(jnp.float32)
    y = xf * lax.rsqrt(jnp.mean(xf * xf, axis=-1, keepdims=True) + RMS_EPS)
    return (y * g.astype(jnp.float32)).astype(x.dtype)


def _linear_recurrence(e1, e2):
    a1, b1 = e1
    a2, b2 = e2
    return a2 * a1, a2 * b1 + b2


def s5_mixer(xn, lam_re, lam_im, b_re, b_im, c_re, c_im, d_skip, log_step, w_glu1, w_glu2):
    b, l, d = xn.shape
    f32 = jnp.float32
    lam = lax.complex(lam_re.astype(f32), lam_im.astype(f32))
    delta = jnp.exp(log_step.astype(f32))[:, None]
    lam_bar = jnp.exp(lam * delta)
    b_c = lax.complex(b_re.astype(f32), b_im.astype(f32))
    b_bar = ((lam_bar - 1.0) / lam)[..., None] * b_c
    c_c = lax.complex(c_re.astype(f32), c_im.astype(f32))
    u = xn.astype(f32).reshape(b, l, SSM_GROUPS, SSM_GROUP)
    bu = jnp.einsum('blgh,gph->blgp', u.astype(jnp.complex64), b_bar)
    a = jnp.broadcast_to(lam_bar, bu.shape)
    _, states = lax.associative_scan(_linear_recurrence, (a, bu), axis=1)
    y = jnp.einsum('blgp,ghp->blgh', states, c_c).real.reshape(b, l, d)
    y = y + d_skip.astype(f32) * xn.astype(f32)
    z = jax.nn.gelu(y).astype(xn.dtype)
    return (z @ w_glu1) * jax.nn.sigmoid(z @ w_glu2)


def t5_bucket(n):
    n_safe = jnp.maximum(n, 1).astype(jnp.float32)
    large = REL_MAX_EXACT + (jnp.log(n_safe / REL_MAX_EXACT)
                             / math.log(REL_MAX_DISTANCE / REL_MAX_EXACT)
                             * (REL_BUCKETS - REL_MAX_EXACT)).astype(jnp.int32)
    large = jnp.minimum(large, REL_BUCKETS - 1)
    return jnp.where(n < REL_MAX_EXACT, n, large)


def shared_kv(h, kv_norm_g, w_kv, k_norm_g):
    b, l, d = h.shape
    kv = rmsnorm(h, kv_norm_g) @ w_kv
    k = rmsnorm(kv[..., :d].reshape(b, l, N_HEADS, 2, HEAD_DIM), k_norm_g)
    k = k.transpose(0, 2, 3, 1, 4)
    v = kv[..., d:].reshape(b, l, N_HEADS, 2 * HEAD_DIM).transpose(0, 2, 1, 3)
    return k, v


def diff_attention(xn, k, v, w_q, q_norm_g, lambda_qk, subln_g, w_o, rel_bias, lam_init):
    b, l, d = xn.shape
    f32 = jnp.float32
    q = rmsnorm((xn @ w_q).reshape(b, l, N_HEADS, 2, HEAD_DIM), q_norm_g) * (HEAD_DIM ** -0.5)
    q = q.transpose(0, 2, 3, 1, 4)
    lq = lambda_qk.astype(f32)
    lam = jnp.exp(jnp.sum(lq[0] * lq[1])) - jnp.exp(jnp.sum(lq[2] * lq[3])) + lam_init
    outs = []
    for i in range(l // Q_BLOCK):
        qs = i * Q_BLOCK
        end = qs + Q_BLOCK
        s = jnp.einsum('bhcqd,bhckd->bhcqk', q[:, :, :, qs:end], k[:, :, :, :end]).astype(f32)
        dist = (qs + jnp.arange(Q_BLOCK, dtype=jnp.int32))[:, None] - jnp.arange(end, dtype=jnp.int32)[None, :]
        bias = rel_bias[t5_bucket(jnp.maximum(dist, 0))].astype(f32).transpose(2, 0, 1)
        s = jnp.where(dist >= 0, s + bias[:, None], -jnp.inf)
        p = jax.nn.softmax(s, axis=-1)
        attn = p[:, :, 0] - lam * p[:, :, 1]
        outs.append(jnp.einsum('bhqk,bhkv->bhqv', attn, v[:, :, :end].astype(f32)))
    o = jnp.concatenate(outs, axis=2)
    o = rmsnorm(o, subln_g) * (1.0 - lam_init)
    o = o.transpose(0, 2, 1, 3).reshape(b, l, d).astype(xn.dtype)
    return o @ w_o


def hier_moe(xn, wg1, bg1, wg2, bg2, w_gate, w_up, w_down):
    b, l, d = xn.shape
    f32 = jnp.float32
    t = xn.reshape(b * l, d)
    g_logits = (t @ wg1 + bg1).astype(f32)
    g_prob = jax.nn.softmax(g_logits, axis=-1)
    g_idx = jnp.argmax(g_logits, axis=-1)
    g_w = jnp.take_along_axis(g_prob, g_idx[:, None], axis=-1)
    e_all = jnp.einsum('nd,gde->nge', t, wg2) + bg2
    e_logits = jnp.take_along_axis(e_all, g_idx[:, None, None], axis=1)[:, 0].astype(f32)
    top_v, top_i = lax.top_k(e_logits, TOP_K)
    e_w = jax.nn.softmax(top_v, axis=-1) * g_w
    expert_id = g_idx[:, None] * EXPERTS_PER_GROUP + top_i
    combine = jnp.sum(jax.nn.one_hot(expert_id, N_EXPERTS, dtype=f32) * e_w[..., None], axis=1)
    y = jnp.zeros((b * l, d), f32)
    for e in range(N_EXPERTS):
        hdn = jax.nn.silu(t @ w_gate[e]) * (t @ w_up[e])
        y = y + combine[:, e:e + 1] * (hdn @ w_down[e]).astype(f32)
    return y.reshape(b, l, d).astype(xn.dtype)


def setup_inputs(seed: int = 0) -> dict:
    key = jax.random.key(seed)
    ks = iter(jax.random.split(key, 40))
    f32 = jnp.float32
    D = D_MODEL

    def nrm(shape, scale):
        return jax.random.normal(next(ks), shape, f32) * scale

    n = jnp.arange(SSM_STATE, dtype=f32)
    return {
        'x': nrm((BATCH, SEQ, D), 1.0),
        'norm_mix_g': 1.0 + nrm((DEPTH, D), 0.02),
        'norm_ffn_g': 1.0 + nrm((DEPTH, D), 0.02),
        'ssm_lam_re': -0.5 + nrm((N_A_LAYERS, SSM_GROUPS, SSM_STATE), 0.01),
        'ssm_lam_im': math.pi * n + nrm((N_A_LAYERS, SSM_GROUPS, SSM_STATE), 0.01),
        'ssm_b_re': nrm((N_A_LAYERS, SSM_GROUPS, SSM_STATE, SSM_GROUP), (0.5 / SSM_GROUP) ** 0.5),
        'ssm_b_im': nrm((N_A_LAYERS, SSM_GROUPS, SSM_STATE, SSM_GROUP), (0.5 / SSM_GROUP) ** 0.5),
        'ssm_c_re': nrm((N_A_LAYERS, SSM_GROUPS, SSM_GROUP, SSM_STATE), (0.5 / SSM_STATE) ** 0.5),
        'ssm_c_im': nrm((N_A_LAYERS, SSM_GROUPS, SSM_GROUP, SSM_STATE), (0.5 / SSM_STATE) ** 0.5),
        'ssm_d': nrm((N_A_LAYERS, D), 1.0),
        'ssm_log_step': jax.random.uniform(next(ks), (N_A_LAYERS, SSM_GROUPS), f32,
                                           math.log(DT_MIN), math.log(DT_MAX)),
        'ssm_w_glu1': nrm((N_A_LAYERS, D, D), D ** -0.5),
        'ssm_w_glu2': nrm((N_A_LAYERS, D, D), D ** -0.5),
        'kv_norm_g': 1.0 + nrm((D,), 0.02),
        'w_kv': nrm((D, 2 * D), D ** -0.5),
        'k_norm_g': 1.0 + nrm((HEAD_DIM,), 0.02),
        'w_q': nrm((N_B_LAYERS, D, D), D ** -0.5),
        'q_norm_g': 1.0 + nrm((N_B_LAYERS, HEAD_DIM), 0.02),
        'lambda_qk': nrm((N_B_LAYERS, 4, HEAD_DIM), 0.1),
        'subln_g': 1.0 + nrm((N_B_LAYERS, 2 * HEAD_DIM), 0.02),
        'w_o': nrm((N_B_LAYERS, D, D), D ** -0.5),
        'rel_bias': nrm((REL_BUCKETS, N_HEADS), 0.5),
        'router_group_w': nrm((DEPTH, D, N_EXPERT_GROUPS), D ** -0.5),
        'router_group_b': nrm((DEPTH, N_EXPERT_GROUPS), 0.01),
        'router_expert_w': nrm((DEPTH, N_EXPERT_GROUPS, D, EXPERTS_PER_GROUP), D ** -0.5),
        'router_expert_b': nrm((DEPTH, N_EXPERT_GROUPS, EXPERTS_PER_GROUP), 0.01),
        'w_gate': nrm((DEPTH, N_EXPERTS, D, D_EXPERT), D ** -0.5),
        'w_up': nrm((DEPTH, N_EXPERTS, D, D_EXPERT), D ** -0.5),
        'w_down': nrm((DEPTH, N_EXPERTS, D_EXPERT, D), D_EXPERT ** -0.5),
    }


def reference(x, norm_mix_g, norm_ffn_g, ssm_lam_re, ssm_lam_im, ssm_b_re, ssm_b_im,
              ssm_c_re, ssm_c_im, ssm_d, ssm_log_step, ssm_w_glu1, ssm_w_glu2,
              kv_norm_g, w_kv, k_norm_g, w_q, q_norm_g, lambda_qk, subln_g, w_o, rel_bias,
              router_group_w, router_group_b, router_expert_w, router_expert_b,
              w_gate, w_up, w_down):
    h = x
    k = None
    v = None
    for layer in range(DEPTH):
        hn = rmsnorm(h, norm_mix_g[layer])
        if layer < N_A_LAYERS:
            a = layer
            h = h + s5_mixer(hn, ssm_lam_re[a], ssm_lam_im[a], ssm_b_re[a], ssm_b_im[a],
                             ssm_c_re[a], ssm_c_im[a], ssm_d[a], ssm_log_step[a],
                             ssm_w_glu1[a], ssm_w_glu2[a])
        else:
            j = layer - N_A_LAYERS
            lam_init = 0.8 - 0.6 * math.exp(-0.3 * layer)
            h = h + diff_attention(hn, k, v, w_q[j], q_norm_g[j], lambda_qk[j], subln_g[j],
                                   w_o[j], rel_bias, lam_init)
        h = h + hier_moe(rmsnorm(h, norm_ffn_g[layer]), router_group_w[layer], router_group_b[layer],
                         router_expert_w[layer], router_expert_b[layer],
                         w_gate[layer], w_up[layer], w_down[layer])
        if layer == N_A_LAYERS - 1:
            k, v = shared_kv(h, kv_norm_g, w_kv, k_norm_g)
    return h
```

```python
import functools
import math

import jax
import jax.numpy as jnp
from jax import lax
from jax.experimental import pallas as pl
from jax.experimental.pallas import tpu as pltpu

F32 = jnp.float32
BF16 = jnp.bfloat16

DEPTH = 4
N_A_LAYERS = DEPTH // 2
SSM_GROUP = 16
SSM_STATE = 64
HEAD_DIM = 64
REL_BUCKETS = 32
REL_MAX_EXACT = REL_BUCKETS // 2
REL_MAX_DISTANCE = 128
N_EXPERT_GROUPS = 4
EXPERTS_PER_GROUP = 4
N_EXPERTS = N_EXPERT_GROUPS * EXPERTS_PER_GROUP
RMS_EPS = 1e-6

LANES = 128
SUBLANES = 8
VMEM_LIMIT = 48 * 1024 * 1024
NEG_BIG = -1e30

S5_SEGMENTS = SUBLANES
S5_GROUPS_PER_BLOCK = 16
S5_TJ = 32


def _cparams(sem):
    return pltpu.CompilerParams(dimension_semantics=sem, vmem_limit_bytes=VMEM_LIMIT)


def _rmsnorm_body(h_ref, g_ref, o_ref):
    x = h_ref[...]
    ms = jnp.mean(x * x, axis=-1, keepdims=True)
    o_ref[...] = (x * lax.rsqrt(ms + RMS_EPS) * g_ref[...]).astype(o_ref.dtype)


def _rmsnorm(h, g, out_dtype, tm=512):
    n, d = h.shape
    return pl.pallas_call(
        _rmsnorm_body,
        grid=(n // tm,),
        in_specs=[pl.BlockSpec((tm, d), lambda i: (i, 0)),
                  pl.BlockSpec((1, d), lambda i: (0, 0))],
        out_specs=pl.BlockSpec((tm, d), lambda i: (i, 0)),
        out_shape=jax.ShapeDtypeStruct((n, d), out_dtype),
        compiler_params=_cparams(("arbitrary",)),
        name="rmsnorm",
    )(h, g.reshape(1, d))


def _mm_body(*refs, n_w, has_res):
    a_ref = refs[0]
    w_refs = refs[1:1 + n_w]
    res_ref = refs[1 + n_w] if has_res else None
    o_ref = refs[1 + n_w + int(has_res)]
    wb_refs = refs[2 + n_w + int(has_res):]

    @pl.when(pl.program_id(1) == 0)
    def _():
        for w_ref, wb_ref in zip(w_refs, wb_refs):
            wb_ref[...] = w_ref[...].astype(BF16)

    a = a_ref[...]
    y = jnp.dot(a, wb_refs[0][...], preferred_element_type=F32)
    if n_w == 2:
        y2 = jnp.dot(a, wb_refs[1][...], preferred_element_type=F32)
        y = y * (1.0 / (1.0 + jnp.exp(-y2)))
    if has_res:
        y = y + res_ref[...]
    o_ref[...] = y.astype(o_ref.dtype)


def _matmul(a, ws, layer, out_dtype, res=None, col_off=0, n_out=None, tm=512, tn=512):
    m, k = a.shape
    n_out = ws[0].shape[-1] if n_out is None else n_out
    n_w = len(ws)
    in_specs = [pl.BlockSpec((tm, k), lambda j, i: (i, 0))]
    for _ in ws:
        in_specs.append(pl.BlockSpec((None, k, tn), lambda j, i: (layer, 0, j + col_off)))
    args = [a, *ws]
    if res is not None:
        in_specs.append(pl.BlockSpec((tm, tn), lambda j, i: (i, j)))
        args.append(res)
    return pl.pallas_call(
        functools.partial(_mm_body, n_w=n_w, has_res=res is not None),
        grid=(n_out // tn, m // tm),
        in_specs=in_specs,
        out_specs=pl.BlockSpec((tm, tn), lambda j, i: (i, j)),
        out_shape=jax.ShapeDtypeStruct((m, n_out), out_dtype),
        scratch_shapes=[pltpu.VMEM((k, tn), BF16) for _ in ws],
        compiler_params=_cparams(("arbitrary", "arbitrary")),
        name="dense_matmul",
    )(*args)


def _gelu_tanh(y):
    c = math.sqrt(2.0 / math.pi)
    return y * (0.5 * (1.0 + jnp.tanh(c * (y + 0.044715 * (y * y * y)))))


def _s5_body(x_ref, b_ref, c_ref, lam_ref, d_ref, z_ref, bu_ref, st_ref, e_ref, init_ref,
             *, seg_len):
    ch = lam_ref.shape[-1]
    rows = S5_TJ * S5_SEGMENTS
    n_tiles = seg_len // S5_TJ
    lr = jnp.broadcast_to(lam_ref[0:1, :], (S5_SEGMENTS, ch))
    li = jnp.broadcast_to(lam_ref[1:2, :], (S5_SEGMENTS, ch))

    def scan_tile(t, s_re, s_im, store):
        r0 = pl.multiple_of(t * rows, rows)
        bu_ref[...] = jnp.dot(x_ref[pl.ds(r0, rows), :], b_ref[...], preferred_element_type=F32)
        for j in range(S5_TJ):
            sl = slice(S5_SEGMENTS * j, S5_SEGMENTS * (j + 1))
            n_re = lr * s_re - li * s_im + bu_ref[sl, 0:ch]
            n_im = lr * s_im + li * s_re + bu_ref[sl, ch:2 * ch]
            s_re, s_im = n_re, n_im
            if store:
                st_ref[sl, 0:ch] = s_re
                st_ref[sl, ch:2 * ch] = s_im
        return s_re, s_im

    zeros = jnp.zeros((S5_SEGMENTS, ch), F32)
    e_re, e_im = lax.fori_loop(0, n_tiles, lambda t, c: scan_tile(t, c[0], c[1], False),
                               (zeros, zeros))
    e_ref[:, 0:ch] = e_re
    e_ref[:, ch:2 * ch] = e_im

    pr, pi = lam_ref[0:1, :], lam_ref[1:2, :]
    for _ in range(seg_len.bit_length() - 1):
        pr, pi = pr * pr - pi * pi, 2.0 * pr * pi
    cr = jnp.zeros((1, ch), F32)
    ci = jnp.zeros((1, ch), F32)
    init_ref[0:1, :] = jnp.zeros((1, 2 * ch), F32)
    for k in range(S5_SEGMENTS - 1):
        er = e_ref[k:k + 1, 0:ch]
        ei = e_ref[k:k + 1, ch:2 * ch]
        cr, ci = pr * cr - pi * ci + er, pr * ci + pi * cr + ei
        init_ref[k + 1:k + 2, 0:ch] = cr
        init_ref[k + 1:k + 2, ch:2 * ch] = ci

    def pass2(t, c):
        s_re, s_im = scan_tile(t, c[0], c[1], True)
        r0 = pl.multiple_of(t * rows, rows)
        y = jnp.dot(st_ref[...].astype(BF16), c_ref[...], preferred_element_type=F32)
        y = y + d_ref[...] * x_ref[pl.ds(r0, rows), :].astype(F32)
        z_ref[pl.ds(r0, rows), :] = _gelu_tanh(y).astype(z_ref.dtype)
        return s_re, s_im

    lax.fori_loop(0, n_tiles, pass2, (init_ref[:, 0:ch], init_ref[:, ch:2 * ch]))


def _s5_params(lam_re, lam_im, b_re, b_im, c_re, c_im, log_step):
    g, p = lam_re.shape
    gb = S5_GROUPS_PER_BLOCK
    nb = g // gb
    lam = lax.complex(lam_re.astype(F32), lam_im.astype(F32))
    delta = jnp.exp(log_step.astype(F32))[:, None]
    lam_bar = jnp.exp(lam * delta)
    b_bar = ((lam_bar - 1.0) / lam)[..., None] * lax.complex(b_re.astype(F32), b_im.astype(F32))
    eye = jnp.eye(gb, dtype=F32)

    def blk_b(part):
        part = part.reshape(nb, gb, p, SSM_GROUP)
        return jnp.einsum('cgph,gk->cghkp', part, eye).reshape(nb, gb * SSM_GROUP, gb * p)

    def blk_c(part):
        part = part.reshape(nb, gb, SSM_GROUP, p)
        return jnp.einsum('cghp,gk->cgpkh', part, eye).reshape(nb, gb * p, gb * SSM_GROUP)

    b_blk = jnp.concatenate([blk_b(jnp.real(b_bar)), blk_b(jnp.imag(b_bar))], axis=-1)
    c_blk = jnp.concatenate([blk_c(c_re.astype(F32)), blk_c(-c_im.astype(F32))], axis=1)
    lam_v = jnp.stack([jnp.real(lam_bar).reshape(nb, gb * p),
                       jnp.imag(lam_bar).reshape(nb, gb * p)], axis=1)
    return b_blk.astype(BF16), c_blk.astype(BF16), lam_v


def _s5_scan(xn, b_blk, c_blk, lam_v, d_skip, batch):
    n, d = xn.shape
    l = n // batch
    seg_len = l // S5_SEGMENTS
    nb, fb, ch2 = b_blk.shape
    rows = S5_TJ * S5_SEGMENTS
    return pl.pallas_call(
        functools.partial(_s5_body, seg_len=seg_len),
        grid=(batch, nb),
        in_specs=[pl.BlockSpec((l, fb), lambda b, c: (b, c)),
                  pl.BlockSpec((None, fb, ch2), lambda b, c: (c, 0, 0)),
                  pl.BlockSpec((None, ch2, fb), lambda b, c: (c, 0, 0)),
                  pl.BlockSpec((None, 2, ch2 // 2), lambda b, c: (c, 0, 0)),
                  pl.BlockSpec((1, fb), lambda b, c: (0, c))],
        out_specs=pl.BlockSpec((l, fb), lambda b, c: (b, c)),
        out_shape=jax.ShapeDtypeStruct((n, d), BF16),
        scratch_shapes=[pltpu.VMEM((rows, ch2), F32), pltpu.VMEM((rows, ch2), F32),
                        pltpu.VMEM((S5_SEGMENTS, ch2), F32), pltpu.VMEM((S5_SEGMENTS, ch2), F32)],
        compiler_params=_cparams(("arbitrary", "arbitrary")),
        name="s5_scan",
    )(xn, b_blk, c_blk, lam_v, d_skip.reshape(1, d).astype(F32))


def _headnorm_body(x_ref, g_ref, o_ref, *, scale):
    tm, d = x_ref.shape
    lane = lax.broadcasted_iota(jnp.int32, (tm, LANES), 1)
    lo = lane < HEAD_DIM
    g = g_ref[...]
    for hb in range(d // LANES):
        x = x_ref[:, hb * LANES:(hb + 1) * LANES]
        x2 = x * x
        s_lo = jnp.sum(jnp.where(lo, x2, 0.0), axis=-1, keepdims=True)
        s_hi = jnp.sum(jnp.where(lo, 0.0, x2), axis=-1, keepdims=True)
        r = jnp.where(lo, lax.rsqrt(s_lo * (1.0 / HEAD_DIM) + RMS_EPS),
                      lax.rsqrt(s_hi * (1.0 / HEAD_DIM) + RMS_EPS))
        o_ref[:, hb * LANES:(hb + 1) * LANES] = ((x * r * g) * scale).astype(o_ref.dtype)


def _headnorm(x, g, scale, tm=512):
    n, d = x.shape
    g2 = jnp.concatenate([g, g]).reshape(1, LANES).astype(F32)
    return pl.pallas_call(
        functools.partial(_headnorm_body, scale=scale),
        grid=(n // tm,),
        in_specs=[pl.BlockSpec((tm, d), lambda i: (i, 0)),
                  pl.BlockSpec((1, LANES), lambda i: (0, 0))],
        out_specs=pl.BlockSpec((tm, d), lambda i: (i, 0)),
        out_shape=jax.ShapeDtypeStruct((n, d), BF16),
        compiler_params=_cparams(("arbitrary",)),
        name="headnorm",
    )(x, g2)


def _t5_bucket(n):
    n_safe = jnp.maximum(n, 1).astype(F32)
    large = REL_MAX_EXACT + (jnp.log(n_safe / REL_MAX_EXACT)
                             / math.log(REL_MAX_DISTANCE / REL_MAX_EXACT)
                             * (REL_BUCKETS - REL_MAX_EXACT)).astype(jnp.int32)
    large = jnp.minimum(large, REL_BUCKETS - 1)
    return jnp.where(n < REL_MAX_EXACT, n, large)


def _attn_bias_tables(rel_bias, tq):
    qi = jnp.arange(tq, dtype=jnp.int32)[:, None]
    ki = jnp.arange(tq, dtype=jnp.int32)[None, :]
    d_diag = qi - ki
    d_prev = d_diag + tq
    rb = rel_bias.astype(F32)
    b_diag = jnp.where((d_diag >= 0)[..., None], rb[_t5_bucket(jnp.maximum(d_diag, 0))], NEG_BIG)
    b_prev = rb[_t5_bucket(d_prev)]
    tabs = jnp.stack([b_prev, b_diag], axis=0).transpose(3, 0, 1, 2)
    tabs = jnp.concatenate([tabs, tabs], axis=2)
    far = jnp.broadcast_to(rb[REL_BUCKETS - 1][:, None, None], (rb.shape[1], 1, LANES))
    return tabs, far


def _attn_body(q_ref, k_ref, v_ref, bias_ref, far_ref, lq_ref, sg_ref, o_ref,
               m_ref, l_ref, acc_ref, *, tq, lam_init):
    qi = pl.program_id(2)
    q = q_ref[...]
    lane = lax.broadcasted_iota(jnp.int32, q.shape, 1)
    zero = jnp.zeros_like(q)
    qq = jnp.concatenate([jnp.where(lane < HEAD_DIM, q, zero),
                          jnp.where(lane < HEAD_DIM, zero, q)], axis=0)
    m_ref[...] = jnp.full(m_ref.shape, NEG_BIG, F32)
    l_ref[...] = jnp.zeros(l_ref.shape, F32)
    acc_ref[...] = jnp.zeros(acc_ref.shape, F32)

    def step(kt, bias):
        r0 = pl.multiple_of(kt * tq, tq)
        k = k_ref[pl.ds(r0, tq), :]
        v = v_ref[pl.ds(r0, tq), :]
        s = lax.dot_general(qq, k, (((1,), (1,)), ((), ())), preferred_element_type=F32) + bias
        m_prev = m_ref[...]
        m_next = jnp.maximum(m_prev, jnp.max(s, axis=1, keepdims=True))
        alpha = jnp.exp(m_prev - m_next)
        p = jnp.exp(s - m_next[:, 0:1])
        l_ref[...] = alpha * l_ref[...] + jnp.sum(p, axis=1, keepdims=True)
        acc_ref[...] = acc_ref[...] * alpha + jnp.dot(p.astype(BF16), v,
                                                     preferred_element_type=F32)
        m_ref[...] = m_next

    far = far_ref[0:1, 0:1]

    def far_step(kt, c):
        step(kt, far)
        return c

    lax.fori_loop(0, jnp.maximum(qi - 1, 0), far_step, 0)

    @pl.when(qi >= 1)
    def _():
        step(qi - 1, bias_ref[0])

    step(qi, bias_ref[1])

    o_all = acc_ref[...] / l_ref[...]
    lq = lq_ref[...]
    lam = (jnp.exp(jnp.sum(lq[0:1] * lq[1:2], axis=1, keepdims=True))
           - jnp.exp(jnp.sum(lq[2:3] * lq[3:4], axis=1, keepdims=True)) + lam_init)
    o = o_all[0:tq] - lam * o_all[tq:2 * tq]
    ms = jnp.mean(o * o, axis=-1, keepdims=True)
    o = (o * lax.rsqrt(ms + RMS_EPS) * sg_ref[...]) * (1.0 - lam_init)
    o_ref[...] = o.astype(o_ref.dtype)


def _diff_attention(qn, kn, vb, bias_tabs, bias_far, lambda_qk, subln_g, lam_init, batch, tq=256):
    n, d = qn.shape
    l = n // batch
    nq = l // tq
    nh = d // LANES
    return pl.pallas_call(
        functools.partial(_attn_body, tq=tq, lam_init=lam_init),
        grid=(batch, nh, nq),
        in_specs=[pl.BlockSpec((tq, LANES), lambda b, h, i: (b * nq + i, h)),
                  pl.BlockSpec((l, LANES), lambda b, h, i: (b, h)),
                  pl.BlockSpec((l, LANES), lambda b, h, i: (b, h)),
                  pl.BlockSpec((None, 2, 2 * tq, tq), lambda b, h, i: (h, 0, 0, 0)),
                  pl.BlockSpec((None, 1, LANES), lambda b, h, i: (h, 0, 0)),
                  pl.BlockSpec((4, HEAD_DIM), lambda b, h, i: (0, 0)),
                  pl.BlockSpec((1, LANES), lambda b, h, i: (0, 0))],
        out_specs=pl.BlockSpec((tq, LANES), lambda b, h, i: (b * nq + i, h)),
        out_shape=jax.ShapeDtypeStruct((n, d), BF16),
        scratch_shapes=[pltpu.VMEM((2 * tq, LANES), F32), pltpu.VMEM((2 * tq, LANES), F32),
                        pltpu.VMEM((2 * tq, LANES), F32)],
        compiler_params=_cparams(("arbitrary", "arbitrary", "arbitrary")),
        name="diff_attention",
    )(qn, kn, vb, bias_tabs, bias_far, lambda_qk.astype(F32), subln_g.reshape(1, LANES).astype(F32))


ROUTER_E0 = N_EXPERT_GROUPS
META_COLS = 8


def _router_body(h_ref, g_ref, wr_ref, br_ref, xn_ref, meta_ref, cnt_ref, base_ref):
    tm = h_ref.shape[0]

    @pl.when(pl.program_id(0) == 0)
    def _():
        base_ref[...] = jnp.zeros(base_ref.shape, F32)

    x = h_ref[...]
    ms = jnp.mean(x * x, axis=-1, keepdims=True)
    xn = x * lax.rsqrt(ms + RMS_EPS) * g_ref[...]
    xn_ref[...] = xn.astype(xn_ref.dtype)
    logits = jnp.dot(xn, wr_ref[...], precision=lax.Precision.HIGHEST,
                     preferred_element_type=F32) + br_ref[...]
    lane = lax.broadcasted_iota(jnp.int32, (tm, LANES), 1)
    neg_inf = -jnp.inf

    def first_argmax(vals):
        vmax = jnp.max(vals, axis=-1, keepdims=True)
        idx = jnp.min(jnp.where(vals == vmax, lane, LANES), axis=-1, keepdims=True)
        return vmax, idx

    is_g = lane < N_EXPERT_GROUPS
    gmax, g_idx = first_argmax(jnp.where(is_g, logits, neg_inf))
    g_w = 1.0 / jnp.sum(jnp.where(is_g, jnp.exp(logits - gmax), 0.0), axis=-1, keepdims=True)
    e_lo = ROUTER_E0 + EXPERTS_PER_GROUP * g_idx
    elog = jnp.where((lane >= e_lo) & (lane < e_lo + EXPERTS_PER_GROUP), logits, neg_inf)
    v0, i0 = first_argmax(elog)
    v1, i1 = first_argmax(jnp.where(lane == i0, neg_inf, elog))
    t = jnp.exp(v1 - v0)
    w0 = g_w / (1.0 + t)
    w1 = g_w * t / (1.0 + t)

    sel0 = lane == i0
    sel1 = lane == i1
    onehot = jnp.where(sel0 | sel1, 1.0, 0.0)
    r_i = lax.broadcasted_iota(jnp.int32, (tm, tm), 0)
    c_i = lax.broadcasted_iota(jnp.int32, (tm, tm), 1)
    tri = jnp.where(c_i < r_i, 1.0, 0.0).astype(BF16)
    before = jnp.dot(tri, onehot.astype(BF16), preferred_element_type=F32) + base_ref[...]
    rank0 = jnp.sum(jnp.where(sel0, before, 0.0), axis=-1, keepdims=True)
    rank1 = jnp.sum(jnp.where(sel1, before, 0.0), axis=-1, keepdims=True)
    base_ref[...] = base_ref[...] + jnp.sum(onehot, axis=0, keepdims=True)
    cnt_ref[...] = base_ref[...]

    eid0 = (i0 - ROUTER_E0).astype(F32)
    eid1 = (i1 - ROUTER_E0).astype(F32)
    meta = jnp.where(lane == 0, eid0, jnp.where(lane == 1, eid1, jnp.where(
        lane == 2, rank0, jnp.where(lane == 3, rank1, jnp.where(
            lane == 4, w0, jnp.where(lane == 5, w1, 0.0))))))
    meta_ref[...] = meta[:, 0:META_COLS]


def _router(h, g, wg1, bg1, wg2, bg2, tm=256):
    n, d = h.shape
    wr = jnp.concatenate([wg1.astype(F32),
                          wg2.astype(F32).transpose(1, 0, 2).reshape(d, N_EXPERTS)], axis=1)
    wr = jnp.pad(wr, ((0, 0), (0, LANES - wr.shape[1])))
    br = jnp.pad(jnp.concatenate([bg1.astype(F32), bg2.astype(F32).reshape(-1)]),
                 (0, LANES - N_EXPERT_GROUPS - N_EXPERTS)).reshape(1, LANES)
    return pl.pallas_call(
        _router_body,
        grid=(n // tm,),
        in_specs=[pl.BlockSpec((tm, d), lambda i: (i, 0)),
                  pl.BlockSpec((1, d), lambda i: (0, 0)),
                  pl.BlockSpec((d, LANES), lambda i: (0, 0)),
                  pl.BlockSpec((1, LANES), lambda i: (0, 0))],
        out_specs=[pl.BlockSpec((tm, d), lambda i: (i, 0)),
                   pl.BlockSpec((tm, META_COLS), lambda i: (i, 0)),
                   pl.BlockSpec((1, LANES), lambda i: (0, 0))],
        out_shape=[jax.ShapeDtypeStruct((n, d), F32),
                   jax.ShapeDtypeStruct((n, META_COLS), F32),
                   jax.ShapeDtypeStruct((1, LANES), F32)],
        scratch_shapes=[pltpu.VMEM((1, LANES), F32)],
        compiler_params=_cparams(("arbitrary",)),
        name="moe_router",
    )(h, g.reshape(1, d).astype(F32), wr, br)


def _dispatch_body(dest_ref, x_hbm, o_hbm, sem, *, ts):
    t = pl.program_id(0)

    def row_copy(tok, dst):
        return pltpu.make_async_copy(x_hbm.at[pl.ds(tok, 1), :], o_hbm.at[pl.ds(dst, 1), :], sem)

    def issue(r, c):
        tok = t * ts + r
        row_copy(tok, dest_ref[0, 2 * r]).start()
        row_copy(tok, dest_ref[0, 2 * r + 1]).start()
        return c

    lax.fori_loop(0, ts, issue, 0)

    def drain(r, c):
        row_copy(0, 0).wait()
        row_copy(0, 0).wait()
        return c

    lax.fori_loop(0, ts, drain, 0)


def _dispatch(xn, dest, ts=512):
    n, d = xn.shape
    dest3 = dest.reshape(n // ts, 1, 2 * ts)
    return pl.pallas_call(
        functools.partial(_dispatch_body, ts=ts),
        grid=(n // ts,),
        in_specs=[pl.BlockSpec((None, 1, 2 * ts), lambda t: (t, 0, 0), memory_space=pltpu.SMEM),
                  pl.BlockSpec(memory_space=pl.ANY)],
        out_specs=pl.BlockSpec(memory_space=pl.ANY),
        out_shape=jax.ShapeDtypeStruct((2 * n, d), xn.dtype),
        scratch_shapes=[pltpu.SemaphoreType.DMA(())],
        compiler_params=_cparams(("arbitrary",)),
        name="moe_dispatch",
    )(dest3, xn)


def _combine_body(dest_ref, h_ref, w_ref, y_hbm, o_ref, y0_ref, y1_ref, sem, *, ts):
    def row_copy(src, buf_ref, r):
        return pltpu.make_async_copy(y_hbm.at[pl.ds(src, 1), :], buf_ref.at[pl.ds(r, 1), :], sem)

    def issue(r, c):
        row_copy(dest_ref[0, 2 * r], y0_ref, r).start()
        row_copy(dest_ref[0, 2 * r + 1], y1_ref, r).start()
        return c

    lax.fori_loop(0, ts, issue, 0)

    def drain(r, c):
        row_copy(0, y0_ref, 0).wait()
        row_copy(0, y1_ref, 0).wait()
        return c

    lax.fori_loop(0, ts, drain, 0)
    w = w_ref[...]
    o_ref[...] = h_ref[...] + w[:, 4:5] * y0_ref[...] + w[:, 5:6] * y1_ref[...]


def _combine(h, meta, y_sorted, dest, ts=256):
    n, d = h.shape
    dest3 = dest.reshape(n // ts, 1, 2 * ts)
    return pl.pallas_call(
        functools.partial(_combine_body, ts=ts),
        grid=(n // ts,),
        in_specs=[pl.BlockSpec((None, 1, 2 * ts), lambda t: (t, 0, 0), memory_space=pltpu.SMEM),
                  pl.BlockSpec((ts, d), lambda t: (t, 0)),
                  pl.BlockSpec((ts, META_COLS), lambda t: (t, 0)),
                  pl.BlockSpec(memory_space=pl.ANY)],
        out_specs=pl.BlockSpec((ts, d), lambda t: (t, 0)),
        out_shape=jax.ShapeDtypeStruct((n, d), F32),
        scratch_shapes=[pltpu.VMEM((ts, d), F32), pltpu.VMEM((ts, d), F32),
                        pltpu.SemaphoreType.DMA(())],
        compiler_params=_cparams(("arbitrary",)),
        name="moe_combine",
    )(dest3, h, meta, y_sorted)


def _expert_body(ie_ref, it_ref, lo_ref, hi_ref, first_ref, x_ref, wg_ref, wu_ref, wd_ref, o_ref,
                 xb_ref, *, tm):
    i = pl.program_id(0)
    c = pl.program_id(1)
    lo = lo_ref[i]
    hi = hi_ref[i]

    @pl.when(c == 0)
    def _():
        xb_ref[...] = x_ref[...].astype(BF16)

    @pl.when(hi > lo)
    def _():
        xb = xb_ref[...]
        g = jnp.dot(xb, wg_ref[...].astype(BF16), preferred_element_type=F32)
        u = jnp.dot(xb, wu_ref[...].astype(BF16), preferred_element_type=F32)
        hdn = (g * (1.0 / (1.0 + jnp.exp(-g)))) * u
        row = it_ref[i] * tm + lax.broadcasted_iota(jnp.int32, (tm, 1), 0)
        hdn = jnp.where((row >= lo) & (row < hi), hdn, 0.0)
        y = jnp.dot(hdn.astype(BF16), wd_ref[...].astype(BF16), preferred_element_type=F32)
        is_first = (first_ref[i] == 1) & (c == 0)

        @pl.when(is_first)
        def _():
            o_ref[...] = y

        @pl.when(jnp.logical_not(is_first))
        def _():
            o_ref[...] = o_ref[...] + y


def _experts(x_sorted, items, w_gate, w_up, w_down, layer, tm=512, th=256):
    r, d = x_sorted.shape
    dh = w_gate.shape[-1]
    hc = dh // th
    n_items = items[0].shape[0]

    def c_eff(i, c, lo, hi):
        return jnp.where(hi[i] > lo[i], c, hc - 1)

    grid_spec = pltpu.PrefetchScalarGridSpec(
        num_scalar_prefetch=5,
        grid=(n_items, hc),
        in_specs=[
            pl.BlockSpec((tm, d), lambda i, c, ie, it, lo, hi, fi: (it[i], 0)),
            pl.BlockSpec((None, None, d, th),
                         lambda i, c, ie, it, lo, hi, fi: (layer, ie[i], 0, c_eff(i, c, lo, hi))),
            pl.BlockSpec((None, None, d, th),
                         lambda i, c, ie, it, lo, hi, fi: (layer, ie[i], 0, c_eff(i, c, lo, hi))),
            pl.BlockSpec((None, None, th, d),
                         lambda i, c, ie, it, lo, hi, fi: (layer, ie[i], c_eff(i, c, lo, hi), 0)),
        ],
        out_specs=pl.BlockSpec((tm, d), lambda i, c, ie, it, lo, hi, fi: (it[i], 0)),
        scratch_shapes=[pltpu.VMEM((tm, d), BF16)],
    )
    return pl.pallas_call(
        functools.partial(_expert_body, tm=tm),
        grid_spec=grid_spec,
        out_shape=jax.ShapeDtypeStruct((r, d), F32),
        compiler_params=_cparams(("arbitrary", "arbitrary")),
        name="moe_experts",
    )(*items, x_sorted, w_gate, w_up, w_down)


def _moe_plan(meta, counts_row, n_rows, tm):
    counts = counts_row[0, ROUTER_E0:ROUTER_E0 + N_EXPERTS].astype(jnp.int32)
    ends = jnp.cumsum(counts)
    starts = ends - counts
    eid = meta[:, 0:2].astype(jnp.int32)
    rank = meta[:, 2:4].astype(jnp.int32)
    dest = starts[eid] + rank

    first_tile = starts // tm
    last_tile = jnp.maximum(ends - 1, 0) // tm
    ntile_e = jnp.where(counts > 0, last_tile - first_tile + 1, 0)
    item_end = jnp.cumsum(ntile_e)
    item_start = item_end - ntile_e
    total = item_end[-1]
    n_items = n_rows // tm + N_EXPERTS - 1
    ii = jnp.arange(n_items, dtype=jnp.int32)
    valid = ii < total
    ii_c = jnp.minimum(ii, total - 1)
    e_i = jnp.searchsorted(item_end, ii_c, side='right').astype(jnp.int32)
    t_i = first_tile[e_i] + (ii_c - item_start[e_i])
    lo = jnp.where(valid, jnp.maximum(starts[e_i], t_i * tm), 0)
    hi = jnp.where(valid, jnp.minimum(ends[e_i], (t_i + 1) * tm), 0)
    prev_t = jnp.concatenate([jnp.full((1,), -1, jnp.int32), t_i[:-1]])
    first = (valid & (t_i != prev_t)).astype(jnp.int32)
    items = (e_i, t_i.astype(jnp.int32), lo.astype(jnp.int32), hi.astype(jnp.int32), first)
    return dest.astype(jnp.int32), items


def _hier_moe(h, g, wg1, bg1, wg2, bg2, w_gate, w_up, w_down, layer, tm=512):
    n, _ = h.shape
    xn, meta, counts_row = _router(h, g, wg1, bg1, wg2, bg2)
    dest, items = _moe_plan(meta, counts_row, 2 * n, tm)
    x_sorted = _dispatch(xn, dest)
    y_sorted = _experts(x_sorted, items, w_gate, w_up, w_down, layer, tm=tm)
    return _combine(h, meta, y_sorted, dest)


def _time_permute(h, batch, inverse=False):
    n, d = h.shape
    l = n // batch
    shape = (batch, l // S5_SEGMENTS, S5_SEGMENTS, d) if inverse else (batch, S5_SEGMENTS, l // S5_SEGMENTS, d)
    return h.reshape(shape).transpose(0, 2, 1, 3).reshape(n, d)


def kernel(x, norm_mix_g, norm_ffn_g, ssm_lam_re, ssm_lam_im, ssm_b_re, ssm_b_im, ssm_c_re, ssm_c_im, ssm_d, ssm_log_step, ssm_w_glu1, ssm_w_glu2, kv_norm_g, w_kv, k_norm_g, w_q, q_norm_g, lambda_qk, subln_g, w_o, rel_bias, router_group_w, router_group_b, router_expert_w, router_expert_b, w_gate, w_up, w_down):
    batch, l, d = x.shape
    n = batch * l
    h = _time_permute(x.astype(F32).reshape(n, d), batch)
    kn = vb = bias_tabs = bias_far = None
    for layer in range(DEPTH):
        if layer < N_A_LAYERS:
            xn = _rmsnorm(h, norm_mix_g[layer], BF16)
            b_blk, c_blk, lam_v = _s5_params(ssm_lam_re[layer], ssm_lam_im[layer], ssm_b_re[layer],
                                             ssm_b_im[layer], ssm_c_re[layer], ssm_c_im[layer],
                                             ssm_log_step[layer])
            z = _s5_scan(xn, b_blk, c_blk, lam_v, ssm_d[layer], batch)
            h = _matmul(z, [ssm_w_glu1, ssm_w_glu2], layer, F32, res=h)
        else:
            j = layer - N_A_LAYERS
            lam_init = 0.8 - 0.6 * math.exp(-0.3 * layer)
            xn = _rmsnorm(h, norm_mix_g[layer], BF16)
            qn = _headnorm(_matmul(xn, [w_q], j, F32), q_norm_g[j], HEAD_DIM ** -0.5)
            o = _diff_attention(qn, kn, vb, bias_tabs, bias_far, lambda_qk[j], subln_g[j],
                                lam_init, batch)
            h = _matmul(o, [w_o], j, F32, res=h)
        h = _hier_moe(h, norm_ffn_g[layer], router_group_w[layer], router_group_b[layer],
                      router_expert_w[layer], router_expert_b[layer], w_gate, w_up, w_down, layer)
        if layer == N_A_LAYERS - 1:
            h = _time_permute(h, batch, inverse=True)
            xkv = _rmsnorm(h, kv_norm_g, BF16)
            w_kv3 = w_kv.reshape(1, d, 2 * d)
            kn = _headnorm(_matmul(xkv, [w_kv3], 0, F32, n_out=d), k_norm_g, 1.0)
            vb = _matmul(xkv, [w_kv3], 0, BF16, col_off=d // 512, n_out=d)
            bias_tabs, bias_far = _attn_bias_tables(rel_bias, 256)
    return h.reshape(batch, l, d)
```

```python
import functools
import math

import jax
import jax.numpy as jnp
from jax import lax
from jax.experimental import pallas as pl
from jax.experimental.pallas import tpu as pltpu

F32 = jnp.float32
BF16 = jnp.bfloat16

DEPTH = 4
N_A_LAYERS = DEPTH // 2
SSM_GROUP = 16
SSM_STATE = 64
HEAD_DIM = 64
REL_BUCKETS = 32
REL_MAX_EXACT = REL_BUCKETS // 2
REL_MAX_DISTANCE = 128
N_EXPERT_GROUPS = 4
EXPERTS_PER_GROUP = 4
N_EXPERTS = N_EXPERT_GROUPS * EXPERTS_PER_GROUP
RMS_EPS = 1e-6

LANES = 128
SUBLANES = 8
VMEM_LIMIT = 48 * 1024 * 1024
NEG_BIG = -1e30

S5_SEGMENTS = SUBLANES
S5_GROUPS_PER_BLOCK = 16
S5_TJ = 32


def _cparams(sem):
    return pltpu.CompilerParams(dimension_semantics=sem, vmem_limit_bytes=VMEM_LIMIT)


def _rmsnorm_body(h_ref, g_ref, o_ref):
    x = h_ref[...]
    ms = jnp.mean(x * x, axis=-1, keepdims=True)
    o_ref[...] = (x * lax.rsqrt(ms + RMS_EPS) * g_ref[...]).astype(o_ref.dtype)


def _rmsnorm(h, g, out_dtype, tm=512):
    n, d = h.shape
    return pl.pallas_call(
        _rmsnorm_body,
        grid=(n // tm,),
        in_specs=[pl.BlockSpec((tm, d), lambda i: (i, 0)),
                  pl.BlockSpec((1, d), lambda i: (0, 0))],
        out_specs=pl.BlockSpec((tm, d), lambda i: (i, 0)),
        out_shape=jax.ShapeDtypeStruct((n, d), out_dtype),
        compiler_params=_cparams(("arbitrary",)),
        name="rmsnorm",
    )(h, g.reshape(1, d))


def _mm_body(*refs, n_w, has_res):
    a_ref = refs[0]
    w_refs = refs[1:1 + n_w]
    res_ref = refs[1 + n_w] if has_res else None
    o_ref = refs[1 + n_w + int(has_res)]
    wb_refs = refs[2 + n_w + int(has_res):]

    @pl.when(pl.program_id(1) == 0)
    def _():
        for w_ref, wb_ref in zip(w_refs, wb_refs):
            wb_ref[...] = w_ref[...].astype(BF16)

    a = a_ref[...]
    y = jnp.dot(a, wb_refs[0][...], preferred_element_type=F32)
    if n_w == 2:
        y2 = jnp.dot(a, wb_refs[1][...], preferred_element_type=F32)
        y = y * (1.0 / (1.0 + jnp.exp(-y2)))
    if has_res:
        y = y + res_ref[...]
    o_ref[...] = y.astype(o_ref.dtype)


def _matmul(a, ws, layer, out_dtype, res=None, col_off=0, n_out=None, tm=512, tn=512):
    m, k = a.shape
    n_out = ws[0].shape[-1] if n_out is None else n_out
    n_w = len(ws)
    in_specs = [pl.BlockSpec((tm, k), lambda j, i: (i, 0))]
    for _ in ws:
        in_specs.append(pl.BlockSpec((None, k, tn), lambda j, i: (layer, 0, j + col_off)))
    args = [a, *ws]
    if res is not None:
        in_specs.append(pl.BlockSpec((tm, tn), lambda j, i: (i, j)))
        args.append(res)
    return pl.pallas_call(
        functools.partial(_mm_body, n_w=n_w, has_res=res is not None),
        grid=(n_out // tn, m // tm),
        in_specs=in_specs,
        out_specs=pl.BlockSpec((tm, tn), lambda j, i: (i, j)),
        out_shape=jax.ShapeDtypeStruct((m, n_out), out_dtype),
        scratch_shapes=[pltpu.VMEM((k, tn), BF16) for _ in ws],
        compiler_params=_cparams(("arbitrary", "arbitrary")),
        name="dense_matmul",
    )(*args)


def _gelu_tanh(y):
    c = math.sqrt(2.0 / math.pi)
    return y * (0.5 * (1.0 + jnp.tanh(c * (y + 0.044715 * (y * y * y)))))


def _s5_body(x_ref, b_ref, c_ref, lam_ref, d_ref, z_ref, bu_ref, st_ref, e_ref, init_ref,
             *, seg_len):
    ch = lam_ref.shape[-1]
    rows = S5_TJ * S5_SEGMENTS
    n_tiles = seg_len // S5_TJ
    lr = jnp.broadcast_to(lam_ref[0:1, :], (S5_SEGMENTS, ch))
    li = jnp.broadcast_to(lam_ref[1:2, :], (S5_SEGMENTS, ch))

    def scan_tile(t, s_re, s_im, store):
        r0 = pl.multiple_of(t * rows, rows)
        bu_ref[...] = jnp.dot(x_ref[pl.ds(r0, rows), :], b_ref[...], preferred_element_type=F32)
        for j in range(S5_TJ):
            sl = slice(S5_SEGMENTS * j, S5_SEGMENTS * (j + 1))
            n_re = lr * s_re - li * s_im + bu_ref[sl, 0:ch]
            n_im = lr * s_im + li * s_re + bu_ref[sl, ch:2 * ch]
            s_re, s_im = n_re, n_im
            if store:
                st_ref[sl, 0:ch] = s_re
                st_ref[sl, ch:2 * ch] = s_im
        return s_re, s_im

    zeros = jnp.zeros((S5_SEGMENTS, ch), F32)
    e_re, e_im = lax.fori_loop(0, n_tiles, lambda t, c: scan_tile(t, c[0], c[1], False),
                               (zeros, zeros))
    e_ref[:, 0:ch] = e_re
    e_ref[:, ch:2 * ch] = e_im

    pr, pi = lam_ref[0:1, :], lam_ref[1:2, :]
    for _ in range(seg_len.bit_length() - 1):
        pr, pi = pr * pr - pi * pi, 2.0 * pr * pi
    cr = jnp.zeros((1, ch), F32)
    ci = jnp.zeros((1, ch), F32)
    init_ref[0:1, :] = jnp.zeros((1, 2 * ch), F32)
    for k in range(S5_SEGMENTS - 1):
        er = e_ref[k:k + 1, 0:ch]
        ei = e_ref[k:k + 1, ch:2 * ch]
        cr, ci = pr * cr - pi * ci + er, pr * ci + pi * cr + ei
        init_ref[k + 1:k + 2, 0:ch] = cr
        init_ref[k + 1:k + 2, ch:2 * ch] = ci

    def pass2(t, c):
        s_re, s_im = scan_tile(t, c[0], c[1], True)
        r0 = pl.multiple_of(t * rows, rows)
        y = jnp.dot(st_ref[...].astype(BF16), c_ref[...], preferred_element_type=F32)
        y = y + d_ref[...] * x_ref[pl.ds(r0, rows), :].astype(F32)
        z_ref[pl.ds(r0, rows), :] = _gelu_tanh(y).astype(z_ref.dtype)
        return s_re, s_im

    lax.fori_loop(0, n_tiles, pass2, (init_ref[:, 0:ch], init_ref[:, ch:2 * ch]))


def _s5_params(lam_re, lam_im, b_re, b_im, c_re, c_im, log_step):
    g, p = lam_re.shape
    gb = S5_GROUPS_PER_BLOCK
    nb = g // gb
    lam = lax.complex(lam_re.astype(F32), lam_im.astype(F32))
    delta = jnp.exp(log_step.astype(F32))[:, None]
    lam_bar = jnp.exp(lam * delta)
    b_bar = ((lam_bar - 1.0) / lam)[..., None] * lax.complex(b_re.astype(F32), b_im.astype(F32))
    eye = jnp.eye(gb, dtype=F32)

    def blk_b(part):
        part = part.reshape(nb, gb, p, SSM_GROUP)
        return jnp.einsum('cgph,gk->cghkp', part, eye).reshape(nb, gb * SSM_GROUP, gb * p)

    def blk_c(part):
        part = part.reshape(nb, gb, SSM_GROUP, p)
        return jnp.einsum('cghp,gk->cgpkh', part, eye).reshape(nb, gb * p, gb * SSM_GROUP)

    b_blk = jnp.concatenate([blk_b(jnp.real(b_bar)), blk_b(jnp.imag(b_bar))], axis=-1)
    c_blk = jnp.concatenate([blk_c(c_re.astype(F32)), blk_c(-c_im.astype(F32))], axis=1)
    lam_v = jnp.stack([jnp.real(lam_bar).reshape(nb, gb * p),
                       jnp.imag(lam_bar).reshape(nb, gb * p)], axis=1)
    return b_blk.astype(BF16), c_blk.astype(BF16), lam_v


def _s5_scan(xn, b_blk, c_blk, lam_v, d_skip, batch):
    n, d = xn.shape
    l = n // batch
    seg_len = l // S5_SEGMENTS
    nb, fb, ch2 = b_blk.shape
    rows = S5_TJ * S5_SEGMENTS
    return pl.pallas_call(
        functools.partial(_s5_body, seg_len=seg_len),
        grid=(batch, nb),
        in_specs=[pl.BlockSpec((l, fb), lambda b, c: (b, c)),
                  pl.BlockSpec((None, fb, ch2), lambda b, c: (c, 0, 0)),
                  pl.BlockSpec((None, ch2, fb), lambda b, c: (c, 0, 0)),
                  pl.BlockSpec((None, 2, ch2 // 2), lambda b, c: (c, 0, 0)),
                  pl.BlockSpec((1, fb), lambda b, c: (0, c))],
        out_specs=pl.BlockSpec((l, fb), lambda b, c: (b, c)),
        out_shape=jax.ShapeDtypeStruct((n, d), BF16),
        scratch_shapes=[pltpu.VMEM((rows, ch2), F32), pltpu.VMEM((rows, ch2), F32),
                        pltpu.VMEM((S5_SEGMENTS, ch2), F32), pltpu.VMEM((S5_SEGMENTS, ch2), F32)],
        compiler_params=_cparams(("arbitrary", "arbitrary")),
        name="s5_scan",
    )(xn, b_blk, c_blk, lam_v, d_skip.reshape(1, d).astype(F32))


def _headnorm_body(x_ref, g_ref, o_ref, *, scale):
    tm, d = x_ref.shape
    lane = lax.broadcasted_iota(jnp.int32, (tm, LANES), 1)
    lo = lane < HEAD_DIM
    g = g_ref[...]
    for hb in range(d // LANES):
        x = x_ref[:, hb * LANES:(hb + 1) * LANES]
        x2 = x * x
        s_lo = jnp.sum(jnp.where(lo, x2, 0.0), axis=-1, keepdims=True)
        s_hi = jnp.sum(jnp.where(lo, 0.0, x2), axis=-1, keepdims=True)
        r = jnp.where(lo, lax.rsqrt(s_lo * (1.0 / HEAD_DIM) + RMS_EPS),
                      lax.rsqrt(s_hi * (1.0 / HEAD_DIM) + RMS_EPS))
        o_ref[:, hb * LANES:(hb + 1) * LANES] = ((x * r * g) * scale).astype(o_ref.dtype)


def _headnorm(x, g, scale, tm=512):
    n, d = x.shape
    g2 = jnp.concatenate([g, g]).reshape(1, LANES).astype(F32)
    return pl.pallas_call(
        functools.partial(_headnorm_body, scale=scale),
        grid=(n // tm,),
        in_specs=[pl.BlockSpec((tm, d), lambda i: (i, 0)),
                  pl.BlockSpec((1, LANES), lambda i: (0, 0))],
        out_specs=pl.BlockSpec((tm, d), lambda i: (i, 0)),
        out_shape=jax.ShapeDtypeStruct((n, d), BF16),
        compiler_params=_cparams(("arbitrary",)),
        name="headnorm",
    )(x, g2)


def _t5_bucket(n):
    n_safe = jnp.maximum(n, 1).astype(F32)
    large = REL_MAX_EXACT + (jnp.log(n_safe / REL_MAX_EXACT)
                             / math.log(REL_MAX_DISTANCE / REL_MAX_EXACT)
                             * (REL_BUCKETS - REL_MAX_EXACT)).astype(jnp.int32)
    large = jnp.minimum(large, REL_BUCKETS - 1)
    return jnp.where(n < REL_MAX_EXACT, n, large)


ATTN_TQ = 256
BIAS_FAR, BIAS_PREV, BIAS_DIAG, BIAS_NULL = 0, 1, 2, 3
M_INIT = -1e29


def _attn_bias_tables(rel_bias, tq):
    qi = jnp.arange(tq, dtype=jnp.int32)[:, None]
    ki = jnp.arange(tq, dtype=jnp.int32)[None, :]
    d_diag = qi - ki
    rb = rel_bias.astype(F32)
    nh = rb.shape[1]

    def lookup(dist):
        onehot = jax.nn.one_hot(_t5_bucket(dist), REL_BUCKETS, dtype=F32)
        return jnp.einsum('qkb,bh->hqk', onehot, rb, precision=lax.Precision.HIGHEST)

    b_diag = jnp.where((d_diag >= 0)[None], lookup(jnp.maximum(d_diag, 0)), NEG_BIG)
    b_prev = lookup(d_diag + tq)
    b_far = jnp.broadcast_to(rb[REL_BUCKETS - 1][:, None, None], (nh, tq, tq))
    b_null = jnp.full((nh, tq, tq), NEG_BIG, F32)
    return jnp.stack([b_far, b_prev, b_diag, b_null], axis=1)


def _attn_pair_tables(nq):
    q_row, slot, kt, bidx = [], [], [], []

    def idle():
        q_row.append(0)
        slot.append(nq)
        kt.append(0)
        bidx.append(BIAS_NULL)

    idle()
    idle()
    for i in range(nq):
        for j in range(i + 1):
            q_row.append(i)
            slot.append(i)
            kt.append(j)
            bidx.append(BIAS_DIAG if j == i else (BIAS_PREV if j == i - 1 else BIAS_FAR))
    idle()
    idle()
    if len(q_row) % 2:
        idle()
    return [jnp.asarray(t, jnp.int32) for t in (q_row, slot, kt, bidx)]


def _attn_body(qrow_ref, slot_ref, kt_ref, bidx_ref, q_ref, k_ref, v_ref, bias_ref, lq_ref, sg_ref,
               o_ref, m_ref, acc_ref, vaug_ref, s0_ref, s1_ref, p0_ref, p1_ref, a0_ref, a1_ref,
               *, tq, nq, n_iter, lam_init):
    rc = 64
    m_ref[...] = jnp.full(m_ref.shape, M_INIT, F32)
    acc_ref[...] = jnp.zeros(acc_ref.shape, F32)
    s0_ref[...] = jnp.zeros(s0_ref.shape, F32)
    s1_ref[...] = jnp.zeros(s1_ref.shape, F32)
    p0_ref[...] = jnp.zeros(p0_ref.shape, BF16)
    a0_ref[...] = jnp.ones(a0_ref.shape, F32)
    vaug_ref[:, 0:LANES] = v_ref[...]
    vaug_ref[:, LANES:2 * LANES] = jnp.ones(v_ref.shape, BF16)
    lane = lax.broadcasted_iota(jnp.int32, (tq, LANES), 1)

    def scores(e, s_ref):
        q = q_ref[pl.ds(pl.multiple_of(qrow_ref[e] * tq, tq), tq), :]
        zero = jnp.zeros_like(q)
        qq = jnp.concatenate([jnp.where(lane < HEAD_DIM, q, zero),
                              jnp.where(lane < HEAD_DIM, zero, q)], axis=0)
        k = k_ref[pl.ds(pl.multiple_of(kt_ref[e] * tq, tq), tq), :]
        s_ref[...] = lax.dot_general(qq, k, (((1,), (1,)), ((), ())), preferred_element_type=F32)

    def softmax(e, s_ref, a_ref, p_ref):
        m_view = m_ref.at[slot_ref[e]]
        b_view = bias_ref.at[bidx_ref[e]]
        for c in range(tq // rc):
            b = b_view[c * rc:(c + 1) * rc, :]
            for half in range(2):
                rows = slice(half * tq + c * rc, half * tq + (c + 1) * rc)
                s = s_ref[rows, :] + b
                m_prev = m_view[rows, :]
                m_next = jnp.maximum(m_prev, jnp.max(s, axis=1, keepdims=True))
                a_ref[rows, :] = jnp.exp(m_prev - m_next)
                p_ref[rows, :] = jnp.exp(s - m_next[:, 0:1]).astype(BF16)
                m_view[rows, :] = m_next

    def values(e, a_ref, p_ref):
        acc = acc_ref.at[slot_ref[e]]
        v = vaug_ref[pl.ds(pl.multiple_of(kt_ref[e] * tq, tq), tq), :]
        pv = jnp.dot(p_ref[...], v, preferred_element_type=F32)
        a = a_ref[...]
        acc[:, 0:LANES] = acc[:, 0:LANES] * a + pv[:, 0:LANES]
        acc[:, LANES:2 * LANES] = acc[:, LANES:2 * LANES] * a + pv[:, LANES:2 * LANES]

    def two_iterations(i2, c):
        it = 2 * i2
        scores(it + 2, s0_ref)
        softmax(it + 1, s1_ref, a1_ref, p1_ref)
        values(it, a0_ref, p0_ref)
        scores(it + 3, s1_ref)
        softmax(it + 2, s0_ref, a0_ref, p0_ref)
        values(it + 1, a1_ref, p1_ref)
        return c

    lax.fori_loop(0, n_iter // 2, two_iterations, 0)

    lq = lq_ref[...]
    lam = (jnp.exp(jnp.sum(lq[0:1] * lq[1:2], axis=1, keepdims=True))
           - jnp.exp(jnp.sum(lq[2:3] * lq[3:4], axis=1, keepdims=True)) + lam_init)

    def finalize(i, c):
        acc = acc_ref[i]
        o_all = acc[:, 0:LANES] / acc[:, LANES:2 * LANES]
        o = o_all[0:tq] - lam * o_all[tq:2 * tq]
        ms = jnp.mean(o * o, axis=-1, keepdims=True)
        o = (o * lax.rsqrt(ms + RMS_EPS) * sg_ref[...]) * (1.0 - lam_init)
        o_ref[pl.ds(pl.multiple_of(i * tq, tq), tq), :] = o.astype(o_ref.dtype)
        return c

    lax.fori_loop(0, nq, finalize, 0)


def _diff_attention(qn, kn, vb, bias_tabs, lambda_qk, subln_g, lam_init, batch):
    n, d = qn.shape
    l = n // batch
    tq = ATTN_TQ
    nq = l // tq
    nh = d // LANES
    tables = _attn_pair_tables(nq)
    n_iter = tables[0].shape[0] - 2
    row_blk = pl.BlockSpec((l, LANES), lambda b, h, *_: (b, h))
    grid_spec = pltpu.PrefetchScalarGridSpec(
        num_scalar_prefetch=4,
        grid=(batch, nh),
        in_specs=[row_blk, row_blk, row_blk,
                  pl.BlockSpec((None, 4, tq, tq), lambda b, h, *_: (h, 0, 0, 0)),
                  pl.BlockSpec((4, HEAD_DIM), lambda b, h, *_: (0, 0)),
                  pl.BlockSpec((1, LANES), lambda b, h, *_: (0, 0))],
        out_specs=row_blk,
        scratch_shapes=[pltpu.VMEM((nq + 1, 2 * tq, LANES), F32),
                        pltpu.VMEM((nq + 1, 2 * tq, 2 * LANES), F32),
                        pltpu.VMEM((l, 2 * LANES), BF16),
                        pltpu.VMEM((2 * tq, tq), F32), pltpu.VMEM((2 * tq, tq), F32),
                        pltpu.VMEM((2 * tq, tq), BF16), pltpu.VMEM((2 * tq, tq), BF16),
                        pltpu.VMEM((2 * tq, LANES), F32), pltpu.VMEM((2 * tq, LANES), F32)],
    )
    return pl.pallas_call(
        functools.partial(_attn_body, tq=tq, nq=nq, n_iter=n_iter, lam_init=lam_init),
        grid_spec=grid_spec,
        out_shape=jax.ShapeDtypeStruct((n, d), BF16),
        compiler_params=_cparams(("arbitrary", "arbitrary")),
        name="diff_attention",
    )(*tables, qn, kn, vb, bias_tabs, lambda_qk.astype(F32),
      subln_g.reshape(1, LANES).astype(F32))


ROUTER_E0 = N_EXPERT_GROUPS
META_COLS = 8


def _router_body(h_ref, g_ref, wr_ref, br_ref, xn_ref, meta_ref, cnt_ref, base_ref):
    tm = h_ref.shape[0]

    @pl.when(pl.program_id(0) == 0)
    def _():
        base_ref[...] = jnp.zeros(base_ref.shape, F32)

    x = h_ref[...]
    ms = jnp.mean(x * x, axis=-1, keepdims=True)
    xn = x * lax.rsqrt(ms + RMS_EPS) * g_ref[...]
    xn_ref[...] = xn.astype(xn_ref.dtype)
    logits = jnp.dot(xn, wr_ref[...], precision=lax.Precision.HIGHEST,
                     preferred_element_type=F32) + br_ref[...]
    lane = lax.broadcasted_iota(jnp.int32, (tm, LANES), 1)
    neg_inf = -jnp.inf

    def first_argmax(vals):
        vmax = jnp.max(vals, axis=-1, keepdims=True)
        idx = jnp.min(jnp.where(vals == vmax, lane, LANES), axis=-1, keepdims=True)
        return vmax, idx

    is_g = lane < N_EXPERT_GROUPS
    gmax, g_idx = first_argmax(jnp.where(is_g, logits, neg_inf))
    g_w = 1.0 / jnp.sum(jnp.where(is_g, jnp.exp(logits - gmax), 0.0), axis=-1, keepdims=True)
    e_lo = ROUTER_E0 + EXPERTS_PER_GROUP * g_idx
    elog = jnp.where((lane >= e_lo) & (lane < e_lo + EXPERTS_PER_GROUP), logits, neg_inf)
    v0, i0 = first_argmax(elog)
    v1, i1 = first_argmax(jnp.where(lane == i0, neg_inf, elog))
    t = jnp.exp(v1 - v0)
    w0 = g_w / (1.0 + t)
    w1 = g_w * t / (1.0 + t)

    sel0 = lane == i0
    sel1 = lane == i1
    onehot = jnp.where(sel0 | sel1, 1.0, 0.0)
    r_i = lax.broadcasted_iota(jnp.int32, (tm, tm), 0)
    c_i = lax.broadcasted_iota(jnp.int32, (tm, tm), 1)
    tri = jnp.where(c_i < r_i, 1.0, 0.0).astype(BF16)
    before = jnp.dot(tri, onehot.astype(BF16), preferred_element_type=F32) + base_ref[...]
    rank0 = jnp.sum(jnp.where(sel0, before, 0.0), axis=-1, keepdims=True)
    rank1 = jnp.sum(jnp.where(sel1, before, 0.0), axis=-1, keepdims=True)
    base_ref[...] = base_ref[...] + jnp.sum(onehot, axis=0, keepdims=True)
    cnt_ref[...] = base_ref[...]

    eid0 = (i0 - ROUTER_E0).astype(F32)
    eid1 = (i1 - ROUTER_E0).astype(F32)
    meta = jnp.where(lane == 0, eid0, jnp.where(lane == 1, eid1, jnp.where(
        lane == 2, rank0, jnp.where(lane == 3, rank1, jnp.where(
            lane == 4, w0, jnp.where(lane == 5, w1, 0.0))))))
    meta_ref[...] = meta[:, 0:META_COLS]


def _router(h, g, wg1, bg1, wg2, bg2, tm=256):
    n, d = h.shape
    wr = jnp.concatenate([wg1.astype(F32),
                          wg2.astype(F32).transpose(1, 0, 2).reshape(d, N_EXPERTS)], axis=1)
    wr = jnp.pad(wr, ((0, 0), (0, LANES - wr.shape[1])))
    br = jnp.pad(jnp.concatenate([bg1.astype(F32), bg2.astype(F32).reshape(-1)]),
                 (0, LANES - N_EXPERT_GROUPS - N_EXPERTS)).reshape(1, LANES)
    return pl.pallas_call(
        _router_body,
        grid=(n // tm,),
        in_specs=[pl.BlockSpec((tm, d), lambda i: (i, 0)),
                  pl.BlockSpec((1, d), lambda i: (0, 0)),
                  pl.BlockSpec((d, LANES), lambda i: (0, 0)),
                  pl.BlockSpec((1, LANES), lambda i: (0, 0))],
        out_specs=[pl.BlockSpec((tm, d), lambda i: (i, 0)),
                   pl.BlockSpec((tm, META_COLS), lambda i: (i, 0)),
                   pl.BlockSpec((1, LANES), lambda i: (0, 0))],
        out_shape=[jax.ShapeDtypeStruct((n, d), F32),
                   jax.ShapeDtypeStruct((n, META_COLS), F32),
                   jax.ShapeDtypeStruct((1, LANES), F32)],
        scratch_shapes=[pltpu.VMEM((1, LANES), F32)],
        compiler_params=_cparams(("arbitrary",)),
        name="moe_router",
    )(h, g.reshape(1, d).astype(F32), wr, br)


def _invert_body(dest_ref, rt_ref, *, ts):
    t = pl.program_id(0)

    def body(r, c):
        tok = t * ts + r
        rt_ref[dest_ref[0, 2 * r]] = tok
        rt_ref[dest_ref[0, 2 * r + 1]] = tok
        return c

    lax.fori_loop(0, ts, body, 0)


def _invert(dest, ts=512):
    n = dest.shape[0]
    dest3 = dest.reshape(n // ts, 1, 2 * ts)
    return pl.pallas_call(
        functools.partial(_invert_body, ts=ts),
        grid=(n // ts,),
        in_specs=[pl.BlockSpec((None, 1, 2 * ts), lambda t: (t, 0, 0), memory_space=pltpu.SMEM)],
        out_specs=pl.BlockSpec(memory_space=pltpu.SMEM),
        out_shape=jax.ShapeDtypeStruct((2 * n,), jnp.int32),
        compiler_params=_cparams(("arbitrary",)),
        name="moe_invert",
    )(dest3)


def _dispatch_body(rt_ref, x_hbm, o_ref, sem, *, tr):
    def row_copy(src, r):
        return pltpu.make_async_copy(x_hbm.at[pl.ds(src, 1), :], o_ref.at[pl.ds(r, 1), :], sem)

    def issue(r, c):
        row_copy(rt_ref[0, r], r).start()
        return c

    lax.fori_loop(0, tr, issue, 0)

    def drain(r, c):
        row_copy(0, 0).wait()
        return c

    lax.fori_loop(0, tr, drain, 0)


def _dispatch(xn, row_token, tr=512):
    _, d = xn.shape
    r_total = row_token.shape[0]
    rt3 = row_token.reshape(r_total // tr, 1, tr)
    return pl.pallas_call(
        functools.partial(_dispatch_body, tr=tr),
        grid=(r_total // tr,),
        in_specs=[pl.BlockSpec((None, 1, tr), lambda t: (t, 0, 0), memory_space=pltpu.SMEM),
                  pl.BlockSpec(memory_space=pl.ANY)],
        out_specs=pl.BlockSpec((tr, d), lambda t: (t, 0)),
        out_shape=jax.ShapeDtypeStruct((r_total, d), xn.dtype),
        scratch_shapes=[pltpu.SemaphoreType.DMA(())],
        compiler_params=_cparams(("arbitrary",)),
        name="moe_dispatch",
    )(rt3, xn)


def _combine_body(dest_ref, h_ref, w_ref, y_hbm, o_ref, y0_ref, y1_ref, sem, *, ts):
    def row_copy(src, buf_ref, r):
        return pltpu.make_async_copy(y_hbm.at[pl.ds(src, 1), :], buf_ref.at[pl.ds(r, 1), :], sem)

    def issue(r, c):
        row_copy(dest_ref[0, 2 * r], y0_ref, r).start()
        row_copy(dest_ref[0, 2 * r + 1], y1_ref, r).start()
        return c

    lax.fori_loop(0, ts, issue, 0)

    def drain(r, c):
        row_copy(0, y0_ref, 0).wait()
        row_copy(0, y1_ref, 0).wait()
        return c

    lax.fori_loop(0, ts, drain, 0)
    w = w_ref[...]
    o_ref[...] = h_ref[...] + w[:, 4:5] * y0_ref[...] + w[:, 5:6] * y1_ref[...]


def _combine(h, meta, y_sorted, dest, ts=256):
    n, d = h.shape
    dest3 = dest.reshape(n // ts, 1, 2 * ts)
    return pl.pallas_call(
        functools.partial(_combine_body, ts=ts),
        grid=(n // ts,),
        in_specs=[pl.BlockSpec((None, 1, 2 * ts), lambda t: (t, 0, 0), memory_space=pltpu.SMEM),
                  pl.BlockSpec((ts, d), lambda t: (t, 0)),
                  pl.BlockSpec((ts, META_COLS), lambda t: (t, 0)),
                  pl.BlockSpec(memory_space=pl.ANY)],
        out_specs=pl.BlockSpec((ts, d), lambda t: (t, 0)),
        out_shape=jax.ShapeDtypeStruct((n, d), F32),
        scratch_shapes=[pltpu.VMEM((ts, d), F32), pltpu.VMEM((ts, d), F32),
                        pltpu.SemaphoreType.DMA(())],
        compiler_params=_cparams(("arbitrary",)),
        name="moe_combine",
    )(dest3, h, meta, y_sorted)


def _expert_body(ie_ref, it_ref, lo_ref, hi_ref, first_ref, x_ref, wg_ref, wu_ref, wd_ref, o_ref,
                 xb_ref, *, tm):
    i = pl.program_id(0)
    c = pl.program_id(1)
    lo = lo_ref[i]
    hi = hi_ref[i]

    @pl.when(c == 0)
    def _():
        xb_ref[...] = x_ref[...].astype(BF16)

    @pl.when(hi > lo)
    def _():
        xb = xb_ref[...]
        g = jnp.dot(xb, wg_ref[...].astype(BF16), preferred_element_type=F32)
        u = jnp.dot(xb, wu_ref[...].astype(BF16), preferred_element_type=F32)
        hdn = (g * (1.0 / (1.0 + jnp.exp(-g)))) * u
        row = it_ref[i] * tm + lax.broadcasted_iota(jnp.int32, (tm, 1), 0)
        hdn = jnp.where((row >= lo) & (row < hi), hdn, 0.0)
        y = jnp.dot(hdn.astype(BF16), wd_ref[...].astype(BF16), preferred_element_type=F32)
        is_first = (first_ref[i] == 1) & (c == 0)

        @pl.when(is_first)
        def _():
            o_ref[...] = y

        @pl.when(jnp.logical_not(is_first))
        def _():
            o_ref[...] = o_ref[...] + y


def _experts(x_sorted, items, w_gate, w_up, w_down, layer, tm=512, th=256):
    r, d = x_sorted.shape
    dh = w_gate.shape[-1]
    hc = dh // th
    n_items = items[0].shape[0]

    def c_eff(i, c, lo, hi):
        return jnp.where(hi[i] > lo[i], c, hc - 1)

    grid_spec = pltpu.PrefetchScalarGridSpec(
        num_scalar_prefetch=5,
        grid=(n_items, hc),
        in_specs=[
            pl.BlockSpec((tm, d), lambda i, c, ie, it, lo, hi, fi: (it[i], 0)),
            pl.BlockSpec((None, None, d, th),
                         lambda i, c, ie, it, lo, hi, fi: (layer, ie[i], 0, c_eff(i, c, lo, hi))),
            pl.BlockSpec((None, None, d, th),
                         lambda i, c, ie, it, lo, hi, fi: (layer, ie[i], 0, c_eff(i, c, lo, hi))),
            pl.BlockSpec((None, None, th, d),
                         lambda i, c, ie, it, lo, hi, fi: (layer, ie[i], c_eff(i, c, lo, hi), 0)),
        ],
        out_specs=pl.BlockSpec((tm, d), lambda i, c, ie, it, lo, hi, fi: (it[i], 0)),
        scratch_shapes=[pltpu.VMEM((tm, d), BF16)],
    )
    return pl.pallas_call(
        functools.partial(_expert_body, tm=tm),
        grid_spec=grid_spec,
        out_shape=jax.ShapeDtypeStruct((r, d), F32),
        compiler_params=_cparams(("arbitrary", "arbitrary")),
        name="moe_experts",
    )(*items, x_sorted, w_gate, w_up, w_down)


def _moe_plan(meta, counts_row, n_rows, tm):
    counts = counts_row[0, ROUTER_E0:ROUTER_E0 + N_EXPERTS].astype(jnp.int32)
    ends = jnp.cumsum(counts)
    starts = ends - counts
    eid = meta[:, 0:2].astype(jnp.int32)
    rank = meta[:, 2:4].astype(jnp.int32)
    dest = starts[eid] + rank

    first_tile = starts // tm
    last_tile = jnp.maximum(ends - 1, 0) // tm
    ntile_e = jnp.where(counts > 0, last_tile - first_tile + 1, 0)
    item_end = jnp.cumsum(ntile_e)
    item_start = item_end - ntile_e
    total = item_end[-1]
    n_items = n_rows // tm + N_EXPERTS - 1
    ii = jnp.arange(n_items, dtype=jnp.int32)
    valid = ii < total
    ii_c = jnp.minimum(ii, total - 1)
    e_i = jnp.searchsorted(item_end, ii_c, side='right').astype(jnp.int32)
    t_i = first_tile[e_i] + (ii_c - item_start[e_i])
    lo = jnp.where(valid, jnp.maximum(starts[e_i], t_i * tm), 0)
    hi = jnp.where(valid, jnp.minimum(ends[e_i], (t_i + 1) * tm), 0)
    prev_t = jnp.concatenate([jnp.full((1,), -1, jnp.int32), t_i[:-1]])
    first = (valid & (t_i != prev_t)).astype(jnp.int32)
    items = (e_i, t_i.astype(jnp.int32), lo.astype(jnp.int32), hi.astype(jnp.int32), first)
    return dest.astype(jnp.int32), items


def _hier_moe(h, g, wg1, bg1, wg2, bg2, w_gate, w_up, w_down, layer, tm=512):
    n, _ = h.shape
    xn, meta, counts_row = _router(h, g, wg1, bg1, wg2, bg2)
    dest, items = _moe_plan(meta, counts_row, 2 * n, tm)
    x_sorted = _dispatch(xn, _invert(dest))
    y_sorted = _experts(x_sorted, items, w_gate, w_up, w_down, layer, tm=tm)
    return _combine(h, meta, y_sorted, dest)


def _time_permute(h, batch, inverse=False):
    n, d = h.shape
    l = n // batch
    shape = (batch, l // S5_SEGMENTS, S5_SEGMENTS, d) if inverse else (batch, S5_SEGMENTS, l // S5_SEGMENTS, d)
    return h.reshape(shape).transpose(0, 2, 1, 3).reshape(n, d)


def kernel(x, norm_mix_g, norm_ffn_g, ssm_lam_re, ssm_lam_im, ssm_b_re, ssm_b_im, ssm_c_re, ssm_c_im, ssm_d, ssm_log_step, ssm_w_glu1, ssm_w_glu2, kv_norm_g, w_kv, k_norm_g, w_q, q_norm_g, lambda_qk, subln_g, w_o, rel_bias, router_group_w, router_group_b, router_expert_w, router_expert_b, w_gate, w_up, w_down):
    batch, l, d = x.shape
    n = batch * l
    h = _time_permute(x.astype(F32).reshape(n, d), batch)
    kn = vb = bias_tabs = None
    for layer in range(DEPTH):
        if layer < N_A_LAYERS:
            xn = _rmsnorm(h, norm_mix_g[layer], BF16)
            b_blk, c_blk, lam_v = _s5_params(ssm_lam_re[layer], ssm_lam_im[layer], ssm_b_re[layer],
                                             ssm_b_im[layer], ssm_c_re[layer], ssm_c_im[layer],
                                             ssm_log_step[layer])
            z = _s5_scan(xn, b_blk, c_blk, lam_v, ssm_d[layer], batch)
            h = _matmul(z, [ssm_w_glu1, ssm_w_glu2], layer, F32, res=h)
        else:
            j = layer - N_A_LAYERS
            lam_init = 0.8 - 0.6 * math.exp(-0.3 * layer)
            xn = _rmsnorm(h, norm_mix_g[layer], BF16)
            qn = _headnorm(_matmul(xn, [w_q], j, F32), q_norm_g[j], HEAD_DIM ** -0.5)
            o = _diff_attention(qn, kn, vb, bias_tabs, lambda_qk[j], subln_g[j], lam_init, batch)
            h = _matmul(o, [w_o], j, F32, res=h)
        h = _hier_moe(h, norm_ffn_g[layer], router_group_w[layer], router_group_b[layer],
                      router_expert_w[layer], router_expert_b[layer], w_gate, w_up, w_down, layer)
        if layer == N_A_LAYERS - 1:
            h = _time_permute(h, batch, inverse=True)
            xkv = _rmsnorm(h, kv_norm_g, BF16)
            w_kv3 = w_kv.reshape(1, d, 2 * d)
            kn = _headnorm(_matmul(xkv, [w_kv3], 0, F32, n_out=d), k_norm_g, 1.0)
            vb = _matmul(xkv, [w_kv3], 0, BF16, col_off=d // 512, n_out=d)
            bias_tabs = _attn_bias_tables(rel_bias, ATTN_TQ)
    return h.reshape(batch, l, d)
```

```python
import functools
import math

import jax
import jax.numpy as jnp
from jax import lax
from jax.experimental import pallas as pl
from jax.experimental.pallas import tpu as pltpu

F32 = jnp.float32
BF16 = jnp.bfloat16

DEPTH = 4
N_A_LAYERS = DEPTH // 2
SSM_GROUP = 16
SSM_STATE = 64
HEAD_DIM = 64
REL_BUCKETS = 32
REL_MAX_EXACT = REL_BUCKETS // 2
REL_MAX_DISTANCE = 128
N_EXPERT_GROUPS = 4
EXPERTS_PER_GROUP = 4
N_EXPERTS = N_EXPERT_GROUPS * EXPERTS_PER_GROUP
RMS_EPS = 1e-6

LANES = 128
SUBLANES = 8
VMEM_LIMIT = 48 * 1024 * 1024
NEG_BIG = -1e30

S5_SEGMENTS = SUBLANES
S5_GROUPS_PER_BLOCK = 16
S5_TJ = 32


def _cparams(sem):
    return pltpu.CompilerParams(dimension_semantics=sem, vmem_limit_bytes=VMEM_LIMIT)


def _rmsnorm_body(h_ref, g_ref, o_ref):
    x = h_ref[...]
    ms = jnp.mean(x * x, axis=-1, keepdims=True)
    o_ref[...] = (x * lax.rsqrt(ms + RMS_EPS) * g_ref[...]).astype(o_ref.dtype)


def _rmsnorm(h, g, out_dtype, tm=512):
    n, d = h.shape
    return pl.pallas_call(
        _rmsnorm_body,
        grid=(n // tm,),
        in_specs=[pl.BlockSpec((tm, d), lambda i: (i, 0)),
                  pl.BlockSpec((1, d), lambda i: (0, 0))],
        out_specs=pl.BlockSpec((tm, d), lambda i: (i, 0)),
        out_shape=jax.ShapeDtypeStruct((n, d), out_dtype),
        compiler_params=_cparams(("arbitrary",)),
        name="rmsnorm",
    )(h, g.reshape(1, d))


def _mm_body(*refs, n_w, has_res):
    a_ref = refs[0]
    w_refs = refs[1:1 + n_w]
    res_ref = refs[1 + n_w] if has_res else None
    o_ref = refs[1 + n_w + int(has_res)]
    wb_refs = refs[2 + n_w + int(has_res):]

    @pl.when(pl.program_id(1) == 0)
    def _():
        for w_ref, wb_ref in zip(w_refs, wb_refs):
            wb_ref[...] = w_ref[...].astype(BF16)

    a = a_ref[...]
    y = jnp.dot(a, wb_refs[0][...], preferred_element_type=F32)
    if n_w == 2:
        y2 = jnp.dot(a, wb_refs[1][...], preferred_element_type=F32)
        y = y * (1.0 / (1.0 + jnp.exp(-y2)))
    if has_res:
        y = y + res_ref[...]
    o_ref[...] = y.astype(o_ref.dtype)


def _matmul(a, ws, layer, out_dtype, res=None, col_off=0, n_out=None, tm=512, tn=512):
    m, k = a.shape
    n_out = ws[0].shape[-1] if n_out is None else n_out
    n_w = len(ws)
    in_specs = [pl.BlockSpec((tm, k), lambda j, i: (i, 0))]
    for _ in ws:
        in_specs.append(pl.BlockSpec((None, k, tn), lambda j, i: (layer, 0, j + col_off)))
    args = [a, *ws]
    if res is not None:
        in_specs.append(pl.BlockSpec((tm, tn), lambda j, i: (i, j)))
        args.append(res)
    return pl.pallas_call(
        functools.partial(_mm_body, n_w=n_w, has_res=res is not None),
        grid=(n_out // tn, m // tm),
        in_specs=in_specs,
        out_specs=pl.BlockSpec((tm, tn), lambda j, i: (i, j)),
        out_shape=jax.ShapeDtypeStruct((m, n_out), out_dtype),
        scratch_shapes=[pltpu.VMEM((k, tn), BF16) for _ in ws],
        compiler_params=_cparams(("arbitrary", "arbitrary")),
        name="dense_matmul",
    )(*args)


def _gelu_tanh(y):
    c = math.sqrt(2.0 / math.pi)
    return y * (0.5 * (1.0 + jnp.tanh(c * (y + 0.044715 * (y * y * y)))))


def _s5_body(x_ref, b_ref, c_ref, lam_ref, d_ref, z_ref, bu0_ref, bu1_ref, st0_ref, st1_ref,
             e_ref, init_ref, *, seg_len):
    ch = lam_ref.shape[-1]
    rows = S5_TJ * S5_SEGMENTS
    n_tiles = seg_len // S5_TJ
    lr = jnp.broadcast_to(lam_ref[0:1, :], (S5_SEGMENTS, ch))
    li = jnp.broadcast_to(lam_ref[1:2, :], (S5_SEGMENTS, ch))

    def row0(t):
        return pl.multiple_of(jnp.clip(t, 0, n_tiles - 1) * rows, rows)

    def bu_tile(t, bu_ref):
        bu_ref[...] = jnp.dot(x_ref[pl.ds(row0(t), rows), :], b_ref[...],
                              preferred_element_type=F32)

    def scan_tile(bu_ref, s_re, s_im, st_ref):
        for j in range(S5_TJ):
            sl = slice(S5_SEGMENTS * j, S5_SEGMENTS * (j + 1))
            n_re = lr * s_re - li * s_im + bu_ref[sl, 0:ch]
            n_im = lr * s_im + li * s_re + bu_ref[sl, ch:2 * ch]
            s_re, s_im = n_re, n_im
            if st_ref is not None:
                st_ref[sl, 0:ch] = s_re
                st_ref[sl, ch:2 * ch] = s_im
        return s_re, s_im

    def project(t, st_ref):
        r0 = row0(t)
        y = jnp.dot(st_ref[...].astype(BF16), c_ref[...], preferred_element_type=F32)
        y = y + d_ref[...] * x_ref[pl.ds(r0, rows), :].astype(F32)
        z_ref[pl.ds(r0, rows), :] = _gelu_tanh(y).astype(z_ref.dtype)

    def pass1(i2, c):
        t = 2 * i2
        bu_tile(t + 1, bu1_ref)
        c = scan_tile(bu0_ref, c[0], c[1], None)
        bu_tile(t + 2, bu0_ref)
        return scan_tile(bu1_ref, c[0], c[1], None)

    zeros = jnp.zeros((S5_SEGMENTS, ch), F32)
    bu_tile(0, bu0_ref)
    e_re, e_im = lax.fori_loop(0, n_tiles // 2, pass1, (zeros, zeros))
    e_ref[:, 0:ch] = e_re
    e_ref[:, ch:2 * ch] = e_im

    pr, pi = lam_ref[0:1, :], lam_ref[1:2, :]
    for _ in range(seg_len.bit_length() - 1):
        pr, pi = pr * pr - pi * pi, 2.0 * pr * pi
    cr = jnp.zeros((1, ch), F32)
    ci = jnp.zeros((1, ch), F32)
    init_ref[0:1, :] = jnp.zeros((1, 2 * ch), F32)
    for k in range(S5_SEGMENTS - 1):
        er = e_ref[k:k + 1, 0:ch]
        ei = e_ref[k:k + 1, ch:2 * ch]
        cr, ci = pr * cr - pi * ci + er, pr * ci + pi * cr + ei
        init_ref[k + 1:k + 2, 0:ch] = cr
        init_ref[k + 1:k + 2, ch:2 * ch] = ci

    def pass2(i2, c):
        t = 2 * i2
        bu_tile(t + 1, bu1_ref)
        c = scan_tile(bu0_ref, c[0], c[1], st0_ref)
        project(t - 1, st1_ref)
        bu_tile(t + 2, bu0_ref)
        c = scan_tile(bu1_ref, c[0], c[1], st1_ref)
        project(t, st0_ref)
        return c

    st1_ref[...] = jnp.zeros(st1_ref.shape, F32)
    bu_tile(0, bu0_ref)
    lax.fori_loop(0, n_tiles // 2, pass2, (init_ref[:, 0:ch], init_ref[:, ch:2 * ch]))
    project(n_tiles - 1, st1_ref)


def _s5_params(lam_re, lam_im, b_re, b_im, c_re, c_im, log_step):
    g, p = lam_re.shape
    gb = S5_GROUPS_PER_BLOCK
    nb = g // gb
    lam = lax.complex(lam_re.astype(F32), lam_im.astype(F32))
    delta = jnp.exp(log_step.astype(F32))[:, None]
    lam_bar = jnp.exp(lam * delta)
    b_bar = ((lam_bar - 1.0) / lam)[..., None] * lax.complex(b_re.astype(F32), b_im.astype(F32))
    eye = jnp.eye(gb, dtype=F32)

    def blk_b(part):
        part = part.reshape(nb, gb, p, SSM_GROUP)
        return jnp.einsum('cgph,gk->cghkp', part, eye).reshape(nb, gb * SSM_GROUP, gb * p)

    def blk_c(part):
        part = part.reshape(nb, gb, SSM_GROUP, p)
        return jnp.einsum('cghp,gk->cgpkh', part, eye).reshape(nb, gb * p, gb * SSM_GROUP)

    b_blk = jnp.concatenate([blk_b(jnp.real(b_bar)), blk_b(jnp.imag(b_bar))], axis=-1)
    c_blk = jnp.concatenate([blk_c(c_re.astype(F32)), blk_c(-c_im.astype(F32))], axis=1)
    lam_v = jnp.stack([jnp.real(lam_bar).reshape(nb, gb * p),
                       jnp.imag(lam_bar).reshape(nb, gb * p)], axis=1)
    return b_blk.astype(BF16), c_blk.astype(BF16), lam_v


def _s5_scan(xn, b_blk, c_blk, lam_v, d_skip, batch):
    n, d = xn.shape
    l = n // batch
    seg_len = l // S5_SEGMENTS
    nb, fb, ch2 = b_blk.shape
    rows = S5_TJ * S5_SEGMENTS
    return pl.pallas_call(
        functools.partial(_s5_body, seg_len=seg_len),
        grid=(batch, nb),
        in_specs=[pl.BlockSpec((l, fb), lambda b, c: (b, c)),
                  pl.BlockSpec((None, fb, ch2), lambda b, c: (c, 0, 0)),
                  pl.BlockSpec((None, ch2, fb), lambda b, c: (c, 0, 0)),
                  pl.BlockSpec((None, 2, ch2 // 2), lambda b, c: (c, 0, 0)),
                  pl.BlockSpec((1, fb), lambda b, c: (0, c))],
        out_specs=pl.BlockSpec((l, fb), lambda b, c: (b, c)),
        out_shape=jax.ShapeDtypeStruct((n, d), BF16),
        scratch_shapes=[pltpu.VMEM((rows, ch2), F32), pltpu.VMEM((rows, ch2), F32),
                        pltpu.VMEM((rows, ch2), F32), pltpu.VMEM((rows, ch2), F32),
                        pltpu.VMEM((S5_SEGMENTS, ch2), F32), pltpu.VMEM((S5_SEGMENTS, ch2), F32)],
        compiler_params=_cparams(("arbitrary", "arbitrary")),
        name="s5_scan",
    )(xn, b_blk, c_blk, lam_v, d_skip.reshape(1, d).astype(F32))


def _headnorm_body(x_ref, g_ref, o_ref, *, scale):
    tm, d = x_ref.shape
    lane = lax.broadcasted_iota(jnp.int32, (tm, LANES), 1)
    lo = lane < HEAD_DIM
    g = g_ref[...]
    for hb in range(d // LANES):
        x = x_ref[:, hb * LANES:(hb + 1) * LANES]
        x2 = x * x
        s_lo = jnp.sum(jnp.where(lo, x2, 0.0), axis=-1, keepdims=True)
        s_hi = jnp.sum(jnp.where(lo, 0.0, x2), axis=-1, keepdims=True)
        r = jnp.where(lo, lax.rsqrt(s_lo * (1.0 / HEAD_DIM) + RMS_EPS),
                      lax.rsqrt(s_hi * (1.0 / HEAD_DIM) + RMS_EPS))
        o_ref[:, hb * LANES:(hb + 1) * LANES] = ((x * r * g) * scale).astype(o_ref.dtype)


def _headnorm(x, g, scale, tm=512):
    n, d = x.shape
    g2 = jnp.concatenate([g, g]).reshape(1, LANES).astype(F32)
    return pl.pallas_call(
        functools.partial(_headnorm_body, scale=scale),
        grid=(n // tm,),
        in_specs=[pl.BlockSpec((tm, d), lambda i: (i, 0)),
                  pl.BlockSpec((1, LANES), lambda i: (0, 0))],
        out_specs=pl.BlockSpec((tm, d), lambda i: (i, 0)),
        out_shape=jax.ShapeDtypeStruct((n, d), BF16),
        compiler_params=_cparams(("arbitrary",)),
        name="headnorm",
    )(x, g2)


def _t5_bucket(n):
    n_safe = jnp.maximum(n, 1).astype(F32)
    large = REL_MAX_EXACT + (jnp.log(n_safe / REL_MAX_EXACT)
                             / math.log(REL_MAX_DISTANCE / REL_MAX_EXACT)
                             * (REL_BUCKETS - REL_MAX_EXACT)).astype(jnp.int32)
    large = jnp.minimum(large, REL_BUCKETS - 1)
    return jnp.where(n < REL_MAX_EXACT, n, large)


ATTN_TQ = 256
BIAS_FAR, BIAS_PREV, BIAS_DIAG, BIAS_NULL = 0, 1, 2, 3
M_INIT = -1e29


def _attn_bias_tables(rel_bias, tq):
    qi = jnp.arange(tq, dtype=jnp.int32)[:, None]
    ki = jnp.arange(tq, dtype=jnp.int32)[None, :]
    d_diag = qi - ki
    rb = rel_bias.astype(F32)
    nh = rb.shape[1]

    def lookup(dist):
        onehot = jax.nn.one_hot(_t5_bucket(dist), REL_BUCKETS, dtype=F32)
        return jnp.einsum('qkb,bh->hqk', onehot, rb, precision=lax.Precision.HIGHEST)

    b_diag = jnp.where((d_diag >= 0)[None], lookup(jnp.maximum(d_diag, 0)), NEG_BIG)
    b_prev = lookup(d_diag + tq)
    b_far = jnp.broadcast_to(rb[REL_BUCKETS - 1][:, None, None], (nh, tq, tq))
    b_null = jnp.full((nh, tq, tq), NEG_BIG, F32)
    return jnp.stack([b_far, b_prev, b_diag, b_null], axis=1)


def _attn_pair_tables(nq):
    q_row, slot, kt, bidx = [], [], [], []

    def idle():
        q_row.append(0)
        slot.append(nq)
        kt.append(0)
        bidx.append(BIAS_NULL)

    idle()
    idle()
    for i in range(nq):
        for j in range(i + 1):
            q_row.append(i)
            slot.append(i)
            kt.append(j)
            bidx.append(BIAS_DIAG if j == i else (BIAS_PREV if j == i - 1 else BIAS_FAR))
    idle()
    idle()
    if len(q_row) % 2:
        idle()
    return [jnp.asarray(t, jnp.int32) for t in (q_row, slot, kt, bidx)]


def _attn_body(qrow_ref, slot_ref, kt_ref, bidx_ref, q_ref, k_ref, v_ref, bias_ref, lq_ref, sg_ref,
               o_ref, m_ref, acc_ref, vaug_ref, s0_ref, s1_ref, p0_ref, p1_ref, a0_ref, a1_ref,
               *, tq, nq, n_iter, lam_init):
    rc = 64
    m_ref[...] = jnp.full(m_ref.shape, M_INIT, F32)
    acc_ref[...] = jnp.zeros(acc_ref.shape, F32)
    s0_ref[...] = jnp.zeros(s0_ref.shape, F32)
    s1_ref[...] = jnp.zeros(s1_ref.shape, F32)
    p0_ref[...] = jnp.zeros(p0_ref.shape, BF16)
    a0_ref[...] = jnp.ones(a0_ref.shape, F32)
    vaug_ref[:, 0:LANES] = v_ref[...]
    vaug_ref[:, LANES:2 * LANES] = jnp.ones(v_ref.shape, BF16)
    lane = lax.broadcasted_iota(jnp.int32, (tq, LANES), 1)

    def scores(e, s_ref):
        q = q_ref[pl.ds(pl.multiple_of(qrow_ref[e] * tq, tq), tq), :]
        zero = jnp.zeros_like(q)
        qq = jnp.concatenate([jnp.where(lane < HEAD_DIM, q, zero),
                              jnp.where(lane < HEAD_DIM, zero, q)], axis=0)
        k = k_ref[pl.ds(pl.multiple_of(kt_ref[e] * tq, tq), tq), :]
        s_ref[...] = lax.dot_general(qq, k, (((1,), (1,)), ((), ())), preferred_element_type=F32)

    def softmax(e, s_ref, a_ref, p_ref):
        m_view = m_ref.at[slot_ref[e]]
        b_view = bias_ref.at[bidx_ref[e]]
        for c in range(tq // rc):
            b = b_view[c * rc:(c + 1) * rc, :]
            for half in range(2):
                rows = slice(half * tq + c * rc, half * tq + (c + 1) * rc)
                s = s_ref[rows, :] + b
                m_prev = m_view[rows, :]
                m_next = jnp.maximum(m_prev, jnp.max(s, axis=1, keepdims=True))
                a_ref[rows, :] = jnp.exp(m_prev - m_next)
                p_ref[rows, :] = jnp.exp(s - m_next[:, 0:1]).astype(BF16)
                m_view[rows, :] = m_next

    def values(e, a_ref, p_ref):
        acc = acc_ref.at[slot_ref[e]]
        v = vaug_ref[pl.ds(pl.multiple_of(kt_ref[e] * tq, tq), tq), :]
        pv = jnp.dot(p_ref[...], v, preferred_element_type=F32)
        a = a_ref[...]
        acc[:, 0:LANES] = acc[:, 0:LANES] * a + pv[:, 0:LANES]
        acc[:, LANES:2 * LANES] = acc[:, LANES:2 * LANES] * a + pv[:, LANES:2 * LANES]

    def two_iterations(i2, c):
        it = 2 * i2
        scores(it + 2, s0_ref)
        softmax(it + 1, s1_ref, a1_ref, p1_ref)
        values(it, a0_ref, p0_ref)
        scores(it + 3, s1_ref)
        softmax(it + 2, s0_ref, a0_ref, p0_ref)
        values(it + 1, a1_ref, p1_ref)
        return c

    lax.fori_loop(0, n_iter // 2, two_iterations, 0)

    lq = lq_ref[...]
    lam = (jnp.exp(jnp.sum(lq[0:1] * lq[1:2], axis=1, keepdims=True))
           - jnp.exp(jnp.sum(lq[2:3] * lq[3:4], axis=1, keepdims=True)) + lam_init)

    def finalize(i, c):
        acc = acc_ref[i]
        o_all = acc[:, 0:LANES] / acc[:, LANES:2 * LANES]
        o = o_all[0:tq] - lam * o_all[tq:2 * tq]
        ms = jnp.mean(o * o, axis=-1, keepdims=True)
        o = (o * lax.rsqrt(ms + RMS_EPS) * sg_ref[...]) * (1.0 - lam_init)
        o_ref[pl.ds(pl.multiple_of(i * tq, tq), tq), :] = o.astype(o_ref.dtype)
        return c

    lax.fori_loop(0, nq, finalize, 0)


def _diff_attention(qn, kn, vb, bias_tabs, lambda_qk, subln_g, lam_init, batch):
    n, d = qn.shape
    l = n // batch
    tq = ATTN_TQ
    nq = l // tq
    nh = d // LANES
    tables = _attn_pair_tables(nq)
    n_iter = tables[0].shape[0] - 2
    row_blk = pl.BlockSpec((l, LANES), lambda b, h, *_: (b, h))
    grid_spec = pltpu.PrefetchScalarGridSpec(
        num_scalar_prefetch=4,
        grid=(batch, nh),
        in_specs=[row_blk, row_blk, row_blk,
                  pl.BlockSpec((None, 4, tq, tq), lambda b, h, *_: (h, 0, 0, 0)),
                  pl.BlockSpec((4, HEAD_DIM), lambda b, h, *_: (0, 0)),
                  pl.BlockSpec((1, LANES), lambda b, h, *_: (0, 0))],
        out_specs=row_blk,
        scratch_shapes=[pltpu.VMEM((nq + 1, 2 * tq, LANES), F32),
                        pltpu.VMEM((nq + 1, 2 * tq, 2 * LANES), F32),
                        pltpu.VMEM((l, 2 * LANES), BF16),
                        pltpu.VMEM((2 * tq, tq), F32), pltpu.VMEM((2 * tq, tq), F32),
                        pltpu.VMEM((2 * tq, tq), BF16), pltpu.VMEM((2 * tq, tq), BF16),
                        pltpu.VMEM((2 * tq, LANES), F32), pltpu.VMEM((2 * tq, LANES), F32)],
    )
    return pl.pallas_call(
        functools.partial(_attn_body, tq=tq, nq=nq, n_iter=n_iter, lam_init=lam_init),
        grid_spec=grid_spec,
        out_shape=jax.ShapeDtypeStruct((n, d), BF16),
        compiler_params=_cparams(("arbitrary", "arbitrary")),
        name="diff_attention",
    )(*tables, qn, kn, vb, bias_tabs, lambda_qk.astype(F32),
      subln_g.reshape(1, LANES).astype(F32))


ROUTER_E0 = N_EXPERT_GROUPS
META_COLS = 8


def _router_body(h_ref, g_ref, wr_ref, br_ref, xn_ref, meta_ref, cnt_ref, base_ref):
    tm = h_ref.shape[0]

    @pl.when(pl.program_id(0) == 0)
    def _():
        base_ref[...] = jnp.zeros(base_ref.shape, F32)

    x = h_ref[...]
    ms = jnp.mean(x * x, axis=-1, keepdims=True)
    xn = x * lax.rsqrt(ms + RMS_EPS) * g_ref[...]
    xn_ref[...] = xn.astype(xn_ref.dtype)
    logits = jnp.dot(xn, wr_ref[...], precision=lax.Precision.HIGHEST,
                     preferred_element_type=F32) + br_ref[...]
    lane = lax.broadcasted_iota(jnp.int32, (tm, LANES), 1)
    neg_inf = -jnp.inf

    def first_argmax(vals):
        vmax = jnp.max(vals, axis=-1, keepdims=True)
        idx = jnp.min(jnp.where(vals == vmax, lane, LANES), axis=-1, keepdims=True)
        return vmax, idx

    is_g = lane < N_EXPERT_GROUPS
    gmax, g_idx = first_argmax(jnp.where(is_g, logits, neg_inf))
    g_w = 1.0 / jnp.sum(jnp.where(is_g, jnp.exp(logits - gmax), 0.0), axis=-1, keepdims=True)
    e_lo = ROUTER_E0 + EXPERTS_PER_GROUP * g_idx
    elog = jnp.where((lane >= e_lo) & (lane < e_lo + EXPERTS_PER_GROUP), logits, neg_inf)
    v0, i0 = first_argmax(elog)
    v1, i1 = first_argmax(jnp.where(lane == i0, neg_inf, elog))
    t = jnp.exp(v1 - v0)
    w0 = g_w / (1.0 + t)
    w1 = g_w * t / (1.0 + t)

    sel0 = lane == i0
    sel1 = lane == i1
    onehot = jnp.where(sel0 | sel1, 1.0, 0.0)
    r_i = lax.broadcasted_iota(jnp.int32, (tm, tm), 0)
    c_i = lax.broadcasted_iota(jnp.int32, (tm, tm), 1)
    tri = jnp.where(c_i < r_i, 1.0, 0.0).astype(BF16)
    before = jnp.dot(tri, onehot.astype(BF16), preferred_element_type=F32) + base_ref[...]
    rank0 = jnp.sum(jnp.where(sel0, before, 0.0), axis=-1, keepdims=True)
    rank1 = jnp.sum(jnp.where(sel1, before, 0.0), axis=-1, keepdims=True)
    base_ref[...] = base_ref[...] + jnp.sum(onehot, axis=0, keepdims=True)
    cnt_ref[...] = base_ref[...]

    eid0 = (i0 - ROUTER_E0).astype(F32)
    eid1 = (i1 - ROUTER_E0).astype(F32)
    meta = jnp.where(lane == 0, eid0, jnp.where(lane == 1, eid1, jnp.where(
        lane == 2, rank0, jnp.where(lane == 3, rank1, jnp.where(
            lane == 4, w0, jnp.where(lane == 5, w1, 0.0))))))
    meta_ref[...] = meta[:, 0:META_COLS]


def _router(h, g, wg1, bg1, wg2, bg2, tm=256):
    n, d = h.shape
    wr = jnp.concatenate([wg1.astype(F32),
                          wg2.astype(F32).transpose(1, 0, 2).reshape(d, N_EXPERTS)], axis=1)
    wr = jnp.pad(wr, ((0, 0), (0, LANES - wr.shape[1])))
    br = jnp.pad(jnp.concatenate([bg1.astype(F32), bg2.astype(F32).reshape(-1)]),
                 (0, LANES - N_EXPERT_GROUPS - N_EXPERTS)).reshape(1, LANES)
    return pl.pallas_call(
        _router_body,
        grid=(n // tm,),
        in_specs=[pl.BlockSpec((tm, d), lambda i: (i, 0)),
                  pl.BlockSpec((1, d), lambda i: (0, 0)),
                  pl.BlockSpec((d, LANES), lambda i: (0, 0)),
                  pl.BlockSpec((1, LANES), lambda i: (0, 0))],
        out_specs=[pl.BlockSpec((tm, d), lambda i: (i, 0)),
                   pl.BlockSpec((tm, META_COLS), lambda i: (i, 0)),
                   pl.BlockSpec((1, LANES), lambda i: (0, 0))],
        out_shape=[jax.ShapeDtypeStruct((n, d), F32),
                   jax.ShapeDtypeStruct((n, META_COLS), F32),
                   jax.ShapeDtypeStruct((1, LANES), F32)],
        scratch_shapes=[pltpu.VMEM((1, LANES), F32)],
        compiler_params=_cparams(("arbitrary",)),
        name="moe_router",
    )(h, g.reshape(1, d).astype(F32), wr, br)


def _invert_body(dest_ref, rt_ref, *, ts):
    t = pl.program_id(0)

    def body(r, c):
        tok = t * ts + r
        rt_ref[dest_ref[0, 2 * r]] = tok
        rt_ref[dest_ref[0, 2 * r + 1]] = tok
        return c

    lax.fori_loop(0, ts, body, 0)


def _invert(dest, ts=512):
    n = dest.shape[0]
    dest3 = dest.reshape(n // ts, 1, 2 * ts)
    return pl.pallas_call(
        functools.partial(_invert_body, ts=ts),
        grid=(n // ts,),
        in_specs=[pl.BlockSpec((None, 1, 2 * ts), lambda t: (t, 0, 0), memory_space=pltpu.SMEM)],
        out_specs=pl.BlockSpec(memory_space=pltpu.SMEM),
        out_shape=jax.ShapeDtypeStruct((2 * n,), jnp.int32),
        compiler_params=_cparams(("arbitrary",)),
        name="moe_invert",
    )(dest3)


def _combine_body(dest_ref, h_ref, w_ref, y_hbm, o_ref, y0_ref, y1_ref, sem, *, ts):
    def issue(r, c):
        pltpu.make_async_copy(y_hbm.at[pl.ds(dest_ref[0, 2 * r], 1), :],
                              y0_ref.at[pl.ds(r, 1), :], sem.at[0]).start()
        pltpu.make_async_copy(y_hbm.at[pl.ds(dest_ref[0, 2 * r + 1], 1), :],
                              y1_ref.at[pl.ds(r, 1), :], sem.at[1]).start()
        return c

    lax.fori_loop(0, ts, issue, 0, unroll=8)
    pltpu.make_async_copy(y_hbm.at[pl.ds(0, ts), :], y0_ref, sem.at[0]).wait()
    pltpu.make_async_copy(y_hbm.at[pl.ds(0, ts), :], y1_ref, sem.at[1]).wait()
    w = w_ref[...]
    o_ref[...] = h_ref[...] + w[:, 4:5] * y0_ref[...] + w[:, 5:6] * y1_ref[...]


def _combine(h, meta, y_sorted, dest, ts=256):
    n, d = h.shape
    dest3 = dest.reshape(n // ts, 1, 2 * ts)
    return pl.pallas_call(
        functools.partial(_combine_body, ts=ts),
        grid=(n // ts,),
        in_specs=[pl.BlockSpec((None, 1, 2 * ts), lambda t: (t, 0, 0), memory_space=pltpu.SMEM),
                  pl.BlockSpec((ts, d), lambda t: (t, 0)),
                  pl.BlockSpec((ts, META_COLS), lambda t: (t, 0)),
                  pl.BlockSpec(memory_space=pl.ANY)],
        out_specs=pl.BlockSpec((ts, d), lambda t: (t, 0)),
        out_shape=jax.ShapeDtypeStruct((n, d), F32),
        scratch_shapes=[pltpu.VMEM((ts, d), F32), pltpu.VMEM((ts, d), F32),
                        pltpu.SemaphoreType.DMA((2,))],
        compiler_params=_cparams(("arbitrary",)),
        name="moe_combine",
    )(dest3, h, meta, y_sorted)


EXPERT_TM = 256
EXPERT_VMEM_LIMIT = 56 * 1024 * 1024


def _expert_body(ie_ref, it_ref, lo_ref, hi_ref, first_ref, rfirst_ref, nxt_ref,
                 rt0_ref, rtn_ref, xn_hbm, wg_hbm, wu_hbm, wd_hbm, o_ref,
                 xbuf, stg_g, stg_u, stg_d, wgb, wub, wdb, sem_x, sem_w, *, tm, layer, n_tiles):
    i = pl.program_id(0)
    e = ie_ref[i]
    t = it_ref[i]
    lo = lo_ref[i]
    hi = hi_ref[i]
    slot = t % 2

    def weight_copies(ex):
        return (pltpu.make_async_copy(wg_hbm.at[layer, ex], stg_g, sem_w.at[0]),
                pltpu.make_async_copy(wu_hbm.at[layer, ex], stg_u, sem_w.at[1]),
                pltpu.make_async_copy(wd_hbm.at[layer, ex], stg_d, sem_w.at[2]))

    def gather_start(rt_ref, dst_slot):
        def issue(r, c):
            pltpu.make_async_copy(xn_hbm.at[pl.ds(rt_ref[0, r], 1), :],
                                  xbuf.at[dst_slot, pl.ds(r, 1), :], sem_x.at[dst_slot]).start()
            return c

        lax.fori_loop(0, tm, issue, 0, unroll=8)

    @pl.when(i == 0)
    def _():
        for cp in weight_copies(e):
            cp.start()
        gather_start(rt0_ref, 0)

    @pl.when(rfirst_ref[i] == 1)
    def _():
        for cp in weight_copies(e):
            cp.wait()
        rows = 256

        def cast_rows(r, c):
            r0 = pl.multiple_of(r * rows, rows)
            wgb[pl.ds(r0, rows), :] = stg_g[pl.ds(r0, rows), :].astype(BF16)
            wub[pl.ds(r0, rows), :] = stg_u[pl.ds(r0, rows), :].astype(BF16)
            return c

        lax.fori_loop(0, stg_g.shape[0] // rows, cast_rows, 0)

        def cast_rows_d(r, c):
            r0 = pl.multiple_of(r * rows, rows)
            wdb[pl.ds(r0, rows), :] = stg_d[pl.ds(r0, rows), :].astype(BF16)
            return c

        lax.fori_loop(0, stg_d.shape[0] // rows, cast_rows_d, 0)

        @pl.when(nxt_ref[i] >= 0)
        def _():
            for cp in weight_copies(nxt_ref[i]):
                cp.start()

    @pl.when(first_ref[i] == 1)
    def _():
        @pl.when(t + 1 < n_tiles)
        def _():
            gather_start(rtn_ref, 1 - slot)

        pltpu.make_async_copy(xn_hbm.at[pl.ds(0, tm), :], xbuf.at[slot], sem_x.at[slot]).wait()

    @pl.when(hi > lo)
    def _():
        xb = xbuf[slot].astype(BF16)
        g = jnp.dot(xb, wgb[...], preferred_element_type=F32)
        u = jnp.dot(xb, wub[...], preferred_element_type=F32)
        hdn = (g * (1.0 / (1.0 + jnp.exp(-g)))) * u
        row = t * tm + lax.broadcasted_iota(jnp.int32, (tm, 1), 0)
        hdn = jnp.where((row >= lo) & (row < hi), hdn, 0.0)
        y = jnp.dot(hdn.astype(BF16), wdb[...], preferred_element_type=F32)

        @pl.when(first_ref[i] == 1)
        def _():
            o_ref[...] = y

        @pl.when(first_ref[i] != 1)
        def _():
            o_ref[...] = o_ref[...] + y


def _experts(xn, row_token, items, w_gate, w_up, w_down, layer, tm):
    _, d = xn.shape
    dh = w_gate.shape[-1]
    r_total = row_token.shape[0]
    n_tiles = r_total // tm
    n_items = items[0].shape[0]
    rt3 = row_token.reshape(n_tiles, 1, tm)
    grid_spec = pltpu.PrefetchScalarGridSpec(
        num_scalar_prefetch=7,
        grid=(n_items,),
        in_specs=[
            pl.BlockSpec((None, 1, tm), lambda i, *_: (0, 0, 0), memory_space=pltpu.SMEM),
            pl.BlockSpec((None, 1, tm), lambda i, ie, it, *_: (jnp.minimum(it[i] + 1, n_tiles - 1), 0, 0),
                         memory_space=pltpu.SMEM),
            pl.BlockSpec(memory_space=pl.ANY), pl.BlockSpec(memory_space=pl.ANY),
            pl.BlockSpec(memory_space=pl.ANY), pl.BlockSpec(memory_space=pl.ANY),
        ],
        out_specs=pl.BlockSpec((tm, d), lambda i, ie, it, *_: (it[i], 0)),
        scratch_shapes=[pltpu.VMEM((2, tm, d), F32),
                        pltpu.VMEM((d, dh), F32), pltpu.VMEM((d, dh), F32), pltpu.VMEM((dh, d), F32),
                        pltpu.VMEM((d, dh), BF16), pltpu.VMEM((d, dh), BF16), pltpu.VMEM((dh, d), BF16),
                        pltpu.SemaphoreType.DMA((2,)), pltpu.SemaphoreType.DMA((3,))],
    )
    return pl.pallas_call(
        functools.partial(_expert_body, tm=tm, layer=layer, n_tiles=n_tiles),
        grid_spec=grid_spec,
        out_shape=jax.ShapeDtypeStruct((r_total, d), F32),
        compiler_params=pltpu.CompilerParams(dimension_semantics=("arbitrary",),
                                             vmem_limit_bytes=EXPERT_VMEM_LIMIT),
        name="moe_experts",
    )(*items, rt3, rt3, xn, w_gate, w_up, w_down)


def _moe_plan(meta, counts_row, n_rows, tm):
    counts = counts_row[0, ROUTER_E0:ROUTER_E0 + N_EXPERTS].astype(jnp.int32)
    ends = jnp.cumsum(counts)
    starts = ends - counts
    eid = meta[:, 0:2].astype(jnp.int32)
    rank = meta[:, 2:4].astype(jnp.int32)
    dest = starts[eid] + rank

    first_tile = starts // tm
    last_tile = jnp.maximum(ends - 1, 0) // tm
    ntile_e = jnp.where(counts > 0, last_tile - first_tile + 1, 0)
    item_end = jnp.cumsum(ntile_e)
    item_start = item_end - ntile_e
    total = item_end[-1]
    n_items = n_rows // tm + N_EXPERTS - 1
    ii = jnp.arange(n_items, dtype=jnp.int32)
    valid = ii < total
    ii_c = jnp.minimum(ii, total - 1)
    e_i = jnp.searchsorted(item_end, ii_c, side='right').astype(jnp.int32)
    t_i = (first_tile[e_i] + (ii_c - item_start[e_i])).astype(jnp.int32)
    lo = jnp.where(valid, jnp.maximum(starts[e_i], t_i * tm), 0).astype(jnp.int32)
    hi = jnp.where(valid, jnp.minimum(ends[e_i], (t_i + 1) * tm), 0).astype(jnp.int32)
    minus1 = jnp.full((1,), -1, jnp.int32)
    first = (valid & (t_i != jnp.concatenate([minus1, t_i[:-1]]))).astype(jnp.int32)
    run_first = (valid & (e_i != jnp.concatenate([minus1, e_i[:-1]]))).astype(jnp.int32)
    ar = jnp.arange(N_EXPERTS, dtype=jnp.int32)
    later = (ar[None, :] > ar[:, None]) & (counts > 0)[None, :]
    nxt_e = jnp.min(jnp.where(later, ar[None, :], N_EXPERTS), axis=1)
    nxt_e = jnp.where(nxt_e == N_EXPERTS, -1, nxt_e).astype(jnp.int32)
    items = (e_i, t_i, lo, hi, first, run_first, nxt_e[e_i])
    return dest.astype(jnp.int32), items


def _hier_moe(h, g, wg1, bg1, wg2, bg2, w_gate, w_up, w_down, layer, tm=EXPERT_TM):
    n, _ = h.shape
    xn, meta, counts_row = _router(h, g, wg1, bg1, wg2, bg2)
    dest, items = _moe_plan(meta, counts_row, 2 * n, tm)
    y_sorted = _experts(xn, _invert(dest), items, w_gate, w_up, w_down, layer, tm)
    return _combine(h, meta, y_sorted, dest)


def _time_permute(h, batch, inverse=False):
    n, d = h.shape
    l = n // batch
    shape = (batch, l // S5_SEGMENTS, S5_SEGMENTS, d) if inverse else (batch, S5_SEGMENTS, l // S5_SEGMENTS, d)
    return h.reshape(shape).transpose(0, 2, 1, 3).reshape(n, d)


def kernel(x, norm_mix_g, norm_ffn_g, ssm_lam_re, ssm_lam_im, ssm_b_re, ssm_b_im, ssm_c_re, ssm_c_im, ssm_d, ssm_log_step, ssm_w_glu1, ssm_w_glu2, kv_norm_g, w_kv, k_norm_g, w_q, q_norm_g, lambda_qk, subln_g, w_o, rel_bias, router_group_w, router_group_b, router_expert_w, router_expert_b, w_gate, w_up, w_down):
    batch, l, d = x.shape
    n = batch * l
    h = _time_permute(x.astype(F32).reshape(n, d), batch)
    kn = vb = bias_tabs = None
    for layer in range(DEPTH):
        if layer < N_A_LAYERS:
            xn = _rmsnorm(h, norm_mix_g[layer], BF16)
            b_blk, c_blk, lam_v = _s5_params(ssm_lam_re[layer], ssm_lam_im[layer], ssm_b_re[layer],
                                             ssm_b_im[layer], ssm_c_re[layer], ssm_c_im[layer],
                                             ssm_log_step[layer])
            z = _s5_scan(xn, b_blk, c_blk, lam_v, ssm_d[layer], batch)
            h = _matmul(z, [ssm_w_glu1, ssm_w_glu2], layer, F32, res=h)
        else:
            j = layer - N_A_LAYERS
            lam_init = 0.8 - 0.6 * math.exp(-0.3 * layer)
            xn = _rmsnorm(h, norm_mix_g[layer], BF16)
            qn = _headnorm(_matmul(xn, [w_q], j, F32), q_norm_g[j], HEAD_DIM ** -0.5)
            o = _diff_attention(qn, kn, vb, bias_tabs, lambda_qk[j], subln_g[j], lam_init, batch)
            h = _matmul(o, [w_o], j, F32, res=h)
        h = _hier_moe(h, norm_ffn_g[layer], router_group_w[layer], router_group_b[layer],
                      router_expert_w[layer], router_expert_b[layer], w_gate, w_up, w_down, layer)
        if layer == N_A_LAYERS - 1:
            h = _time_permute(h, batch, inverse=True)
            xkv = _rmsnorm(h, kv_norm_g, BF16)
            w_kv3 = w_kv.reshape(1, d, 2 * d)
            kn = _headnorm(_matmul(xkv, [w_kv3], 0, F32, n_out=d), k_norm_g, 1.0)
            vb = _matmul(xkv, [w_kv3], 0, BF16, col_off=d // 512, n_out=d)
            bias_tabs = _attn_bias_tables(rel_bias, ATTN_TQ)
    return h.reshape(batch, l, d)
```

```python
import functools
import math

import jax
import jax.numpy as jnp
from jax import lax
from jax.experimental import pallas as pl
from jax.experimental.pallas import tpu as pltpu

F32 = jnp.float32
BF16 = jnp.bfloat16

DEPTH = 4
N_A_LAYERS = DEPTH // 2
SSM_GROUP = 16
SSM_STATE = 64
HEAD_DIM = 64
REL_BUCKETS = 32
REL_MAX_EXACT = REL_BUCKETS // 2
REL_MAX_DISTANCE = 128
N_EXPERT_GROUPS = 4
EXPERTS_PER_GROUP = 4
N_EXPERTS = N_EXPERT_GROUPS * EXPERTS_PER_GROUP
RMS_EPS = 1e-6

LANES = 128
SUBLANES = 8
VMEM_LIMIT = 48 * 1024 * 1024
NEG_BIG = -1e30

S5_SEGMENTS = SUBLANES
S5_GROUPS_PER_BLOCK = 16
S5_TJ = 32


def _cparams(sem):
    return pltpu.CompilerParams(dimension_semantics=sem, vmem_limit_bytes=VMEM_LIMIT)


def _rmsnorm_body(h_ref, g_ref, o_ref):
    x = h_ref[...]
    ms = jnp.mean(x * x, axis=-1, keepdims=True)
    o_ref[...] = (x * lax.rsqrt(ms + RMS_EPS) * g_ref[...]).astype(o_ref.dtype)


def _rmsnorm(h, g, out_dtype, tm=512):
    n, d = h.shape
    return pl.pallas_call(
        _rmsnorm_body,
        grid=(n // tm,),
        in_specs=[pl.BlockSpec((tm, d), lambda i: (i, 0)),
                  pl.BlockSpec((1, d), lambda i: (0, 0))],
        out_specs=pl.BlockSpec((tm, d), lambda i: (i, 0)),
        out_shape=jax.ShapeDtypeStruct((n, d), out_dtype),
        compiler_params=_cparams(("arbitrary",)),
        name="rmsnorm",
    )(h, g.reshape(1, d))


def _mm_body(*refs, n_w, has_res):
    a_ref = refs[0]
    w_refs = refs[1:1 + n_w]
    res_ref = refs[1 + n_w] if has_res else None
    o_ref = refs[1 + n_w + int(has_res)]
    wb_refs = refs[2 + n_w + int(has_res):]

    @pl.when(pl.program_id(1) == 0)
    def _():
        for w_ref, wb_ref in zip(w_refs, wb_refs):
            wb_ref[...] = w_ref[...].astype(BF16)

    a = a_ref[...]
    y = jnp.dot(a, wb_refs[0][...], preferred_element_type=F32)
    if n_w == 2:
        y2 = jnp.dot(a, wb_refs[1][...], preferred_element_type=F32)
        y = y * (1.0 / (1.0 + jnp.exp(-y2)))
    if has_res:
        y = y + res_ref[...]
    o_ref[...] = y.astype(o_ref.dtype)


def _matmul(a, ws, layer, out_dtype, res=None, col_off=0, n_out=None, tm=512, tn=512):
    m, k = a.shape
    n_out = ws[0].shape[-1] if n_out is None else n_out
    n_w = len(ws)
    in_specs = [pl.BlockSpec((tm, k), lambda j, i: (i, 0))]
    for _ in ws:
        in_specs.append(pl.BlockSpec((None, k, tn), lambda j, i: (layer, 0, j + col_off)))
    args = [a, *ws]
    if res is not None:
        in_specs.append(pl.BlockSpec((tm, tn), lambda j, i: (i, j)))
        args.append(res)
    return pl.pallas_call(
        functools.partial(_mm_body, n_w=n_w, has_res=res is not None),
        grid=(n_out // tn, m // tm),
        in_specs=in_specs,
        out_specs=pl.BlockSpec((tm, tn), lambda j, i: (i, j)),
        out_shape=jax.ShapeDtypeStruct((m, n_out), out_dtype),
        scratch_shapes=[pltpu.VMEM((k, tn), BF16) for _ in ws],
        compiler_params=_cparams(("arbitrary", "arbitrary")),
        name="dense_matmul",
    )(*args)


def _gelu_tanh(y):
    c = math.sqrt(2.0 / math.pi)
    return y * (0.5 * (1.0 + jnp.tanh(c * (y + 0.044715 * (y * y * y)))))


def _s5_body(x_ref, b_ref, c_ref, lam_ref, d_ref, z_ref, bu0_ref, bu1_ref, st0_ref, st1_ref,
             e_ref, init_ref, *, seg_len):
    ch = lam_ref.shape[-1]
    rows = S5_TJ * S5_SEGMENTS
    n_tiles = seg_len // S5_TJ
    lr = jnp.broadcast_to(lam_ref[0:1, :], (S5_SEGMENTS, ch))
    li = jnp.broadcast_to(lam_ref[1:2, :], (S5_SEGMENTS, ch))

    def row0(t):
        return pl.multiple_of(jnp.clip(t, 0, n_tiles - 1) * rows, rows)

    def bu_tile(t, bu_ref):
        bu_ref[...] = jnp.dot(x_ref[pl.ds(row0(t), rows), :], b_ref[...],
                              preferred_element_type=F32)

    def scan_tile(bu_ref, s_re, s_im, st_ref):
        for j in range(S5_TJ):
            sl = slice(S5_SEGMENTS * j, S5_SEGMENTS * (j + 1))
            n_re = lr * s_re - li * s_im + bu_ref[sl, 0:ch]
            n_im = lr * s_im + li * s_re + bu_ref[sl, ch:2 * ch]
            s_re, s_im = n_re, n_im
            if st_ref is not None:
                st_ref[sl, 0:ch] = s_re
                st_ref[sl, ch:2 * ch] = s_im
        return s_re, s_im

    def project(t, st_ref):
        r0 = row0(t)
        y = jnp.dot(st_ref[...].astype(BF16), c_ref[...], preferred_element_type=F32)
        y = y + d_ref[...] * x_ref[pl.ds(r0, rows), :].astype(F32)
        z_ref[pl.ds(r0, rows), :] = _gelu_tanh(y).astype(z_ref.dtype)

    def pass1(i2, c):
        t = 2 * i2
        bu_tile(t + 1, bu1_ref)
        c = scan_tile(bu0_ref, c[0], c[1], None)
        bu_tile(t + 2, bu0_ref)
        return scan_tile(bu1_ref, c[0], c[1], None)

    zeros = jnp.zeros((S5_SEGMENTS, ch), F32)
    bu_tile(0, bu0_ref)
    e_re, e_im = lax.fori_loop(0, n_tiles // 2, pass1, (zeros, zeros))
    e_ref[:, 0:ch] = e_re
    e_ref[:, ch:2 * ch] = e_im

    pr, pi = lam_ref[0:1, :], lam_ref[1:2, :]
    for _ in range(seg_len.bit_length() - 1):
        pr, pi = pr * pr - pi * pi, 2.0 * pr * pi
    cr = jnp.zeros((1, ch), F32)
    ci = jnp.zeros((1, ch), F32)
    init_ref[0:1, :] = jnp.zeros((1, 2 * ch), F32)
    for k in range(S5_SEGMENTS - 1):
        er = e_ref[k:k + 1, 0:ch]
        ei = e_ref[k:k + 1, ch:2 * ch]
        cr, ci = pr * cr - pi * ci + er, pr * ci + pi * cr + ei
        init_ref[k + 1:k + 2, 0:ch] = cr
        init_ref[k + 1:k + 2, ch:2 * ch] = ci

    def pass2(i2, c):
        t = 2 * i2
        bu_tile(t + 1, bu1_ref)
        c = scan_tile(bu0_ref, c[0], c[1], st0_ref)
        project(t - 1, st1_ref)
        bu_tile(t + 2, bu0_ref)
        c = scan_tile(bu1_ref, c[0], c[1], st1_ref)
        project(t, st0_ref)
        return c

    st1_ref[...] = jnp.zeros(st1_ref.shape, F32)
    bu_tile(0, bu0_ref)
    lax.fori_loop(0, n_tiles // 2, pass2, (init_ref[:, 0:ch], init_ref[:, ch:2 * ch]))
    project(n_tiles - 1, st1_ref)


def _s5_params(lam_re, lam_im, b_re, b_im, c_re, c_im, log_step):
    g, p = lam_re.shape
    gb = S5_GROUPS_PER_BLOCK
    nb = g // gb
    lam = lax.complex(lam_re.astype(F32), lam_im.astype(F32))
    delta = jnp.exp(log_step.astype(F32))[:, None]
    lam_bar = jnp.exp(lam * delta)
    b_bar = ((lam_bar - 1.0) / lam)[..., None] * lax.complex(b_re.astype(F32), b_im.astype(F32))
    eye = jnp.eye(gb, dtype=F32)

    def blk_b(part):
        part = part.reshape(nb, gb, p, SSM_GROUP)
        return jnp.einsum('cgph,gk->cghkp', part, eye).reshape(nb, gb * SSM_GROUP, gb * p)

    def blk_c(part):
        part = part.reshape(nb, gb, SSM_GROUP, p)
        return jnp.einsum('cghp,gk->cgpkh', part, eye).reshape(nb, gb * p, gb * SSM_GROUP)

    b_blk = jnp.concatenate([blk_b(jnp.real(b_bar)), blk_b(jnp.imag(b_bar))], axis=-1)
    c_blk = jnp.concatenate([blk_c(c_re.astype(F32)), blk_c(-c_im.astype(F32))], axis=1)
    lam_v = jnp.stack([jnp.real(lam_bar).reshape(nb, gb * p),
                       jnp.imag(lam_bar).reshape(nb, gb * p)], axis=1)
    return b_blk.astype(BF16), c_blk.astype(BF16), lam_v


def _s5_scan(xn, b_blk, c_blk, lam_v, d_skip, batch):
    n, d = xn.shape
    l = n // batch
    seg_len = l // S5_SEGMENTS
    nb, fb, ch2 = b_blk.shape
    rows = S5_TJ * S5_SEGMENTS
    return pl.pallas_call(
        functools.partial(_s5_body, seg_len=seg_len),
        grid=(batch, nb),
        in_specs=[pl.BlockSpec((l, fb), lambda b, c: (b, c)),
                  pl.BlockSpec((None, fb, ch2), lambda b, c: (c, 0, 0)),
                  pl.BlockSpec((None, ch2, fb), lambda b, c: (c, 0, 0)),
                  pl.BlockSpec((None, 2, ch2 // 2), lambda b, c: (c, 0, 0)),
                  pl.BlockSpec((1, fb), lambda b, c: (0, c))],
        out_specs=pl.BlockSpec((l, fb), lambda b, c: (b, c)),
        out_shape=jax.ShapeDtypeStruct((n, d), BF16),
        scratch_shapes=[pltpu.VMEM((rows, ch2), F32), pltpu.VMEM((rows, ch2), F32),
                        pltpu.VMEM((rows, ch2), F32), pltpu.VMEM((rows, ch2), F32),
                        pltpu.VMEM((S5_SEGMENTS, ch2), F32), pltpu.VMEM((S5_SEGMENTS, ch2), F32)],
        compiler_params=_cparams(("arbitrary", "arbitrary")),
        name="s5_scan",
    )(xn, b_blk, c_blk, lam_v, d_skip.reshape(1, d).astype(F32))


def _headnorm_body(x_ref, g_ref, o_ref, *, scale):
    tm, d = x_ref.shape
    lane = lax.broadcasted_iota(jnp.int32, (tm, LANES), 1)
    lo = lane < HEAD_DIM
    g = g_ref[...]
    for hb in range(d // LANES):
        x = x_ref[:, hb * LANES:(hb + 1) * LANES]
        x2 = x * x
        s_lo = jnp.sum(jnp.where(lo, x2, 0.0), axis=-1, keepdims=True)
        s_hi = jnp.sum(jnp.where(lo, 0.0, x2), axis=-1, keepdims=True)
        r = jnp.where(lo, lax.rsqrt(s_lo * (1.0 / HEAD_DIM) + RMS_EPS),
                      lax.rsqrt(s_hi * (1.0 / HEAD_DIM) + RMS_EPS))
        o_ref[:, hb * LANES:(hb + 1) * LANES] = ((x * r * g) * scale).astype(o_ref.dtype)


def _headnorm(x, g, scale, tm=512):
    n, d = x.shape
    g2 = jnp.concatenate([g, g]).reshape(1, LANES).astype(F32)
    return pl.pallas_call(
        functools.partial(_headnorm_body, scale=scale),
        grid=(n // tm,),
        in_specs=[pl.BlockSpec((tm, d), lambda i: (i, 0)),
                  pl.BlockSpec((1, LANES), lambda i: (0, 0))],
        out_specs=pl.BlockSpec((tm, d), lambda i: (i, 0)),
        out_shape=jax.ShapeDtypeStruct((n, d), BF16),
        compiler_params=_cparams(("arbitrary",)),
        name="headnorm",
    )(x, g2)


def _t5_bucket(n):
    n_safe = jnp.maximum(n, 1).astype(F32)
    large = REL_MAX_EXACT + (jnp.log(n_safe / REL_MAX_EXACT)
                             / math.log(REL_MAX_DISTANCE / REL_MAX_EXACT)
                             * (REL_BUCKETS - REL_MAX_EXACT)).astype(jnp.int32)
    large = jnp.minimum(large, REL_BUCKETS - 1)
    return jnp.where(n < REL_MAX_EXACT, n, large)


ATTN_TQ = 256
M_INIT = -1e29
LOG2E = 1.4426950408889634


def _attn_bias_tables(rel_bias, tq):
    qi = jnp.arange(tq, dtype=jnp.int32)[:, None]
    ki = jnp.arange(tq, dtype=jnp.int32)[None, :]
    d_diag = qi - ki
    rb = rel_bias.astype(F32)
    rb = (rb - rb[REL_BUCKETS - 1][None, :]) * LOG2E

    def lookup(dist):
        onehot = jax.nn.one_hot(_t5_bucket(dist), REL_BUCKETS, dtype=F32)
        return jnp.einsum('qkb,bh->hqk', onehot, rb, precision=lax.Precision.HIGHEST)

    b_diag = jnp.where((d_diag >= 0)[None], lookup(jnp.maximum(d_diag, 0)), NEG_BIG)
    b_prev = lookup(d_diag + tq)
    return jnp.stack([b_prev, b_diag], axis=1)


def _attn_pair_tables(nq):
    def framed(pairs):
        idle = (0, nq, 0, 0)
        rows = [idle, idle] + pairs + [idle, idle]
        if len(rows) % 2:
            rows.append(idle)
        return rows

    wide = [(i, i, 2 * j, 0) for i in range(nq) for j in range((i - 1) // 2)]
    odd = [(i, i, i - 2, 0) for i in range(2, nq) if (i - 1) % 2]
    near = [(i, i, j, 1 if j == i else 0) for i in range(nq) for j in range(max(i - 1, 0), i + 1)]
    lists = [framed(wide), framed(odd), framed(near)]
    cols = list(zip(*(lists[0] + lists[1] + lists[2])))
    return [jnp.asarray(c, jnp.int32) for c in cols], [len(x) for x in lists]


def _attn_body(qrow_ref, slot_ref, kt_ref, bidx_ref, q_ref, k_ref, v_ref, bias_ref, lq_ref, sg_ref,
               o_ref, m_ref, acc_ref, vaug_ref, s0_ref, s1_ref, p0_ref, p1_ref, a0_ref, a1_ref,
               sw0_ref, sw1_ref, pw0_ref, pw1_ref, *, tq, nq, n_rows, lam_init):
    m_ref[...] = jnp.full(m_ref.shape, M_INIT, F32)
    acc_ref[...] = jnp.zeros(acc_ref.shape, F32)
    vaug_ref[:, 0:LANES] = v_ref[...]
    vaug_ref[:, LANES:2 * LANES] = jnp.ones(v_ref.shape, BF16)
    lane = lax.broadcasted_iota(jnp.int32, (tq, LANES), 1)

    def scores(e, s_ref, tk):
        q = q_ref[pl.ds(pl.multiple_of(qrow_ref[e] * tq, tq), tq), :]
        zero = jnp.zeros_like(q)
        qq = jnp.concatenate([jnp.where(lane < HEAD_DIM, q, zero),
                              jnp.where(lane < HEAD_DIM, zero, q)], axis=0)
        k = k_ref[pl.ds(pl.multiple_of(kt_ref[e] * tq, tq), tk), :]
        s_ref[...] = lax.dot_general(qq, k, (((1,), (1,)), ((), ())), preferred_element_type=F32)

    def softmax(e, s_ref, a_ref, p_ref, with_bias, rc):
        m_view = m_ref.at[slot_ref[e]]
        b_view = bias_ref.at[bidx_ref[e]]
        for c in range(tq // rc):
            b = b_view[c * rc:(c + 1) * rc, :] if with_bias else None
            for half in range(2):
                rows = slice(half * tq + c * rc, half * tq + (c + 1) * rc)
                s = s_ref[rows, :]
                if with_bias:
                    s = s + b
                m_prev = m_view[rows, :]
                m_next = jnp.maximum(m_prev, jnp.max(s, axis=1, keepdims=True))
                a_ref[rows, :] = jnp.exp2(m_prev - m_next)
                p_ref[rows, :] = jnp.exp2(s - m_next[:, 0:1]).astype(BF16)
                m_view[rows, :] = m_next

    def values(e, a_ref, p_ref, tk):
        acc = acc_ref.at[slot_ref[e]]
        v = vaug_ref[pl.ds(pl.multiple_of(kt_ref[e] * tq, tq), tk), :]
        pv = jnp.dot(p_ref[...], v, preferred_element_type=F32)
        a = a_ref[...]
        acc[:, 0:LANES] = acc[:, 0:LANES] * a + pv[:, 0:LANES]
        acc[:, LANES:2 * LANES] = acc[:, LANES:2 * LANES] * a + pv[:, LANES:2 * LANES]

    def run_pipeline(base, rows_n, with_bias, tk, sa_ref, sb_ref, pa_ref, pb_ref):
        rc = 64 if tk == tq else 32
        pa_ref[...] = jnp.zeros(pa_ref.shape, BF16)
        a0_ref[...] = jnp.ones(a0_ref.shape, F32)
        scores(base + 1, sb_ref, tk)

        def two_iterations(i2, c):
            it = base + 2 * i2
            scores(it + 2, sa_ref, tk)
            softmax(it + 1, sb_ref, a1_ref, pb_ref, with_bias, rc)
            values(it, a0_ref, pa_ref, tk)
            scores(it + 3, sb_ref, tk)
            softmax(it + 2, sa_ref, a0_ref, pa_ref, with_bias, rc)
            values(it + 1, a1_ref, pb_ref, tk)
            return c

        lax.fori_loop(0, (rows_n - 2) // 2, two_iterations, 0)

    n_wide, n_odd, n_near = n_rows
    run_pipeline(0, n_wide, False, 2 * tq, sw0_ref, sw1_ref, pw0_ref, pw1_ref)
    run_pipeline(n_wide, n_odd, False, tq, s0_ref, s1_ref, p0_ref, p1_ref)
    run_pipeline(n_wide + n_odd, n_near, True, tq, s0_ref, s1_ref, p0_ref, p1_ref)

    lq = lq_ref[...]
    lam = (jnp.exp(jnp.sum(lq[0:1] * lq[1:2], axis=1, keepdims=True))
           - jnp.exp(jnp.sum(lq[2:3] * lq[3:4], axis=1, keepdims=True)) + lam_init)

    def finalize(i, c):
        acc = acc_ref[i]
        o_all = acc[:, 0:LANES] / acc[:, LANES:2 * LANES]
        o = o_all[0:tq] - lam * o_all[tq:2 * tq]
        ms = jnp.mean(o * o, axis=-1, keepdims=True)
        o = (o * lax.rsqrt(ms + RMS_EPS) * sg_ref[...]) * (1.0 - lam_init)
        o_ref[pl.ds(pl.multiple_of(i * tq, tq), tq), :] = o.astype(o_ref.dtype)
        return c

    lax.fori_loop(0, nq, finalize, 0)


def _diff_attention(qn, kn, vb, bias_tabs, lambda_qk, subln_g, lam_init, batch):
    n, d = qn.shape
    l = n // batch
    tq = ATTN_TQ
    nq = l // tq
    nh = d // LANES
    tables, n_rows = _attn_pair_tables(nq)
    row_blk = pl.BlockSpec((l, LANES), lambda b, h, *_: (b, h))
    grid_spec = pltpu.PrefetchScalarGridSpec(
        num_scalar_prefetch=4,
        grid=(batch, nh),
        in_specs=[row_blk, row_blk, row_blk,
                  pl.BlockSpec((None, 2, tq, tq), lambda b, h, *_: (h, 0, 0, 0)),
                  pl.BlockSpec((4, HEAD_DIM), lambda b, h, *_: (0, 0)),
                  pl.BlockSpec((1, LANES), lambda b, h, *_: (0, 0))],
        out_specs=row_blk,
        scratch_shapes=[pltpu.VMEM((nq + 1, 2 * tq, LANES), F32),
                        pltpu.VMEM((nq + 1, 2 * tq, 2 * LANES), F32),
                        pltpu.VMEM((l, 2 * LANES), BF16),
                        pltpu.VMEM((2 * tq, tq), F32), pltpu.VMEM((2 * tq, tq), F32),
                        pltpu.VMEM((2 * tq, tq), BF16), pltpu.VMEM((2 * tq, tq), BF16),
                        pltpu.VMEM((2 * tq, LANES), F32), pltpu.VMEM((2 * tq, LANES), F32),
                        pltpu.VMEM((2 * tq, 2 * tq), F32), pltpu.VMEM((2 * tq, 2 * tq), F32),
                        pltpu.VMEM((2 * tq, 2 * tq), BF16), pltpu.VMEM((2 * tq, 2 * tq), BF16)],
    )
    return pl.pallas_call(
        functools.partial(_attn_body, tq=tq, nq=nq, n_rows=tuple(n_rows), lam_init=lam_init),
        grid_spec=grid_spec,
        out_shape=jax.ShapeDtypeStruct((n, d), BF16),
        compiler_params=_cparams(("arbitrary", "arbitrary")),
        name="diff_attention",
    )(*tables, qn, kn, vb, bias_tabs, lambda_qk.astype(F32),
      subln_g.reshape(1, LANES).astype(F32))


ROUTER_E0 = N_EXPERT_GROUPS
META_COLS = 8


def _router_body(h_ref, g_ref, wr_ref, br_ref, xn_ref, meta_ref, cnt_ref, base_ref):
    tm = h_ref.shape[0]

    @pl.when(pl.program_id(0) == 0)
    def _():
        base_ref[...] = jnp.zeros(base_ref.shape, F32)

    x = h_ref[...]
    ms = jnp.mean(x * x, axis=-1, keepdims=True)
    xn = x * lax.rsqrt(ms + RMS_EPS) * g_ref[...]
    xn_ref[...] = xn.astype(xn_ref.dtype)
    logits = jnp.dot(xn, wr_ref[...], precision=lax.Precision.HIGHEST,
                     preferred_element_type=F32) + br_ref[...]
    lane = lax.broadcasted_iota(jnp.int32, (tm, LANES), 1)
    neg_inf = -jnp.inf

    def first_argmax(vals):
        vmax = jnp.max(vals, axis=-1, keepdims=True)
        idx = jnp.min(jnp.where(vals == vmax, lane, LANES), axis=-1, keepdims=True)
        return vmax, idx

    is_g = lane < N_EXPERT_GROUPS
    gmax, g_idx = first_argmax(jnp.where(is_g, logits, neg_inf))
    g_w = 1.0 / jnp.sum(jnp.where(is_g, jnp.exp(logits - gmax), 0.0), axis=-1, keepdims=True)
    e_lo = ROUTER_E0 + EXPERTS_PER_GROUP * g_idx
    elog = jnp.where((lane >= e_lo) & (lane < e_lo + EXPERTS_PER_GROUP), logits, neg_inf)
    v0, i0 = first_argmax(elog)
    v1, i1 = first_argmax(jnp.where(lane == i0, neg_inf, elog))
    t = jnp.exp(v1 - v0)
    w0 = g_w / (1.0 + t)
    w1 = g_w * t / (1.0 + t)

    sel0 = lane == i0
    sel1 = lane == i1
    onehot = jnp.where(sel0 | sel1, 1.0, 0.0)
    r_i = lax.broadcasted_iota(jnp.int32, (tm, tm), 0)
    c_i = lax.broadcasted_iota(jnp.int32, (tm, tm), 1)
    tri = jnp.where(c_i < r_i, 1.0, 0.0).astype(BF16)
    before = jnp.dot(tri, onehot.astype(BF16), preferred_element_type=F32) + base_ref[...]
    rank0 = jnp.sum(jnp.where(sel0, before, 0.0), axis=-1, keepdims=True)
    rank1 = jnp.sum(jnp.where(sel1, before, 0.0), axis=-1, keepdims=True)
    base_ref[...] = base_ref[...] + jnp.sum(onehot, axis=0, keepdims=True)
    cnt_ref[...] = base_ref[...]

    eid0 = (i0 - ROUTER_E0).astype(F32)
    eid1 = (i1 - ROUTER_E0).astype(F32)
    meta = jnp.where(lane == 0, eid0, jnp.where(lane == 1, eid1, jnp.where(
        lane == 2, rank0, jnp.where(lane == 3, rank1, jnp.where(
            lane == 4, w0, jnp.where(lane == 5, w1, 0.0))))))
    meta_ref[...] = meta[:, 0:META_COLS]


def _router(h, g, wg1, bg1, wg2, bg2, tm=256):
    n, d = h.shape
    wr = jnp.concatenate([wg1.astype(F32),
                          wg2.astype(F32).transpose(1, 0, 2).reshape(d, N_EXPERTS)], axis=1)
    wr = jnp.pad(wr, ((0, 0), (0, LANES - wr.shape[1])))
    br = jnp.pad(jnp.concatenate([bg1.astype(F32), bg2.astype(F32).reshape(-1)]),
                 (0, LANES - N_EXPERT_GROUPS - N_EXPERTS)).reshape(1, LANES)
    return pl.pallas_call(
        _router_body,
        grid=(n // tm,),
        in_specs=[pl.BlockSpec((tm, d), lambda i: (i, 0)),
                  pl.BlockSpec((1, d), lambda i: (0, 0)),
                  pl.BlockSpec((d, LANES), lambda i: (0, 0)),
                  pl.BlockSpec((1, LANES), lambda i: (0, 0))],
        out_specs=[pl.BlockSpec((tm, d), lambda i: (i, 0)),
                   pl.BlockSpec((tm, META_COLS), lambda i: (i, 0)),
                   pl.BlockSpec((1, LANES), lambda i: (0, 0))],
        out_shape=[jax.ShapeDtypeStruct((n, d), F32),
                   jax.ShapeDtypeStruct((n, META_COLS), F32),
                   jax.ShapeDtypeStruct((1, LANES), F32)],
        scratch_shapes=[pltpu.VMEM((1, LANES), F32)],
        compiler_params=_cparams(("arbitrary",)),
        name="moe_router",
    )(h, g.reshape(1, d).astype(F32), wr, br)


def _invert_body(dest_ref, rt_ref, *, ts):
    t = pl.program_id(0)

    def body(r, c):
        tok = t * ts + r
        rt_ref[dest_ref[0, 2 * r]] = tok
        rt_ref[dest_ref[0, 2 * r + 1]] = tok
        return c

    lax.fori_loop(0, ts, body, 0, unroll=8)


def _invert(dest, ts=512):
    n = dest.shape[0]
    dest3 = dest.reshape(n // ts, 1, 2 * ts)
    return pl.pallas_call(
        functools.partial(_invert_body, ts=ts),
        grid=(n // ts,),
        in_specs=[pl.BlockSpec((None, 1, 2 * ts), lambda t: (t, 0, 0), memory_space=pltpu.SMEM)],
        out_specs=pl.BlockSpec(memory_space=pltpu.SMEM),
        out_shape=jax.ShapeDtypeStruct((2 * n,), jnp.int32),
        compiler_params=_cparams(("arbitrary",)),
        name="moe_invert",
    )(dest3)


def _combine_body(dest_ref, h_ref, w_ref, y_hbm, o_ref, y0_ref, y1_ref, sem, *, ts):
    def issue(r, c):
        pltpu.make_async_copy(y_hbm.at[pl.ds(dest_ref[0, 2 * r], 1), :],
                              y0_ref.at[pl.ds(r, 1), :], sem.at[0]).start()
        pltpu.make_async_copy(y_hbm.at[pl.ds(dest_ref[0, 2 * r + 1], 1), :],
                              y1_ref.at[pl.ds(r, 1), :], sem.at[1]).start(priority=1)
        return c

    lax.fori_loop(0, ts, issue, 0, unroll=8)
    pltpu.make_async_copy(y_hbm.at[pl.ds(0, ts), :], y0_ref, sem.at[0]).wait()
    pltpu.make_async_copy(y_hbm.at[pl.ds(0, ts), :], y1_ref, sem.at[1]).wait()
    w = w_ref[...]
    o_ref[...] = h_ref[...] + w[:, 4:5] * y0_ref[...] + w[:, 5:6] * y1_ref[...]


def _combine(h, meta, y_sorted, dest, ts=256):
    n, d = h.shape
    dest3 = dest.reshape(n // ts, 1, 2 * ts)
    return pl.pallas_call(
        functools.partial(_combine_body, ts=ts),
        grid=(n // ts,),
        in_specs=[pl.BlockSpec((None, 1, 2 * ts), lambda t: (t, 0, 0), memory_space=pltpu.SMEM),
                  pl.BlockSpec((ts, d), lambda t: (t, 0)),
                  pl.BlockSpec((ts, META_COLS), lambda t: (t, 0)),
                  pl.BlockSpec(memory_space=pl.ANY)],
        out_specs=pl.BlockSpec((ts, d), lambda t: (t, 0)),
        out_shape=jax.ShapeDtypeStruct((n, d), F32),
        scratch_shapes=[pltpu.VMEM((ts, d), F32), pltpu.VMEM((ts, d), F32),
                        pltpu.SemaphoreType.DMA((2,))],
        compiler_params=_cparams(("arbitrary",)),
        name="moe_combine",
    )(dest3, h, meta, y_sorted)


EXPERT_TM = 256
EXPERT_VMEM_LIMIT = 56 * 1024 * 1024


def _expert_body(ie_ref, it_ref, lo_ref, hi_ref, first_ref, rfirst_ref, nxt_ref,
                 rt0_ref, rtn_ref, xn_hbm, wg_hbm, wu_hbm, wd_hbm, o_ref,
                 xbuf, stg_g, stg_u, stg_d, wgb, wub, wdb, sem_x, sem_w, *, tm, layer, n_tiles):
    i = pl.program_id(0)
    e = ie_ref[i]
    t = it_ref[i]
    lo = lo_ref[i]
    hi = hi_ref[i]
    slot = t % 2

    def weight_copies(ex):
        return (pltpu.make_async_copy(wg_hbm.at[layer, ex], stg_g, sem_w.at[0]),
                pltpu.make_async_copy(wu_hbm.at[layer, ex], stg_u, sem_w.at[1]),
                pltpu.make_async_copy(wd_hbm.at[layer, ex], stg_d, sem_w.at[2]))

    def gather_start(rt_ref, dst_slot):
        def issue(r2, c):
            for par in range(2):
                r = 2 * r2 + par
                pltpu.make_async_copy(xn_hbm.at[pl.ds(rt_ref[0, r], 1), :],
                                      xbuf.at[dst_slot, pl.ds(r, 1), :],
                                      sem_x.at[dst_slot]).start(priority=par)
            return c

        lax.fori_loop(0, tm // 2, issue, 0, unroll=4)

    @pl.when(i == 0)
    def _():
        for cp in weight_copies(e):
            cp.start()
        gather_start(rt0_ref, 0)

    @pl.when(rfirst_ref[i] == 1)
    def _():
        for cp in weight_copies(e):
            cp.wait()
        rows = 256

        def cast_rows(r, c):
            r0 = pl.multiple_of(r * rows, rows)
            wgb[pl.ds(r0, rows), :] = stg_g[pl.ds(r0, rows), :].astype(BF16)
            wub[pl.ds(r0, rows), :] = stg_u[pl.ds(r0, rows), :].astype(BF16)
            return c

        lax.fori_loop(0, stg_g.shape[0] // rows, cast_rows, 0)

        def cast_rows_d(r, c):
            r0 = pl.multiple_of(r * rows, rows)
            wdb[pl.ds(r0, rows), :] = stg_d[pl.ds(r0, rows), :].astype(BF16)
            return c

        lax.fori_loop(0, stg_d.shape[0] // rows, cast_rows_d, 0)

        @pl.when(nxt_ref[i] >= 0)
        def _():
            for cp in weight_copies(nxt_ref[i]):
                cp.start()

    @pl.when(first_ref[i] == 1)
    def _():
        @pl.when(t + 1 < n_tiles)
        def _():
            gather_start(rtn_ref, 1 - slot)

        pltpu.make_async_copy(xn_hbm.at[pl.ds(0, tm), :], xbuf.at[slot], sem_x.at[slot]).wait()

    @pl.when(hi > lo)
    def _():
        xb = xbuf[slot].astype(BF16)
        g = jnp.dot(xb, wgb[...], preferred_element_type=F32)
        u = jnp.dot(xb, wub[...], preferred_element_type=F32)
        hdn = (g * (1.0 / (1.0 + jnp.exp(-g)))) * u
        row = t * tm + lax.broadcasted_iota(jnp.int32, (tm, 1), 0)
        hdn = jnp.where((row >= lo) & (row < hi), hdn, 0.0)
        y = jnp.dot(hdn.astype(BF16), wdb[...], preferred_element_type=F32)

        @pl.when(first_ref[i] == 1)
        def _():
            o_ref[...] = y

        @pl.when(first_ref[i] != 1)
        def _():
            o_ref[...] = o_ref[...] + y


def _experts(xn, row_token, items, w_gate, w_up, w_down, layer, tm):
    _, d = xn.shape
    dh = w_gate.shape[-1]
    r_total = row_token.shape[0]
    n_tiles = r_total // tm
    n_items = items[0].shape[0]
    rt3 = row_token.reshape(n_tiles, 1, tm)
    grid_spec = pltpu.PrefetchScalarGridSpec(
        num_scalar_prefetch=7,
        grid=(n_items,),
        in_specs=[
            pl.BlockSpec((None, 1, tm), lambda i, *_: (0, 0, 0), memory_space=pltpu.SMEM),
            pl.BlockSpec((None, 1, tm), lambda i, ie, it, *_: (jnp.minimum(it[i] + 1, n_tiles - 1), 0, 0),
                         memory_space=pltpu.SMEM),
            pl.BlockSpec(memory_space=pl.ANY), pl.BlockSpec(memory_space=pl.ANY),
            pl.BlockSpec(memory_space=pl.ANY), pl.BlockSpec(memory_space=pl.ANY),
        ],
        out_specs=pl.BlockSpec((tm, d), lambda i, ie, it, *_: (it[i], 0)),
        scratch_shapes=[pltpu.VMEM((2, tm, d), F32),
                        pltpu.VMEM((d, dh), F32), pltpu.VMEM((d, dh), F32), pltpu.VMEM((dh, d), F32),
                        pltpu.VMEM((d, dh), BF16), pltpu.VMEM((d, dh), BF16), pltpu.VMEM((dh, d), BF16),
                        pltpu.SemaphoreType.DMA((2,)), pltpu.SemaphoreType.DMA((3,))],
    )
    return pl.pallas_call(
        functools.partial(_expert_body, tm=tm, layer=layer, n_tiles=n_tiles),
        grid_spec=grid_spec,
        out_shape=jax.ShapeDtypeStruct((r_total, d), F32),
        compiler_params=pltpu.CompilerParams(dimension_semantics=("arbitrary",),
                                             vmem_limit_bytes=EXPERT_VMEM_LIMIT),
        name="moe_experts",
    )(*items, rt3, rt3, xn, w_gate, w_up, w_down)


def _moe_plan(meta, counts_row, n_rows, tm):
    counts = counts_row[0, ROUTER_E0:ROUTER_E0 + N_EXPERTS].astype(jnp.int32)
    ends = jnp.cumsum(counts)
    starts = ends - counts
    eid = meta[:, 0:2].astype(jnp.int32)
    rank = meta[:, 2:4].astype(jnp.int32)
    dest = starts[eid] + rank

    first_tile = starts // tm
    last_tile = jnp.maximum(ends - 1, 0) // tm
    ntile_e = jnp.where(counts > 0, last_tile - first_tile + 1, 0)
    item_end = jnp.cumsum(ntile_e)
    item_start = item_end - ntile_e
    total = item_end[-1]
    n_items = n_rows // tm + N_EXPERTS - 1
    ii = jnp.arange(n_items, dtype=jnp.int32)
    valid = ii < total
    ii_c = jnp.minimum(ii, total - 1)
    e_i = jnp.searchsorted(item_end, ii_c, side='right').astype(jnp.int32)
    t_i = (first_tile[e_i] + (ii_c - item_start[e_i])).astype(jnp.int32)
    lo = jnp.where(valid, jnp.maximum(starts[e_i], t_i * tm), 0).astype(jnp.int32)
    hi = jnp.where(valid, jnp.minimum(ends[e_i], (t_i + 1) * tm), 0).astype(jnp.int32)
    minus1 = jnp.full((1,), -1, jnp.int32)
    first = (valid & (t_i != jnp.concatenate([minus1, t_i[:-1]]))).astype(jnp.int32)
    run_first = (valid & (e_i != jnp.concatenate([minus1, e_i[:-1]]))).astype(jnp.int32)
    ar = jnp.arange(N_EXPERTS, dtype=jnp.int32)
    later = (ar[None, :] > ar[:, None]) & (counts > 0)[None, :]
    nxt_e = jnp.min(jnp.where(later, ar[None, :], N_EXPERTS), axis=1)
    nxt_e = jnp.where(nxt_e == N_EXPERTS, -1, nxt_e).astype(jnp.int32)
    items = (e_i, t_i, lo, hi, first, run_first, nxt_e[e_i])
    return dest.astype(jnp.int32), items


def _hier_moe(h, g, wg1, bg1, wg2, bg2, w_gate, w_up, w_down, layer, tm=EXPERT_TM):
    n, _ = h.shape
    xn, meta, counts_row = _router(h, g, wg1, bg1, wg2, bg2)
    dest, items = _moe_plan(meta, counts_row, 2 * n, tm)
    y_sorted = _experts(xn, _invert(dest), items, w_gate, w_up, w_down, layer, tm)
    return _combine(h, meta, y_sorted, dest)


def _time_permute(h, batch, inverse=False):
    n, d = h.shape
    l = n // batch
    shape = (batch, l // S5_SEGMENTS, S5_SEGMENTS, d) if inverse else (batch, S5_SEGMENTS, l // S5_SEGMENTS, d)
    return h.reshape(shape).transpose(0, 2, 1, 3).reshape(n, d)


def kernel(x, norm_mix_g, norm_ffn_g, ssm_lam_re, ssm_lam_im, ssm_b_re, ssm_b_im, ssm_c_re, ssm_c_im, ssm_d, ssm_log_step, ssm_w_glu1, ssm_w_glu2, kv_norm_g, w_kv, k_norm_g, w_q, q_norm_g, lambda_qk, subln_g, w_o, rel_bias, router_group_w, router_group_b, router_expert_w, router_expert_b, w_gate, w_up, w_down):
    batch, l, d = x.shape
    n = batch * l
    h = _time_permute(x.astype(F32).reshape(n, d), batch)
    kn = vb = bias_tabs = None
    for layer in range(DEPTH):
        if layer < N_A_LAYERS:
            xn = _rmsnorm(h, norm_mix_g[layer], BF16)
            b_blk, c_blk, lam_v = _s5_params(ssm_lam_re[layer], ssm_lam_im[layer], ssm_b_re[layer],
                                             ssm_b_im[layer], ssm_c_re[layer], ssm_c_im[layer],
                                             ssm_log_step[layer])
            z = _s5_scan(xn, b_blk, c_blk, lam_v, ssm_d[layer], batch)
            h = _matmul(z, [ssm_w_glu1, ssm_w_glu2], layer, F32, res=h)
        else:
            j = layer - N_A_LAYERS
            lam_init = 0.8 - 0.6 * math.exp(-0.3 * layer)
            xn = _rmsnorm(h, norm_mix_g[layer], BF16)
            qn = _headnorm(_matmul(xn, [w_q], j, F32), q_norm_g[j], HEAD_DIM ** -0.5 * LOG2E)
            o = _diff_attention(qn, kn, vb, bias_tabs, lambda_qk[j], subln_g[j], lam_init, batch)
            h = _matmul(o, [w_o], j, F32, res=h)
        h = _hier_moe(h, norm_ffn_g[layer], router_group_w[layer], router_group_b[layer],
                      router_expert_w[layer], router_expert_b[layer], w_gate, w_up, w_down, layer)
        if layer == N_A_LAYERS - 1:
            h = _time_permute(h, batch, inverse=True)
            xkv = _rmsnorm(h, kv_norm_g, BF16)
            w_kv3 = w_kv.reshape(1, d, 2 * d)
            kn = _headnorm(_matmul(xkv, [w_kv3], 0, F32, n_out=d), k_norm_g, 1.0)
            vb = _matmul(xkv, [w_kv3], 0, BF16, col_off=d // 512, n_out=d)
            bias_tabs = _attn_bias_tables(rel_bias, ATTN_TQ)
    return h.reshape(batch, l, d)
```

```python
import functools
import math

import jax
import jax.numpy as jnp
from jax import lax
from jax.experimental import pallas as pl
from jax.experimental.pallas import tpu as pltpu

F32 = jnp.float32
BF16 = jnp.bfloat16

DEPTH = 4
N_A_LAYERS = DEPTH // 2
SSM_GROUP = 16
SSM_STATE = 64
HEAD_DIM = 64
REL_BUCKETS = 32
REL_MAX_EXACT = REL_BUCKETS // 2
REL_MAX_DISTANCE = 128
N_EXPERT_GROUPS = 4
EXPERTS_PER_GROUP = 4
N_EXPERTS = N_EXPERT_GROUPS * EXPERTS_PER_GROUP
RMS_EPS = 1e-6

LANES = 128
SUBLANES = 8
VMEM_LIMIT = 48 * 1024 * 1024
NEG_BIG = -1e30

S5_SEGMENTS = SUBLANES
S5_GROUPS_PER_BLOCK = 16
S5_TJ = 32
PROJ_TN = 1024


def _cparams(sem):
    return pltpu.CompilerParams(dimension_semantics=sem, vmem_limit_bytes=VMEM_LIMIT)


def _rmsnorm_body(h_ref, g_ref, o_ref):
    x = h_ref[...]
    ms = jnp.mean(x * x, axis=-1, keepdims=True)
    o_ref[...] = (x * lax.rsqrt(ms + RMS_EPS) * g_ref[...]).astype(o_ref.dtype)


def _rmsnorm(h, g, out_dtype, tm=512):
    n, d = h.shape
    return pl.pallas_call(
        _rmsnorm_body,
        grid=(n // tm,),
        in_specs=[pl.BlockSpec((tm, d), lambda i: (i, 0)),
                  pl.BlockSpec((1, d), lambda i: (0, 0))],
        out_specs=pl.BlockSpec((tm, d), lambda i: (i, 0)),
        out_shape=jax.ShapeDtypeStruct((n, d), out_dtype),
        compiler_params=_cparams(("arbitrary",)),
        name="rmsnorm",
    )(h, g.reshape(1, d))


def _mm_body(*refs, n_w, has_res, hn_scale):
    a_ref = refs[0]
    w_refs = refs[1:1 + n_w]
    pos = 1 + n_w
    res_ref = refs[pos] if has_res else None
    pos += int(has_res)
    g_ref = refs[pos] if hn_scale is not None else None
    pos += int(hn_scale is not None)
    o_ref = refs[pos]
    wb_refs = refs[pos + 1:]

    @pl.when(pl.program_id(1) == 0)
    def _():
        for w_ref, wb_ref in zip(w_refs, wb_refs):
            wb_ref[...] = w_ref[...].astype(BF16)

    a = a_ref[...]
    y = jnp.dot(a, wb_refs[0][...], preferred_element_type=F32)
    if n_w == 2:
        y2 = jnp.dot(a, wb_refs[1][...], preferred_element_type=F32)
        y = y * (1.0 / (1.0 + jnp.exp(-y2)))
    if has_res:
        y = y + res_ref[...]
    if hn_scale is None:
        o_ref[...] = y.astype(o_ref.dtype)
    else:
        tm, tn = y.shape
        lane = lax.broadcasted_iota(jnp.int32, (tm, LANES), 1)
        lo = lane < HEAD_DIM
        g = g_ref[...]
        for hb in range(tn // LANES):
            x = y[:, hb * LANES:(hb + 1) * LANES]
            x2 = x * x
            s_lo = jnp.sum(jnp.where(lo, x2, 0.0), axis=-1, keepdims=True)
            s_hi = jnp.sum(jnp.where(lo, 0.0, x2), axis=-1, keepdims=True)
            r = jnp.where(lo, lax.rsqrt(s_lo * (1.0 / HEAD_DIM) + RMS_EPS),
                          lax.rsqrt(s_hi * (1.0 / HEAD_DIM) + RMS_EPS))
            o_ref[:, hb * LANES:(hb + 1) * LANES] = ((x * r * g) * hn_scale).astype(o_ref.dtype)


def _matmul(a, ws, layer, out_dtype, res=None, col_off=0, n_out=None, head_norm=None,
            tm=512, tn=512):
    m, k = a.shape
    n_out = ws[0].shape[-1] if n_out is None else n_out
    n_w = len(ws)
    in_specs = [pl.BlockSpec((tm, k), lambda j, i: (i, 0))]
    for _ in ws:
        in_specs.append(pl.BlockSpec((None, k, tn), lambda j, i: (layer, 0, j + col_off)))
    args = [a, *ws]
    if res is not None:
        in_specs.append(pl.BlockSpec((tm, tn), lambda j, i: (i, j)))
        args.append(res)
    hn_scale = None
    if head_norm is not None:
        gain, hn_scale = head_norm
        in_specs.append(pl.BlockSpec((1, LANES), lambda j, i: (0, 0)))
        args.append(jnp.concatenate([gain, gain]).reshape(1, LANES).astype(F32))
    return pl.pallas_call(
        functools.partial(_mm_body, n_w=n_w, has_res=res is not None, hn_scale=hn_scale),
        grid=(n_out // tn, m // tm),
        in_specs=in_specs,
        out_specs=pl.BlockSpec((tm, tn), lambda j, i: (i, j)),
        out_shape=jax.ShapeDtypeStruct((m, n_out), out_dtype),
        scratch_shapes=[pltpu.VMEM((k, tn), BF16) for _ in ws],
        compiler_params=_cparams(("arbitrary", "arbitrary")),
        name="dense_matmul",
    )(*args)


def _gelu_tanh(y):
    c = math.sqrt(2.0 / math.pi)
    return y * (0.5 * (1.0 + jnp.tanh(c * (y + 0.044715 * (y * y * y)))))


def _s5_body(x_ref, b_ref, c_ref, lam_ref, d_ref, z_ref, bu0_ref, bu1_ref, st0_ref, st1_ref,
             e_ref, init_ref, *, seg_len):
    ch = lam_ref.shape[-1]
    rows = S5_TJ * S5_SEGMENTS
    n_tiles = seg_len // S5_TJ
    lr = jnp.broadcast_to(lam_ref[0:1, :], (S5_SEGMENTS, ch))
    li = jnp.broadcast_to(lam_ref[1:2, :], (S5_SEGMENTS, ch))

    def row0(t):
        return pl.multiple_of(jnp.clip(t, 0, n_tiles - 1) * rows, rows)

    def bu_tile(t, bu_ref):
        bu_ref[...] = jnp.dot(x_ref[pl.ds(row0(t), rows), :], b_ref[...],
                              preferred_element_type=F32)

    def scan_tile(bu_ref, s_re, s_im, st_ref):
        for j in range(S5_TJ):
            sl = slice(S5_SEGMENTS * j, S5_SEGMENTS * (j + 1))
            n_re = lr * s_re - li * s_im + bu_ref[sl, 0:ch]
            n_im = lr * s_im + li * s_re + bu_ref[sl, ch:2 * ch]
            s_re, s_im = n_re, n_im
            if st_ref is not None:
                st_ref[sl, 0:ch] = s_re
                st_ref[sl, ch:2 * ch] = s_im
        return s_re, s_im

    def project(t, st_ref):
        r0 = row0(t)
        y = jnp.dot(st_ref[...].astype(BF16), c_ref[...], preferred_element_type=F32)
        y = y + d_ref[...] * x_ref[pl.ds(r0, rows), :].astype(F32)
        z_ref[pl.ds(r0, rows), :] = _gelu_tanh(y).astype(z_ref.dtype)

    def pass1(i2, c):
        t = 2 * i2
        bu_tile(t + 1, bu1_ref)
        c = scan_tile(bu0_ref, c[0], c[1], None)
        bu_tile(t + 2, bu0_ref)
        return scan_tile(bu1_ref, c[0], c[1], None)

    zeros = jnp.zeros((S5_SEGMENTS, ch), F32)
    bu_tile(0, bu0_ref)
    e_re, e_im = lax.fori_loop(0, n_tiles // 2, pass1, (zeros, zeros))
    e_ref[:, 0:ch] = e_re
    e_ref[:, ch:2 * ch] = e_im

    pr, pi = lam_ref[0:1, :], lam_ref[1:2, :]
    for _ in range(seg_len.bit_length() - 1):
        pr, pi = pr * pr - pi * pi, 2.0 * pr * pi
    cr = jnp.zeros((1, ch), F32)
    ci = jnp.zeros((1, ch), F32)
    init_ref[0:1, :] = jnp.zeros((1, 2 * ch), F32)
    for k in range(S5_SEGMENTS - 1):
        er = e_ref[k:k + 1, 0:ch]
        ei = e_ref[k:k + 1, ch:2 * ch]
        cr, ci = pr * cr - pi * ci + er, pr * ci + pi * cr + ei
        init_ref[k + 1:k + 2, 0:ch] = cr
        init_ref[k + 1:k + 2, ch:2 * ch] = ci

    def pass2(i2, c):
        t = 2 * i2
        bu_tile(t + 1, bu1_ref)
        c = scan_tile(bu0_ref, c[0], c[1], st0_ref)
        project(t - 1, st1_ref)
        bu_tile(t + 2, bu0_ref)
        c = scan_tile(bu1_ref, c[0], c[1], st1_ref)
        project(t, st0_ref)
        return c

    st1_ref[...] = jnp.zeros(st1_ref.shape, F32)
    bu_tile(0, bu0_ref)
    lax.fori_loop(0, n_tiles // 2, pass2, (init_ref[:, 0:ch], init_ref[:, ch:2 * ch]))
    project(n_tiles - 1, st1_ref)


def _s5_params(lam_re, lam_im, b_re, b_im, c_re, c_im, log_step):
    g, p = lam_re.shape
    gb = S5_GROUPS_PER_BLOCK
    nb = g // gb
    lam = lax.complex(lam_re.astype(F32), lam_im.astype(F32))
    delta = jnp.exp(log_step.astype(F32))[:, None]
    lam_bar = jnp.exp(lam * delta)
    b_bar = ((lam_bar - 1.0) / lam)[..., None] * lax.complex(b_re.astype(F32), b_im.astype(F32))
    eye = jnp.eye(gb, dtype=F32)

    def blk_b(part):
        part = part.reshape(nb, gb, p, SSM_GROUP)
        return jnp.einsum('cgph,gk->cghkp', part, eye).reshape(nb, gb * SSM_GROUP, gb * p)

    def blk_c(part):
        part = part.reshape(nb, gb, SSM_GROUP, p)
        return jnp.einsum('cghp,gk->cgpkh', part, eye).reshape(nb, gb * p, gb * SSM_GROUP)

    b_blk = jnp.concatenate([blk_b(jnp.real(b_bar)), blk_b(jnp.imag(b_bar))], axis=-1)
    c_blk = jnp.concatenate([blk_c(c_re.astype(F32)), blk_c(-c_im.astype(F32))], axis=1)
    lam_v = jnp.stack([jnp.real(lam_bar).reshape(nb, gb * p),
                       jnp.imag(lam_bar).reshape(nb, gb * p)], axis=1)
    return b_blk.astype(BF16), c_blk.astype(BF16), lam_v


def _s5_scan(xn, b_blk, c_blk, lam_v, d_skip, batch):
    n, d = xn.shape
    l = n // batch
    seg_len = l // S5_SEGMENTS
    nb, fb, ch2 = b_blk.shape
    rows = S5_TJ * S5_SEGMENTS
    return pl.pallas_call(
        functools.partial(_s5_body, seg_len=seg_len),
        grid=(batch, nb),
        in_specs=[pl.BlockSpec((l, fb), lambda b, c: (b, c)),
                  pl.BlockSpec((None, fb, ch2), lambda b, c: (c, 0, 0)),
                  pl.BlockSpec((None, ch2, fb), lambda b, c: (c, 0, 0)),
                  pl.BlockSpec((None, 2, ch2 // 2), lambda b, c: (c, 0, 0)),
                  pl.BlockSpec((1, fb), lambda b, c: (0, c))],
        out_specs=pl.BlockSpec((l, fb), lambda b, c: (b, c)),
        out_shape=jax.ShapeDtypeStruct((n, d), BF16),
        scratch_shapes=[pltpu.VMEM((rows, ch2), F32), pltpu.VMEM((rows, ch2), F32),
                        pltpu.VMEM((rows, ch2), F32), pltpu.VMEM((rows, ch2), F32),
                        pltpu.VMEM((S5_SEGMENTS, ch2), F32), pltpu.VMEM((S5_SEGMENTS, ch2), F32)],
        compiler_params=_cparams(("arbitrary", "arbitrary")),
        name="s5_scan",
    )(xn, b_blk, c_blk, lam_v, d_skip.reshape(1, d).astype(F32))


def _t5_bucket(n):
    n_safe = jnp.maximum(n, 1).astype(F32)
    large = REL_MAX_EXACT + (jnp.log(n_safe / REL_MAX_EXACT)
                             / math.log(REL_MAX_DISTANCE / REL_MAX_EXACT)
                             * (REL_BUCKETS - REL_MAX_EXACT)).astype(jnp.int32)
    large = jnp.minimum(large, REL_BUCKETS - 1)
    return jnp.where(n < REL_MAX_EXACT, n, large)


ATTN_TQ = 256
M_INIT = -1e29
LOG2E = 1.4426950408889634


def _attn_bias_tables(rel_bias, tq):
    qi = jnp.arange(tq, dtype=jnp.int32)[:, None]
    ki = jnp.arange(tq, dtype=jnp.int32)[None, :]
    d_diag = qi - ki
    rb = rel_bias.astype(F32)
    rb = (rb - rb[REL_BUCKETS - 1][None, :]) * LOG2E

    def lookup(dist):
        onehot = jax.nn.one_hot(_t5_bucket(dist), REL_BUCKETS, dtype=F32)
        return jnp.einsum('qkb,bh->hqk', onehot, rb, precision=lax.Precision.HIGHEST)

    b_diag = jnp.where((d_diag >= 0)[None], lookup(jnp.maximum(d_diag, 0)), NEG_BIG)
    b_prev = lookup(d_diag + tq)
    masked = jnp.full_like(b_diag, NEG_BIG)
    return jnp.stack([jnp.concatenate([b_prev, b_diag], axis=-1),
                      jnp.concatenate([b_diag, masked], axis=-1)], axis=1)


def _attn_pair_tables(nq):
    def framed(pairs):
        idle = (0, nq, 0, 0)
        rows = [idle, idle] + pairs + [idle, idle]
        if len(rows) % 2:
            rows.append(idle)
        return rows

    wide = [(i, i, 2 * j, 0) for i in range(nq) for j in range((i - 1) // 2)]
    odd = [(i, i, i - 2, 0) for i in range(2, nq) if (i - 1) % 2]
    near = [(i, i, max(i - 1, 0), 0 if i else 1) for i in range(nq)]
    lists = [framed(wide), framed(odd), framed(near)]
    cols = list(zip(*(lists[0] + lists[1] + lists[2])))
    return [jnp.asarray(c, jnp.int32) for c in cols], [len(x) for x in lists]


def _attn_body(qrow_ref, slot_ref, kt_ref, bidx_ref, q_ref, k_ref, v_ref, bias_ref, lq_ref, sg_ref,
               o_ref, m_ref, acc_ref, vaug_ref, s0_ref, s1_ref, p0_ref, p1_ref, a0_ref, a1_ref,
               sw0_ref, sw1_ref, pw0_ref, pw1_ref, *, tq, nq, n_rows, lam_init):
    m_ref[...] = jnp.full(m_ref.shape, M_INIT, F32)
    acc_ref[...] = jnp.zeros(acc_ref.shape, F32)
    vaug_ref[:, 0:LANES] = v_ref[...]
    vaug_ref[:, LANES:2 * LANES] = jnp.ones(v_ref.shape, BF16)
    lane = lax.broadcasted_iota(jnp.int32, (tq, LANES), 1)

    def scores(e, s_ref, tk):
        q = q_ref[pl.ds(pl.multiple_of(qrow_ref[e] * tq, tq), tq), :]
        zero = jnp.zeros_like(q)
        qq = jnp.concatenate([jnp.where(lane < HEAD_DIM, q, zero),
                              jnp.where(lane < HEAD_DIM, zero, q)], axis=0)
        k = k_ref[pl.ds(pl.multiple_of(kt_ref[e] * tq, tq), tk), :]
        s_ref[...] = lax.dot_general(qq, k, (((1,), (1,)), ((), ())), preferred_element_type=F32)

    def softmax(e, s_ref, a_ref, p_ref, with_bias, rc):
        m_view = m_ref.at[slot_ref[e]]
        b_view = bias_ref.at[bidx_ref[e]]
        for c in range(tq // rc):
            b = b_view[c * rc:(c + 1) * rc, :] if with_bias else None
            for half in range(2):
                rows = slice(half * tq + c * rc, half * tq + (c + 1) * rc)
                s = s_ref[rows, :]
                if with_bias:
                    s = s + b
                m_prev = m_view[rows, :]
                m_next = jnp.maximum(m_prev, jnp.max(s, axis=1, keepdims=True))
                a_ref[rows, :] = jnp.exp2(m_prev - m_next)
                p_ref[rows, :] = jnp.exp2(s - m_next[:, 0:1]).astype(BF16)
                m_view[rows, :] = m_next

    def values(e, a_ref, p_ref, tk):
        acc = acc_ref.at[slot_ref[e]]
        v = vaug_ref[pl.ds(pl.multiple_of(kt_ref[e] * tq, tq), tk), :]
        pv = jnp.dot(p_ref[...], v, preferred_element_type=F32)
        a = a_ref[...]
        acc[:, 0:LANES] = acc[:, 0:LANES] * a + pv[:, 0:LANES]
        acc[:, LANES:2 * LANES] = acc[:, LANES:2 * LANES] * a + pv[:, LANES:2 * LANES]

    def run_pipeline(base, rows_n, with_bias, tk, sa_ref, sb_ref, pa_ref, pb_ref):
        rc = 64 if tk == tq else 32
        pa_ref[...] = jnp.zeros(pa_ref.shape, BF16)
        a0_ref[...] = jnp.ones(a0_ref.shape, F32)
        scores(base + 1, sb_ref, tk)

        def two_iterations(i2, c):
            it = base + 2 * i2
            scores(it + 2, sa_ref, tk)
            softmax(it + 1, sb_ref, a1_ref, pb_ref, with_bias, rc)
            values(it, a0_ref, pa_ref, tk)
            scores(it + 3, sb_ref, tk)
            softmax(it + 2, sa_ref, a0_ref, pa_ref, with_bias, rc)
            values(it + 1, a1_ref, pb_ref, tk)
            return c

        lax.fori_loop(0, (rows_n - 2) // 2, two_iterations, 0)

    n_wide, n_odd, n_near = n_rows
    run_pipeline(0, n_wide, False, 2 * tq, sw0_ref, sw1_ref, pw0_ref, pw1_ref)
    run_pipeline(n_wide, n_odd, False, tq, s0_ref, s1_ref, p0_ref, p1_ref)
    run_pipeline(n_wide + n_odd, n_near, True, 2 * tq, sw0_ref, sw1_ref, pw0_ref, pw1_ref)

    lq = lq_ref[...]
    lam = (jnp.exp(jnp.sum(lq[0:1] * lq[1:2], axis=1, keepdims=True))
           - jnp.exp(jnp.sum(lq[2:3] * lq[3:4], axis=1, keepdims=True)) + lam_init)

    def finalize(i, c):
        acc = acc_ref[i]
        o_all = acc[:, 0:LANES] / acc[:, LANES:2 * LANES]
        o = o_all[0:tq] - lam * o_all[tq:2 * tq]
        ms = jnp.mean(o * o, axis=-1, keepdims=True)
        o = (o * lax.rsqrt(ms + RMS_EPS) * sg_ref[...]) * (1.0 - lam_init)
        o_ref[pl.ds(pl.multiple_of(i * tq, tq), tq), :] = o.astype(o_ref.dtype)
        return c

    lax.fori_loop(0, nq, finalize, 0)


def _diff_attention(qn, kn, vb, bias_tabs, lambda_qk, subln_g, lam_init, batch):
    n, d = qn.shape
    l = n // batch
    tq = ATTN_TQ
    nq = l // tq
    nh = d // LANES
    tables, n_rows = _attn_pair_tables(nq)
    row_blk = pl.BlockSpec((l, LANES), lambda b, h, *_: (b, h))
    grid_spec = pltpu.PrefetchScalarGridSpec(
        num_scalar_prefetch=4,
        grid=(batch, nh),
        in_specs=[row_blk, row_blk, row_blk,
                  pl.BlockSpec((None, 2, tq, 2 * tq), lambda b, h, *_: (h, 0, 0, 0)),
                  pl.BlockSpec((4, HEAD_DIM), lambda b, h, *_: (0, 0)),
                  pl.BlockSpec((1, LANES), lambda b, h, *_: (0, 0))],
        out_specs=row_blk,
        scratch_shapes=[pltpu.VMEM((nq + 1, 2 * tq, LANES), F32),
                        pltpu.VMEM((nq + 1, 2 * tq, 2 * LANES), F32),
                        pltpu.VMEM((l, 2 * LANES), BF16),
                        pltpu.VMEM((2 * tq, tq), F32), pltpu.VMEM((2 * tq, tq), F32),
                        pltpu.VMEM((2 * tq, tq), BF16), pltpu.VMEM((2 * tq, tq), BF16),
                        pltpu.VMEM((2 * tq, LANES), F32), pltpu.VMEM((2 * tq, LANES), F32),
                        pltpu.VMEM((2 * tq, 2 * tq), F32), pltpu.VMEM((2 * tq, 2 * tq), F32),
                        pltpu.VMEM((2 * tq, 2 * tq), BF16), pltpu.VMEM((2 * tq, 2 * tq), BF16)],
    )
    return pl.pallas_call(
        functools.partial(_attn_body, tq=tq, nq=nq, n_rows=tuple(n_rows), lam_init=lam_init),
        grid_spec=grid_spec,
        out_shape=jax.ShapeDtypeStruct((n, d), BF16),
        compiler_params=_cparams(("arbitrary", "arbitrary")),
        name="diff_attention",
    )(*tables, qn, kn, vb, bias_tabs, lambda_qk.astype(F32),
      subln_g.reshape(1, LANES).astype(F32))


ROUTER_E0 = N_EXPERT_GROUPS
META_COLS = 8


def _router_body(h_ref, g_ref, wr_ref, br_ref, xn_ref, meta_ref, cnt_ref, base_ref):
    tm = h_ref.shape[0]

    @pl.when(pl.program_id(0) == 0)
    def _():
        base_ref[...] = jnp.zeros(base_ref.shape, F32)

    x = h_ref[...]
    ms = jnp.mean(x * x, axis=-1, keepdims=True)
    xn = x * lax.rsqrt(ms + RMS_EPS) * g_ref[...]
    xn_ref[...] = xn.astype(xn_ref.dtype)
    logits = jnp.dot(xn, wr_ref[...], precision=lax.Precision.HIGHEST,
                     preferred_element_type=F32) + br_ref[...]
    lane = lax.broadcasted_iota(jnp.int32, (tm, LANES), 1)
    neg_inf = -jnp.inf

    def first_argmax(vals):
        vmax = jnp.max(vals, axis=-1, keepdims=True)
        idx = jnp.min(jnp.where(vals == vmax, lane, LANES), axis=-1, keepdims=True)
        return vmax, idx

    is_g = lane < N_EXPERT_GROUPS
    gmax, g_idx = first_argmax(jnp.where(is_g, logits, neg_inf))
    g_w = 1.0 / jnp.sum(jnp.where(is_g, jnp.exp(logits - gmax), 0.0), axis=-1, keepdims=True)
    e_lo = ROUTER_E0 + EXPERTS_PER_GROUP * g_idx
    elog = jnp.where((lane >= e_lo) & (lane < e_lo + EXPERTS_PER_GROUP), logits, neg_inf)
    v0, i0 = first_argmax(elog)
    v1, i1 = first_argmax(jnp.where(lane == i0, neg_inf, elog))
    t = jnp.exp(v1 - v0)
    w0 = g_w / (1.0 + t)
    w1 = g_w * t / (1.0 + t)

    sel0 = lane == i0
    sel1 = lane == i1
    onehot = jnp.where(sel0 | sel1, 1.0, 0.0)
    r_i = lax.broadcasted_iota(jnp.int32, (tm, tm), 0)
    c_i = lax.broadcasted_iota(jnp.int32, (tm, tm), 1)
    tri = jnp.where(c_i < r_i, 1.0, 0.0).astype(BF16)
    before = jnp.dot(tri, onehot.astype(BF16), preferred_element_type=F32) + base_ref[...]
    rank0 = jnp.sum(jnp.where(sel0, before, 0.0), axis=-1, keepdims=True)
    rank1 = jnp.sum(jnp.where(sel1, before, 0.0), axis=-1, keepdims=True)
    base_ref[...] = base_ref[...] + jnp.sum(onehot, axis=0, keepdims=True)
    cnt_ref[...] = base_ref[...]

    eid0 = (i0 - ROUTER_E0).astype(F32)
    eid1 = (i1 - ROUTER_E0).astype(F32)
    meta = jnp.where(lane == 0, eid0, jnp.where(lane == 1, eid1, jnp.where(
        lane == 2, rank0, jnp.where(lane == 3, rank1, jnp.where(
            lane == 4, w0, jnp.where(lane == 5, w1, 0.0))))))
    meta_ref[...] = meta[:, 0:META_COLS]


def _router(h, g, wg1, bg1, wg2, bg2, tm=256):
    n, d = h.shape
    wr = jnp.concatenate([wg1.astype(F32),
                          wg2.astype(F32).transpose(1, 0, 2).reshape(d, N_EXPERTS)], axis=1)
    wr = jnp.pad(wr, ((0, 0), (0, LANES - wr.shape[1])))
    br = jnp.pad(jnp.concatenate([bg1.astype(F32), bg2.astype(F32).reshape(-1)]),
                 (0, LANES - N_EXPERT_GROUPS - N_EXPERTS)).reshape(1, LANES)
    return pl.pallas_call(
        _router_body,
        grid=(n // tm,),
        in_specs=[pl.BlockSpec((tm, d), lambda i: (i, 0)),
                  pl.BlockSpec((1, d), lambda i: (0, 0)),
                  pl.BlockSpec((d, LANES), lambda i: (0, 0)),
                  pl.BlockSpec((1, LANES), lambda i: (0, 0))],
        out_specs=[pl.BlockSpec((tm, d), lambda i: (i, 0)),
                   pl.BlockSpec((tm, META_COLS), lambda i: (i, 0)),
                   pl.BlockSpec((1, LANES), lambda i: (0, 0))],
        out_shape=[jax.ShapeDtypeStruct((n, d), F32),
                   jax.ShapeDtypeStruct((n, META_COLS), F32),
                   jax.ShapeDtypeStruct((1, LANES), F32)],
        scratch_shapes=[pltpu.VMEM((1, LANES), F32)],
        compiler_params=_cparams(("arbitrary",)),
        name="moe_router",
    )(h, g.reshape(1, d).astype(F32), wr, br)


def _invert_body(dest_ref, rt_ref, *, ts):
    t = pl.program_id(0)

    def body(r, c):
        tok = t * ts + r
        rt_ref[dest_ref[0, 2 * r]] = tok
        rt_ref[dest_ref[0, 2 * r + 1]] = tok
        return c

    lax.fori_loop(0, ts, body, 0, unroll=8)


def _invert(dest, ts=512):
    n = dest.shape[0]
    dest3 = dest.reshape(n // ts, 1, 2 * ts)
    return pl.pallas_call(
        functools.partial(_invert_body, ts=ts),
        grid=(n // ts,),
        in_specs=[pl.BlockSpec((None, 1, 2 * ts), lambda t: (t, 0, 0), memory_space=pltpu.SMEM)],
        out_specs=pl.BlockSpec(memory_space=pltpu.SMEM),
        out_shape=jax.ShapeDtypeStruct((2 * n,), jnp.int32),
        compiler_params=_cparams(("arbitrary",)),
        name="moe_invert",
    )(dest3)


def _combine_body(dest_ref, h_ref, w_ref, y_hbm, o_ref, y0_ref, y1_ref, sem, *, ts):
    def issue(r, c):
        pltpu.make_async_copy(y_hbm.at[pl.ds(dest_ref[0, 2 * r], 1), :],
                              y0_ref.at[pl.ds(r, 1), :], sem.at[0]).start()
        pltpu.make_async_copy(y_hbm.at[pl.ds(dest_ref[0, 2 * r + 1], 1), :],
                              y1_ref.at[pl.ds(r, 1), :], sem.at[1]).start(priority=1)
        return c

    lax.fori_loop(0, ts, issue, 0, unroll=8)
    pltpu.make_async_copy(y_hbm.at[pl.ds(0, ts), :], y0_ref, sem.at[0]).wait()
    pltpu.make_async_copy(y_hbm.at[pl.ds(0, ts), :], y1_ref, sem.at[1]).wait()
    w = w_ref[...]
    o_ref[...] = h_ref[...] + w[:, 4:5] * y0_ref[...] + w[:, 5:6] * y1_ref[...]


def _combine(h, meta, y_sorted, dest, ts=256):
    n, d = h.shape
    dest3 = dest.reshape(n // ts, 1, 2 * ts)
    return pl.pallas_call(
        functools.partial(_combine_body, ts=ts),
        grid=(n // ts,),
        in_specs=[pl.BlockSpec((None, 1, 2 * ts), lambda t: (t, 0, 0), memory_space=pltpu.SMEM),
                  pl.BlockSpec((ts, d), lambda t: (t, 0)),
                  pl.BlockSpec((ts, META_COLS), lambda t: (t, 0)),
                  pl.BlockSpec(memory_space=pl.ANY)],
        out_specs=pl.BlockSpec((ts, d), lambda t: (t, 0)),
        out_shape=jax.ShapeDtypeStruct((n, d), F32),
        scratch_shapes=[pltpu.VMEM((ts, d), F32), pltpu.VMEM((ts, d), F32),
                        pltpu.SemaphoreType.DMA((2,))],
        compiler_params=_cparams(("arbitrary",)),
        name="moe_combine",
    )(dest3, h, meta, y_sorted)


EXPERT_TM = 256
EXPERT_VMEM_LIMIT = 56 * 1024 * 1024


def _expert_body(ie_ref, it_ref, lo_ref, hi_ref, first_ref, rfirst_ref, nxt_ref,
                 rt0_ref, rtn_ref, xn_hbm, wg_hbm, wu_hbm, wd_hbm, o_ref,
                 xbuf, stg_g, stg_u, stg_d, wgb, wub, wdb, sem_x, sem_w, *, tm, layer, n_tiles):
    i = pl.program_id(0)
    e = ie_ref[i]
    t = it_ref[i]
    lo = lo_ref[i]
    hi = hi_ref[i]
    slot = t % 2

    def weight_copies(ex):
        return (pltpu.make_async_copy(wg_hbm.at[layer, ex], stg_g, sem_w.at[0]),
                pltpu.make_async_copy(wu_hbm.at[layer, ex], stg_u, sem_w.at[1]),
                pltpu.make_async_copy(wd_hbm.at[layer, ex], stg_d, sem_w.at[2]))

    def gather_start(rt_ref, dst_slot):
        def issue(r2, c):
            for par in range(2):
                r = 2 * r2 + par
                pltpu.make_async_copy(xn_hbm.at[pl.ds(rt_ref[0, r], 1), :],
                                      xbuf.at[dst_slot, pl.ds(r, 1), :],
                                      sem_x.at[dst_slot]).start(priority=par)
            return c

        lax.fori_loop(0, tm // 2, issue, 0, unroll=4)

    @pl.when(i == 0)
    def _():
        for cp in weight_copies(e):
            cp.start()
        gather_start(rt0_ref, 0)

    @pl.when(rfirst_ref[i] == 1)
    def _():
        for cp in weight_copies(e):
            cp.wait()
        rows = 256

        def cast_rows(r, c):
            r0 = pl.multiple_of(r * rows, rows)
            wgb[pl.ds(r0, rows), :] = stg_g[pl.ds(r0, rows), :].astype(BF16)
            wub[pl.ds(r0, rows), :] = stg_u[pl.ds(r0, rows), :].astype(BF16)
            return c

        lax.fori_loop(0, stg_g.shape[0] // rows, cast_rows, 0)

        def cast_rows_d(r, c):
            r0 = pl.multiple_of(r * rows, rows)
            wdb[pl.ds(r0, rows), :] = stg_d[pl.ds(r0, rows), :].astype(BF16)
            return c

        lax.fori_loop(0, stg_d.shape[0] // rows, cast_rows_d, 0)

        @pl.when(nxt_ref[i] >= 0)
        def _():
            for cp in weight_copies(nxt_ref[i]):
                cp.start()

    @pl.when(first_ref[i] == 1)
    def _():
        @pl.when(t + 1 < n_tiles)
        def _():
            gather_start(rtn_ref, 1 - slot)

        pltpu.make_async_copy(xn_hbm.at[pl.ds(0, tm), :], xbuf.at[slot], sem_x.at[slot]).wait()

    @pl.when(hi > lo)
    def _():
        xb = xbuf[slot].astype(BF16)
        g = jnp.dot(xb, wgb[...], preferred_element_type=F32)
        u = jnp.dot(xb, wub[...], preferred_element_type=F32)
        hdn = (g * (1.0 / (1.0 + jnp.exp(-g)))) * u
        row = t * tm + lax.broadcasted_iota(jnp.int32, (tm, 1), 0)
        hdn = jnp.where((row >= lo) & (row < hi), hdn, 0.0)
        y = jnp.dot(hdn.astype(BF16), wdb[...], preferred_element_type=F32)

        @pl.when(first_ref[i] == 1)
        def _():
            o_ref[...] = y

        @pl.when(first_ref[i] != 1)
        def _():
            o_ref[...] = o_ref[...] + y


def _experts(xn, row_token, items, w_gate, w_up, w_down, layer, tm):
    _, d = xn.shape
    dh = w_gate.shape[-1]
    r_total = row_token.shape[0]
    n_tiles = r_total // tm
    n_items = items[0].shape[0]
    rt3 = row_token.reshape(n_tiles, 1, tm)
    grid_spec = pltpu.PrefetchScalarGridSpec(
        num_scalar_prefetch=7,
        grid=(n_items,),
        in_specs=[
            pl.BlockSpec((None, 1, tm), lambda i, *_: (0, 0, 0), memory_space=pltpu.SMEM),
            pl.BlockSpec((None, 1, tm), lambda i, ie, it, *_: (jnp.minimum(it[i] + 1, n_tiles - 1), 0, 0),
                         memory_space=pltpu.SMEM),
            pl.BlockSpec(memory_space=pl.ANY), pl.BlockSpec(memory_space=pl.ANY),
            pl.BlockSpec(memory_space=pl.ANY), pl.BlockSpec(memory_space=pl.ANY),
        ],
        out_specs=pl.BlockSpec((tm, d), lambda i, ie, it, *_: (it[i], 0)),
        scratch_shapes=[pltpu.VMEM((2, tm, d), F32),
                        pltpu.VMEM((d, dh), F32), pltpu.VMEM((d, dh), F32), pltpu.VMEM((dh, d), F32),
                        pltpu.VMEM((d, dh), BF16), pltpu.VMEM((d, dh), BF16), pltpu.VMEM((dh, d), BF16),
                        pltpu.SemaphoreType.DMA((2,)), pltpu.SemaphoreType.DMA((3,))],
    )
    return pl.pallas_call(
        functools.partial(_expert_body, tm=tm, layer=layer, n_tiles=n_tiles),
        grid_spec=grid_spec,
        out_shape=jax.ShapeDtypeStruct((r_total, d), F32),
        compiler_params=pltpu.CompilerParams(dimension_semantics=("arbitrary",),
                                             vmem_limit_bytes=EXPERT_VMEM_LIMIT),
        name="moe_experts",
    )(*items, rt3, rt3, xn, w_gate, w_up, w_down)


def _moe_plan(meta, counts_row, n_rows, tm):
    counts = counts_row[0, ROUTER_E0:ROUTER_E0 + N_EXPERTS].astype(jnp.int32)
    ends = jnp.cumsum(counts)
    starts = ends - counts
    eid = meta[:, 0:2].astype(jnp.int32)
    rank = meta[:, 2:4].astype(jnp.int32)
    dest = starts[eid] + rank

    first_tile = starts // tm
    last_tile = jnp.maximum(ends - 1, 0) // tm
    ntile_e = jnp.where(counts > 0, last_tile - first_tile + 1, 0)
    item_end = jnp.cumsum(ntile_e)
    item_start = item_end - ntile_e
    total = item_end[-1]
    n_items = n_rows // tm + N_EXPERTS - 1
    ii = jnp.arange(n_items, dtype=jnp.int32)
    valid = ii < total
    ii_c = jnp.minimum(ii, total - 1)
    e_i = jnp.searchsorted(item_end, ii_c, side='right').astype(jnp.int32)
    t_i = (first_tile[e_i] + (ii_c - item_start[e_i])).astype(jnp.int32)
    lo = jnp.where(valid, jnp.maximum(starts[e_i], t_i * tm), 0).astype(jnp.int32)
    hi = jnp.where(valid, jnp.minimum(ends[e_i], (t_i + 1) * tm), 0).astype(jnp.int32)
    minus1 = jnp.full((1,), -1, jnp.int32)
    first = (valid & (t_i != jnp.concatenate([minus1, t_i[:-1]]))).astype(jnp.int32)
    run_first = (valid & (e_i != jnp.concatenate([minus1, e_i[:-1]]))).astype(jnp.int32)
    ar = jnp.arange(N_EXPERTS, dtype=jnp.int32)
    later = (ar[None, :] > ar[:, None]) & (counts > 0)[None, :]
    nxt_e = jnp.min(jnp.where(later, ar[None, :], N_EXPERTS), axis=1)
    nxt_e = jnp.where(nxt_e == N_EXPERTS, -1, nxt_e).astype(jnp.int32)
    items = (e_i, t_i, lo, hi, first, run_first, nxt_e[e_i])
    return dest.astype(jnp.int32), items


def _hier_moe(h, g, wg1, bg1, wg2, bg2, w_gate, w_up, w_down, layer, tm=EXPERT_TM):
    n, _ = h.shape
    xn, meta, counts_row = _router(h, g, wg1, bg1, wg2, bg2)
    dest, items = _moe_plan(meta, counts_row, 2 * n, tm)
    y_sorted = _experts(xn, _invert(dest), items, w_gate, w_up, w_down, layer, tm)
    return _combine(h, meta, y_sorted, dest)


def _time_permute(h, batch, inverse=False):
    n, d = h.shape
    l = n // batch
    shape = (batch, l // S5_SEGMENTS, S5_SEGMENTS, d) if inverse else (batch, S5_SEGMENTS, l // S5_SEGMENTS, d)
    return h.reshape(shape).transpose(0, 2, 1, 3).reshape(n, d)


def kernel(x, norm_mix_g, norm_ffn_g, ssm_lam_re, ssm_lam_im, ssm_b_re, ssm_b_im, ssm_c_re, ssm_c_im, ssm_d, ssm_log_step, ssm_w_glu1, ssm_w_glu2, kv_norm_g, w_kv, k_norm_g, w_q, q_norm_g, lambda_qk, subln_g, w_o, rel_bias, router_group_w, router_group_b, router_expert_w, router_expert_b, w_gate, w_up, w_down):
    batch, l, d = x.shape
    n = batch * l
    h = _time_permute(x.astype(F32).reshape(n, d), batch)
    kn = vb = bias_tabs = None
    for layer in range(DEPTH):
        if layer < N_A_LAYERS:
            xn = _rmsnorm(h, norm_mix_g[layer], BF16)
            b_blk, c_blk, lam_v = _s5_params(ssm_lam_re[layer], ssm_lam_im[layer], ssm_b_re[layer],
                                             ssm_b_im[layer], ssm_c_re[layer], ssm_c_im[layer],
                                             ssm_log_step[layer])
            z = _s5_scan(xn, b_blk, c_blk, lam_v, ssm_d[layer], batch)
            h = _matmul(z, [ssm_w_glu1, ssm_w_glu2], layer, F32, res=h)
        else:
            j = layer - N_A_LAYERS
            lam_init = 0.8 - 0.6 * math.exp(-0.3 * layer)
            xn = _rmsnorm(h, norm_mix_g[layer], BF16)
            qn = _matmul(xn, [w_q], j, BF16, head_norm=(q_norm_g[j], HEAD_DIM ** -0.5 * LOG2E),
                         tn=PROJ_TN)
            o = _diff_attention(qn, kn, vb, bias_tabs, lambda_qk[j], subln_g[j], lam_init, batch)
            h = _matmul(o, [w_o], j, F32, res=h, tn=PROJ_TN)
        h = _hier_moe(h, norm_ffn_g[layer], router_group_w[layer], router_group_b[layer],
                      router_expert_w[layer], router_expert_b[layer], w_gate, w_up, w_down, layer)
        if layer == N_A_LAYERS - 1:
            h = _time_permute(h, batch, inverse=True)
            xkv = _rmsnorm(h, kv_norm_g, BF16)
            w_kv3 = w_kv.reshape(1, d, 2 * d)
            kn = _matmul(xkv, [w_kv3], 0, BF16, n_out=d, head_norm=(k_norm_g, 1.0), tn=PROJ_TN)
            vb = _matmul(xkv, [w_kv3], 0, BF16, col_off=d // PROJ_TN, n_out=d, tn=PROJ_TN)
            bias_tabs = _attn_bias_tables(rel_bias, ATTN_TQ)
    return h.reshape(batch, l, d)
```

```python
import functools
import math

import jax
import jax.numpy as jnp
from jax import lax
from jax.experimental import pallas as pl
from jax.experimental.pallas import tpu as pltpu

F32 = jnp.float32
BF16 = jnp.bfloat16

DEPTH = 4
N_A_LAYERS = DEPTH // 2
SSM_GROUP = 16
SSM_STATE = 64
HEAD_DIM = 64
REL_BUCKETS = 32
REL_MAX_EXACT = REL_BUCKETS // 2
REL_MAX_DISTANCE = 128
N_EXPERT_GROUPS = 4
EXPERTS_PER_GROUP = 4
N_EXPERTS = N_EXPERT_GROUPS * EXPERTS_PER_GROUP
RMS_EPS = 1e-6

LANES = 128
SUBLANES = 8
VMEM_LIMIT = 48 * 1024 * 1024
NEG_BIG = -1e30

S5_SEGMENTS = SUBLANES
S5_GROUPS_PER_BLOCK = 16
S5_TJ = 32
PROJ_TN = 1024


def _cparams(sem):
    return pltpu.CompilerParams(dimension_semantics=sem, vmem_limit_bytes=VMEM_LIMIT)


def _rmsnorm_body(h_ref, g_ref, o_ref):
    x = h_ref[...]
    ms = jnp.mean(x * x, axis=-1, keepdims=True)
    o_ref[...] = (x * lax.rsqrt(ms + RMS_EPS) * g_ref[...]).astype(o_ref.dtype)


def _rmsnorm(h, g, out_dtype, tm=512):
    n, d = h.shape
    return pl.pallas_call(
        _rmsnorm_body,
        grid=(n // tm,),
        in_specs=[pl.BlockSpec((tm, d), lambda i: (i, 0)),
                  pl.BlockSpec((1, d), lambda i: (0, 0))],
        out_specs=pl.BlockSpec((tm, d), lambda i: (i, 0)),
        out_shape=jax.ShapeDtypeStruct((n, d), out_dtype),
        compiler_params=_cparams(("arbitrary",)),
        name="rmsnorm",
    )(h, g.reshape(1, d))


def _mm_body(*refs, n_w, has_res, hn_scale):
    a_ref = refs[0]
    w_refs = refs[1:1 + n_w]
    pos = 1 + n_w
    res_ref = refs[pos] if has_res else None
    pos += int(has_res)
    g_ref = refs[pos] if hn_scale is not None else None
    pos += int(hn_scale is not None)
    o_ref = refs[pos]
    wb_refs = refs[pos + 1:]

    @pl.when(pl.program_id(1) == 0)
    def _():
        for w_ref, wb_ref in zip(w_refs, wb_refs):
            wb_ref[...] = w_ref[...].astype(BF16)

    a = a_ref[...]
    y = jnp.dot(a, wb_refs[0][...], preferred_element_type=F32)
    if n_w == 2:
        y2 = jnp.dot(a, wb_refs[1][...], preferred_element_type=F32)
        y = y * (1.0 / (1.0 + jnp.exp(-y2)))
    if has_res:
        y = y + res_ref[...]
    if hn_scale is None:
        o_ref[...] = y.astype(o_ref.dtype)
    else:
        tm, tn = y.shape
        lane = lax.broadcasted_iota(jnp.int32, (tm, LANES), 1)
        lo = lane < HEAD_DIM
        g = g_ref[...]
        for hb in range(tn // LANES):
            x = y[:, hb * LANES:(hb + 1) * LANES]
            x2 = x * x
            s_lo = jnp.sum(jnp.where(lo, x2, 0.0), axis=-1, keepdims=True)
            s_hi = jnp.sum(jnp.where(lo, 0.0, x2), axis=-1, keepdims=True)
            r = jnp.where(lo, lax.rsqrt(s_lo * (1.0 / HEAD_DIM) + RMS_EPS),
                          lax.rsqrt(s_hi * (1.0 / HEAD_DIM) + RMS_EPS))
            o_ref[:, hb * LANES:(hb + 1) * LANES] = ((x * r * g) * hn_scale).astype(o_ref.dtype)


def _matmul(a, ws, layer, out_dtype, res=None, col_off=0, n_out=None, head_norm=None,
            tm=512, tn=512):
    m, k = a.shape
    n_out = ws[0].shape[-1] if n_out is None else n_out
    n_w = len(ws)
    in_specs = [pl.BlockSpec((tm, k), lambda j, i: (i, 0))]
    for _ in ws:
        in_specs.append(pl.BlockSpec((None, k, tn), lambda j, i: (layer, 0, j + col_off)))
    args = [a, *ws]
    if res is not None:
        in_specs.append(pl.BlockSpec((tm, tn), lambda j, i: (i, j)))
        args.append(res)
    hn_scale = None
    if head_norm is not None:
        gain, hn_scale = head_norm
        in_specs.append(pl.BlockSpec((1, LANES), lambda j, i: (0, 0)))
        args.append(jnp.concatenate([gain, gain]).reshape(1, LANES).astype(F32))
    return pl.pallas_call(
        functools.partial(_mm_body, n_w=n_w, has_res=res is not None, hn_scale=hn_scale),
        grid=(n_out // tn, m // tm),
        in_specs=in_specs,
        out_specs=pl.BlockSpec((tm, tn), lambda j, i: (i, j)),
        out_shape=jax.ShapeDtypeStruct((m, n_out), out_dtype),
        scratch_shapes=[pltpu.VMEM((k, tn), BF16) for _ in ws],
        compiler_params=_cparams(("arbitrary", "arbitrary")),
        name="dense_matmul",
    )(*args)


def _gelu_tanh(y):
    c = math.sqrt(2.0 / math.pi)
    return y * (0.5 * (1.0 + jnp.tanh(c * (y + 0.044715 * (y * y * y)))))


def _s5_body(x_ref, b_ref, c_ref, lam_ref, d_ref, z_ref, bu0_ref, bu1_ref, st0_ref, st1_ref,
             e_ref, init_ref, *, seg_len):
    ch = lam_ref.shape[-1]
    rows = S5_TJ * S5_SEGMENTS
    n_tiles = seg_len // S5_TJ
    lr = jnp.broadcast_to(lam_ref[0:1, :], (S5_SEGMENTS, ch))
    li = jnp.broadcast_to(lam_ref[1:2, :], (S5_SEGMENTS, ch))

    def row0(t):
        return pl.multiple_of(jnp.clip(t, 0, n_tiles - 1) * rows, rows)

    def bu_tile(t, bu_ref):
        bu_ref[...] = jnp.dot(x_ref[pl.ds(row0(t), rows), :], b_ref[...],
                              preferred_element_type=F32)

    def scan_tile(bu_ref, s_re, s_im, st_ref):
        for j in range(S5_TJ):
            sl = slice(S5_SEGMENTS * j, S5_SEGMENTS * (j + 1))
            n_re = lr * s_re - li * s_im + bu_ref[sl, 0:ch]
            n_im = lr * s_im + li * s_re + bu_ref[sl, ch:2 * ch]
            s_re, s_im = n_re, n_im
            if st_ref is not None:
                st_ref[sl, 0:ch] = s_re
                st_ref[sl, ch:2 * ch] = s_im
        return s_re, s_im

    def project(t, st_ref):
        r0 = row0(t)
        y = jnp.dot(st_ref[...].astype(BF16), c_ref[...], preferred_element_type=F32)
        y = y + d_ref[...] * x_ref[pl.ds(r0, rows), :].astype(F32)
        z_ref[pl.ds(r0, rows), :] = _gelu_tanh(y).astype(z_ref.dtype)

    def pass1(i2, c):
        t = 2 * i2
        bu_tile(t + 1, bu1_ref)
        c = scan_tile(bu0_ref, c[0], c[1], None)
        bu_tile(t + 2, bu0_ref)
        return scan_tile(bu1_ref, c[0], c[1], None)

    zeros = jnp.zeros((S5_SEGMENTS, ch), F32)
    bu_tile(0, bu0_ref)
    e_re, e_im = lax.fori_loop(0, n_tiles // 2, pass1, (zeros, zeros))
    e_ref[:, 0:ch] = e_re
    e_ref[:, ch:2 * ch] = e_im

    pr, pi = lam_ref[0:1, :], lam_ref[1:2, :]
    for _ in range(seg_len.bit_length() - 1):
        pr, pi = pr * pr - pi * pi, 2.0 * pr * pi
    cr = jnp.zeros((1, ch), F32)
    ci = jnp.zeros((1, ch), F32)
    init_ref[0:1, :] = jnp.zeros((1, 2 * ch), F32)
    for k in range(S5_SEGMENTS - 1):
        er = e_ref[k:k + 1, 0:ch]
        ei = e_ref[k:k + 1, ch:2 * ch]
        cr, ci = pr * cr - pi * ci + er, pr * ci + pi * cr + ei
        init_ref[k + 1:k + 2, 0:ch] = cr
        init_ref[k + 1:k + 2, ch:2 * ch] = ci

    def pass2(i2, c):
        t = 2 * i2
        bu_tile(t + 1, bu1_ref)
        c = scan_tile(bu0_ref, c[0], c[1], st0_ref)
        project(t - 1, st1_ref)
        bu_tile(t + 2, bu0_ref)
        c = scan_tile(bu1_ref, c[0], c[1], st1_ref)
        project(t, st0_ref)
        return c

    st1_ref[...] = jnp.zeros(st1_ref.shape, F32)
    bu_tile(0, bu0_ref)
    lax.fori_loop(0, n_tiles // 2, pass2, (init_ref[:, 0:ch], init_ref[:, ch:2 * ch]))
    project(n_tiles - 1, st1_ref)


def _s5_params(lam_re, lam_im, b_re, b_im, c_re, c_im, log_step):
    g, p = lam_re.shape
    gb = S5_GROUPS_PER_BLOCK
    nb = g // gb
    lam = lax.complex(lam_re.astype(F32), lam_im.astype(F32))
    delta = jnp.exp(log_step.astype(F32))[:, None]
    lam_bar = jnp.exp(lam * delta)
    b_bar = ((lam_bar - 1.0) / lam)[..., None] * lax.complex(b_re.astype(F32), b_im.astype(F32))
    eye = jnp.eye(gb, dtype=F32)

    def blk_b(part):
        part = part.reshape(nb, gb, p, SSM_GROUP)
        return jnp.einsum('cgph,gk->cghkp', part, eye).reshape(nb, gb * SSM_GROUP, gb * p)

    def blk_c(part):
        part = part.reshape(nb, gb, SSM_GROUP, p)
        return jnp.einsum('cghp,gk->cgpkh', part, eye).reshape(nb, gb * p, gb * SSM_GROUP)

    b_blk = jnp.concatenate([blk_b(jnp.real(b_bar)), blk_b(jnp.imag(b_bar))], axis=-1)
    c_blk = jnp.concatenate([blk_c(c_re.astype(F32)), blk_c(-c_im.astype(F32))], axis=1)
    lam_v = jnp.stack([jnp.real(lam_bar).reshape(nb, gb * p),
                       jnp.imag(lam_bar).reshape(nb, gb * p)], axis=1)
    return b_blk.astype(BF16), c_blk.astype(BF16), lam_v


def _s5_scan(xn, b_blk, c_blk, lam_v, d_skip, batch):
    n, d = xn.shape
    l = n // batch
    seg_len = l // S5_SEGMENTS
    nb, fb, ch2 = b_blk.shape
    rows = S5_TJ * S5_SEGMENTS
    return pl.pallas_call(
        functools.partial(_s5_body, seg_len=seg_len),
        grid=(batch, nb),
        in_specs=[pl.BlockSpec((l, fb), lambda b, c: (b, c)),
                  pl.BlockSpec((None, fb, ch2), lambda b, c: (c, 0, 0)),
                  pl.BlockSpec((None, ch2, fb), lambda b, c: (c, 0, 0)),
                  pl.BlockSpec((None, 2, ch2 // 2), lambda b, c: (c, 0, 0)),
                  pl.BlockSpec((1, fb), lambda b, c: (0, c))],
        out_specs=pl.BlockSpec((l, fb), lambda b, c: (b, c)),
        out_shape=jax.ShapeDtypeStruct((n, d), BF16),
        scratch_shapes=[pltpu.VMEM((rows, ch2), F32), pltpu.VMEM((rows, ch2), F32),
                        pltpu.VMEM((rows, ch2), F32), pltpu.VMEM((rows, ch2), F32),
                        pltpu.VMEM((S5_SEGMENTS, ch2), F32), pltpu.VMEM((S5_SEGMENTS, ch2), F32)],
        compiler_params=_cparams(("arbitrary", "arbitrary")),
        name="s5_scan",
    )(xn, b_blk, c_blk, lam_v, d_skip.reshape(1, d).astype(F32))


def _t5_bucket(n):
    n_safe = jnp.maximum(n, 1).astype(F32)
    large = REL_MAX_EXACT + (jnp.log(n_safe / REL_MAX_EXACT)
                             / math.log(REL_MAX_DISTANCE / REL_MAX_EXACT)
                             * (REL_BUCKETS - REL_MAX_EXACT)).astype(jnp.int32)
    large = jnp.minimum(large, REL_BUCKETS - 1)
    return jnp.where(n < REL_MAX_EXACT, n, large)


ATTN_TQ = 256
M_INIT = -1e29
LOG2E = 1.4426950408889634


def _attn_bias_tables(rel_bias, tq):
    qi = jnp.arange(tq, dtype=jnp.int32)[:, None]
    ki = jnp.arange(tq, dtype=jnp.int32)[None, :]
    d_diag = qi - ki
    rb = rel_bias.astype(F32)
    rb = (rb - rb[REL_BUCKETS - 1][None, :]) * LOG2E

    def lookup(dist):
        onehot = jax.nn.one_hot(_t5_bucket(dist), REL_BUCKETS, dtype=F32)
        return jnp.einsum('qkb,bh->hqk', onehot, rb, precision=lax.Precision.HIGHEST)

    b_diag = jnp.where((d_diag >= 0)[None], lookup(jnp.maximum(d_diag, 0)), NEG_BIG)
    b_prev = lookup(d_diag + tq)
    masked = jnp.full_like(b_diag, NEG_BIG)
    return jnp.stack([jnp.concatenate([b_prev, b_diag], axis=-1),
                      jnp.concatenate([b_diag, masked], axis=-1)], axis=1)


def _attn_pair_tables(nq):
    def framed(pairs):
        idle = (0, nq, 0, 0)
        rows = [idle, idle] + pairs + [idle, idle]
        if len(rows) % 2:
            rows.append(idle)
        return rows

    wide = [(i, i, 2 * j, 0) for i in range(nq) for j in range((i - 1) // 2)]
    odd = [(i, i, i - 2, 0) for i in range(2, nq) if (i - 1) % 2]
    near = [(i, i, max(i - 1, 0), 0 if i else 1) for i in range(nq)]
    lists = [framed(wide), framed(odd), framed(near)]
    cols = list(zip(*(lists[0] + lists[1] + lists[2])))
    return [jnp.asarray(c, jnp.int32) for c in cols], [len(x) for x in lists]


def _attn_body(qrow_ref, slot_ref, kt_ref, bidx_ref, q_ref, k_ref, v_ref, bias_ref, lq_ref, sg_ref,
               o_ref, m_ref, acc_ref, vaug_ref, s0_ref, s1_ref, p0_ref, p1_ref, a0_ref, a1_ref,
               sw0_ref, sw1_ref, pw0_ref, pw1_ref, *, tq, nq, n_rows, lam_init):
    m_ref[...] = jnp.full(m_ref.shape, M_INIT, F32)
    acc_ref[...] = jnp.zeros(acc_ref.shape, F32)
    vaug_ref[:, 0:LANES] = v_ref[...]
    vaug_ref[:, LANES:2 * LANES] = jnp.ones(v_ref.shape, BF16)
    lane = lax.broadcasted_iota(jnp.int32, (tq, LANES), 1)

    def scores(e, s_ref, tk):
        q = q_ref[pl.ds(pl.multiple_of(qrow_ref[e] * tq, tq), tq), :]
        zero = jnp.zeros_like(q)
        qq = jnp.concatenate([jnp.where(lane < HEAD_DIM, q, zero),
                              jnp.where(lane < HEAD_DIM, zero, q)], axis=0)
        k = k_ref[pl.ds(pl.multiple_of(kt_ref[e] * tq, tq), tk), :]
        s_ref[...] = lax.dot_general(qq, k, (((1,), (1,)), ((), ())), preferred_element_type=F32)

    def softmax(e, s_ref, a_ref, p_ref, with_bias, rc):
        m_view = m_ref.at[slot_ref[e]]
        b_view = bias_ref.at[bidx_ref[e]]
        for c in range(tq // rc):
            b = b_view[c * rc:(c + 1) * rc, :] if with_bias else None
            for half in range(2):
                rows = slice(half * tq + c * rc, half * tq + (c + 1) * rc)
                s = s_ref[rows, :]
                if with_bias:
                    s = s + b
                m_prev = m_view[rows, :]
                m_next = jnp.maximum(m_prev, jnp.max(s, axis=1, keepdims=True))
                a_ref[rows, :] = jnp.exp2(m_prev - m_next)
                p_ref[rows, :] = jnp.exp2(s - m_next[:, 0:1]).astype(BF16)
                m_view[rows, :] = m_next

    def values(e, a_ref, p_ref, tk):
        acc = acc_ref.at[slot_ref[e]]
        v = vaug_ref[pl.ds(pl.multiple_of(kt_ref[e] * tq, tq), tk), :]
        pv = jnp.dot(p_ref[...], v, preferred_element_type=F32)
        a = a_ref[...]
        acc[:, 0:LANES] = acc[:, 0:LANES] * a + pv[:, 0:LANES]
        acc[:, LANES:2 * LANES] = acc[:, LANES:2 * LANES] * a + pv[:, LANES:2 * LANES]

    def run_pipeline(base, rows_n, with_bias, tk, sa_ref, sb_ref, pa_ref, pb_ref):
        rc = 64 if tk == tq else 32
        pa_ref[...] = jnp.zeros(pa_ref.shape, BF16)
        a0_ref[...] = jnp.ones(a0_ref.shape, F32)
        scores(base + 1, sb_ref, tk)

        def two_iterations(i2, c):
            it = base + 2 * i2
            scores(it + 2, sa_ref, tk)
            softmax(it + 1, sb_ref, a1_ref, pb_ref, with_bias, rc)
            values(it, a0_ref, pa_ref, tk)
            scores(it + 3, sb_ref, tk)
            softmax(it + 2, sa_ref, a0_ref, pa_ref, with_bias, rc)
            values(it + 1, a1_ref, pb_ref, tk)
            return c

        lax.fori_loop(0, (rows_n - 2) // 2, two_iterations, 0)

    n_wide, n_odd, n_near = n_rows
    run_pipeline(0, n_wide, False, 2 * tq, sw0_ref, sw1_ref, pw0_ref, pw1_ref)
    run_pipeline(n_wide, n_odd, False, tq, s0_ref, s1_ref, p0_ref, p1_ref)
    run_pipeline(n_wide + n_odd, n_near, True, 2 * tq, sw0_ref, sw1_ref, pw0_ref, pw1_ref)

    lq = lq_ref[...]
    lam = (jnp.exp(jnp.sum(lq[0:1] * lq[1:2], axis=1, keepdims=True))
           - jnp.exp(jnp.sum(lq[2:3] * lq[3:4], axis=1, keepdims=True)) + lam_init)

    def finalize(i, c):
        acc = acc_ref[i]
        o_all = acc[:, 0:LANES] / acc[:, LANES:2 * LANES]
        o = o_all[0:tq] - lam * o_all[tq:2 * tq]
        ms = jnp.mean(o * o, axis=-1, keepdims=True)
        o = (o * lax.rsqrt(ms + RMS_EPS) * sg_ref[...]) * (1.0 - lam_init)
        o_ref[pl.ds(pl.multiple_of(i * tq, tq), tq), :] = o.astype(o_ref.dtype)
        return c

    lax.fori_loop(0, nq, finalize, 0)


def _diff_attention(qn, kn, vb, bias_tabs, lambda_qk, subln_g, lam_init, batch):
    n, d = qn.shape
    l = n // batch
    tq = ATTN_TQ
    nq = l // tq
    nh = d // LANES
    tables, n_rows = _attn_pair_tables(nq)
    row_blk = pl.BlockSpec((l, LANES), lambda b, h, *_: (b, h))
    grid_spec = pltpu.PrefetchScalarGridSpec(
        num_scalar_prefetch=4,
        grid=(batch, nh),
        in_specs=[row_blk, row_blk, row_blk,
                  pl.BlockSpec((None, 2, tq, 2 * tq), lambda b, h, *_: (h, 0, 0, 0)),
                  pl.BlockSpec((4, HEAD_DIM), lambda b, h, *_: (0, 0)),
                  pl.BlockSpec((1, LANES), lambda b, h, *_: (0, 0))],
        out_specs=row_blk,
        scratch_shapes=[pltpu.VMEM((nq + 1, 2 * tq, LANES), F32),
                        pltpu.VMEM((nq + 1, 2 * tq, 2 * LANES), F32),
                        pltpu.VMEM((l, 2 * LANES), BF16),
                        pltpu.VMEM((2 * tq, tq), F32), pltpu.VMEM((2 * tq, tq), F32),
                        pltpu.VMEM((2 * tq, tq), BF16), pltpu.VMEM((2 * tq, tq), BF16),
                        pltpu.VMEM((2 * tq, LANES), F32), pltpu.VMEM((2 * tq, LANES), F32),
                        pltpu.VMEM((2 * tq, 2 * tq), F32), pltpu.VMEM((2 * tq, 2 * tq), F32),
                        pltpu.VMEM((2 * tq, 2 * tq), BF16), pltpu.VMEM((2 * tq, 2 * tq), BF16)],
    )
    return pl.pallas_call(
        functools.partial(_attn_body, tq=tq, nq=nq, n_rows=tuple(n_rows), lam_init=lam_init),
        grid_spec=grid_spec,
        out_shape=jax.ShapeDtypeStruct((n, d), BF16),
        compiler_params=_cparams(("arbitrary", "arbitrary")),
        name="diff_attention",
    )(*tables, qn, kn, vb, bias_tabs, lambda_qk.astype(F32),
      subln_g.reshape(1, LANES).astype(F32))


ROUTER_E0 = N_EXPERT_GROUPS
META_COLS = 8


def _router_body(h_ref, g_ref, wr_ref, br_ref, xn_ref, meta_ref, cnt_ref, base_ref):
    tm = h_ref.shape[0]

    @pl.when(pl.program_id(0) == 0)
    def _():
        base_ref[...] = jnp.zeros(base_ref.shape, F32)

    x = h_ref[...]
    ms = jnp.mean(x * x, axis=-1, keepdims=True)
    xn = x * lax.rsqrt(ms + RMS_EPS) * g_ref[...]
    xn_ref[...] = xn.astype(xn_ref.dtype)
    x_hi = xn.astype(BF16)
    x_lo = (xn - x_hi.astype(F32)).astype(BF16)
    logits = (jnp.dot(x_hi, wr_ref[0], preferred_element_type=F32)
              + (jnp.dot(x_hi, wr_ref[1], preferred_element_type=F32)
                 + jnp.dot(x_lo, wr_ref[0], preferred_element_type=F32))) + br_ref[...]
    lane = lax.broadcasted_iota(jnp.int32, (tm, LANES), 1)
    neg_inf = -jnp.inf

    def first_argmax(vals):
        vmax = jnp.max(vals, axis=-1, keepdims=True)
        idx = jnp.min(jnp.where(vals == vmax, lane, LANES), axis=-1, keepdims=True)
        return vmax, idx

    is_g = lane < N_EXPERT_GROUPS
    gmax, g_idx = first_argmax(jnp.where(is_g, logits, neg_inf))
    g_w = 1.0 / jnp.sum(jnp.where(is_g, jnp.exp(logits - gmax), 0.0), axis=-1, keepdims=True)
    e_lo = ROUTER_E0 + EXPERTS_PER_GROUP * g_idx
    elog = jnp.where((lane >= e_lo) & (lane < e_lo + EXPERTS_PER_GROUP), logits, neg_inf)
    v0, i0 = first_argmax(elog)
    v1, i1 = first_argmax(jnp.where(lane == i0, neg_inf, elog))
    t = jnp.exp(v1 - v0)
    w0 = g_w / (1.0 + t)
    w1 = g_w * t / (1.0 + t)

    sel0 = lane == i0
    sel1 = lane == i1
    onehot = jnp.where(sel0 | sel1, 1.0, 0.0)
    r_i = lax.broadcasted_iota(jnp.int32, (tm, tm), 0)
    c_i = lax.broadcasted_iota(jnp.int32, (tm, tm), 1)
    tri = jnp.where(c_i < r_i, 1.0, 0.0).astype(BF16)
    before = jnp.dot(tri, onehot.astype(BF16), preferred_element_type=F32) + base_ref[...]
    rank0 = jnp.sum(jnp.where(sel0, before, 0.0), axis=-1, keepdims=True)
    rank1 = jnp.sum(jnp.where(sel1, before, 0.0), axis=-1, keepdims=True)
    base_ref[...] = base_ref[...] + jnp.sum(onehot, axis=0, keepdims=True)
    cnt_ref[...] = base_ref[...]

    eid0 = (i0 - ROUTER_E0).astype(F32)
    eid1 = (i1 - ROUTER_E0).astype(F32)
    meta = jnp.where(lane == 0, eid0, jnp.where(lane == 1, eid1, jnp.where(
        lane == 2, rank0, jnp.where(lane == 3, rank1, jnp.where(
            lane == 4, w0, jnp.where(lane == 5, w1, 0.0))))))
    meta_ref[...] = meta[:, 0:META_COLS]


def _router(h, g, wg1, bg1, wg2, bg2, tm=256):
    n, d = h.shape
    wr = jnp.concatenate([wg1.astype(F32),
                          wg2.astype(F32).transpose(1, 0, 2).reshape(d, N_EXPERTS)], axis=1)
    wr = jnp.pad(wr, ((0, 0), (0, LANES - wr.shape[1])))
    wr_hi = wr.astype(BF16)
    wr = jnp.stack([wr_hi, (wr - wr_hi.astype(F32)).astype(BF16)])
    br = jnp.pad(jnp.concatenate([bg1.astype(F32), bg2.astype(F32).reshape(-1)]),
                 (0, LANES - N_EXPERT_GROUPS - N_EXPERTS)).reshape(1, LANES)
    return pl.pallas_call(
        _router_body,
        grid=(n // tm,),
        in_specs=[pl.BlockSpec((tm, d), lambda i: (i, 0)),
                  pl.BlockSpec((1, d), lambda i: (0, 0)),
                  pl.BlockSpec((2, d, LANES), lambda i: (0, 0, 0)),
                  pl.BlockSpec((1, LANES), lambda i: (0, 0))],
        out_specs=[pl.BlockSpec((tm, d), lambda i: (i, 0)),
                   pl.BlockSpec((tm, META_COLS), lambda i: (i, 0)),
                   pl.BlockSpec((1, LANES), lambda i: (0, 0))],
        out_shape=[jax.ShapeDtypeStruct((n, d), F32),
                   jax.ShapeDtypeStruct((n, META_COLS), F32),
                   jax.ShapeDtypeStruct((1, LANES), F32)],
        scratch_shapes=[pltpu.VMEM((1, LANES), F32)],
        compiler_params=_cparams(("arbitrary",)),
        name="moe_router",
    )(h, g.reshape(1, d).astype(F32), wr, br)


def _invert_body(dest_ref, rt_ref, *, ts):
    t = pl.program_id(0)

    def body(r, c):
        tok = t * ts + r
        rt_ref[dest_ref[0, 2 * r]] = tok
        rt_ref[dest_ref[0, 2 * r + 1]] = tok
        return c

    lax.fori_loop(0, ts, body, 0, unroll=8)


def _invert(dest, ts=512):
    n = dest.shape[0]
    dest3 = dest.reshape(n // ts, 1, 2 * ts)
    return pl.pallas_call(
        functools.partial(_invert_body, ts=ts),
        grid=(n // ts,),
        in_specs=[pl.BlockSpec((None, 1, 2 * ts), lambda t: (t, 0, 0), memory_space=pltpu.SMEM)],
        out_specs=pl.BlockSpec(memory_space=pltpu.SMEM),
        out_shape=jax.ShapeDtypeStruct((2 * n,), jnp.int32),
        compiler_params=_cparams(("arbitrary",)),
        name="moe_invert",
    )(dest3)


def _combine_body(d0_ref, dn_ref, h_ref, w_ref, y_hbm, *rest, ts, nt, with_norm):
    if with_norm:
        g_ref, o_ref, xn_ref, y0_ref, y1_ref, sem = rest
    else:
        o_ref, y0_ref, y1_ref, sem = rest
    t = pl.program_id(0)
    slot = t % 2

    def gather_start(dest_ref, dst):
        def issue(r, c):
            pltpu.make_async_copy(y_hbm.at[pl.ds(dest_ref[0, 2 * r], 1), :],
                                  y0_ref.at[dst, pl.ds(r, 1), :], sem.at[dst, 0]).start()
            pltpu.make_async_copy(y_hbm.at[pl.ds(dest_ref[0, 2 * r + 1], 1), :],
                                  y1_ref.at[dst, pl.ds(r, 1), :], sem.at[dst, 1]).start(priority=1)
            return c

        lax.fori_loop(0, ts, issue, 0, unroll=8)

    @pl.when(t == 0)
    def _():
        gather_start(d0_ref, 0)

    @pl.when(t + 1 < nt)
    def _():
        gather_start(dn_ref, 1 - slot)

    pltpu.make_async_copy(y_hbm.at[pl.ds(0, ts), :], y0_ref.at[slot], sem.at[slot, 0]).wait()
    pltpu.make_async_copy(y_hbm.at[pl.ds(0, ts), :], y1_ref.at[slot], sem.at[slot, 1]).wait()
    w = w_ref[...]
    out = h_ref[...] + w[:, 4:5] * y0_ref[slot] + w[:, 5:6] * y1_ref[slot]
    o_ref[...] = out
    if with_norm:
        ms = jnp.mean(out * out, axis=-1, keepdims=True)
        xn_ref[...] = (out * lax.rsqrt(ms + RMS_EPS) * g_ref[...]).astype(xn_ref.dtype)


def _combine(h, meta, y_sorted, dest, next_norm_g=None, ts=256):
    n, d = h.shape
    nt = n // ts
    dest3 = dest.reshape(nt, 1, 2 * ts)
    with_norm = next_norm_g is not None
    row_blk = pl.BlockSpec((ts, d), lambda t: (t, 0))
    in_specs = [pl.BlockSpec((None, 1, 2 * ts), lambda t: (0, 0, 0), memory_space=pltpu.SMEM),
                pl.BlockSpec((None, 1, 2 * ts), lambda t: (jnp.minimum(t + 1, nt - 1), 0, 0),
                             memory_space=pltpu.SMEM),
                row_blk,
                pl.BlockSpec((ts, META_COLS), lambda t: (t, 0)),
                pl.BlockSpec(memory_space=pl.ANY)]
    args = [dest3, dest3, h, meta, y_sorted]
    out_specs, out_shape = row_blk, jax.ShapeDtypeStruct((n, d), F32)
    if with_norm:
        in_specs.append(pl.BlockSpec((1, d), lambda t: (0, 0)))
        args.append(next_norm_g.reshape(1, d).astype(F32))
        out_specs = [row_blk, row_blk]
        out_shape = [out_shape, jax.ShapeDtypeStruct((n, d), BF16)]
    return pl.pallas_call(
        functools.partial(_combine_body, ts=ts, nt=nt, with_norm=with_norm),
        grid=(nt,),
        in_specs=in_specs,
        out_specs=out_specs,
        out_shape=out_shape,
        scratch_shapes=[pltpu.VMEM((2, ts, d), F32), pltpu.VMEM((2, ts, d), F32),
                        pltpu.SemaphoreType.DMA((2, 2))],
        compiler_params=_cparams(("arbitrary",)),
        name="moe_combine",
    )(*args)


EXPERT_TM = 256
EXPERT_VMEM_LIMIT = 56 * 1024 * 1024


def _expert_body(ie_ref, it_ref, lo_ref, hi_ref, first_ref, rfirst_ref, nxt_ref,
                 rt0_ref, rtn_ref, xn_hbm, wg_hbm, wu_hbm, wd_hbm, o_ref,
                 xbuf, stg_g, stg_u, stg_d, wgb, wub, wdb, sem_x, sem_w, *, tm, layer, n_tiles):
    i = pl.program_id(0)
    e = ie_ref[i]
    t = it_ref[i]
    lo = lo_ref[i]
    hi = hi_ref[i]
    slot = t % 2

    def weight_copies(ex):
        return (pltpu.make_async_copy(wg_hbm.at[layer, ex], stg_g, sem_w.at[0]),
                pltpu.make_async_copy(wu_hbm.at[layer, ex], stg_u, sem_w.at[1]),
                pltpu.make_async_copy(wd_hbm.at[layer, ex], stg_d, sem_w.at[2]))

    def gather_start(rt_ref, dst_slot):
        def issue(r2, c):
            for par in range(2):
                r = 2 * r2 + par
                pltpu.make_async_copy(xn_hbm.at[pl.ds(rt_ref[0, r], 1), :],
                                      xbuf.at[dst_slot, pl.ds(r, 1), :],
                                      sem_x.at[dst_slot]).start(priority=par)
            return c

        lax.fori_loop(0, tm // 2, issue, 0, unroll=4)

    @pl.when(i == 0)
    def _():
        for cp in weight_copies(e):
            cp.start()
        gather_start(rt0_ref, 0)

    @pl.when(rfirst_ref[i] == 1)
    def _():
        for cp in weight_copies(e):
            cp.wait()
        rows = 256

        def cast_rows(r, c):
            r0 = pl.multiple_of(r * rows, rows)
            wgb[pl.ds(r0, rows), :] = stg_g[pl.ds(r0, rows), :].astype(BF16)
            wub[pl.ds(r0, rows), :] = stg_u[pl.ds(r0, rows), :].astype(BF16)
            return c

        lax.fori_loop(0, stg_g.shape[0] // rows, cast_rows, 0)

        def cast_rows_d(r, c):
            r0 = pl.multiple_of(r * rows, rows)
            wdb[pl.ds(r0, rows), :] = stg_d[pl.ds(r0, rows), :].astype(BF16)
            return c

        lax.fori_loop(0, stg_d.shape[0] // rows, cast_rows_d, 0)

        @pl.when(nxt_ref[i] >= 0)
        def _():
            for cp in weight_copies(nxt_ref[i]):
                cp.start()

    @pl.when(first_ref[i] == 1)
    def _():
        @pl.when(t + 1 < n_tiles)
        def _():
            gather_start(rtn_ref, 1 - slot)

        pltpu.make_async_copy(xn_hbm.at[pl.ds(0, tm), :], xbuf.at[slot], sem_x.at[slot]).wait()

    @pl.when(hi > lo)
    def _():
        xb = xbuf[slot].astype(BF16)
        g = jnp.dot(xb, wgb[...], preferred_element_type=F32)
        u = jnp.dot(xb, wub[...], preferred_element_type=F32)
        hdn = (g * (1.0 / (1.0 + jnp.exp(-g)))) * u
        row = t * tm + lax.broadcasted_iota(jnp.int32, (tm, 1), 0)
        hdn = jnp.where((row >= lo) & (row < hi), hdn, 0.0)
        y = jnp.dot(hdn.astype(BF16), wdb[...], preferred_element_type=F32)

        @pl.when(first_ref[i] == 1)
        def _():
            o_ref[...] = y

        @pl.when(first_ref[i] != 1)
        def _():
            o_ref[...] = o_ref[...] + y


def _experts(xn, row_token, items, w_gate, w_up, w_down, layer, tm):
    _, d = xn.shape
    dh = w_gate.shape[-1]
    r_total = row_token.shape[0]
    n_tiles = r_total // tm
    n_items = items[0].shape[0]
    rt3 = row_token.reshape(n_tiles, 1, tm)
    grid_spec = pltpu.PrefetchScalarGridSpec(
        num_scalar_prefetch=7,
        grid=(n_items,),
        in_specs=[
            pl.BlockSpec((None, 1, tm), lambda i, *_: (0, 0, 0), memory_space=pltpu.SMEM),
            pl.BlockSpec((None, 1, tm), lambda i, ie, it, *_: (jnp.minimum(it[i] + 1, n_tiles - 1), 0, 0),
                         memory_space=pltpu.SMEM),
            pl.BlockSpec(memory_space=pl.ANY), pl.BlockSpec(memory_space=pl.ANY),
            pl.BlockSpec(memory_space=pl.ANY), pl.BlockSpec(memory_space=pl.ANY),
        ],
        out_specs=pl.BlockSpec((tm, d), lambda i, ie, it, *_: (it[i], 0)),
        scratch_shapes=[pltpu.VMEM((2, tm, d), F32),
                        pltpu.VMEM((d, dh), F32), pltpu.VMEM((d, dh), F32), pltpu.VMEM((dh, d), F32),
                        pltpu.VMEM((d, dh), BF16), pltpu.VMEM((d, dh), BF16), pltpu.VMEM((dh, d), BF16),
                        pltpu.SemaphoreType.DMA((2,)), pltpu.SemaphoreType.DMA((3,))],
    )
    return pl.pallas_call(
        functools.partial(_expert_body, tm=tm, layer=layer, n_tiles=n_tiles),
        grid_spec=grid_spec,
        out_shape=jax.ShapeDtypeStruct((r_total, d), F32),
        compiler_params=pltpu.CompilerParams(dimension_semantics=("arbitrary",),
                                             vmem_limit_bytes=EXPERT_VMEM_LIMIT),
        name="moe_experts",
    )(*items, rt3, rt3, xn, w_gate, w_up, w_down)


def _moe_plan(meta, counts_row, n_rows, tm):
    counts = counts_row[0, ROUTER_E0:ROUTER_E0 + N_EXPERTS].astype(jnp.int32)
    ends = jnp.cumsum(counts)
    starts = ends - counts
    eid = meta[:, 0:2].astype(jnp.int32)
    rank = meta[:, 2:4].astype(jnp.int32)
    dest = starts[eid] + rank

    first_tile = starts // tm
    last_tile = jnp.maximum(ends - 1, 0) // tm
    ntile_e = jnp.where(counts > 0, last_tile - first_tile + 1, 0)
    item_end = jnp.cumsum(ntile_e)
    item_start = item_end - ntile_e
    total = item_end[-1]
    n_items = n_rows // tm + N_EXPERTS - 1
    ii = jnp.arange(n_items, dtype=jnp.int32)
    valid = ii < total
    ii_c = jnp.minimum(ii, total - 1)
    e_i = jnp.searchsorted(item_end, ii_c, side='right').astype(jnp.int32)
    t_i = (first_tile[e_i] + (ii_c - item_start[e_i])).astype(jnp.int32)
    lo = jnp.where(valid, jnp.maximum(starts[e_i], t_i * tm), 0).astype(jnp.int32)
    hi = jnp.where(valid, jnp.minimum(ends[e_i], (t_i + 1) * tm), 0).astype(jnp.int32)
    minus1 = jnp.full((1,), -1, jnp.int32)
    first = (valid & (t_i != jnp.concatenate([minus1, t_i[:-1]]))).astype(jnp.int32)
    run_first = (valid & (e_i != jnp.concatenate([minus1, e_i[:-1]]))).astype(jnp.int32)
    ar = jnp.arange(N_EXPERTS, dtype=jnp.int32)
    later = (ar[None, :] > ar[:, None]) & (counts > 0)[None, :]
    nxt_e = jnp.min(jnp.where(later, ar[None, :], N_EXPERTS), axis=1)
    nxt_e = jnp.where(nxt_e == N_EXPERTS, -1, nxt_e).astype(jnp.int32)
    items = (e_i, t_i, lo, hi, first, run_first, nxt_e[e_i])
    return dest.astype(jnp.int32), items


def _hier_moe(h, g, wg1, bg1, wg2, bg2, w_gate, w_up, w_down, layer, next_norm_g=None,
              tm=EXPERT_TM):
    n, _ = h.shape
    xn, meta, counts_row = _router(h, g, wg1, bg1, wg2, bg2)
    dest, items = _moe_plan(meta, counts_row, 2 * n, tm)
    y_sorted = _experts(xn, _invert(dest), items, w_gate, w_up, w_down, layer, tm)
    return _combine(h, meta, y_sorted, dest, next_norm_g)


def _time_permute(h, batch, inverse=False):
    n, d = h.shape
    l = n // batch
    shape = (batch, l // S5_SEGMENTS, S5_SEGMENTS, d) if inverse else (batch, S5_SEGMENTS, l // S5_SEGMENTS, d)
    return h.reshape(shape).transpose(0, 2, 1, 3).reshape(n, d)


def kernel(x, norm_mix_g, norm_ffn_g, ssm_lam_re, ssm_lam_im, ssm_b_re, ssm_b_im, ssm_c_re, ssm_c_im, ssm_d, ssm_log_step, ssm_w_glu1, ssm_w_glu2, kv_norm_g, w_kv, k_norm_g, w_q, q_norm_g, lambda_qk, subln_g, w_o, rel_bias, router_group_w, router_group_b, router_expert_w, router_expert_b, w_gate, w_up, w_down):
    batch, l, d = x.shape
    n = batch * l
    h = _time_permute(x.astype(F32).reshape(n, d), batch)
    kn = vb = bias_tabs = None
    xn = _rmsnorm(h, norm_mix_g[0], BF16)
    for layer in range(DEPTH):
        if layer < N_A_LAYERS:
            b_blk, c_blk, lam_v = _s5_params(ssm_lam_re[layer], ssm_lam_im[layer], ssm_b_re[layer],
                                             ssm_b_im[layer], ssm_c_re[layer], ssm_c_im[layer],
                                             ssm_log_step[layer])
            z = _s5_scan(xn, b_blk, c_blk, lam_v, ssm_d[layer], batch)
            h = _matmul(z, [ssm_w_glu1, ssm_w_glu2], layer, F32, res=h)
        else:
            j = layer - N_A_LAYERS
            lam_init = 0.8 - 0.6 * math.exp(-0.3 * layer)
            qn = _matmul(xn, [w_q], j, BF16, head_norm=(q_norm_g[j], HEAD_DIM ** -0.5 * LOG2E),
                         tn=PROJ_TN)
            o = _diff_attention(qn, kn, vb, bias_tabs, lambda_qk[j], subln_g[j], lam_init, batch)
            h = _matmul(o, [w_o], j, F32, res=h, tn=PROJ_TN)
        fuse_next = layer + 1 < DEPTH and layer != N_A_LAYERS - 1
        moe_out = _hier_moe(h, norm_ffn_g[layer], router_group_w[layer], router_group_b[layer],
                            router_expert_w[layer], router_expert_b[layer], w_gate, w_up, w_down,
                            layer, norm_mix_g[layer + 1] if fuse_next else None)
        h, xn = moe_out if fuse_next else (moe_out, None)
        if layer == N_A_LAYERS - 1:
            h = _time_permute(h, batch, inverse=True)
            xkv = _rmsnorm(h, kv_norm_g, BF16)
            w_kv3 = w_kv.reshape(1, d, 2 * d)
            kn = _matmul(xkv, [w_kv3], 0, BF16, n_out=d, head_norm=(k_norm_g, 1.0), tn=PROJ_TN)
            vb = _matmul(xkv, [w_kv3], 0, BF16, col_off=d // PROJ_TN, n_out=d, tn=PROJ_TN)
            bias_tabs = _attn_bias_tables(rel_bias, ATTN_TQ)
            xn = _rmsnorm(h, norm_mix_g[layer + 1], BF16)
    return h.reshape(batch, l, d)
```

```python
import functools
import math

import jax
import jax.numpy as jnp
from jax import lax
from jax.experimental import pallas as pl
from jax.experimental.pallas import tpu as pltpu

F32 = jnp.float32
BF16 = jnp.bfloat16

DEPTH = 4
N_A_LAYERS = DEPTH // 2
SSM_GROUP = 16
SSM_STATE = 64
HEAD_DIM = 64
REL_BUCKETS = 32
REL_MAX_EXACT = REL_BUCKETS // 2
REL_MAX_DISTANCE = 128
N_EXPERT_GROUPS = 4
EXPERTS_PER_GROUP = 4
N_EXPERTS = N_EXPERT_GROUPS * EXPERTS_PER_GROUP
RMS_EPS = 1e-6

LANES = 128
SUBLANES = 8
VMEM_LIMIT = 48 * 1024 * 1024
NEG_BIG = -1e30

S5_SEGMENTS = SUBLANES
S5_GROUPS_PER_BLOCK = 16
S5_TJ = 32
PROJ_TN = 1024


def _cparams(sem):
    return pltpu.CompilerParams(dimension_semantics=sem, vmem_limit_bytes=VMEM_LIMIT)


def _rmsnorm_body(h_ref, g_ref, o_ref):
    x = h_ref[...]
    ms = jnp.mean(x * x, axis=-1, keepdims=True)
    o_ref[...] = (x * lax.rsqrt(ms + RMS_EPS) * g_ref[...]).astype(o_ref.dtype)


def _rmsnorm(h, g, out_dtype, tm=512):
    n, d = h.shape
    return pl.pallas_call(
        _rmsnorm_body,
        grid=(n // tm,),
        in_specs=[pl.BlockSpec((tm, d), lambda i: (i, 0)),
                  pl.BlockSpec((1, d), lambda i: (0, 0))],
        out_specs=pl.BlockSpec((tm, d), lambda i: (i, 0)),
        out_shape=jax.ShapeDtypeStruct((n, d), out_dtype),
        compiler_params=_cparams(("arbitrary",)),
        name="rmsnorm",
    )(h, g.reshape(1, d))


def _mm_body(*refs, n_w, has_res, hn_scale):
    a_ref = refs[0]
    w_refs = refs[1:1 + n_w]
    pos = 1 + n_w
    res_ref = refs[pos] if has_res else None
    pos += int(has_res)
    g_ref = refs[pos] if hn_scale is not None else None
    pos += int(hn_scale is not None)
    o_ref = refs[pos]
    wb_refs = refs[pos + 1:]

    @pl.when(pl.program_id(1) == 0)
    def _():
        for w_ref, wb_ref in zip(w_refs, wb_refs):
            wb_ref[...] = w_ref[...].astype(BF16)

    a = a_ref[...]
    y = jnp.dot(a, wb_refs[0][...], preferred_element_type=F32)
    if n_w == 2:
        y2 = jnp.dot(a, wb_refs[1][...], preferred_element_type=F32)
        y = y * (1.0 / (1.0 + jnp.exp(-y2)))
    if has_res:
        y = y + res_ref[...]
    if hn_scale is None:
        o_ref[...] = y.astype(o_ref.dtype)
    else:
        tm, tn = y.shape
        lane = lax.broadcasted_iota(jnp.int32, (tm, LANES), 1)
        lo = lane < HEAD_DIM
        g = g_ref[...]
        for hb in range(tn // LANES):
            x = y[:, hb * LANES:(hb + 1) * LANES]
            x2 = x * x
            s_lo = jnp.sum(jnp.where(lo, x2, 0.0), axis=-1, keepdims=True)
            s_hi = jnp.sum(jnp.where(lo, 0.0, x2), axis=-1, keepdims=True)
            r = jnp.where(lo, lax.rsqrt(s_lo * (1.0 / HEAD_DIM) + RMS_EPS),
                          lax.rsqrt(s_hi * (1.0 / HEAD_DIM) + RMS_EPS))
            o_ref[:, hb * LANES:(hb + 1) * LANES] = ((x * r * g) * hn_scale).astype(o_ref.dtype)


def _matmul(a, ws, layer, out_dtype, res=None, col_off=0, n_out=None, head_norm=None,
            tm=512, tn=512):
    m, k = a.shape
    n_out = ws[0].shape[-1] if n_out is None else n_out
    n_w = len(ws)
    in_specs = [pl.BlockSpec((tm, k), lambda j, i: (i, 0))]
    for _ in ws:
        in_specs.append(pl.BlockSpec((None, k, tn), lambda j, i: (layer, 0, j + col_off)))
    args = [a, *ws]
    if res is not None:
        in_specs.append(pl.BlockSpec((tm, tn), lambda j, i: (i, j)))
        args.append(res)
    hn_scale = None
    if head_norm is not None:
        gain, hn_scale = head_norm
        in_specs.append(pl.BlockSpec((1, LANES), lambda j, i: (0, 0)))
        args.append(jnp.concatenate([gain, gain]).reshape(1, LANES).astype(F32))
    return pl.pallas_call(
        functools.partial(_mm_body, n_w=n_w, has_res=res is not None, hn_scale=hn_scale),
        grid=(n_out // tn, m // tm),
        in_specs=in_specs,
        out_specs=pl.BlockSpec((tm, tn), lambda j, i: (i, j)),
        out_shape=jax.ShapeDtypeStruct((m, n_out), out_dtype),
        scratch_shapes=[pltpu.VMEM((k, tn), BF16) for _ in ws],
        compiler_params=_cparams(("arbitrary", "arbitrary")),
        name="dense_matmul",
    )(*args)


def _gelu_tanh(y):
    c = math.sqrt(2.0 / math.pi)
    return y * (0.5 * (1.0 + jnp.tanh(c * (y + 0.044715 * (y * y * y)))))


def _s5_body(x_ref, b_ref, c_ref, lam_ref, d_ref, z_ref, bu0_ref, bu1_ref, st0_ref, st1_ref,
             e_ref, init_ref, *, seg_len):
    ch = lam_ref.shape[-1]
    rows = S5_TJ * S5_SEGMENTS
    n_tiles = seg_len // S5_TJ
    lr = jnp.broadcast_to(lam_ref[0:1, :], (S5_SEGMENTS, ch))
    li = jnp.broadcast_to(lam_ref[1:2, :], (S5_SEGMENTS, ch))

    def row0(t):
        return pl.multiple_of(jnp.clip(t, 0, n_tiles - 1) * rows, rows)

    def bu_tile(t, bu_ref):
        bu_ref[...] = jnp.dot(x_ref[pl.ds(row0(t), rows), :], b_ref[...],
                              preferred_element_type=F32)

    def scan_tile(bu_ref, s_re, s_im, st_ref):
        for j in range(S5_TJ):
            sl = slice(S5_SEGMENTS * j, S5_SEGMENTS * (j + 1))
            n_re = lr * s_re - li * s_im + bu_ref[sl, 0:ch]
            n_im = lr * s_im + li * s_re + bu_ref[sl, ch:2 * ch]
            s_re, s_im = n_re, n_im
            if st_ref is not None:
                st_ref[sl, 0:ch] = s_re
                st_ref[sl, ch:2 * ch] = s_im
        return s_re, s_im

    def project(t, st_ref):
        r0 = row0(t)
        y = jnp.dot(st_ref[...].astype(BF16), c_ref[...], preferred_element_type=F32)
        y = y + d_ref[...] * x_ref[pl.ds(r0, rows), :].astype(F32)
        z_ref[pl.ds(r0, rows), :] = _gelu_tanh(y).astype(z_ref.dtype)

    def pass1(i2, c):
        t = 2 * i2
        bu_tile(t + 1, bu1_ref)
        c = scan_tile(bu0_ref, c[0], c[1], None)
        bu_tile(t + 2, bu0_ref)
        return scan_tile(bu1_ref, c[0], c[1], None)

    zeros = jnp.zeros((S5_SEGMENTS, ch), F32)
    bu_tile(0, bu0_ref)
    e_re, e_im = lax.fori_loop(0, n_tiles // 2, pass1, (zeros, zeros))
    e_ref[:, 0:ch] = e_re
    e_ref[:, ch:2 * ch] = e_im

    pr, pi = lam_ref[0:1, :], lam_ref[1:2, :]
    for _ in range(seg_len.bit_length() - 1):
        pr, pi = pr * pr - pi * pi, 2.0 * pr * pi
    cr = jnp.zeros((1, ch), F32)
    ci = jnp.zeros((1, ch), F32)
    init_ref[0:1, :] = jnp.zeros((1, 2 * ch), F32)
    for k in range(S5_SEGMENTS - 1):
        er = e_ref[k:k + 1, 0:ch]
        ei = e_ref[k:k + 1, ch:2 * ch]
        cr, ci = pr * cr - pi * ci + er, pr * ci + pi * cr + ei
        init_ref[k + 1:k + 2, 0:ch] = cr
        init_ref[k + 1:k + 2, ch:2 * ch] = ci

    def pass2(i2, c):
        t = 2 * i2
        bu_tile(t + 1, bu1_ref)
        c = scan_tile(bu0_ref, c[0], c[1], st0_ref)
        project(t - 1, st1_ref)
        bu_tile(t + 2, bu0_ref)
        c = scan_tile(bu1_ref, c[0], c[1], st1_ref)
        project(t, st0_ref)
        return c

    st1_ref[...] = jnp.zeros(st1_ref.shape, F32)
    bu_tile(0, bu0_ref)
    lax.fori_loop(0, n_tiles // 2, pass2, (init_ref[:, 0:ch], init_ref[:, ch:2 * ch]))
    project(n_tiles - 1, st1_ref)


def _s5_params(lam_re, lam_im, b_re, b_im, c_re, c_im, log_step):
    g, p = lam_re.shape
    gb = S5_GROUPS_PER_BLOCK
    nb = g // gb
    lam = lax.complex(lam_re.astype(F32), lam_im.astype(F32))
    delta = jnp.exp(log_step.astype(F32))[:, None]
    lam_bar = jnp.exp(lam * delta)
    b_bar = ((lam_bar - 1.0) / lam)[..., None] * lax.complex(b_re.astype(F32), b_im.astype(F32))
    eye = jnp.eye(gb, dtype=F32)

    def blk_b(part):
        part = part.reshape(nb, gb, p, SSM_GROUP)
        return jnp.einsum('cgph,gk->cghkp', part, eye).reshape(nb, gb * SSM_GROUP, gb * p)

    def blk_c(part):
        part = part.reshape(nb, gb, SSM_GROUP, p)
        return jnp.einsum('cghp,gk->cgpkh', part, eye).reshape(nb, gb * p, gb * SSM_GROUP)

    b_blk = jnp.concatenate([blk_b(jnp.real(b_bar)), blk_b(jnp.imag(b_bar))], axis=-1)
    c_blk = jnp.concatenate([blk_c(c_re.astype(F32)), blk_c(-c_im.astype(F32))], axis=1)
    lam_v = jnp.stack([jnp.real(lam_bar).reshape(nb, gb * p),
                       jnp.imag(lam_bar).reshape(nb, gb * p)], axis=1)
    return b_blk.astype(BF16), c_blk.astype(BF16), lam_v


def _s5_scan(xn, b_blk, c_blk, lam_v, d_skip, batch):
    n, d = xn.shape
    l = n // batch
    seg_len = l // S5_SEGMENTS
    nb, fb, ch2 = b_blk.shape
    rows = S5_TJ * S5_SEGMENTS
    return pl.pallas_call(
        functools.partial(_s5_body, seg_len=seg_len),
        grid=(batch, nb),
        in_specs=[pl.BlockSpec((l, fb), lambda b, c: (b, c)),
                  pl.BlockSpec((None, fb, ch2), lambda b, c: (c, 0, 0)),
                  pl.BlockSpec((None, ch2, fb), lambda b, c: (c, 0, 0)),
                  pl.BlockSpec((None, 2, ch2 // 2), lambda b, c: (c, 0, 0)),
                  pl.BlockSpec((1, fb), lambda b, c: (0, c))],
        out_specs=pl.BlockSpec((l, fb), lambda b, c: (b, c)),
        out_shape=jax.ShapeDtypeStruct((n, d), BF16),
        scratch_shapes=[pltpu.VMEM((rows, ch2), F32), pltpu.VMEM((rows, ch2), F32),
                        pltpu.VMEM((rows, ch2), F32), pltpu.VMEM((rows, ch2), F32),
                        pltpu.VMEM((S5_SEGMENTS, ch2), F32), pltpu.VMEM((S5_SEGMENTS, ch2), F32)],
        compiler_params=_cparams(("arbitrary", "arbitrary")),
        name="s5_scan",
    )(xn, b_blk, c_blk, lam_v, d_skip.reshape(1, d).astype(F32))


def _t5_bucket(n):
    n_safe = jnp.maximum(n, 1).astype(F32)
    large = REL_MAX_EXACT + (jnp.log(n_safe / REL_MAX_EXACT)
                             / math.log(REL_MAX_DISTANCE / REL_MAX_EXACT)
                             * (REL_BUCKETS - REL_MAX_EXACT)).astype(jnp.int32)
    large = jnp.minimum(large, REL_BUCKETS - 1)
    return jnp.where(n < REL_MAX_EXACT, n, large)


ATTN_TQ = 256
M_INIT = -1e29
LOG2E = 1.4426950408889634


def _attn_bias_tables(rel_bias, tq):
    qi = jnp.arange(tq, dtype=jnp.int32)[:, None]
    ki = jnp.arange(tq, dtype=jnp.int32)[None, :]
    d_diag = qi - ki
    rb = rel_bias.astype(F32)
    rb = (rb - rb[REL_BUCKETS - 1][None, :]) * LOG2E

    def lookup(dist):
        onehot = jax.nn.one_hot(_t5_bucket(dist), REL_BUCKETS, dtype=F32)
        return jnp.einsum('qkb,bh->hqk', onehot, rb, precision=lax.Precision.HIGHEST)

    b_diag = jnp.where((d_diag >= 0)[None], lookup(jnp.maximum(d_diag, 0)), NEG_BIG)
    b_prev = lookup(d_diag + tq)
    masked = jnp.full_like(b_diag, NEG_BIG)
    return jnp.stack([jnp.concatenate([b_prev, b_diag], axis=-1),
                      jnp.concatenate([b_diag, masked], axis=-1)], axis=1)


def _attn_pair_tables(nq):
    def framed(pairs):
        idle = (0, nq, 0, 0)
        rows = [idle, idle] + pairs + [idle, idle]
        if len(rows) % 2:
            rows.append(idle)
        return rows

    wide = [(i, i, 2 * j, 0) for i in range(nq) for j in range((i - 1) // 2)]
    odd = [(i, i, i - 2, 0) for i in range(2, nq) if (i - 1) % 2]
    near = [(i, i, max(i - 1, 0), 0 if i else 1) for i in range(nq)]
    lists = [framed(wide), framed(odd), framed(near)]
    cols = list(zip(*(lists[0] + lists[1] + lists[2])))
    return [jnp.asarray(c, jnp.int32) for c in cols], [len(x) for x in lists]


def _attn_body(qrow_ref, slot_ref, kt_ref, bidx_ref, q_ref, k_ref, v_ref, bias_ref, lq_ref, sg_ref,
               o_ref, m_ref, acc_ref, vaug_ref, s0_ref, s1_ref, p0_ref, p1_ref, a0_ref, a1_ref,
               sw0_ref, sw1_ref, pw0_ref, pw1_ref, *, tq, nq, n_rows, lam_init):
    m_ref[...] = jnp.full(m_ref.shape, M_INIT, F32)
    acc_ref[...] = jnp.zeros(acc_ref.shape, F32)
    vaug_ref[:, 0:LANES] = v_ref[...]
    vaug_ref[:, LANES:2 * LANES] = jnp.ones(v_ref.shape, BF16)
    lane = lax.broadcasted_iota(jnp.int32, (tq, LANES), 1)

    def scores(e, s_ref, tk):
        q = q_ref[pl.ds(pl.multiple_of(qrow_ref[e] * tq, tq), tq), :]
        zero = jnp.zeros_like(q)
        qq = jnp.concatenate([jnp.where(lane < HEAD_DIM, q, zero),
                              jnp.where(lane < HEAD_DIM, zero, q)], axis=0)
        k = k_ref[pl.ds(pl.multiple_of(kt_ref[e] * tq, tq), tk), :]
        s_ref[...] = lax.dot_general(qq, k, (((1,), (1,)), ((), ())), preferred_element_type=F32)

    def softmax(e, s_ref, a_ref, p_ref, with_bias, rc):
        m_view = m_ref.at[slot_ref[e]]
        b_view = bias_ref.at[bidx_ref[e]]
        for c in range(tq // rc):
            b = b_view[c * rc:(c + 1) * rc, :] if with_bias else None
            for half in range(2):
                rows = slice(half * tq + c * rc, half * tq + (c + 1) * rc)
                s = s_ref[rows, :]
                if with_bias:
                    s = s + b
                m_prev = m_view[rows, :]
                m_next = jnp.maximum(m_prev, jnp.max(s, axis=1, keepdims=True))
                a_ref[rows, :] = jnp.exp2(m_prev - m_next)
                p_ref[rows, :] = jnp.exp2(s - m_next[:, 0:1]).astype(BF16)
                m_view[rows, :] = m_next

    def values(e, a_ref, p_ref, tk):
        acc = acc_ref.at[slot_ref[e]]
        v = vaug_ref[pl.ds(pl.multiple_of(kt_ref[e] * tq, tq), tk), :]
        pv = jnp.dot(p_ref[...], v, preferred_element_type=F32)
        a = a_ref[...]
        acc[:, 0:LANES] = acc[:, 0:LANES] * a + pv[:, 0:LANES]
        acc[:, LANES:2 * LANES] = acc[:, LANES:2 * LANES] * a + pv[:, LANES:2 * LANES]

    def run_pipeline(base, rows_n, with_bias, tk, sa_ref, sb_ref, pa_ref, pb_ref):
        rc = 64 if tk == tq else 32
        pa_ref[...] = jnp.zeros(pa_ref.shape, BF16)
        a0_ref[...] = jnp.ones(a0_ref.shape, F32)
        scores(base + 1, sb_ref, tk)

        def two_iterations(i2, c):
            it = base + 2 * i2
            scores(it + 2, sa_ref, tk)
            softmax(it + 1, sb_ref, a1_ref, pb_ref, with_bias, rc)
            values(it, a0_ref, pa_ref, tk)
            scores(it + 3, sb_ref, tk)
            softmax(it + 2, sa_ref, a0_ref, pa_ref, with_bias, rc)
            values(it + 1, a1_ref, pb_ref, tk)
            return c

        lax.fori_loop(0, (rows_n - 2) // 2, two_iterations, 0)

    n_wide, n_odd, n_near = n_rows
    run_pipeline(0, n_wide, False, 2 * tq, sw0_ref, sw1_ref, pw0_ref, pw1_ref)
    run_pipeline(n_wide, n_odd, False, tq, s0_ref, s1_ref, p0_ref, p1_ref)
    run_pipeline(n_wide + n_odd, n_near, True, 2 * tq, sw0_ref, sw1_ref, pw0_ref, pw1_ref)

    lq = lq_ref[...]
    lam = (jnp.exp(jnp.sum(lq[0:1] * lq[1:2], axis=1, keepdims=True))
           - jnp.exp(jnp.sum(lq[2:3] * lq[3:4], axis=1, keepdims=True)) + lam_init)

    def finalize(i, c):
        acc = acc_ref[i]
        o_all = acc[:, 0:LANES] / acc[:, LANES:2 * LANES]
        o = o_all[0:tq] - lam * o_all[tq:2 * tq]
        ms = jnp.mean(o * o, axis=-1, keepdims=True)
        o = (o * lax.rsqrt(ms + RMS_EPS) * sg_ref[...]) * (1.0 - lam_init)
        o_ref[pl.ds(pl.multiple_of(i * tq, tq), tq), :] = o.astype(o_ref.dtype)
        return c

    lax.fori_loop(0, nq, finalize, 0)


def _diff_attention(qn, kn, vb, bias_tabs, lambda_qk, subln_g, lam_init, batch):
    n, d = qn.shape
    l = n // batch
    tq = ATTN_TQ
    nq = l // tq
    nh = d // LANES
    tables, n_rows = _attn_pair_tables(nq)
    row_blk = pl.BlockSpec((l, LANES), lambda b, h, *_: (b, h))
    grid_spec = pltpu.PrefetchScalarGridSpec(
        num_scalar_prefetch=4,
        grid=(batch, nh),
        in_specs=[row_blk, row_blk, row_blk,
                  pl.BlockSpec((None, 2, tq, 2 * tq), lambda b, h, *_: (h, 0, 0, 0)),
                  pl.BlockSpec((4, HEAD_DIM), lambda b, h, *_: (0, 0)),
                  pl.BlockSpec((1, LANES), lambda b, h, *_: (0, 0))],
        out_specs=row_blk,
        scratch_shapes=[pltpu.VMEM((nq + 1, 2 * tq, LANES), F32),
                        pltpu.VMEM((nq + 1, 2 * tq, 2 * LANES), F32),
                        pltpu.VMEM((l, 2 * LANES), BF16),
                        pltpu.VMEM((2 * tq, tq), F32), pltpu.VMEM((2 * tq, tq), F32),
                        pltpu.VMEM((2 * tq, tq), BF16), pltpu.VMEM((2 * tq, tq), BF16),
                        pltpu.VMEM((2 * tq, LANES), F32), pltpu.VMEM((2 * tq, LANES), F32),
                        pltpu.VMEM((2 * tq, 2 * tq), F32), pltpu.VMEM((2 * tq, 2 * tq), F32),
                        pltpu.VMEM((2 * tq, 2 * tq), BF16), pltpu.VMEM((2 * tq, 2 * tq), BF16)],
    )
    return pl.pallas_call(
        functools.partial(_attn_body, tq=tq, nq=nq, n_rows=tuple(n_rows), lam_init=lam_init),
        grid_spec=grid_spec,
        out_shape=jax.ShapeDtypeStruct((n, d), BF16),
        compiler_params=_cparams(("arbitrary", "arbitrary")),
        name="diff_attention",
    )(*tables, qn, kn, vb, bias_tabs, lambda_qk.astype(F32),
      subln_g.reshape(1, LANES).astype(F32))


ROUTER_E0 = N_EXPERT_GROUPS
META_COLS = 8


def _router_body(h_ref, g_ref, wr_ref, br_ref, xn_ref, meta_ref, cnt_ref, base_ref):
    tm = h_ref.shape[0]

    @pl.when(pl.program_id(0) == 0)
    def _():
        base_ref[...] = jnp.zeros(base_ref.shape, F32)

    x = h_ref[...]
    ms = jnp.mean(x * x, axis=-1, keepdims=True)
    xn = x * lax.rsqrt(ms + RMS_EPS) * g_ref[...]
    xn_ref[...] = xn.astype(xn_ref.dtype)
    x_hi = xn.astype(BF16)
    x_lo = (xn - x_hi.astype(F32)).astype(BF16)
    logits = (jnp.dot(x_hi, wr_ref[0], preferred_element_type=F32)
              + (jnp.dot(x_hi, wr_ref[1], preferred_element_type=F32)
                 + jnp.dot(x_lo, wr_ref[0], preferred_element_type=F32))) + br_ref[...]
    lane = lax.broadcasted_iota(jnp.int32, (tm, LANES), 1)
    neg_inf = -jnp.inf

    def first_argmax(vals):
        vmax = jnp.max(vals, axis=-1, keepdims=True)
        idx = jnp.min(jnp.where(vals == vmax, lane, LANES), axis=-1, keepdims=True)
        return vmax, idx

    is_g = lane < N_EXPERT_GROUPS
    gmax, g_idx = first_argmax(jnp.where(is_g, logits, neg_inf))
    g_w = 1.0 / jnp.sum(jnp.where(is_g, jnp.exp(logits - gmax), 0.0), axis=-1, keepdims=True)
    e_lo = ROUTER_E0 + EXPERTS_PER_GROUP * g_idx
    elog = jnp.where((lane >= e_lo) & (lane < e_lo + EXPERTS_PER_GROUP), logits, neg_inf)
    v0, i0 = first_argmax(elog)
    v1, i1 = first_argmax(jnp.where(lane == i0, neg_inf, elog))
    t = jnp.exp(v1 - v0)
    w0 = g_w / (1.0 + t)
    w1 = g_w * t / (1.0 + t)

    sel0 = lane == i0
    sel1 = lane == i1
    onehot = jnp.where(sel0 | sel1, 1.0, 0.0)
    r_i = lax.broadcasted_iota(jnp.int32, (tm, tm), 0)
    c_i = lax.broadcasted_iota(jnp.int32, (tm, tm), 1)
    tri = jnp.where(c_i < r_i, 1.0, 0.0).astype(BF16)
    before = jnp.dot(tri, onehot.astype(BF16), preferred_element_type=F32) + base_ref[...]
    rank0 = jnp.sum(jnp.where(sel0, before, 0.0), axis=-1, keepdims=True)
    rank1 = jnp.sum(jnp.where(sel1, before, 0.0), axis=-1, keepdims=True)
    base_ref[...] = base_ref[...] + jnp.sum(onehot, axis=0, keepdims=True)
    cnt_ref[...] = base_ref[...]

    eid0 = (i0 - ROUTER_E0).astype(F32)
    eid1 = (i1 - ROUTER_E0).astype(F32)
    meta = jnp.where(lane == 0, eid0, jnp.where(lane == 1, eid1, jnp.where(
        lane == 2, rank0, jnp.where(lane == 3, rank1, jnp.where(
            lane == 4, w0, jnp.where(lane == 5, w1, 0.0))))))
    meta_ref[...] = meta[:, 0:META_COLS]


def _router(h, g, wg1, bg1, wg2, bg2, tm=256):
    n, d = h.shape
    wr = jnp.concatenate([wg1.astype(F32),
                          wg2.astype(F32).transpose(1, 0, 2).reshape(d, N_EXPERTS)], axis=1)
    wr = jnp.pad(wr, ((0, 0), (0, LANES - wr.shape[1])))
    wr_hi = wr.astype(BF16)
    wr = jnp.stack([wr_hi, (wr - wr_hi.astype(F32)).astype(BF16)])
    br = jnp.pad(jnp.concatenate([bg1.astype(F32), bg2.astype(F32).reshape(-1)]),
                 (0, LANES - N_EXPERT_GROUPS - N_EXPERTS)).reshape(1, LANES)
    return pl.pallas_call(
        _router_body,
        grid=(n // tm,),
        in_specs=[pl.BlockSpec((tm, d), lambda i: (i, 0)),
                  pl.BlockSpec((1, d), lambda i: (0, 0)),
                  pl.BlockSpec((2, d, LANES), lambda i: (0, 0, 0)),
                  pl.BlockSpec((1, LANES), lambda i: (0, 0))],
        out_specs=[pl.BlockSpec((tm, d), lambda i: (i, 0)),
                   pl.BlockSpec((tm, META_COLS), lambda i: (i, 0)),
                   pl.BlockSpec((1, LANES), lambda i: (0, 0))],
        out_shape=[jax.ShapeDtypeStruct((n, d), F32),
                   jax.ShapeDtypeStruct((n, META_COLS), F32),
                   jax.ShapeDtypeStruct((1, LANES), F32)],
        scratch_shapes=[pltpu.VMEM((1, LANES), F32)],
        compiler_params=_cparams(("arbitrary",)),
        name="moe_router",
    )(h, g.reshape(1, d).astype(F32), wr, br)


def _invert_body(dest_ref, rt_ref, *, ts):
    t = pl.program_id(0)

    def body(r, c):
        tok = t * ts + r
        rt_ref[dest_ref[0, 2 * r]] = tok
        rt_ref[dest_ref[0, 2 * r + 1]] = tok
        return c

    lax.fori_loop(0, ts, body, 0, unroll=True)


def _invert(dest, ts=512):
    n = dest.shape[0]
    dest3 = dest.reshape(n // ts, 1, 2 * ts)
    return pl.pallas_call(
        functools.partial(_invert_body, ts=ts),
        grid=(n // ts,),
        in_specs=[pl.BlockSpec((None, 1, 2 * ts), lambda t: (t, 0, 0), memory_space=pltpu.SMEM)],
        out_specs=pl.BlockSpec(memory_space=pltpu.SMEM),
        out_shape=jax.ShapeDtypeStruct((2 * n,), jnp.int32),
        compiler_params=_cparams(("arbitrary",)),
        name="moe_invert",
    )(dest3)


def _combine_body(d0_ref, dn_ref, h_ref, w_ref, y_hbm, *rest, ts, nt, with_norm):
    if with_norm:
        g_ref, o_ref, xn_ref, y0_ref, y1_ref, sem = rest
    else:
        o_ref, y0_ref, y1_ref, sem = rest
    t = pl.program_id(0)
    slot = t % 2

    def gather_start(dest_ref, dst):
        def issue(r, c):
            pltpu.make_async_copy(y_hbm.at[pl.ds(dest_ref[0, 2 * r], 1), :],
                                  y0_ref.at[dst, pl.ds(r, 1), :], sem.at[dst, 0]).start()
            pltpu.make_async_copy(y_hbm.at[pl.ds(dest_ref[0, 2 * r + 1], 1), :],
                                  y1_ref.at[dst, pl.ds(r, 1), :], sem.at[dst, 1]).start(priority=1)
            return c

        lax.fori_loop(0, ts, issue, 0, unroll=True)

    @pl.when(t == 0)
    def _():
        gather_start(d0_ref, 0)

    @pl.when(t + 1 < nt)
    def _():
        gather_start(dn_ref, 1 - slot)

    pltpu.make_async_copy(y_hbm.at[pl.ds(0, ts), :], y0_ref.at[slot], sem.at[slot, 0]).wait()
    pltpu.make_async_copy(y_hbm.at[pl.ds(0, ts), :], y1_ref.at[slot], sem.at[slot, 1]).wait()
    w = w_ref[...]
    out = h_ref[...] + w[:, 4:5] * y0_ref[slot] + w[:, 5:6] * y1_ref[slot]
    o_ref[...] = out
    if with_norm:
        ms = jnp.mean(out * out, axis=-1, keepdims=True)
        xn_ref[...] = (out * lax.rsqrt(ms + RMS_EPS) * g_ref[...]).astype(xn_ref.dtype)


def _combine(h, meta, y_sorted, dest, next_norm_g=None, ts=256):
    n, d = h.shape
    nt = n // ts
    dest3 = dest.reshape(nt, 1, 2 * ts)
    with_norm = next_norm_g is not None
    row_blk = pl.BlockSpec((ts, d), lambda t: (t, 0))
    in_specs = [pl.BlockSpec((None, 1, 2 * ts), lambda t: (0, 0, 0), memory_space=pltpu.SMEM),
                pl.BlockSpec((None, 1, 2 * ts), lambda t: (jnp.minimum(t + 1, nt - 1), 0, 0),
                             memory_space=pltpu.SMEM),
                row_blk,
                pl.BlockSpec((ts, META_COLS), lambda t: (t, 0)),
                pl.BlockSpec(memory_space=pl.ANY)]
    args = [dest3, dest3, h, meta, y_sorted]
    out_specs, out_shape = row_blk, jax.ShapeDtypeStruct((n, d), F32)
    if with_norm:
        in_specs.append(pl.BlockSpec((1, d), lambda t: (0, 0)))
        args.append(next_norm_g.reshape(1, d).astype(F32))
        out_specs = [row_blk, row_blk]
        out_shape = [out_shape, jax.ShapeDtypeStruct((n, d), BF16)]
    return pl.pallas_call(
        functools.partial(_combine_body, ts=ts, nt=nt, with_norm=with_norm),
        grid=(nt,),
        in_specs=in_specs,
        out_specs=out_specs,
        out_shape=out_shape,
        scratch_shapes=[pltpu.VMEM((2, ts, d), F32), pltpu.VMEM((2, ts, d), F32),
                        pltpu.SemaphoreType.DMA((2, 2))],
        compiler_params=_cparams(("arbitrary",)),
        name="moe_combine",
    )(*args)


EXPERT_TM = 256
EXPERT_VMEM_LIMIT = 56 * 1024 * 1024


def _expert_body(ie_ref, it_ref, lo_ref, hi_ref, first_ref, rfirst_ref, nxt_ref,
                 rt0_ref, rtn_ref, xn_hbm, wg_hbm, wu_hbm, wd_hbm, o_ref,
                 xbuf, stg_g, stg_u, stg_d, wgb, wub, wdb, sem_x, sem_w, *, tm, layer, n_tiles):
    i = pl.program_id(0)
    e = ie_ref[i]
    t = it_ref[i]
    lo = lo_ref[i]
    hi = hi_ref[i]
    slot = t % 2

    def weight_copies(ex):
        return (pltpu.make_async_copy(wg_hbm.at[layer, ex], stg_g, sem_w.at[0]),
                pltpu.make_async_copy(wu_hbm.at[layer, ex], stg_u, sem_w.at[1]),
                pltpu.make_async_copy(wd_hbm.at[layer, ex], stg_d, sem_w.at[2]))

    def gather_start(rt_ref, dst_slot):
        def issue(r2, c):
            for par in range(2):
                r = 2 * r2 + par
                pltpu.make_async_copy(xn_hbm.at[pl.ds(rt_ref[0, r], 1), :],
                                      xbuf.at[dst_slot, pl.ds(r, 1), :],
                                      sem_x.at[dst_slot]).start(priority=par)
            return c

        lax.fori_loop(0, tm // 2, issue, 0, unroll=True)

    @pl.when(i == 0)
    def _():
        for cp in weight_copies(e):
            cp.start()
        gather_start(rt0_ref, 0)

    @pl.when(rfirst_ref[i] == 1)
    def _():
        for cp in weight_copies(e):
            cp.wait()
        rows = 256

        def cast_rows(r, c):
            r0 = pl.multiple_of(r * rows, rows)
            wgb[pl.ds(r0, rows), :] = stg_g[pl.ds(r0, rows), :].astype(BF16)
            wub[pl.ds(r0, rows), :] = stg_u[pl.ds(r0, rows), :].astype(BF16)
            return c

        lax.fori_loop(0, stg_g.shape[0] // rows, cast_rows, 0)

        def cast_rows_d(r, c):
            r0 = pl.multiple_of(r * rows, rows)
            wdb[pl.ds(r0, rows), :] = stg_d[pl.ds(r0, rows), :].astype(BF16)
            return c

        lax.fori_loop(0, stg_d.shape[0] // rows, cast_rows_d, 0)

        @pl.when(nxt_ref[i] >= 0)
        def _():
            for cp in weight_copies(nxt_ref[i]):
                cp.start()

    @pl.when(first_ref[i] == 1)
    def _():
        @pl.when(t + 1 < n_tiles)
        def _():
            gather_start(rtn_ref, 1 - slot)

        pltpu.make_async_copy(xn_hbm.at[pl.ds(0, tm), :], xbuf.at[slot], sem_x.at[slot]).wait()

    @pl.when(hi > lo)
    def _():
        xb = xbuf[slot].astype(BF16)
        g = jnp.dot(xb, wgb[...], preferred_element_type=F32)
        u = jnp.dot(xb, wub[...], preferred_element_type=F32)
        hdn = (g * (1.0 / (1.0 + jnp.exp(-g)))) * u
        row = t * tm + lax.broadcasted_iota(jnp.int32, (tm, 1), 0)
        hdn = jnp.where((row >= lo) & (row < hi), hdn, 0.0)
        y = jnp.dot(hdn.astype(BF16), wdb[...], preferred_element_type=F32)

        @pl.when(first_ref[i] == 1)
        def _():
            o_ref[...] = y

        @pl.when(first_ref[i] != 1)
        def _():
            o_ref[...] = o_ref[...] + y


def _experts(xn, row_token, items, w_gate, w_up, w_down, layer, tm):
    _, d = xn.shape
    dh = w_gate.shape[-1]
    r_total = row_token.shape[0]
    n_tiles = r_total // tm
    n_items = items[0].shape[0]
    rt3 = row_token.reshape(n_tiles, 1, tm)
    grid_spec = pltpu.PrefetchScalarGridSpec(
        num_scalar_prefetch=7,
        grid=(n_items,),
        in_specs=[
            pl.BlockSpec((None, 1, tm), lambda i, *_: (0, 0, 0), memory_space=pltpu.SMEM),
            pl.BlockSpec((None, 1, tm), lambda i, ie, it, *_: (jnp.minimum(it[i] + 1, n_tiles - 1), 0, 0),
                         memory_space=pltpu.SMEM),
            pl.BlockSpec(memory_space=pl.ANY), pl.BlockSpec(memory_space=pl.ANY),
            pl.BlockSpec(memory_space=pl.ANY), pl.BlockSpec(memory_space=pl.ANY),
        ],
        out_specs=pl.BlockSpec((tm, d), lambda i, ie, it, *_: (it[i], 0)),
        scratch_shapes=[pltpu.VMEM((2, tm, d), F32),
                        pltpu.VMEM((d, dh), F32), pltpu.VMEM((d, dh), F32), pltpu.VMEM((dh, d), F32),
                        pltpu.VMEM((d, dh), BF16), pltpu.VMEM((d, dh), BF16), pltpu.VMEM((dh, d), BF16),
                        pltpu.SemaphoreType.DMA((2,)), pltpu.SemaphoreType.DMA((3,))],
    )
    return pl.pallas_call(
        functools.partial(_expert_body, tm=tm, layer=layer, n_tiles=n_tiles),
        grid_spec=grid_spec,
        out_shape=jax.ShapeDtypeStruct((r_total, d), F32),
        compiler_params=pltpu.CompilerParams(dimension_semantics=("arbitrary",),
                                             vmem_limit_bytes=EXPERT_VMEM_LIMIT),
        name="moe_experts",
    )(*items, rt3, rt3, xn, w_gate, w_up, w_down)


def _moe_plan(meta, counts_row, n_rows, tm):
    counts = counts_row[0, ROUTER_E0:ROUTER_E0 + N_EXPERTS].astype(jnp.int32)
    ends = jnp.cumsum(counts)
    starts = ends - counts
    eid = meta[:, 0:2].astype(jnp.int32)
    rank = meta[:, 2:4].astype(jnp.int32)
    dest = starts[eid] + rank

    first_tile = starts // tm
    last_tile = jnp.maximum(ends - 1, 0) // tm
    ntile_e = jnp.where(counts > 0, last_tile - first_tile + 1, 0)
    item_end = jnp.cumsum(ntile_e)
    item_start = item_end - ntile_e
    total = item_end[-1]
    n_items = n_rows // tm + N_EXPERTS - 1
    ii = jnp.arange(n_items, dtype=jnp.int32)
    valid = ii < total
    ii_c = jnp.minimum(ii, total - 1)
    e_i = jnp.searchsorted(item_end, ii_c, side='right').astype(jnp.int32)
    t_i = (first_tile[e_i] + (ii_c - item_start[e_i])).astype(jnp.int32)
    lo = jnp.where(valid, jnp.maximum(starts[e_i], t_i * tm), 0).astype(jnp.int32)
    hi = jnp.where(valid, jnp.minimum(ends[e_i], (t_i + 1) * tm), 0).astype(jnp.int32)
    minus1 = jnp.full((1,), -1, jnp.int32)
    first = (valid & (t_i != jnp.concatenate([minus1, t_i[:-1]]))).astype(jnp.int32)
    run_first = (valid & (e_i != jnp.concatenate([minus1, e_i[:-1]]))).astype(jnp.int32)
    ar = jnp.arange(N_EXPERTS, dtype=jnp.int32)
    later = (ar[None, :] > ar[:, None]) & (counts > 0)[None, :]
    nxt_e = jnp.min(jnp.where(later, ar[None, :], N_EXPERTS), axis=1)
    nxt_e = jnp.where(nxt_e == N_EXPERTS, -1, nxt_e).astype(jnp.int32)
    items = (e_i, t_i, lo, hi, first, run_first, nxt_e[e_i])
    return dest.astype(jnp.int32), items


def _hier_moe(h, g, wg1, bg1, wg2, bg2, w_gate, w_up, w_down, layer, next_norm_g=None,
              tm=EXPERT_TM):
    n, _ = h.shape
    xn, meta, counts_row = _router(h, g, wg1, bg1, wg2, bg2)
    dest, items = _moe_plan(meta, counts_row, 2 * n, tm)
    y_sorted = _experts(xn, _invert(dest), items, w_gate, w_up, w_down, layer, tm)
    return _combine(h, meta, y_sorted, dest, next_norm_g)


def _time_permute(h, batch, inverse=False):
    n, d = h.shape
    l = n // batch
    shape = (batch, l // S5_SEGMENTS, S5_SEGMENTS, d) if inverse else (batch, S5_SEGMENTS, l // S5_SEGMENTS, d)
    return h.reshape(shape).transpose(0, 2, 1, 3).reshape(n, d)


def kernel(x, norm_mix_g, norm_ffn_g, ssm_lam_re, ssm_lam_im, ssm_b_re, ssm_b_im, ssm_c_re, ssm_c_im, ssm_d, ssm_log_step, ssm_w_glu1, ssm_w_glu2, kv_norm_g, w_kv, k_norm_g, w_q, q_norm_g, lambda_qk, subln_g, w_o, rel_bias, router_group_w, router_group_b, router_expert_w, router_expert_b, w_gate, w_up, w_down):
    batch, l, d = x.shape
    n = batch * l
    h = _time_permute(x.astype(F32).reshape(n, d), batch)
    kn = vb = bias_tabs = None
    xn = _rmsnorm(h, norm_mix_g[0], BF16)
    for layer in range(DEPTH):
        if layer < N_A_LAYERS:
            b_blk, c_blk, lam_v = _s5_params(ssm_lam_re[layer], ssm_lam_im[layer], ssm_b_re[layer],
                                             ssm_b_im[layer], ssm_c_re[layer], ssm_c_im[layer],
                                             ssm_log_step[layer])
            z = _s5_scan(xn, b_blk, c_blk, lam_v, ssm_d[layer], batch)
            h = _matmul(z, [ssm_w_glu1, ssm_w_glu2], layer, F32, res=h)
        else:
            j = layer - N_A_LAYERS
            lam_init = 0.8 - 0.6 * math.exp(-0.3 * layer)
            qn = _matmul(xn, [w_q], j, BF16, head_norm=(q_norm_g[j], HEAD_DIM ** -0.5 * LOG2E),
                         tn=PROJ_TN)
            o = _diff_attention(qn, kn, vb, bias_tabs, lambda_qk[j], subln_g[j], lam_init, batch)
            h = _matmul(o, [w_o], j, F32, res=h, tn=PROJ_TN)
        fuse_next = layer + 1 < DEPTH and layer != N_A_LAYERS - 1
        moe_out = _hier_moe(h, norm_ffn_g[layer], router_group_w[layer], router_group_b[layer],
                            router_expert_w[layer], router_expert_b[layer], w_gate, w_up, w_down,
                            layer, norm_mix_g[layer + 1] if fuse_next else None)
        h, xn = moe_out if fuse_next else (moe_out, None)
        if layer == N_A_LAYERS - 1:
            h = _time_permute(h, batch, inverse=True)
            xkv = _rmsnorm(h, kv_norm_g, BF16)
            w_kv3 = w_kv.reshape(1, d, 2 * d)
            kn = _matmul(xkv, [w_kv3], 0, BF16, n_out=d, head_norm=(k_norm_g, 1.0), tn=PROJ_TN)
            vb = _matmul(xkv, [w_kv3], 0, BF16, col_off=d // PROJ_TN, n_out=d, tn=PROJ_TN)
            bias_tabs = _attn_bias_tables(rel_bias, ATTN_TQ)
            xn = _rmsnorm(h, norm_mix_g[layer + 1], BF16)
    return h.reshape(batch, l, d)
```

```python
import functools
import math

import jax
import jax.numpy as jnp
from jax import lax
from jax.experimental import pallas as pl
from jax.experimental.pallas import tpu as pltpu

F32 = jnp.float32
BF16 = jnp.bfloat16

DEPTH = 4
N_A_LAYERS = DEPTH // 2
SSM_GROUP = 16
SSM_STATE = 64
HEAD_DIM = 64
REL_BUCKETS = 32
REL_MAX_EXACT = REL_BUCKETS // 2
REL_MAX_DISTANCE = 128
N_EXPERT_GROUPS = 4
EXPERTS_PER_GROUP = 4
N_EXPERTS = N_EXPERT_GROUPS * EXPERTS_PER_GROUP
RMS_EPS = 1e-6

LANES = 128
SUBLANES = 8
VMEM_LIMIT = 48 * 1024 * 1024
NEG_BIG = -1e30

S5_SEGMENTS = SUBLANES
S5_GROUPS_PER_BLOCK = 16
S5_TJ = 32
PROJ_TN = 1024


def _cparams(sem):
    return pltpu.CompilerParams(dimension_semantics=sem, vmem_limit_bytes=VMEM_LIMIT)


def _rmsnorm_body(h_ref, g_ref, o_ref):
    x = h_ref[...]
    ms = jnp.mean(x * x, axis=-1, keepdims=True)
    o_ref[...] = (x * lax.rsqrt(ms + RMS_EPS) * g_ref[...]).astype(o_ref.dtype)


def _rmsnorm(h, g, out_dtype, tm=512):
    n, d = h.shape
    return pl.pallas_call(
        _rmsnorm_body,
        grid=(n // tm,),
        in_specs=[pl.BlockSpec((tm, d), lambda i: (i, 0)),
                  pl.BlockSpec((1, d), lambda i: (0, 0))],
        out_specs=pl.BlockSpec((tm, d), lambda i: (i, 0)),
        out_shape=jax.ShapeDtypeStruct((n, d), out_dtype),
        compiler_params=_cparams(("arbitrary",)),
        name="rmsnorm",
    )(h, g.reshape(1, d))


def _mm_body(*refs, n_w, has_res, hn_scale):
    a_ref = refs[0]
    w_refs = refs[1:1 + n_w]
    pos = 1 + n_w
    res_ref = refs[pos] if has_res else None
    pos += int(has_res)
    g_ref = refs[pos] if hn_scale is not None else None
    pos += int(hn_scale is not None)
    o_ref = refs[pos]
    wb_refs = refs[pos + 1:]

    @pl.when(pl.program_id(1) == 0)
    def _():
        for w_ref, wb_ref in zip(w_refs, wb_refs):
            wb_ref[...] = w_ref[...].astype(BF16)

    a = a_ref[...]
    y = jnp.dot(a, wb_refs[0][...], preferred_element_type=F32)
    if n_w == 2:
        y2 = jnp.dot(a, wb_refs[1][...], preferred_element_type=F32)
        y = y * (1.0 / (1.0 + jnp.exp(-y2)))
    if has_res:
        y = y + res_ref[...]
    if hn_scale is None:
        o_ref[...] = y.astype(o_ref.dtype)
    else:
        tm, tn = y.shape
        lane = lax.broadcasted_iota(jnp.int32, (tm, LANES), 1)
        lo = lane < HEAD_DIM
        g = g_ref[...]
        for hb in range(tn // LANES):
            x = y[:, hb * LANES:(hb + 1) * LANES]
            x2 = x * x
            s_lo = jnp.sum(jnp.where(lo, x2, 0.0), axis=-1, keepdims=True)
            s_hi = jnp.sum(jnp.where(lo, 0.0, x2), axis=-1, keepdims=True)
            r = jnp.where(lo, lax.rsqrt(s_lo * (1.0 / HEAD_DIM) + RMS_EPS),
                          lax.rsqrt(s_hi * (1.0 / HEAD_DIM) + RMS_EPS))
            o_ref[:, hb * LANES:(hb + 1) * LANES] = ((x * r * g) * hn_scale).astype(o_ref.dtype)


def _matmul(a, ws, layer, out_dtype, res=None, col_off=0, n_out=None, head_norm=None,
            tm=512, tn=512):
    m, k = a.shape
    n_out = ws[0].shape[-1] if n_out is None else n_out
    n_w = len(ws)
    in_specs = [pl.BlockSpec((tm, k), lambda j, i: (i, 0))]
    for _ in ws:
        in_specs.append(pl.BlockSpec((None, k, tn), lambda j, i: (layer, 0, j + col_off)))
    args = [a, *ws]
    if res is not None:
        in_specs.append(pl.BlockSpec((tm, tn), lambda j, i: (i, j)))
        args.append(res)
    hn_scale = None
    if head_norm is not None:
        gain, hn_scale = head_norm
        in_specs.append(pl.BlockSpec((1, LANES), lambda j, i: (0, 0)))
        args.append(jnp.concatenate([gain, gain]).reshape(1, LANES).astype(F32))
    return pl.pallas_call(
        functools.partial(_mm_body, n_w=n_w, has_res=res is not None, hn_scale=hn_scale),
        grid=(n_out // tn, m // tm),
        in_specs=in_specs,
        out_specs=pl.BlockSpec((tm, tn), lambda j, i: (i, j)),
        out_shape=jax.ShapeDtypeStruct((m, n_out), out_dtype),
        scratch_shapes=[pltpu.VMEM((k, tn), BF16) for _ in ws],
        compiler_params=_cparams(("arbitrary", "arbitrary")),
        name="dense_matmul",
    )(*args)


def _gelu_tanh(y):
    c = math.sqrt(2.0 / math.pi)
    return y * (0.5 * (1.0 + jnp.tanh(c * (y + 0.044715 * (y * y * y)))))


def _s5_body(x_ref, b_ref, c_ref, lam_ref, d_ref, z_ref, bu0_ref, bu1_ref, st0_ref, st1_ref,
             e_ref, init_ref, *, seg_len):
    ch = lam_ref.shape[-1]
    rows = S5_TJ * S5_SEGMENTS
    n_tiles = seg_len // S5_TJ
    lr = jnp.broadcast_to(lam_ref[0:1, :], (S5_SEGMENTS, ch))
    li = jnp.broadcast_to(lam_ref[1:2, :], (S5_SEGMENTS, ch))

    def row0(t):
        return pl.multiple_of(jnp.clip(t, 0, n_tiles - 1) * rows, rows)

    def bu_tile(t, bu_ref):
        bu_ref[...] = jnp.dot(x_ref[pl.ds(row0(t), rows), :], b_ref[...],
                              preferred_element_type=F32)

    def scan_tile(bu_ref, s_re, s_im, st_ref):
        for j in range(S5_TJ):
            sl = slice(S5_SEGMENTS * j, S5_SEGMENTS * (j + 1))
            n_re = lr * s_re - li * s_im + bu_ref[sl, 0:ch]
            n_im = lr * s_im + li * s_re + bu_ref[sl, ch:2 * ch]
            s_re, s_im = n_re, n_im
            if st_ref is not None:
                st_ref[sl, 0:ch] = s_re
                st_ref[sl, ch:2 * ch] = s_im
        return s_re, s_im

    def project(t, st_ref):
        r0 = row0(t)
        y = jnp.dot(st_ref[...].astype(BF16), c_ref[...], preferred_element_type=F32)
        y = y + d_ref[...] * x_ref[pl.ds(r0, rows), :].astype(F32)
        z_ref[pl.ds(r0, rows), :] = _gelu_tanh(y).astype(z_ref.dtype)

    def pass1(i2, c):
        t = 2 * i2
        bu_tile(t + 1, bu1_ref)
        c = scan_tile(bu0_ref, c[0], c[1], None)
        bu_tile(t + 2, bu0_ref)
        return scan_tile(bu1_ref, c[0], c[1], None)

    zeros = jnp.zeros((S5_SEGMENTS, ch), F32)
    bu_tile(0, bu0_ref)
    e_re, e_im = lax.fori_loop(0, n_tiles // 2, pass1, (zeros, zeros))
    e_ref[:, 0:ch] = e_re
    e_ref[:, ch:2 * ch] = e_im

    pr, pi = lam_ref[0:1, :], lam_ref[1:2, :]
    for _ in range(seg_len.bit_length() - 1):
        pr, pi = pr * pr - pi * pi, 2.0 * pr * pi
    cr = jnp.zeros((1, ch), F32)
    ci = jnp.zeros((1, ch), F32)
    init_ref[0:1, :] = jnp.zeros((1, 2 * ch), F32)
    for k in range(S5_SEGMENTS - 1):
        er = e_ref[k:k + 1, 0:ch]
        ei = e_ref[k:k + 1, ch:2 * ch]
        cr, ci = pr * cr - pi * ci + er, pr * ci + pi * cr + ei
        init_ref[k + 1:k + 2, 0:ch] = cr
        init_ref[k + 1:k + 2, ch:2 * ch] = ci

    def pass2(i2, c):
        t = 2 * i2
        bu_tile(t + 1, bu1_ref)
        c = scan_tile(bu0_ref, c[0], c[1], st0_ref)
        project(t - 1, st1_ref)
        bu_tile(t + 2, bu0_ref)
        c = scan_tile(bu1_ref, c[0], c[1], st1_ref)
        project(t, st0_ref)
        return c

    st1_ref[...] = jnp.zeros(st1_ref.shape, F32)
    bu_tile(0, bu0_ref)
    lax.fori_loop(0, n_tiles // 2, pass2, (init_ref[:, 0:ch], init_ref[:, ch:2 * ch]))
    project(n_tiles - 1, st1_ref)


def _s5_params(lam_re, lam_im, b_re, b_im, c_re, c_im, log_step):
    g, p = lam_re.shape
    gb = S5_GROUPS_PER_BLOCK
    nb = g // gb
    lam = lax.complex(lam_re.astype(F32), lam_im.astype(F32))
    delta = jnp.exp(log_step.astype(F32))[:, None]
    lam_bar = jnp.exp(lam * delta)
    b_bar = ((lam_bar - 1.0) / lam)[..., None] * lax.complex(b_re.astype(F32), b_im.astype(F32))
    eye = jnp.eye(gb, dtype=F32)

    def blk_b(part):
        part = part.reshape(nb, gb, p, SSM_GROUP)
        return jnp.einsum('cgph,gk->cghkp', part, eye).reshape(nb, gb * SSM_GROUP, gb * p)

    def blk_c(part):
        part = part.reshape(nb, gb, SSM_GROUP, p)
        return jnp.einsum('cghp,gk->cgpkh', part, eye).reshape(nb, gb * p, gb * SSM_GROUP)

    b_blk = jnp.concatenate([blk_b(jnp.real(b_bar)), blk_b(jnp.imag(b_bar))], axis=-1)
    c_blk = jnp.concatenate([blk_c(c_re.astype(F32)), blk_c(-c_im.astype(F32))], axis=1)
    lam_v = jnp.stack([jnp.real(lam_bar).reshape(nb, gb * p),
                       jnp.imag(lam_bar).reshape(nb, gb * p)], axis=1)
    return b_blk.astype(BF16), c_blk.astype(BF16), lam_v


def _s5_scan(xn, b_blk, c_blk, lam_v, d_skip, batch):
    n, d = xn.shape
    l = n // batch
    seg_len = l // S5_SEGMENTS
    nb, fb, ch2 = b_blk.shape
    rows = S5_TJ * S5_SEGMENTS
    return pl.pallas_call(
        functools.partial(_s5_body, seg_len=seg_len),
        grid=(batch, nb),
        in_specs=[pl.BlockSpec((l, fb), lambda b, c: (b, c)),
                  pl.BlockSpec((None, fb, ch2), lambda b, c: (c, 0, 0)),
                  pl.BlockSpec((None, ch2, fb), lambda b, c: (c, 0, 0)),
                  pl.BlockSpec((None, 2, ch2 // 2), lambda b, c: (c, 0, 0)),
                  pl.BlockSpec((1, fb), lambda b, c: (0, c))],
        out_specs=pl.BlockSpec((l, fb), lambda b, c: (b, c)),
        out_shape=jax.ShapeDtypeStruct((n, d), BF16),
        scratch_shapes=[pltpu.VMEM((rows, ch2), F32), pltpu.VMEM((rows, ch2), F32),
                        pltpu.VMEM((rows, ch2), F32), pltpu.VMEM((rows, ch2), F32),
                        pltpu.VMEM((S5_SEGMENTS, ch2), F32), pltpu.VMEM((S5_SEGMENTS, ch2), F32)],
        compiler_params=_cparams(("arbitrary", "arbitrary")),
        name="s5_scan",
    )(xn, b_blk, c_blk, lam_v, d_skip.reshape(1, d).astype(F32))


def _t5_bucket(n):
    n_safe = jnp.maximum(n, 1).astype(F32)
    large = REL_MAX_EXACT + (jnp.log(n_safe / REL_MAX_EXACT)
                             / math.log(REL_MAX_DISTANCE / REL_MAX_EXACT)
                             * (REL_BUCKETS - REL_MAX_EXACT)).astype(jnp.int32)
    large = jnp.minimum(large, REL_BUCKETS - 1)
    return jnp.where(n < REL_MAX_EXACT, n, large)


ATTN_TQ = 256
M_INIT = -1e29
LOG2E = 1.4426950408889634


def _attn_bias_tables(rel_bias, tq):
    qi = jnp.arange(tq, dtype=jnp.int32)[:, None]
    ki = jnp.arange(tq, dtype=jnp.int32)[None, :]
    d_diag = qi - ki
    rb = rel_bias.astype(F32)
    rb = (rb - rb[REL_BUCKETS - 1][None, :]) * LOG2E

    def lookup(dist):
        onehot = jax.nn.one_hot(_t5_bucket(dist), REL_BUCKETS, dtype=F32)
        return jnp.einsum('qkb,bh->hqk', onehot, rb, precision=lax.Precision.HIGHEST)

    b_diag = jnp.where((d_diag >= 0)[None], lookup(jnp.maximum(d_diag, 0)), NEG_BIG)
    b_prev = lookup(d_diag + tq)
    masked = jnp.full_like(b_diag, NEG_BIG)
    return jnp.stack([jnp.concatenate([b_prev, b_diag], axis=-1),
                      jnp.concatenate([b_diag, masked], axis=-1)], axis=1)


def _attn_pair_tables(nq):
    def framed(pairs):
        idle = (0, nq, 0, 0)
        rows = [idle, idle] + pairs + [idle, idle]
        if len(rows) % 2:
            rows.append(idle)
        return rows

    wide = [(i, i, 2 * j, 0) for i in range(nq) for j in range((i - 1) // 2)]
    odd = [(i, i, i - 2, 0) for i in range(2, nq) if (i - 1) % 2]
    near = [(i, i, max(i - 1, 0), 0 if i else 1) for i in range(nq)]
    lists = [framed(wide), framed(odd), framed(near)]
    cols = list(zip(*(lists[0] + lists[1] + lists[2])))
    return [jnp.asarray(c, jnp.int32) for c in cols], [len(x) for x in lists]


def _attn_body(qrow_ref, slot_ref, kt_ref, bidx_ref, q_ref, k_ref, v_ref, bias_ref, lq_ref, sg_ref,
               o_ref, m_ref, acc_ref, vaug_ref, s0_ref, s1_ref, p0_ref, p1_ref, a0_ref, a1_ref,
               sw0_ref, sw1_ref, pw0_ref, pw1_ref, *, tq, nq, n_rows, lam_init):
    m_ref[...] = jnp.full(m_ref.shape, M_INIT, F32)
    acc_ref[...] = jnp.zeros(acc_ref.shape, F32)
    vaug_ref[:, 0:LANES] = v_ref[...]
    vaug_ref[:, LANES:2 * LANES] = jnp.ones(v_ref.shape, BF16)
    lane = lax.broadcasted_iota(jnp.int32, (tq, LANES), 1)

    def scores(e, s_ref, tk):
        q = q_ref[pl.ds(pl.multiple_of(qrow_ref[e] * tq, tq), tq), :]
        zero = jnp.zeros_like(q)
        qq = jnp.concatenate([jnp.where(lane < HEAD_DIM, q, zero),
                              jnp.where(lane < HEAD_DIM, zero, q)], axis=0)
        k = k_ref[pl.ds(pl.multiple_of(kt_ref[e] * tq, tq), tk), :]
        s_ref[...] = lax.dot_general(qq, k, (((1,), (1,)), ((), ())), preferred_element_type=F32)

    def softmax(e, s_ref, a_ref, p_ref, with_bias, rc):
        m_view = m_ref.at[slot_ref[e]]
        b_view = bias_ref.at[bidx_ref[e]]
        for c in range(tq // rc):
            b = b_view[c * rc:(c + 1) * rc, :] if with_bias else None
            for half in range(2):
                rows = slice(half * tq + c * rc, half * tq + (c + 1) * rc)
                s = s_ref[rows, :]
                if with_bias:
                    s = s + b
                m_prev = m_view[rows, :]
                m_next = jnp.maximum(m_prev, jnp.max(s, axis=1, keepdims=True))
                a_ref[rows, :] = jnp.exp2(m_prev - m_next)
                p_ref[rows, :] = jnp.exp2(s - m_next[:, 0:1]).astype(BF16)
                m_view[rows, :] = m_next

    def values(e, a_ref, p_ref, tk):
        acc = acc_ref.at[slot_ref[e]]
        v = vaug_ref[pl.ds(pl.multiple_of(kt_ref[e] * tq, tq), tk), :]
        pv = jnp.dot(p_ref[...], v, preferred_element_type=F32)
        a = a_ref[...]
        acc[:, 0:LANES] = acc[:, 0:LANES] * a + pv[:, 0:LANES]
        acc[:, LANES:2 * LANES] = acc[:, LANES:2 * LANES] * a + pv[:, LANES:2 * LANES]

    def run_pipeline(base, rows_n, with_bias, tk, sa_ref, sb_ref, pa_ref, pb_ref):
        rc = 64 if tk == tq else 32
        pa_ref[...] = jnp.zeros(pa_ref.shape, BF16)
        a0_ref[...] = jnp.ones(a0_ref.shape, F32)
        scores(base + 1, sb_ref, tk)

        def two_iterations(i2, c):
            it = base + 2 * i2
            scores(it + 2, sa_ref, tk)
            softmax(it + 1, sb_ref, a1_ref, pb_ref, with_bias, rc)
            values(it, a0_ref, pa_ref, tk)
            scores(it + 3, sb_ref, tk)
            softmax(it + 2, sa_ref, a0_ref, pa_ref, with_bias, rc)
            values(it + 1, a1_ref, pb_ref, tk)
            return c

        lax.fori_loop(0, (rows_n - 2) // 2, two_iterations, 0)

    n_wide, n_odd, n_near = n_rows
    run_pipeline(0, n_wide, False, 2 * tq, sw0_ref, sw1_ref, pw0_ref, pw1_ref)
    run_pipeline(n_wide, n_odd, False, tq, s0_ref, s1_ref, p0_ref, p1_ref)
    run_pipeline(n_wide + n_odd, n_near, True, 2 * tq, sw0_ref, sw1_ref, pw0_ref, pw1_ref)

    lq = lq_ref[...]
    lam = (jnp.exp(jnp.sum(lq[0:1] * lq[1:2], axis=1, keepdims=True))
           - jnp.exp(jnp.sum(lq[2:3] * lq[3:4], axis=1, keepdims=True)) + lam_init)

    def finalize(i, c):
        acc = acc_ref[i]
        o_all = acc[:, 0:LANES] / acc[:, LANES:2 * LANES]
        o = o_all[0:tq] - lam * o_all[tq:2 * tq]
        ms = jnp.mean(o * o, axis=-1, keepdims=True)
        o = (o * lax.rsqrt(ms + RMS_EPS) * sg_ref[...]) * (1.0 - lam_init)
        o_ref[pl.ds(pl.multiple_of(i * tq, tq), tq), :] = o.astype(o_ref.dtype)
        return c

    lax.fori_loop(0, nq, finalize, 0)


def _diff_attention(qn, kn, vb, bias_tabs, lambda_qk, subln_g, lam_init, batch):
    n, d = qn.shape
    l = n // batch
    tq = ATTN_TQ
    nq = l // tq
    nh = d // LANES
    tables, n_rows = _attn_pair_tables(nq)
    row_blk = pl.BlockSpec((l, LANES), lambda b, h, *_: (b, h))
    grid_spec = pltpu.PrefetchScalarGridSpec(
        num_scalar_prefetch=4,
        grid=(batch, nh),
        in_specs=[row_blk, row_blk, row_blk,
                  pl.BlockSpec((None, 2, tq, 2 * tq), lambda b, h, *_: (h, 0, 0, 0)),
                  pl.BlockSpec((4, HEAD_DIM), lambda b, h, *_: (0, 0)),
                  pl.BlockSpec((1, LANES), lambda b, h, *_: (0, 0))],
        out_specs=row_blk,
        scratch_shapes=[pltpu.VMEM((nq + 1, 2 * tq, LANES), F32),
                        pltpu.VMEM((nq + 1, 2 * tq, 2 * LANES), F32),
                        pltpu.VMEM((l, 2 * LANES), BF16),
                        pltpu.VMEM((2 * tq, tq), F32), pltpu.VMEM((2 * tq, tq), F32),
                        pltpu.VMEM((2 * tq, tq), BF16), pltpu.VMEM((2 * tq, tq), BF16),
                        pltpu.VMEM((2 * tq, LANES), F32), pltpu.VMEM((2 * tq, LANES), F32),
                        pltpu.VMEM((2 * tq, 2 * tq), F32), pltpu.VMEM((2 * tq, 2 * tq), F32),
                        pltpu.VMEM((2 * tq, 2 * tq), BF16), pltpu.VMEM((2 * tq, 2 * tq), BF16)],
    )
    return pl.pallas_call(
        functools.partial(_attn_body, tq=tq, nq=nq, n_rows=tuple(n_rows), lam_init=lam_init),
        grid_spec=grid_spec,
        out_shape=jax.ShapeDtypeStruct((n, d), BF16),
        compiler_params=_cparams(("arbitrary", "arbitrary")),
        name="diff_attention",
    )(*tables, qn, kn, vb, bias_tabs, lambda_qk.astype(F32),
      subln_g.reshape(1, LANES).astype(F32))


ROUTER_E0 = N_EXPERT_GROUPS
META_COLS = 8


def _router_body(h_ref, g_ref, wr_ref, br_ref, xn_ref, meta_ref, cnt_ref, base_ref):
    tm = h_ref.shape[0]

    @pl.when(pl.program_id(0) == 0)
    def _():
        base_ref[...] = jnp.zeros(base_ref.shape, F32)

    x = h_ref[...]
    ms = jnp.mean(x * x, axis=-1, keepdims=True)
    xn = x * lax.rsqrt(ms + RMS_EPS) * g_ref[...]
    xn_ref[...] = xn.astype(xn_ref.dtype)
    x_hi = xn.astype(BF16)
    x_lo = (xn - x_hi.astype(F32)).astype(BF16)
    logits = (jnp.dot(x_hi, wr_ref[0], preferred_element_type=F32)
              + (jnp.dot(x_hi, wr_ref[1], preferred_element_type=F32)
                 + jnp.dot(x_lo, wr_ref[0], preferred_element_type=F32))) + br_ref[...]
    lane = lax.broadcasted_iota(jnp.int32, (tm, LANES), 1)
    neg_inf = -jnp.inf

    def first_argmax(vals):
        vmax = jnp.max(vals, axis=-1, keepdims=True)
        idx = jnp.min(jnp.where(vals == vmax, lane, LANES), axis=-1, keepdims=True)
        return vmax, idx

    is_g = lane < N_EXPERT_GROUPS
    gmax, g_idx = first_argmax(jnp.where(is_g, logits, neg_inf))
    g_w = 1.0 / jnp.sum(jnp.where(is_g, jnp.exp(logits - gmax), 0.0), axis=-1, keepdims=True)
    e_lo = ROUTER_E0 + EXPERTS_PER_GROUP * g_idx
    elog = jnp.where((lane >= e_lo) & (lane < e_lo + EXPERTS_PER_GROUP), logits, neg_inf)
    v0, i0 = first_argmax(elog)
    v1, i1 = first_argmax(jnp.where(lane == i0, neg_inf, elog))
    t = jnp.exp(v1 - v0)
    w0 = g_w / (1.0 + t)
    w1 = g_w * t / (1.0 + t)

    sel0 = lane == i0
    sel1 = lane == i1
    onehot = jnp.where(sel0 | sel1, 1.0, 0.0)
    r_i = lax.broadcasted_iota(jnp.int32, (tm, tm), 0)
    c_i = lax.broadcasted_iota(jnp.int32, (tm, tm), 1)
    tri = jnp.where(c_i < r_i, 1.0, 0.0).astype(BF16)
    before = jnp.dot(tri, onehot.astype(BF16), preferred_element_type=F32) + base_ref[...]
    rank0 = jnp.sum(jnp.where(sel0, before, 0.0), axis=-1, keepdims=True)
    rank1 = jnp.sum(jnp.where(sel1, before, 0.0), axis=-1, keepdims=True)
    base_ref[...] = base_ref[...] + jnp.sum(onehot, axis=0, keepdims=True)
    cnt_ref[...] = base_ref[...]

    eid0 = (i0 - ROUTER_E0).astype(F32)
    eid1 = (i1 - ROUTER_E0).astype(F32)
    meta = jnp.where(lane == 0, eid0, jnp.where(lane == 1, eid1, jnp.where(
        lane == 2, rank0, jnp.where(lane == 3, rank1, jnp.where(
            lane == 4, w0, jnp.where(lane == 5, w1, 0.0))))))
    meta_ref[...] = meta[:, 0:META_COLS]


def _router(h, g, wg1, bg1, wg2, bg2, tm=256):
    n, d = h.shape
    wr = jnp.concatenate([wg1.astype(F32),
                          wg2.astype(F32).transpose(1, 0, 2).reshape(d, N_EXPERTS)], axis=1)
    wr = jnp.pad(wr, ((0, 0), (0, LANES - wr.shape[1])))
    wr_hi = wr.astype(BF16)
    wr = jnp.stack([wr_hi, (wr - wr_hi.astype(F32)).astype(BF16)])
    br = jnp.pad(jnp.concatenate([bg1.astype(F32), bg2.astype(F32).reshape(-1)]),
                 (0, LANES - N_EXPERT_GROUPS - N_EXPERTS)).reshape(1, LANES)
    return pl.pallas_call(
        _router_body,
        grid=(n // tm,),
        in_specs=[pl.BlockSpec((tm, d), lambda i: (i, 0)),
                  pl.BlockSpec((1, d), lambda i: (0, 0)),
                  pl.BlockSpec((2, d, LANES), lambda i: (0, 0, 0)),
                  pl.BlockSpec((1, LANES), lambda i: (0, 0))],
        out_specs=[pl.BlockSpec((tm, d), lambda i: (i, 0)),
                   pl.BlockSpec((tm, META_COLS), lambda i: (i, 0)),
                   pl.BlockSpec((1, LANES), lambda i: (0, 0))],
        out_shape=[jax.ShapeDtypeStruct((n, d), F32),
                   jax.ShapeDtypeStruct((n, META_COLS), F32),
                   jax.ShapeDtypeStruct((1, LANES), F32)],
        scratch_shapes=[pltpu.VMEM((1, LANES), F32)],
        compiler_params=_cparams(("arbitrary",)),
        name="moe_router",
    )(h, g.reshape(1, d).astype(F32), wr, br)


def _invert_body(dest_ref, rt_ref, *, ts, n_rows):
    t = pl.program_id(0)

    @pl.when(t == 0)
    def _():
        def zero(r, c):
            rt_ref[r] = 0
            return c

        lax.fori_loop(0, n_rows, zero, 0, unroll=8)

    def body(r, c):
        tok = t * ts + r
        rt_ref[dest_ref[0, 2 * r]] = tok
        rt_ref[dest_ref[0, 2 * r + 1]] = tok
        return c

    lax.fori_loop(0, ts, body, 0, unroll=True)


def _invert(dest3, n_rows):
    nt, _, ts2 = dest3.shape
    ts = ts2 // 2
    return pl.pallas_call(
        functools.partial(_invert_body, ts=ts, n_rows=n_rows),
        grid=(nt,),
        in_specs=[pl.BlockSpec((None, 1, ts2), lambda t: (t, 0, 0), memory_space=pltpu.SMEM)],
        out_specs=pl.BlockSpec(memory_space=pltpu.SMEM),
        out_shape=jax.ShapeDtypeStruct((n_rows,), jnp.int32),
        compiler_params=_cparams(("arbitrary",)),
        name="moe_invert",
    )(dest3)


def _combine_body(d0_ref, dn_ref, h_ref, w_ref, y_hbm, *rest, ts, nt, with_norm):
    if with_norm:
        g_ref, o_ref, xn_ref, y0_ref, y1_ref, sem = rest
    else:
        o_ref, y0_ref, y1_ref, sem = rest
    t = pl.program_id(0)
    slot = t % 2

    def gather_start(dest_ref, dst):
        def issue(r, c):
            pltpu.make_async_copy(y_hbm.at[pl.ds(dest_ref[0, 2 * r], 1), :],
                                  y0_ref.at[dst, pl.ds(r, 1), :], sem.at[dst, 0]).start()
            pltpu.make_async_copy(y_hbm.at[pl.ds(dest_ref[0, 2 * r + 1], 1), :],
                                  y1_ref.at[dst, pl.ds(r, 1), :], sem.at[dst, 1]).start(priority=1)
            return c

        lax.fori_loop(0, ts, issue, 0, unroll=True)

    @pl.when(t == 0)
    def _():
        gather_start(d0_ref, 0)

    @pl.when(t + 1 < nt)
    def _():
        gather_start(dn_ref, 1 - slot)

    pltpu.make_async_copy(y_hbm.at[pl.ds(0, ts), :], y0_ref.at[slot], sem.at[slot, 0]).wait()
    pltpu.make_async_copy(y_hbm.at[pl.ds(0, ts), :], y1_ref.at[slot], sem.at[slot, 1]).wait()
    w = w_ref[...]
    out = h_ref[...] + w[:, 4:5] * y0_ref[slot] + w[:, 5:6] * y1_ref[slot]
    o_ref[...] = out
    if with_norm:
        ms = jnp.mean(out * out, axis=-1, keepdims=True)
        xn_ref[...] = (out * lax.rsqrt(ms + RMS_EPS) * g_ref[...]).astype(xn_ref.dtype)


def _combine(h, meta, y_sorted, dest3, next_norm_g=None):
    n, d = h.shape
    nt, _, ts2 = dest3.shape
    ts = ts2 // 2
    with_norm = next_norm_g is not None
    row_blk = pl.BlockSpec((ts, d), lambda t: (t, 0))
    in_specs = [pl.BlockSpec((None, 1, 2 * ts), lambda t: (0, 0, 0), memory_space=pltpu.SMEM),
                pl.BlockSpec((None, 1, 2 * ts), lambda t: (jnp.minimum(t + 1, nt - 1), 0, 0),
                             memory_space=pltpu.SMEM),
                row_blk,
                pl.BlockSpec((ts, META_COLS), lambda t: (t, 0)),
                pl.BlockSpec(memory_space=pl.ANY)]
    args = [dest3, dest3, h, meta, y_sorted]
    out_specs, out_shape = row_blk, jax.ShapeDtypeStruct((n, d), F32)
    if with_norm:
        in_specs.append(pl.BlockSpec((1, d), lambda t: (0, 0)))
        args.append(next_norm_g.reshape(1, d).astype(F32))
        out_specs = [row_blk, row_blk]
        out_shape = [out_shape, jax.ShapeDtypeStruct((n, d), BF16)]
    return pl.pallas_call(
        functools.partial(_combine_body, ts=ts, nt=nt, with_norm=with_norm),
        grid=(nt,),
        in_specs=in_specs,
        out_specs=out_specs,
        out_shape=out_shape,
        scratch_shapes=[pltpu.VMEM((2, ts, d), F32), pltpu.VMEM((2, ts, d), F32),
                        pltpu.SemaphoreType.DMA((2, 2))],
        compiler_params=_cparams(("arbitrary",)),
        name="moe_combine",
    )(*args)


EXPERT_TM = 256
COMBINE_TS = 256
EXPERT_VMEM_LIMIT = 56 * 1024 * 1024


def _expert_body(te_ref, valid_ref, rfirst_ref, nxt_ref, ntile_ref, last_ref,
                 rt0_ref, rtn_ref, xn_hbm, wg_hbm, wu_hbm, wd_hbm, o_ref,
                 xbuf, stg_g, stg_u, stg_d, wgb, wub, wdb, sem_x, sem_w, *, tm, layer):
    t = pl.program_id(0)
    e = te_ref[t]
    slot = t % 2

    def weight_copies(ex):
        return (pltpu.make_async_copy(wg_hbm.at[layer, ex], stg_g, sem_w.at[0]),
                pltpu.make_async_copy(wu_hbm.at[layer, ex], stg_u, sem_w.at[1]),
                pltpu.make_async_copy(wd_hbm.at[layer, ex], stg_d, sem_w.at[2]))

    def gather_start(rt_ref, dst_slot):
        def issue(r2, c):
            for par in range(2):
                r = 2 * r2 + par
                pltpu.make_async_copy(xn_hbm.at[pl.ds(rt_ref[0, r], 1), :],
                                      xbuf.at[dst_slot, pl.ds(r, 1), :],
                                      sem_x.at[dst_slot]).start(priority=par)
            return c

        lax.fori_loop(0, tm // 2, issue, 0, unroll=True)

    def gather_wait(src_slot):
        pltpu.make_async_copy(xn_hbm.at[pl.ds(0, tm), :], xbuf.at[src_slot],
                              sem_x.at[src_slot]).wait()

    @pl.when(t == 0)
    def _():
        for cp in weight_copies(e):
            cp.start()
        gather_start(rt0_ref, 0)

    @pl.when(rfirst_ref[t] == 1)
    def _():
        for cp in weight_copies(e):
            cp.wait()
        rows = 256

        def cast_rows(r, c):
            r0 = pl.multiple_of(r * rows, rows)
            wgb[pl.ds(r0, rows), :] = stg_g[pl.ds(r0, rows), :].astype(BF16)
            wub[pl.ds(r0, rows), :] = stg_u[pl.ds(r0, rows), :].astype(BF16)
            return c

        lax.fori_loop(0, stg_g.shape[0] // rows, cast_rows, 0)

        def cast_rows_d(r, c):
            r0 = pl.multiple_of(r * rows, rows)
            wdb[pl.ds(r0, rows), :] = stg_d[pl.ds(r0, rows), :].astype(BF16)
            return c

        lax.fori_loop(0, stg_d.shape[0] // rows, cast_rows_d, 0)

        @pl.when(nxt_ref[t] >= 0)
        def _():
            for cp in weight_copies(nxt_ref[t]):
                cp.start()

    @pl.when(valid_ref[t] == 1)
    def _():
        @pl.when(last_ref[t] == 0)
        def _():
            gather_start(rtn_ref, 1 - slot)

        gather_wait(slot)
        xb = xbuf[slot].astype(BF16)
        g = jnp.dot(xb, wgb[...], preferred_element_type=F32)
        u = jnp.dot(xb, wub[...], preferred_element_type=F32)
        hdn = (g * (1.0 / (1.0 + jnp.exp(-g)))) * u
        o_ref[...] = jnp.dot(hdn.astype(BF16), wdb[...], preferred_element_type=F32)

    @pl.when(valid_ref[t] == 0)
    def _():
        o_ref[...] = jnp.zeros(o_ref.shape, F32)


def _experts(xn, row_token, plan, w_gate, w_up, w_down, layer, tm):
    _, d = xn.shape
    dh = w_gate.shape[-1]
    r_total = row_token.shape[0]
    n_tiles = r_total // tm
    rt3 = row_token.reshape(n_tiles, 1, tm)
    grid_spec = pltpu.PrefetchScalarGridSpec(
        num_scalar_prefetch=6,
        grid=(n_tiles,),
        in_specs=[
            pl.BlockSpec((None, 1, tm), lambda t, *_: (0, 0, 0), memory_space=pltpu.SMEM),
            pl.BlockSpec((None, 1, tm), lambda t, te, va, rf, nx, ntile, *_: (ntile[t], 0, 0),
                         memory_space=pltpu.SMEM),
            pl.BlockSpec(memory_space=pl.ANY), pl.BlockSpec(memory_space=pl.ANY),
            pl.BlockSpec(memory_space=pl.ANY), pl.BlockSpec(memory_space=pl.ANY),
        ],
        out_specs=pl.BlockSpec((tm, d), lambda t, *_: (t, 0)),
        scratch_shapes=[pltpu.VMEM((2, tm, d), F32),
                        pltpu.VMEM((d, dh), F32), pltpu.VMEM((d, dh), F32), pltpu.VMEM((dh, d), F32),
                        pltpu.VMEM((d, dh), BF16), pltpu.VMEM((d, dh), BF16), pltpu.VMEM((dh, d), BF16),
                        pltpu.SemaphoreType.DMA((2,)), pltpu.SemaphoreType.DMA((3,))],
    )
    return pl.pallas_call(
        functools.partial(_expert_body, tm=tm, layer=layer),
        grid_spec=grid_spec,
        out_shape=jax.ShapeDtypeStruct((r_total, d), F32),
        compiler_params=pltpu.CompilerParams(dimension_semantics=("arbitrary",),
                                             vmem_limit_bytes=EXPERT_VMEM_LIMIT),
        name="moe_experts",
    )(*plan, rt3, rt3, xn, w_gate, w_up, w_down)


def _moe_plan(meta, counts_row, n_tiles, tm, ts):
    counts = counts_row[0, ROUTER_E0:ROUTER_E0 + N_EXPERTS].astype(jnp.int32)
    ntile_e = (counts + tm - 1) // tm
    tile_end = jnp.cumsum(ntile_e)
    starts = (tile_end - ntile_e) * tm
    eid = meta[:, 0:2].astype(jnp.int32)
    rank = meta[:, 2:4].astype(jnp.int32)
    dest = (starts[eid] + rank).astype(jnp.int32)
    dest3 = dest.reshape(dest.shape[0] // ts, 1, 2 * ts)

    n_used = tile_end[-1]
    tt = jnp.arange(n_tiles, dtype=jnp.int32)
    valid = tt < n_used
    last_used = n_used - 1
    tt_c = jnp.minimum(tt, last_used)
    e_t = jnp.searchsorted(tile_end, tt_c, side='right').astype(jnp.int32)
    minus1 = jnp.full((1,), -1, jnp.int32)
    run_first = (valid & (e_t != jnp.concatenate([minus1, e_t[:-1]]))).astype(jnp.int32)
    ar = jnp.arange(N_EXPERTS, dtype=jnp.int32)
    later = (ar[None, :] > ar[:, None]) & (counts > 0)[None, :]
    nxt_e = jnp.min(jnp.where(later, ar[None, :], N_EXPERTS), axis=1)
    nxt_e = jnp.where(nxt_e == N_EXPERTS, -1, nxt_e).astype(jnp.int32)
    next_tile = jnp.minimum(tt + 1, last_used).astype(jnp.int32)
    is_last = (tt >= last_used).astype(jnp.int32)
    plan = (e_t, valid.astype(jnp.int32), run_first, nxt_e[e_t], next_tile, is_last)
    return dest3, plan


def _hier_moe(h, g, wg1, bg1, wg2, bg2, w_gate, w_up, w_down, layer, next_norm_g=None,
              tm=EXPERT_TM):
    n, _ = h.shape
    n_tiles = 2 * n // tm + N_EXPERTS
    xn, meta, counts_row = _router(h, g, wg1, bg1, wg2, bg2)
    dest3, plan = _moe_plan(meta, counts_row, n_tiles, tm, COMBINE_TS)
    row_token = _invert(dest3, n_tiles * tm)
    y_sorted = _experts(xn, row_token, plan, w_gate, w_up, w_down, layer, tm)
    return _combine(h, meta, y_sorted, dest3, next_norm_g)


def _time_permute(h, batch, inverse=False):
    n, d = h.shape
    l = n // batch
    shape = (batch, l // S5_SEGMENTS, S5_SEGMENTS, d) if inverse else (batch, S5_SEGMENTS, l // S5_SEGMENTS, d)
    return h.reshape(shape).transpose(0, 2, 1, 3).reshape(n, d)


def kernel(x, norm_mix_g, norm_ffn_g, ssm_lam_re, ssm_lam_im, ssm_b_re, ssm_b_im, ssm_c_re, ssm_c_im, ssm_d, ssm_log_step, ssm_w_glu1, ssm_w_glu2, kv_norm_g, w_kv, k_norm_g, w_q, q_norm_g, lambda_qk, subln_g, w_o, rel_bias, router_group_w, router_group_b, router_expert_w, router_expert_b, w_gate, w_up, w_down):
    batch, l, d = x.shape
    n = batch * l
    h = _time_permute(x.astype(F32).reshape(n, d), batch)
    kn = vb = bias_tabs = None
    xn = _rmsnorm(h, norm_mix_g[0], BF16)
    for layer in range(DEPTH):
        if layer < N_A_LAYERS:
            b_blk, c_blk, lam_v = _s5_params(ssm_lam_re[layer], ssm_lam_im[layer], ssm_b_re[layer],
                                             ssm_b_im[layer], ssm_c_re[layer], ssm_c_im[layer],
                                             ssm_log_step[layer])
            z = _s5_scan(xn, b_blk, c_blk, lam_v, ssm_d[layer], batch)
            h = _matmul(z, [ssm_w_glu1, ssm_w_glu2], layer, F32, res=h)
        else:
            j = layer - N_A_LAYERS
            lam_init = 0.8 - 0.6 * math.exp(-0.3 * layer)
            qn = _matmul(xn, [w_q], j, BF16, head_norm=(q_norm_g[j], HEAD_DIM ** -0.5 * LOG2E),
                         tn=PROJ_TN)
            o = _diff_attention(qn, kn, vb, bias_tabs, lambda_qk[j], subln_g[j], lam_init, batch)
            h = _matmul(o, [w_o], j, F32, res=h, tn=PROJ_TN)
        fuse_next = layer + 1 < DEPTH and layer != N_A_LAYERS - 1
        moe_out = _hier_moe(h, norm_ffn_g[layer], router_group_w[layer], router_group_b[layer],
                            router_expert_w[layer], router_expert_b[layer], w_gate, w_up, w_down,
                            layer, norm_mix_g[layer + 1] if fuse_next else None)
        h, xn = moe_out if fuse_next else (moe_out, None)
        if layer == N_A_LAYERS - 1:
            h = _time_permute(h, batch, inverse=True)
            xkv = _rmsnorm(h, kv_norm_g, BF16)
            w_kv3 = w_kv.reshape(1, d, 2 * d)
            kn = _matmul(xkv, [w_kv3], 0, BF16, n_out=d, head_norm=(k_norm_g, 1.0), tn=PROJ_TN)
            vb = _matmul(xkv, [w_kv3], 0, BF16, col_off=d // PROJ_TN, n_out=d, tn=PROJ_TN)
            bias_tabs = _attn_bias_tables(rel_bias, ATTN_TQ)
            xn = _rmsnorm(h, norm_mix_g[layer + 1], BF16)
    return h.reshape(batch, l, d)
```

```python
import functools
import math

import jax
import jax.numpy as jnp
from jax import lax
from jax.experimental import pallas as pl
from jax.experimental.pallas import tpu as pltpu

F32 = jnp.float32
BF16 = jnp.bfloat16

DEPTH = 4
N_A_LAYERS = DEPTH // 2
SSM_GROUP = 16
SSM_STATE = 64
HEAD_DIM = 64
REL_BUCKETS = 32
REL_MAX_EXACT = REL_BUCKETS // 2
REL_MAX_DISTANCE = 128
N_EXPERT_GROUPS = 4
EXPERTS_PER_GROUP = 4
N_EXPERTS = N_EXPERT_GROUPS * EXPERTS_PER_GROUP
RMS_EPS = 1e-6

LANES = 128
SUBLANES = 8
VMEM_LIMIT = 48 * 1024 * 1024
NEG_BIG = -1e30

S5_SEGMENTS = SUBLANES
S5_GROUPS_PER_BLOCK = 16
S5_TJ = 32
PROJ_TN = 1024


def _cparams(sem):
    return pltpu.CompilerParams(dimension_semantics=sem, vmem_limit_bytes=VMEM_LIMIT)


def _rmsnorm_body(h_ref, g_ref, o_ref):
    x = h_ref[...]
    ms = jnp.mean(x * x, axis=-1, keepdims=True)
    o_ref[...] = (x * lax.rsqrt(ms + RMS_EPS) * g_ref[...]).astype(o_ref.dtype)


def _rmsnorm(h, g, out_dtype, tm=512):
    n, d = h.shape
    return pl.pallas_call(
        _rmsnorm_body,
        grid=(n // tm,),
        in_specs=[pl.BlockSpec((tm, d), lambda i: (i, 0)),
                  pl.BlockSpec((1, d), lambda i: (0, 0))],
        out_specs=pl.BlockSpec((tm, d), lambda i: (i, 0)),
        out_shape=jax.ShapeDtypeStruct((n, d), out_dtype),
        compiler_params=_cparams(("arbitrary",)),
        name="rmsnorm",
    )(h, g.reshape(1, d))


def _mm_body(*refs, n_w, has_res, hn_scale):
    a_ref = refs[0]
    w_refs = refs[1:1 + n_w]
    pos = 1 + n_w
    res_ref = refs[pos] if has_res else None
    pos += int(has_res)
    g_ref = refs[pos] if hn_scale is not None else None
    pos += int(hn_scale is not None)
    o_ref = refs[pos]
    wb_refs = refs[pos + 1:]

    @pl.when(pl.program_id(1) == 0)
    def _():
        for w_ref, wb_ref in zip(w_refs, wb_refs):
            wb_ref[...] = w_ref[...].astype(BF16)

    a = a_ref[...]
    y = jnp.dot(a, wb_refs[0][...], preferred_element_type=F32)
    if n_w == 2:
        y2 = jnp.dot(a, wb_refs[1][...], preferred_element_type=F32)
        y = y * (1.0 / (1.0 + jnp.exp(-y2)))
    if has_res:
        y = y + res_ref[...]
    if hn_scale is None:
        o_ref[...] = y.astype(o_ref.dtype)
    else:
        tm, tn = y.shape
        lane = lax.broadcasted_iota(jnp.int32, (tm, LANES), 1)
        lo = lane < HEAD_DIM
        g = g_ref[...]
        for hb in range(tn // LANES):
            x = y[:, hb * LANES:(hb + 1) * LANES]
            x2 = x * x
            s_lo = jnp.sum(jnp.where(lo, x2, 0.0), axis=-1, keepdims=True)
            s_hi = jnp.sum(jnp.where(lo, 0.0, x2), axis=-1, keepdims=True)
            r = jnp.where(lo, lax.rsqrt(s_lo * (1.0 / HEAD_DIM) + RMS_EPS),
                          lax.rsqrt(s_hi * (1.0 / HEAD_DIM) + RMS_EPS))
            o_ref[:, hb * LANES:(hb + 1) * LANES] = ((x * r * g) * hn_scale).astype(o_ref.dtype)


def _matmul(a, ws, layer, out_dtype, res=None, col_off=0, n_out=None, head_norm=None,
            tm=512, tn=512):
    m, k = a.shape
    n_out = ws[0].shape[-1] if n_out is None else n_out
    n_w = len(ws)
    in_specs = [pl.BlockSpec((tm, k), lambda j, i: (i, 0))]
    for _ in ws:
        in_specs.append(pl.BlockSpec((None, k, tn), lambda j, i: (layer, 0, j + col_off)))
    args = [a, *ws]
    if res is not None:
        in_specs.append(pl.BlockSpec((tm, tn), lambda j, i: (i, j)))
        args.append(res)
    hn_scale = None
    if head_norm is not None:
        gain, hn_scale = head_norm
        in_specs.append(pl.BlockSpec((1, LANES), lambda j, i: (0, 0)))
        args.append(jnp.concatenate([gain, gain]).reshape(1, LANES).astype(F32))
    return pl.pallas_call(
        functools.partial(_mm_body, n_w=n_w, has_res=res is not None, hn_scale=hn_scale),
        grid=(n_out // tn, m // tm),
        in_specs=in_specs,
        out_specs=pl.BlockSpec((tm, tn), lambda j, i: (i, j)),
        out_shape=jax.ShapeDtypeStruct((m, n_out), out_dtype),
        scratch_shapes=[pltpu.VMEM((k, tn), BF16) for _ in ws],
        compiler_params=_cparams(("arbitrary", "arbitrary")),
        name="dense_matmul",
    )(*args)


def _gelu_tanh(y):
    c = math.sqrt(2.0 / math.pi)
    return y * (0.5 * (1.0 + jnp.tanh(c * (y + 0.044715 * (y * y * y)))))


def _s5_body(x_ref, b_ref, c_ref, lam_ref, d_ref, z_ref, bu0_ref, bu1_ref, st0_ref, st1_ref,
             e_ref, init_ref, *, seg_len):
    ch = lam_ref.shape[-1]
    rows = S5_TJ * S5_SEGMENTS
    n_tiles = seg_len // S5_TJ
    lr = jnp.broadcast_to(lam_ref[0:1, :], (S5_SEGMENTS, ch))
    li = jnp.broadcast_to(lam_ref[1:2, :], (S5_SEGMENTS, ch))

    def row0(t):
        return pl.multiple_of(jnp.clip(t, 0, n_tiles - 1) * rows, rows)

    def bu_tile(t, bu_ref):
        bu_ref[...] = jnp.dot(x_ref[pl.ds(row0(t), rows), :], b_ref[...],
                              preferred_element_type=F32)

    def scan_tile(bu_ref, s_re, s_im, st_ref):
        for j in range(S5_TJ):
            sl = slice(S5_SEGMENTS * j, S5_SEGMENTS * (j + 1))
            n_re = lr * s_re - li * s_im + bu_ref[sl, 0:ch]
            n_im = lr * s_im + li * s_re + bu_ref[sl, ch:2 * ch]
            s_re, s_im = n_re, n_im
            if st_ref is not None:
                st_ref[sl, 0:ch] = s_re
                st_ref[sl, ch:2 * ch] = s_im
        return s_re, s_im

    def project(t, st_ref):
        r0 = row0(t)
        y = jnp.dot(st_ref[...].astype(BF16), c_ref[...], preferred_element_type=F32)
        y = y + d_ref[...] * x_ref[pl.ds(r0, rows), :].astype(F32)
        z_ref[pl.ds(r0, rows), :] = _gelu_tanh(y).astype(z_ref.dtype)

    def pass1(i2, c):
        t = 2 * i2
        bu_tile(t + 1, bu1_ref)
        c = scan_tile(bu0_ref, c[0], c[1], None)
        bu_tile(t + 2, bu0_ref)
        return scan_tile(bu1_ref, c[0], c[1], None)

    zeros = jnp.zeros((S5_SEGMENTS, ch), F32)
    bu_tile(0, bu0_ref)
    e_re, e_im = lax.fori_loop(0, n_tiles // 2, pass1, (zeros, zeros))
    e_ref[:, 0:ch] = e_re
    e_ref[:, ch:2 * ch] = e_im

    pr, pi = lam_ref[0:1, :], lam_ref[1:2, :]
    for _ in range(seg_len.bit_length() - 1):
        pr, pi = pr * pr - pi * pi, 2.0 * pr * pi
    cr = jnp.zeros((1, ch), F32)
    ci = jnp.zeros((1, ch), F32)
    init_ref[0:1, :] = jnp.zeros((1, 2 * ch), F32)
    for k in range(S5_SEGMENTS - 1):
        er = e_ref[k:k + 1, 0:ch]
        ei = e_ref[k:k + 1, ch:2 * ch]
        cr, ci = pr * cr - pi * ci + er, pr * ci + pi * cr + ei
        init_ref[k + 1:k + 2, 0:ch] = cr
        init_ref[k + 1:k + 2, ch:2 * ch] = ci

    def pass2(i2, c):
        t = 2 * i2
        bu_tile(t + 1, bu1_ref)
        c = scan_tile(bu0_ref, c[0], c[1], st0_ref)
        project(t - 1, st1_ref)
        bu_tile(t + 2, bu0_ref)
        c = scan_tile(bu1_ref, c[0], c[1], st1_ref)
        project(t, st0_ref)
        return c

    st1_ref[...] = jnp.zeros(st1_ref.shape, F32)
    bu_tile(0, bu0_ref)
    lax.fori_loop(0, n_tiles // 2, pass2, (init_ref[:, 0:ch], init_ref[:, ch:2 * ch]))
    project(n_tiles - 1, st1_ref)


def _s5_params(lam_re, lam_im, b_re, b_im, c_re, c_im, log_step):
    g, p = lam_re.shape
    gb = S5_GROUPS_PER_BLOCK
    nb = g // gb
    lam = lax.complex(lam_re.astype(F32), lam_im.astype(F32))
    delta = jnp.exp(log_step.astype(F32))[:, None]
    lam_bar = jnp.exp(lam * delta)
    b_bar = ((lam_bar - 1.0) / lam)[..., None] * lax.complex(b_re.astype(F32), b_im.astype(F32))
    eye = jnp.eye(gb, dtype=F32)

    def blk_b(part):
        part = part.reshape(nb, gb, p, SSM_GROUP)
        return jnp.einsum('cgph,gk->cghkp', part, eye).reshape(nb, gb * SSM_GROUP, gb * p)

    def blk_c(part):
        part = part.reshape(nb, gb, SSM_GROUP, p)
        return jnp.einsum('cghp,gk->cgpkh', part, eye).reshape(nb, gb * p, gb * SSM_GROUP)

    b_blk = jnp.concatenate([blk_b(jnp.real(b_bar)), blk_b(jnp.imag(b_bar))], axis=-1)
    c_blk = jnp.concatenate([blk_c(c_re.astype(F32)), blk_c(-c_im.astype(F32))], axis=1)
    lam_v = jnp.stack([jnp.real(lam_bar).reshape(nb, gb * p),
                       jnp.imag(lam_bar).reshape(nb, gb * p)], axis=1)
    return b_blk.astype(BF16), c_blk.astype(BF16), lam_v


def _s5_scan(xn, b_blk, c_blk, lam_v, d_skip, batch):
    n, d = xn.shape
    l = n // batch
    seg_len = l // S5_SEGMENTS
    nb, fb, ch2 = b_blk.shape
    rows = S5_TJ * S5_SEGMENTS
    return pl.pallas_call(
        functools.partial(_s5_body, seg_len=seg_len),
        grid=(batch, nb),
        in_specs=[pl.BlockSpec((l, fb), lambda b, c: (b, c)),
                  pl.BlockSpec((None, fb, ch2), lambda b, c: (c, 0, 0)),
                  pl.BlockSpec((None, ch2, fb), lambda b, c: (c, 0, 0)),
                  pl.BlockSpec((None, 2, ch2 // 2), lambda b, c: (c, 0, 0)),
                  pl.BlockSpec((1, fb), lambda b, c: (0, c))],
        out_specs=pl.BlockSpec((l, fb), lambda b, c: (b, c)),
        out_shape=jax.ShapeDtypeStruct((n, d), BF16),
        scratch_shapes=[pltpu.VMEM((rows, ch2), F32), pltpu.VMEM((rows, ch2), F32),
                        pltpu.VMEM((rows, ch2), F32), pltpu.VMEM((rows, ch2), F32),
                        pltpu.VMEM((S5_SEGMENTS, ch2), F32), pltpu.VMEM((S5_SEGMENTS, ch2), F32)],
        compiler_params=_cparams(("arbitrary", "arbitrary")),
        name="s5_scan",
    )(xn, b_blk, c_blk, lam_v, d_skip.reshape(1, d).astype(F32))


def _t5_bucket(n):
    n_safe = jnp.maximum(n, 1).astype(F32)
    large = REL_MAX_EXACT + (jnp.log(n_safe / REL_MAX_EXACT)
                             / math.log(REL_MAX_DISTANCE / REL_MAX_EXACT)
                             * (REL_BUCKETS - REL_MAX_EXACT)).astype(jnp.int32)
    large = jnp.minimum(large, REL_BUCKETS - 1)
    return jnp.where(n < REL_MAX_EXACT, n, large)


ATTN_TQ = 256
M_INIT = -1e29
LOG2E = 1.4426950408889634


def _attn_bias_tables(rel_bias, tq):
    qi = jnp.arange(tq, dtype=jnp.int32)[:, None]
    ki = jnp.arange(tq, dtype=jnp.int32)[None, :]
    d_diag = qi - ki
    rb = rel_bias.astype(F32)
    rb = (rb - rb[REL_BUCKETS - 1][None, :]) * LOG2E

    def lookup(dist):
        onehot = jax.nn.one_hot(_t5_bucket(dist), REL_BUCKETS, dtype=F32)
        return jnp.einsum('qkb,bh->hqk', onehot, rb, precision=lax.Precision.HIGHEST)

    b_diag = jnp.where((d_diag >= 0)[None], lookup(jnp.maximum(d_diag, 0)), NEG_BIG)
    b_prev = lookup(d_diag + tq)
    masked = jnp.full_like(b_diag, NEG_BIG)
    return jnp.stack([jnp.concatenate([b_prev, b_diag], axis=-1),
                      jnp.concatenate([b_diag, masked], axis=-1)], axis=1)


def _attn_pair_tables(nq):
    idle = (0, nq, 0, 0)
    wide = [(i, i, 2 * j, 0) for i in range(nq) for j in range((i - 1) // 2)]
    odd = [(i, i, i - 2, 0) for i in range(2, nq) if (i - 1) % 2]
    near = [(i, i, max(i - 1, 0), 0 if i else 1) for i in range(nq)]
    rows = [idle, idle] + wide + odd + near + [idle, idle]
    kinds = 'WW' + 'W' * len(wide) + 'O' * len(odd) + 'N' * len(near) + 'NN'
    cols = list(zip(*rows))
    return [jnp.asarray(c, jnp.int32) for c in cols], kinds


def _attn_body(qrow_ref, slot_ref, kt_ref, bidx_ref, q_ref, k_ref, v_ref, bias_ref, lq_ref, sg_ref,
               o_ref, m_ref, acc_ref, vaug_ref, s0_ref, s1_ref, p0_ref, p1_ref, a0_ref, a1_ref,
               sw0_ref, sw1_ref, pw0_ref, pw1_ref, *, tq, nq, kinds, lam_init):
    m_ref[...] = jnp.full(m_ref.shape, M_INIT, F32)
    acc_ref[...] = jnp.zeros(acc_ref.shape, F32)
    vaug_ref[:, 0:LANES] = v_ref[...]
    vaug_ref[:, LANES:2 * LANES] = jnp.ones(v_ref.shape, BF16)
    lane = lax.broadcasted_iota(jnp.int32, (tq, LANES), 1)

    def scores(e, s_ref, tk):
        q = q_ref[pl.ds(pl.multiple_of(qrow_ref[e] * tq, tq), tq), :]
        zero = jnp.zeros_like(q)
        qq = jnp.concatenate([jnp.where(lane < HEAD_DIM, q, zero),
                              jnp.where(lane < HEAD_DIM, zero, q)], axis=0)
        k = k_ref[pl.ds(pl.multiple_of(kt_ref[e] * tq, tq), tk), :]
        s_ref[...] = lax.dot_general(qq, k, (((1,), (1,)), ((), ())), preferred_element_type=F32)

    def softmax(e, s_ref, a_ref, p_ref, with_bias, rc):
        m_view = m_ref.at[slot_ref[e]]
        b_view = bias_ref.at[bidx_ref[e]]
        for c in range(tq // rc):
            b = b_view[c * rc:(c + 1) * rc, :] if with_bias else None
            for half in range(2):
                rows = slice(half * tq + c * rc, half * tq + (c + 1) * rc)
                s = s_ref[rows, :]
                if with_bias:
                    s = s + b
                m_prev = m_view[rows, :]
                m_next = jnp.maximum(m_prev, jnp.max(s, axis=1, keepdims=True))
                a_ref[rows, :] = jnp.exp2(m_prev - m_next)
                p_ref[rows, :] = jnp.exp2(s - m_next[:, 0:1]).astype(BF16)
                m_view[rows, :] = m_next

    def values(e, a_ref, p_ref, tk):
        acc = acc_ref.at[slot_ref[e]]
        v = vaug_ref[pl.ds(pl.multiple_of(kt_ref[e] * tq, tq), tk), :]
        pv = jnp.dot(p_ref[...], v, preferred_element_type=F32)
        a = a_ref[...]
        acc[:, 0:LANES] = acc[:, 0:LANES] * a + pv[:, 0:LANES]
        acc[:, LANES:2 * LANES] = acc[:, LANES:2 * LANES] * a + pv[:, LANES:2 * LANES]

    kind_cfg = {'W': (2 * tq, False, 32, (sw0_ref, sw1_ref), (pw0_ref, pw1_ref)),
                'O': (tq, False, 64, (s0_ref, s1_ref), (p0_ref, p1_ref)),
                'N': (2 * tq, True, 32, (sw0_ref, sw1_ref), (pw0_ref, pw1_ref))}
    a_refs = (a0_ref, a1_ref)

    def iteration(it, par, k_a, k_b, k_c):
        tk_a, _, _, s_a, _ = kind_cfg[k_a]
        _, bias_b, rc_b, s_b, p_b = kind_cfg[k_b]
        tk_c, _, _, _, p_c = kind_cfg[k_c]
        scores(it + 2, s_a[par], tk_a)
        softmax(it + 1, s_b[1 - par], a_refs[1 - par], p_b[1 - par], bias_b, rc_b)
        values(it, a_refs[par], p_c[par], tk_c)

    pw0_ref[...] = jnp.zeros(pw0_ref.shape, BF16)
    a0_ref[...] = jnp.ones(a0_ref.shape, F32)
    scores(1, sw1_ref, 2 * tq)

    n_it = len(kinds) - 2
    it = 0
    while it < n_it:
        trio = (kinds[it + 2], kinds[it + 1], kinds[it])
        run = 1
        while it + run < n_it and (kinds[it + run + 2], kinds[it + run + 1], kinds[it + run]) == trio:
            run += 1
        start, stop = it, it + run
        if start % 2 and start < stop:
            iteration(start, 1, *trio)
            start += 1
        n_pairs = (stop - start) // 2
        if n_pairs > 0:
            def two_iterations(i2, c, start=start, trio=trio):
                iteration(start + 2 * i2, 0, *trio)
                iteration(start + 2 * i2 + 1, 1, *trio)
                return c

            lax.fori_loop(0, n_pairs, two_iterations, 0)
        if (stop - start) % 2:
            iteration(stop - 1, (stop - 1) % 2, *trio)
        it = stop

    lq = lq_ref[...]
    lam = (jnp.exp(jnp.sum(lq[0:1] * lq[1:2], axis=1, keepdims=True))
           - jnp.exp(jnp.sum(lq[2:3] * lq[3:4], axis=1, keepdims=True)) + lam_init)

    def finalize(i, c):
        acc = acc_ref[i]
        o_all = acc[:, 0:LANES] / acc[:, LANES:2 * LANES]
        o = o_all[0:tq] - lam * o_all[tq:2 * tq]
        ms = jnp.mean(o * o, axis=-1, keepdims=True)
        o = (o * lax.rsqrt(ms + RMS_EPS) * sg_ref[...]) * (1.0 - lam_init)
        o_ref[pl.ds(pl.multiple_of(i * tq, tq), tq), :] = o.astype(o_ref.dtype)
        return c

    lax.fori_loop(0, nq, finalize, 0)


def _diff_attention(qn, kn, vb, bias_tabs, lambda_qk, subln_g, lam_init, batch):
    n, d = qn.shape
    l = n // batch
    tq = ATTN_TQ
    nq = l // tq
    nh = d // LANES
    tables, kinds = _attn_pair_tables(nq)
    row_blk = pl.BlockSpec((l, LANES), lambda b, h, *_: (b, h))
    grid_spec = pltpu.PrefetchScalarGridSpec(
        num_scalar_prefetch=4,
        grid=(batch, nh),
        in_specs=[row_blk, row_blk, row_blk,
                  pl.BlockSpec((None, 2, tq, 2 * tq), lambda b, h, *_: (h, 0, 0, 0)),
                  pl.BlockSpec((4, HEAD_DIM), lambda b, h, *_: (0, 0)),
                  pl.BlockSpec((1, LANES), lambda b, h, *_: (0, 0))],
        out_specs=row_blk,
        scratch_shapes=[pltpu.VMEM((nq + 1, 2 * tq, LANES), F32),
                        pltpu.VMEM((nq + 1, 2 * tq, 2 * LANES), F32),
                        pltpu.VMEM((l, 2 * LANES), BF16),
                        pltpu.VMEM((2 * tq, tq), F32), pltpu.VMEM((2 * tq, tq), F32),
                        pltpu.VMEM((2 * tq, tq), BF16), pltpu.VMEM((2 * tq, tq), BF16),
                        pltpu.VMEM((2 * tq, LANES), F32), pltpu.VMEM((2 * tq, LANES), F32),
                        pltpu.VMEM((2 * tq, 2 * tq), F32), pltpu.VMEM((2 * tq, 2 * tq), F32),
                        pltpu.VMEM((2 * tq, 2 * tq), BF16), pltpu.VMEM((2 * tq, 2 * tq), BF16)],
    )
    return pl.pallas_call(
        functools.partial(_attn_body, tq=tq, nq=nq, kinds=kinds, lam_init=lam_init),
        grid_spec=grid_spec,
        out_shape=jax.ShapeDtypeStruct((n, d), BF16),
        compiler_params=_cparams(("arbitrary", "arbitrary")),
        name="diff_attention",
    )(*tables, qn, kn, vb, bias_tabs, lambda_qk.astype(F32),
      subln_g.reshape(1, LANES).astype(F32))


ROUTER_E0 = N_EXPERT_GROUPS
META_COLS = 8


def _router_body(h_ref, g_ref, wr_ref, br_ref, xn_ref, meta_ref, cnt_ref, base_ref):
    tm = h_ref.shape[0]

    @pl.when(pl.program_id(0) == 0)
    def _():
        base_ref[...] = jnp.zeros(base_ref.shape, F32)

    x = h_ref[...]
    ms = jnp.mean(x * x, axis=-1, keepdims=True)
    xn = x * lax.rsqrt(ms + RMS_EPS) * g_ref[...]
    xn_ref[...] = xn.astype(xn_ref.dtype)
    x_hi = xn.astype(BF16)
    x_lo = (xn - x_hi.astype(F32)).astype(BF16)
    logits = (jnp.dot(x_hi, wr_ref[0], preferred_element_type=F32)
              + (jnp.dot(x_hi, wr_ref[1], preferred_element_type=F32)
                 + jnp.dot(x_lo, wr_ref[0], preferred_element_type=F32))) + br_ref[...]
    lane = lax.broadcasted_iota(jnp.int32, (tm, LANES), 1)
    neg_inf = -jnp.inf

    def first_argmax(vals):
        vmax = jnp.max(vals, axis=-1, keepdims=True)
        idx = jnp.min(jnp.where(vals == vmax, lane, LANES), axis=-1, keepdims=True)
        return vmax, idx

    is_g = lane < N_EXPERT_GROUPS
    gmax, g_idx = first_argmax(jnp.where(is_g, logits, neg_inf))
    g_w = 1.0 / jnp.sum(jnp.where(is_g, jnp.exp(logits - gmax), 0.0), axis=-1, keepdims=True)
    e_lo = ROUTER_E0 + EXPERTS_PER_GROUP * g_idx
    elog = jnp.where((lane >= e_lo) & (lane < e_lo + EXPERTS_PER_GROUP), logits, neg_inf)
    v0, i0 = first_argmax(elog)
    v1, i1 = first_argmax(jnp.where(lane == i0, neg_inf, elog))
    t = jnp.exp(v1 - v0)
    w0 = g_w / (1.0 + t)
    w1 = g_w * t / (1.0 + t)

    sel0 = lane == i0
    sel1 = lane == i1
    onehot = jnp.where(sel0 | sel1, 1.0, 0.0)
    r_i = lax.broadcasted_iota(jnp.int32, (tm, tm), 0)
    c_i = lax.broadcasted_iota(jnp.int32, (tm, tm), 1)
    tri = jnp.where(c_i < r_i, 1.0, 0.0).astype(BF16)
    before = jnp.dot(tri, onehot.astype(BF16), preferred_element_type=F32) + base_ref[...]
    rank0 = jnp.sum(jnp.where(sel0, before, 0.0), axis=-1, keepdims=True)
    rank1 = jnp.sum(jnp.where(sel1, before, 0.0), axis=-1, keepdims=True)
    base_ref[...] = base_ref[...] + jnp.sum(onehot, axis=0, keepdims=True)
    cnt_ref[...] = base_ref[...]

    eid0 = (i0 - ROUTER_E0).astype(F32)
    eid1 = (i1 - ROUTER_E0).astype(F32)
    meta = jnp.where(lane == 0, eid0, jnp.where(lane == 1, eid1, jnp.where(
        lane == 2, rank0, jnp.where(lane == 3, rank1, jnp.where(
            lane == 4, w0, jnp.where(lane == 5, w1, 0.0))))))
    meta_ref[...] = meta[:, 0:META_COLS]


def _router(h, g, wg1, bg1, wg2, bg2, tm=256):
    n, d = h.shape
    wr = jnp.concatenate([wg1.astype(F32),
                          wg2.astype(F32).transpose(1, 0, 2).reshape(d, N_EXPERTS)], axis=1)
    wr = jnp.pad(wr, ((0, 0), (0, LANES - wr.shape[1])))
    wr_hi = wr.astype(BF16)
    wr = jnp.stack([wr_hi, (wr - wr_hi.astype(F32)).astype(BF16)])
    br = jnp.pad(jnp.concatenate([bg1.astype(F32), bg2.astype(F32).reshape(-1)]),
                 (0, LANES - N_EXPERT_GROUPS - N_EXPERTS)).reshape(1, LANES)
    return pl.pallas_call(
        _router_body,
        grid=(n // tm,),
        in_specs=[pl.BlockSpec((tm, d), lambda i: (i, 0)),
                  pl.BlockSpec((1, d), lambda i: (0, 0)),
                  pl.BlockSpec((2, d, LANES), lambda i: (0, 0, 0)),
                  pl.BlockSpec((1, LANES), lambda i: (0, 0))],
        out_specs=[pl.BlockSpec((tm, d), lambda i: (i, 0)),
                   pl.BlockSpec((tm, META_COLS), lambda i: (i, 0)),
                   pl.BlockSpec((1, LANES), lambda i: (0, 0))],
        out_shape=[jax.ShapeDtypeStruct((n, d), F32),
                   jax.ShapeDtypeStruct((n, META_COLS), F32),
                   jax.ShapeDtypeStruct((1, LANES), F32)],
        scratch_shapes=[pltpu.VMEM((1, LANES), F32)],
        compiler_params=_cparams(("arbitrary",)),
        name="moe_router",
    )(h, g.reshape(1, d).astype(F32), wr, br)


def _invert_body(dest_ref, rt_ref, *, ts):
    t = pl.program_id(0)

    def body(r, c):
        tok = t * ts + r
        rt_ref[dest_ref[0, 2 * r]] = tok
        rt_ref[dest_ref[0, 2 * r + 1]] = tok
        return c

    lax.fori_loop(0, ts, body, 0, unroll=True)


def _invert(dest, ts=512):
    n = dest.shape[0]
    dest3 = dest.reshape(n // ts, 1, 2 * ts)
    return pl.pallas_call(
        functools.partial(_invert_body, ts=ts),
        grid=(n // ts,),
        in_specs=[pl.BlockSpec((None, 1, 2 * ts), lambda t: (t, 0, 0), memory_space=pltpu.SMEM)],
        out_specs=pl.BlockSpec(memory_space=pltpu.SMEM),
        out_shape=jax.ShapeDtypeStruct((2 * n,), jnp.int32),
        compiler_params=_cparams(("arbitrary",)),
        name="moe_invert",
    )(dest3)


def _combine_body(d0_ref, dn_ref, h_ref, w_ref, y_hbm, *rest, ts, nt, with_norm):
    if with_norm:
        g_ref, o_ref, xn_ref, y0_ref, y1_ref, sem = rest
    else:
        o_ref, y0_ref, y1_ref, sem = rest
    t = pl.program_id(0)
    slot = t % 2

    def gather_start(dest_ref, dst):
        def issue(r, c):
            pltpu.make_async_copy(y_hbm.at[pl.ds(dest_ref[0, 2 * r], 1), :],
                                  y0_ref.at[dst, pl.ds(r, 1), :], sem.at[dst, 0]).start()
            pltpu.make_async_copy(y_hbm.at[pl.ds(dest_ref[0, 2 * r + 1], 1), :],
                                  y1_ref.at[dst, pl.ds(r, 1), :], sem.at[dst, 1]).start(priority=1)
            return c

        lax.fori_loop(0, ts, issue, 0, unroll=True)

    @pl.when(t == 0)
    def _():
        gather_start(d0_ref, 0)

    @pl.when(t + 1 < nt)
    def _():
        gather_start(dn_ref, 1 - slot)

    pltpu.make_async_copy(y_hbm.at[pl.ds(0, ts), :], y0_ref.at[slot], sem.at[slot, 0]).wait()
    pltpu.make_async_copy(y_hbm.at[pl.ds(0, ts), :], y1_ref.at[slot], sem.at[slot, 1]).wait()
    w = w_ref[...]
    out = h_ref[...] + w[:, 4:5] * y0_ref[slot] + w[:, 5:6] * y1_ref[slot]
    o_ref[...] = out
    if with_norm:
        ms = jnp.mean(out * out, axis=-1, keepdims=True)
        xn_ref[...] = (out * lax.rsqrt(ms + RMS_EPS) * g_ref[...]).astype(xn_ref.dtype)


def _combine(h, meta, y_sorted, dest, next_norm_g=None, ts=256):
    n, d = h.shape
    nt = n // ts
    dest3 = dest.reshape(nt, 1, 2 * ts)
    with_norm = next_norm_g is not None
    row_blk = pl.BlockSpec((ts, d), lambda t: (t, 0))
    in_specs = [pl.BlockSpec((None, 1, 2 * ts), lambda t: (0, 0, 0), memory_space=pltpu.SMEM),
                pl.BlockSpec((None, 1, 2 * ts), lambda t: (jnp.minimum(t + 1, nt - 1), 0, 0),
                             memory_space=pltpu.SMEM),
                row_blk,
                pl.BlockSpec((ts, META_COLS), lambda t: (t, 0)),
                pl.BlockSpec(memory_space=pl.ANY)]
    args = [dest3, dest3, h, meta, y_sorted]
    out_specs, out_shape = row_blk, jax.ShapeDtypeStruct((n, d), F32)
    if with_norm:
        in_specs.append(pl.BlockSpec((1, d), lambda t: (0, 0)))
        args.append(next_norm_g.reshape(1, d).astype(F32))
        out_specs = [row_blk, row_blk]
        out_shape = [out_shape, jax.ShapeDtypeStruct((n, d), BF16)]
    return pl.pallas_call(
        functools.partial(_combine_body, ts=ts, nt=nt, with_norm=with_norm),
        grid=(nt,),
        in_specs=in_specs,
        out_specs=out_specs,
        out_shape=out_shape,
        scratch_shapes=[pltpu.VMEM((2, ts, d), F32), pltpu.VMEM((2, ts, d), F32),
                        pltpu.SemaphoreType.DMA((2, 2))],
        compiler_params=_cparams(("arbitrary",)),
        name="moe_combine",
    )(*args)


EXPERT_TM = 256
EXPERT_VMEM_LIMIT = 56 * 1024 * 1024


def _expert_body(ie_ref, it_ref, lo_ref, hi_ref, first_ref, rfirst_ref, nxt_ref,
                 rt0_ref, rtn_ref, xn_hbm, wg_hbm, wu_hbm, wd_hbm, o_ref,
                 xbuf, stg_g, stg_u, stg_d, wgb, wub, wdb, sem_x, sem_w, *, tm, layer, n_tiles):
    i = pl.program_id(0)
    e = ie_ref[i]
    t = it_ref[i]
    lo = lo_ref[i]
    hi = hi_ref[i]
    slot = t % 2

    def weight_copies(ex):
        return (pltpu.make_async_copy(wg_hbm.at[layer, ex], stg_g, sem_w.at[0]),
                pltpu.make_async_copy(wu_hbm.at[layer, ex], stg_u, sem_w.at[1]),
                pltpu.make_async_copy(wd_hbm.at[layer, ex], stg_d, sem_w.at[2]))

    def gather_start(rt_ref, dst_slot):
        def issue(r2, c):
            for par in range(2):
                r = 2 * r2 + par
                pltpu.make_async_copy(xn_hbm.at[pl.ds(rt_ref[0, r], 1), :],
                                      xbuf.at[dst_slot, pl.ds(r, 1), :],
                                      sem_x.at[dst_slot]).start(priority=par)
            return c

        lax.fori_loop(0, tm // 2, issue, 0, unroll=True)

    @pl.when(i == 0)
    def _():
        for cp in weight_copies(e):
            cp.start()
        gather_start(rt0_ref, 0)

    @pl.when(rfirst_ref[i] == 1)
    def _():
        for cp in weight_copies(e):
            cp.wait()
        rows = 256

        def cast_rows(r, c):
            r0 = pl.multiple_of(r * rows, rows)
            wgb[pl.ds(r0, rows), :] = stg_g[pl.ds(r0, rows), :].astype(BF16)
            wub[pl.ds(r0, rows), :] = stg_u[pl.ds(r0, rows), :].astype(BF16)
            return c

        lax.fori_loop(0, stg_g.shape[0] // rows, cast_rows, 0)

        def cast_rows_d(r, c):
            r0 = pl.multiple_of(r * rows, rows)
            wdb[pl.ds(r0, rows), :] = stg_d[pl.ds(r0, rows), :].astype(BF16)
            return c

        lax.fori_loop(0, stg_d.shape[0] // rows, cast_rows_d, 0)

        @pl.when(nxt_ref[i] >= 0)
        def _():
            for cp in weight_copies(nxt_ref[i]):
                cp.start()

    @pl.when(first_ref[i] == 1)
    def _():
        @pl.when(t + 1 < n_tiles)
        def _():
            gather_start(rtn_ref, 1 - slot)

        pltpu.make_async_copy(xn_hbm.at[pl.ds(0, tm), :], xbuf.at[slot], sem_x.at[slot]).wait()

    @pl.when(hi > lo)
    def _():
        xb = xbuf[slot].astype(BF16)
        g = jnp.dot(xb, wgb[...], preferred_element_type=F32)
        u = jnp.dot(xb, wub[...], preferred_element_type=F32)
        hdn = (g * (1.0 / (1.0 + jnp.exp(-g)))) * u
        row = t * tm + lax.broadcasted_iota(jnp.int32, (tm, 1), 0)
        hdn = jnp.where((row >= lo) & (row < hi), hdn, 0.0)
        y = jnp.dot(hdn.astype(BF16), wdb[...], preferred_element_type=F32)

        @pl.when(first_ref[i] == 1)
        def _():
            o_ref[...] = y

        @pl.when(first_ref[i] != 1)
        def _():
            o_ref[...] = o_ref[...] + y


def _experts(xn, row_token, items, w_gate, w_up, w_down, layer, tm):
    _, d = xn.shape
    dh = w_gate.shape[-1]
    r_total = row_token.shape[0]
    n_tiles = r_total // tm
    n_items = items[0].shape[0]
    rt3 = row_token.reshape(n_tiles, 1, tm)
    grid_spec = pltpu.PrefetchScalarGridSpec(
        num_scalar_prefetch=7,
        grid=(n_items,),
        in_specs=[
            pl.BlockSpec((None, 1, tm), lambda i, *_: (0, 0, 0), memory_space=pltpu.SMEM),
            pl.BlockSpec((None, 1, tm), lambda i, ie, it, *_: (jnp.minimum(it[i] + 1, n_tiles - 1), 0, 0),
                         memory_space=pltpu.SMEM),
            pl.BlockSpec(memory_space=pl.ANY), pl.BlockSpec(memory_space=pl.ANY),
            pl.BlockSpec(memory_space=pl.ANY), pl.BlockSpec(memory_space=pl.ANY),
        ],
        out_specs=pl.BlockSpec((tm, d), lambda i, ie, it, *_: (it[i], 0)),
        scratch_shapes=[pltpu.VMEM((2, tm, d), F32),
                        pltpu.VMEM((d, dh), F32), pltpu.VMEM((d, dh), F32), pltpu.VMEM((dh, d), F32),
                        pltpu.VMEM((d, dh), BF16), pltpu.VMEM((d, dh), BF16), pltpu.VMEM((dh, d), BF16),
                        pltpu.SemaphoreType.DMA((2,)), pltpu.SemaphoreType.DMA((3,))],
    )
    return pl.pallas_call(
        functools.partial(_expert_body, tm=tm, layer=layer, n_tiles=n_tiles),
        grid_spec=grid_spec,
        out_shape=jax.ShapeDtypeStruct((r_total, d), F32),
        compiler_params=pltpu.CompilerParams(dimension_semantics=("arbitrary",),
                                             vmem_limit_bytes=EXPERT_VMEM_LIMIT),
        name="moe_experts",
    )(*items, rt3, rt3, xn, w_gate, w_up, w_down)


def _moe_plan(meta, counts_row, n_rows, tm):
    counts = counts_row[0, ROUTER_E0:ROUTER_E0 + N_EXPERTS].astype(jnp.int32)
    ends = jnp.cumsum(counts)
    starts = ends - counts
    eid = meta[:, 0:2].astype(jnp.int32)
    rank = meta[:, 2:4].astype(jnp.int32)
    dest = starts[eid] + rank

    first_tile = starts // tm
    last_tile = jnp.maximum(ends - 1, 0) // tm
    ntile_e = jnp.where(counts > 0, last_tile - first_tile + 1, 0)
    item_end = jnp.cumsum(ntile_e)
    item_start = item_end - ntile_e
    total = item_end[-1]
    n_items = n_rows // tm + N_EXPERTS - 1
    ii = jnp.arange(n_items, dtype=jnp.int32)
    valid = ii < total
    ii_c = jnp.minimum(ii, total - 1)
    e_i = jnp.searchsorted(item_end, ii_c, side='right').astype(jnp.int32)
    t_i = (first_tile[e_i] + (ii_c - item_start[e_i])).astype(jnp.int32)
    lo = jnp.where(valid, jnp.maximum(starts[e_i], t_i * tm), 0).astype(jnp.int32)
    hi = jnp.where(valid, jnp.minimum(ends[e_i], (t_i + 1) * tm), 0).astype(jnp.int32)
    minus1 = jnp.full((1,), -1, jnp.int32)
    first = (valid & (t_i != jnp.concatenate([minus1, t_i[:-1]]))).astype(jnp.int32)
    run_first = (valid & (e_i != jnp.concatenate([minus1, e_i[:-1]]))).astype(jnp.int32)
    ar = jnp.arange(N_EXPERTS, dtype=jnp.int32)
    later = (ar[None, :] > ar[:, None]) & (counts > 0)[None, :]
    nxt_e = jnp.min(jnp.where(later, ar[None, :], N_EXPERTS), axis=1)
    nxt_e = jnp.where(nxt_e == N_EXPERTS, -1, nxt_e).astype(jnp.int32)
    items = (e_i, t_i, lo, hi, first, run_first, nxt_e[e_i])
    return dest.astype(jnp.int32), items


def _hier_moe(h, g, wg1, bg1, wg2, bg2, w_gate, w_up, w_down, layer, next_norm_g=None,
              tm=EXPERT_TM):
    n, _ = h.shape
    xn, meta, counts_row = _router(h, g, wg1, bg1, wg2, bg2)
    dest, items = _moe_plan(meta, counts_row, 2 * n, tm)
    y_sorted = _experts(xn, _invert(dest), items, w_gate, w_up, w_down, layer, tm)
    return _combine(h, meta, y_sorted, dest, next_norm_g)


def _time_permute(h, batch, inverse=False):
    n, d = h.shape
    l = n // batch
    shape = (batch, l // S5_SEGMENTS, S5_SEGMENTS, d) if inverse else (batch, S5_SEGMENTS, l // S5_SEGMENTS, d)
    return h.reshape(shape).transpose(0, 2, 1, 3).reshape(n, d)


def kernel(x, norm_mix_g, norm_ffn_g, ssm_lam_re, ssm_lam_im, ssm_b_re, ssm_b_im, ssm_c_re, ssm_c_im, ssm_d, ssm_log_step, ssm_w_glu1, ssm_w_glu2, kv_norm_g, w_kv, k_norm_g, w_q, q_norm_g, lambda_qk, subln_g, w_o, rel_bias, router_group_w, router_group_b, router_expert_w, router_expert_b, w_gate, w_up, w_down):
    batch, l, d = x.shape
    n = batch * l
    h = _time_permute(x.astype(F32).reshape(n, d), batch)
    kn = vb = bias_tabs = None
    xn = _rmsnorm(h, norm_mix_g[0], BF16)
    for layer in range(DEPTH):
        if layer < N_A_LAYERS:
            b_blk, c_blk, lam_v = _s5_params(ssm_lam_re[layer], ssm_lam_im[layer], ssm_b_re[layer],
                                             ssm_b_im[layer], ssm_c_re[layer], ssm_c_im[layer],
                                             ssm_log_step[layer])
            z = _s5_scan(xn, b_blk, c_blk, lam_v, ssm_d[layer], batch)
            h = _matmul(z, [ssm_w_glu1, ssm_w_glu2], layer, F32, res=h)
        else:
            j = layer - N_A_LAYERS
            lam_init = 0.8 - 0.6 * math.exp(-0.3 * layer)
            qn = _matmul(xn, [w_q], j, BF16, head_norm=(q_norm_g[j], HEAD_DIM ** -0.5 * LOG2E),
                         tn=PROJ_TN)
            o = _diff_attention(qn, kn, vb, bias_tabs, lambda_qk[j], subln_g[j], lam_init, batch)
            h = _matmul(o, [w_o], j, F32, res=h, tn=PROJ_TN)
        fuse_next = layer + 1 < DEPTH and layer != N_A_LAYERS - 1
        moe_out = _hier_moe(h, norm_ffn_g[layer], router_group_w[layer], router_group_b[layer],
                            router_expert_w[layer], router_expert_b[layer], w_gate, w_up, w_down,
                            layer, norm_mix_g[layer + 1] if fuse_next else None)
        h, xn = moe_out if fuse_next else (moe_out, None)
        if layer == N_A_LAYERS - 1:
            h = _time_permute(h, batch, inverse=True)
            xkv = _rmsnorm(h, kv_norm_g, BF16)
            w_kv3 = w_kv.reshape(1, d, 2 * d)
            kn = _matmul(xkv, [w_kv3], 0, BF16, n_out=d, head_norm=(k_norm_g, 1.0), tn=PROJ_TN)
            vb = _matmul(xkv, [w_kv3], 0, BF16, col_off=d // PROJ_TN, n_out=d, tn=PROJ_TN)
            bias_tabs = _attn_bias_tables(rel_bias, ATTN_TQ)
            xn = _rmsnorm(h, norm_mix_g[layer + 1], BF16)
    return h.reshape(batch, l, d)
```

```python
import functools
import math

import jax
import jax.numpy as jnp
from jax import lax
from jax.experimental import pallas as pl
from jax.experimental.pallas import tpu as pltpu

F32 = jnp.float32
BF16 = jnp.bfloat16

DEPTH = 4
N_A_LAYERS = DEPTH // 2
SSM_GROUP = 16
SSM_STATE = 64
HEAD_DIM = 64
REL_BUCKETS = 32
REL_MAX_EXACT = REL_BUCKETS // 2
REL_MAX_DISTANCE = 128
N_EXPERT_GROUPS = 4
EXPERTS_PER_GROUP = 4
N_EXPERTS = N_EXPERT_GROUPS * EXPERTS_PER_GROUP
RMS_EPS = 1e-6

LANES = 128
SUBLANES = 8
VMEM_LIMIT = 48 * 1024 * 1024
NEG_BIG = -1e30

S5_SEGMENTS = SUBLANES
S5_GROUPS_PER_BLOCK = 16
S5_TJ = 32
PROJ_TN = 1024


def _cparams(sem):
    return pltpu.CompilerParams(dimension_semantics=sem, vmem_limit_bytes=VMEM_LIMIT)


def _rmsnorm_body(h_ref, g_ref, o_ref):
    x = h_ref[...]
    ms = jnp.mean(x * x, axis=-1, keepdims=True)
    o_ref[...] = (x * lax.rsqrt(ms + RMS_EPS) * g_ref[...]).astype(o_ref.dtype)


def _rmsnorm(h, g, out_dtype, tm=512):
    n, d = h.shape
    return pl.pallas_call(
        _rmsnorm_body,
        grid=(n // tm,),
        in_specs=[pl.BlockSpec((tm, d), lambda i: (i, 0)),
                  pl.BlockSpec((1, d), lambda i: (0, 0))],
        out_specs=pl.BlockSpec((tm, d), lambda i: (i, 0)),
        out_shape=jax.ShapeDtypeStruct((n, d), out_dtype),
        compiler_params=_cparams(("arbitrary",)),
        name="rmsnorm",
    )(h, g.reshape(1, d))


def _mm_body(*refs, n_w, has_res, hn_scale):
    a_ref = refs[0]
    w_refs = refs[1:1 + n_w]
    pos = 1 + n_w
    res_ref = refs[pos] if has_res else None
    pos += int(has_res)
    g_ref = refs[pos] if hn_scale is not None else None
    pos += int(hn_scale is not None)
    o_ref = refs[pos]
    wb_refs = refs[pos + 1:]

    @pl.when(pl.program_id(1) == 0)
    def _():
        for w_ref, wb_ref in zip(w_refs, wb_refs):
            wb_ref[...] = w_ref[...].astype(BF16)

    a = a_ref[...]
    y = jnp.dot(a, wb_refs[0][...], preferred_element_type=F32)
    if n_w == 2:
        y2 = jnp.dot(a, wb_refs[1][...], preferred_element_type=F32)
        y = y * (1.0 / (1.0 + jnp.exp(-y2)))
    if has_res:
        y = y + res_ref[...]
    if hn_scale is None:
        o_ref[...] = y.astype(o_ref.dtype)
    else:
        tm, tn = y.shape
        lane = lax.broadcasted_iota(jnp.int32, (tm, LANES), 1)
        lo = lane < HEAD_DIM
        g = g_ref[...]
        for hb in range(tn // LANES):
            x = y[:, hb * LANES:(hb + 1) * LANES]
            x2 = x * x
            s_lo = jnp.sum(jnp.where(lo, x2, 0.0), axis=-1, keepdims=True)
            s_hi = jnp.sum(jnp.where(lo, 0.0, x2), axis=-1, keepdims=True)
            r = jnp.where(lo, lax.rsqrt(s_lo * (1.0 / HEAD_DIM) + RMS_EPS),
                          lax.rsqrt(s_hi * (1.0 / HEAD_DIM) + RMS_EPS))
            o_ref[:, hb * LANES:(hb + 1) * LANES] = ((x * r * g) * hn_scale).astype(o_ref.dtype)


def _matmul(a, ws, layer, out_dtype, res=None, col_off=0, n_out=None, head_norm=None,
            tm=512, tn=512):
    m, k = a.shape
    n_out = ws[0].shape[-1] if n_out is None else n_out
    n_w = len(ws)
    in_specs = [pl.BlockSpec((tm, k), lambda j, i: (i, 0))]
    for _ in ws:
        in_specs.append(pl.BlockSpec((None, k, tn), lambda j, i: (layer, 0, j + col_off)))
    args = [a, *ws]
    if res is not None:
        in_specs.append(pl.BlockSpec((tm, tn), lambda j, i: (i, j)))
        args.append(res)
    hn_scale = None
    if head_norm is not None:
        gain, hn_scale = head_norm
        in_specs.append(pl.BlockSpec((1, LANES), lambda j, i: (0, 0)))
        args.append(jnp.concatenate([gain, gain]).reshape(1, LANES).astype(F32))
    return pl.pallas_call(
        functools.partial(_mm_body, n_w=n_w, has_res=res is not None, hn_scale=hn_scale),
        grid=(n_out // tn, m // tm),
        in_specs=in_specs,
        out_specs=pl.BlockSpec((tm, tn), lambda j, i: (i, j)),
        out_shape=jax.ShapeDtypeStruct((m, n_out), out_dtype),
        scratch_shapes=[pltpu.VMEM((k, tn), BF16) for _ in ws],
        compiler_params=_cparams(("arbitrary", "arbitrary")),
        name="dense_matmul",
    )(*args)


def _gelu_tanh(y):
    c = math.sqrt(2.0 / math.pi)
    return y * (0.5 * (1.0 + jnp.tanh(c * (y + 0.044715 * (y * y * y)))))


def _s5_body(x_ref, b_ref, c_ref, lam_ref, d_ref, z_ref, bu0_ref, bu1_ref, st0_ref, st1_ref,
             e_ref, init_ref, *, seg_len):
    ch = lam_ref.shape[-1]
    rows = S5_TJ * S5_SEGMENTS
    n_tiles = seg_len // S5_TJ
    lr = jnp.broadcast_to(lam_ref[0:1, :], (S5_SEGMENTS, ch))
    li = jnp.broadcast_to(lam_ref[1:2, :], (S5_SEGMENTS, ch))

    def row0(t):
        return pl.multiple_of(jnp.clip(t, 0, n_tiles - 1) * rows, rows)

    def bu_tile(t, bu_ref):
        bu_ref[...] = jnp.dot(x_ref[pl.ds(row0(t), rows), :], b_ref[...],
                              preferred_element_type=F32)

    def scan_tile(bu_ref, s_re, s_im, st_ref):
        for j in range(S5_TJ):
            sl = slice(S5_SEGMENTS * j, S5_SEGMENTS * (j + 1))
            n_re = lr * s_re - li * s_im + bu_ref[sl, 0:ch]
            n_im = lr * s_im + li * s_re + bu_ref[sl, ch:2 * ch]
            s_re, s_im = n_re, n_im
            if st_ref is not None:
                st_ref[sl, 0:ch] = s_re
                st_ref[sl, ch:2 * ch] = s_im
        return s_re, s_im

    def project(t, st_ref):
        r0 = row0(t)
        y = jnp.dot(st_ref[...].astype(BF16), c_ref[...], preferred_element_type=F32)
        y = y + d_ref[...] * x_ref[pl.ds(r0, rows), :].astype(F32)
        z_ref[pl.ds(r0, rows), :] = _gelu_tanh(y).astype(z_ref.dtype)

    def pass1(i2, c):
        t = 2 * i2
        bu_tile(t + 1, bu1_ref)
        c = scan_tile(bu0_ref, c[0], c[1], None)
        bu_tile(t + 2, bu0_ref)
        return scan_tile(bu1_ref, c[0], c[1], None)

    zeros = jnp.zeros((S5_SEGMENTS, ch), F32)
    bu_tile(0, bu0_ref)
    e_re, e_im = lax.fori_loop(0, n_tiles // 2, pass1, (zeros, zeros))
    e_ref[:, 0:ch] = e_re
    e_ref[:, ch:2 * ch] = e_im

    pr, pi = lam_ref[0:1, :], lam_ref[1:2, :]
    for _ in range(seg_len.bit_length() - 1):
        pr, pi = pr * pr - pi * pi, 2.0 * pr * pi
    cr = jnp.zeros((1, ch), F32)
    ci = jnp.zeros((1, ch), F32)
    init_ref[0:1, :] = jnp.zeros((1, 2 * ch), F32)
    for k in range(S5_SEGMENTS - 1):
        er = e_ref[k:k + 1, 0:ch]
        ei = e_ref[k:k + 1, ch:2 * ch]
        cr, ci = pr * cr - pi * ci + er, pr * ci + pi * cr + ei
        init_ref[k + 1:k + 2, 0:ch] = cr
        init_ref[k + 1:k + 2, ch:2 * ch] = ci

    def pass2(i2, c):
        t = 2 * i2
        bu_tile(t + 1, bu1_ref)
        c = scan_tile(bu0_ref, c[0], c[1], st0_ref)
        project(t - 1, st1_ref)
        bu_tile(t + 2, bu0_ref)
        c = scan_tile(bu1_ref, c[0], c[1], st1_ref)
        project(t, st0_ref)
        return c

    st1_ref[...] = jnp.zeros(st1_ref.shape, F32)
    bu_tile(0, bu0_ref)
    lax.fori_loop(0, n_tiles // 2, pass2, (init_ref[:, 0:ch], init_ref[:, ch:2 * ch]))
    project(n_tiles - 1, st1_ref)


def _s5_params(lam_re, lam_im, b_re, b_im, c_re, c_im, log_step):
    g, p = lam_re.shape
    gb = S5_GROUPS_PER_BLOCK
    nb = g // gb
    lam = lax.complex(lam_re.astype(F32), lam_im.astype(F32))
    delta = jnp.exp(log_step.astype(F32))[:, None]
    lam_bar = jnp.exp(lam * delta)
    b_bar = ((lam_bar - 1.0) / lam)[..., None] * lax.complex(b_re.astype(F32), b_im.astype(F32))
    eye = jnp.eye(gb, dtype=F32)

    def blk_b(part):
        part = part.reshape(nb, gb, p, SSM_GROUP)
        return jnp.einsum('cgph,gk->cghkp', part, eye).reshape(nb, gb * SSM_GROUP, gb * p)

    def blk_c(part):
        part = part.reshape(nb, gb, SSM_GROUP, p)
        return jnp.einsum('cghp,gk->cgpkh', part, eye).reshape(nb, gb * p, gb * SSM_GROUP)

    b_blk = jnp.concatenate([blk_b(jnp.real(b_bar)), blk_b(jnp.imag(b_bar))], axis=-1)
    c_blk = jnp.concatenate([blk_c(c_re.astype(F32)), blk_c(-c_im.astype(F32))], axis=1)
    lam_v = jnp.stack([jnp.real(lam_bar).reshape(nb, gb * p),
                       jnp.imag(lam_bar).reshape(nb, gb * p)], axis=1)
    return b_blk.astype(BF16), c_blk.astype(BF16), lam_v


def _s5_scan(xn, b_blk, c_blk, lam_v, d_skip, batch):
    n, d = xn.shape
    l = n // batch
    seg_len = l // S5_SEGMENTS
    nb, fb, ch2 = b_blk.shape
    rows = S5_TJ * S5_SEGMENTS
    return pl.pallas_call(
        functools.partial(_s5_body, seg_len=seg_len),
        grid=(batch, nb),
        in_specs=[pl.BlockSpec((l, fb), lambda b, c: (b, c)),
                  pl.BlockSpec((None, fb, ch2), lambda b, c: (c, 0, 0)),
                  pl.BlockSpec((None, ch2, fb), lambda b, c: (c, 0, 0)),
                  pl.BlockSpec((None, 2, ch2 // 2), lambda b, c: (c, 0, 0)),
                  pl.BlockSpec((1, fb), lambda b, c: (0, c))],
        out_specs=pl.BlockSpec((l, fb), lambda b, c: (b, c)),
        out_shape=jax.ShapeDtypeStruct((n, d), BF16),
        scratch_shapes=[pltpu.VMEM((rows, ch2), F32), pltpu.VMEM((rows, ch2), F32),
                        pltpu.VMEM((rows, ch2), F32), pltpu.VMEM((rows, ch2), F32),
                        pltpu.VMEM((S5_SEGMENTS, ch2), F32), pltpu.VMEM((S5_SEGMENTS, ch2), F32)],
        compiler_params=_cparams(("arbitrary", "arbitrary")),
        name="s5_scan",
    )(xn, b_blk, c_blk, lam_v, d_skip.reshape(1, d).astype(F32))


def _t5_bucket(n):
    n_safe = jnp.maximum(n, 1).astype(F32)
    large = REL_MAX_EXACT + (jnp.log(n_safe / REL_MAX_EXACT)
                             / math.log(REL_MAX_DISTANCE / REL_MAX_EXACT)
                             * (REL_BUCKETS - REL_MAX_EXACT)).astype(jnp.int32)
    large = jnp.minimum(large, REL_BUCKETS - 1)
    return jnp.where(n < REL_MAX_EXACT, n, large)


ATTN_TQ = 256
M_INIT = -1e29
LOG2E = 1.4426950408889634


def _attn_bias_tables(rel_bias, tq):
    qi = jnp.arange(tq, dtype=jnp.int32)[:, None]
    ki = jnp.arange(tq, dtype=jnp.int32)[None, :]
    d_diag = qi - ki
    rb = rel_bias.astype(F32)
    rb = (rb - rb[REL_BUCKETS - 1][None, :]) * LOG2E

    def lookup(dist):
        onehot = jax.nn.one_hot(_t5_bucket(dist), REL_BUCKETS, dtype=F32)
        return jnp.einsum('qkb,bh->hqk', onehot, rb, precision=lax.Precision.HIGHEST)

    b_diag = jnp.where((d_diag >= 0)[None], lookup(jnp.maximum(d_diag, 0)), NEG_BIG)
    b_prev = lookup(d_diag + tq)
    masked = jnp.full_like(b_diag, NEG_BIG)
    return jnp.stack([jnp.concatenate([b_prev, b_diag], axis=-1),
                      jnp.concatenate([b_diag, masked], axis=-1)], axis=1)


def _attn_pair_tables(nq):
    idle = (0, nq, 0, 0)
    wide = [(i, i, 2 * j, 0) for i in range(nq) for j in range((i - 1) // 2)]
    odd = [(i, i, i - 2, 0) for i in range(2, nq) if (i - 1) % 2]
    near = [(i, i, max(i - 1, 0), 0 if i else 1) for i in range(nq)]
    rows = [idle, idle] + wide + odd + near + [idle, idle]
    kinds = 'WW' + 'W' * len(wide) + 'O' * len(odd) + 'N' * len(near) + 'NN'
    cols = list(zip(*rows))
    return [jnp.asarray(c, jnp.int32) for c in cols], kinds


def _attn_body(qrow_ref, slot_ref, kt_ref, bidx_ref, q_ref, k_ref, v_ref, bias_ref, lq_ref, sg_ref,
               o_ref, m_ref, acc_ref, vaug_ref, s0_ref, s1_ref, p0_ref, p1_ref, a0_ref, a1_ref,
               sw0_ref, sw1_ref, pw0_ref, pw1_ref, *, tq, nq, kinds, lam_init):
    m_ref[...] = jnp.full(m_ref.shape, M_INIT, F32)
    acc_ref[...] = jnp.zeros(acc_ref.shape, F32)
    vaug_ref[:, 0:LANES] = v_ref[...]
    vaug_ref[:, LANES:2 * LANES] = jnp.ones(v_ref.shape, BF16)
    lane = lax.broadcasted_iota(jnp.int32, (tq, LANES), 1)

    def scores(e, s_ref, tk):
        q = q_ref[pl.ds(pl.multiple_of(qrow_ref[e] * tq, tq), tq), :]
        zero = jnp.zeros_like(q)
        qq = jnp.concatenate([jnp.where(lane < HEAD_DIM, q, zero),
                              jnp.where(lane < HEAD_DIM, zero, q)], axis=0)
        k = k_ref[pl.ds(pl.multiple_of(kt_ref[e] * tq, tq), tk), :]
        s_ref[...] = lax.dot_general(qq, k, (((1,), (1,)), ((), ())), preferred_element_type=F32)

    def softmax(e, s_ref, a_ref, p_ref, with_bias, rc):
        m_view = m_ref.at[slot_ref[e]]
        b_view = bias_ref.at[bidx_ref[e]]
        for c in range(tq // rc):
            b = b_view[c * rc:(c + 1) * rc, :] if with_bias else None
            for half in range(2):
                rows = slice(half * tq + c * rc, half * tq + (c + 1) * rc)
                s = s_ref[rows, :]
                if with_bias:
                    s = s + b
                m_prev = m_view[rows, :]
                m_next = jnp.maximum(m_prev, jnp.max(s, axis=1, keepdims=True))
                a_ref[rows, :] = jnp.exp2(m_prev - m_next)
                p_ref[rows, :] = jnp.exp2(s - m_next[:, 0:1]).astype(BF16)
                m_view[rows, :] = m_next

    def values(e, a_ref, p_ref, tk):
        acc = acc_ref.at[slot_ref[e]]
        v = vaug_ref[pl.ds(pl.multiple_of(kt_ref[e] * tq, tq), tk), :]
        pv = jnp.dot(p_ref[...], v, preferred_element_type=F32)
        a = a_ref[...]
        acc[:, 0:LANES] = acc[:, 0:LANES] * a + pv[:, 0:LANES]
        acc[:, LANES:2 * LANES] = acc[:, LANES:2 * LANES] * a + pv[:, LANES:2 * LANES]

    kind_cfg = {'W': (2 * tq, False, 32, (sw0_ref, sw1_ref), (pw0_ref, pw1_ref)),
                'O': (tq, False, 64, (s0_ref, s1_ref), (p0_ref, p1_ref)),
                'N': (2 * tq, True, 32, (sw0_ref, sw1_ref), (pw0_ref, pw1_ref))}
    a_refs = (a0_ref, a1_ref)

    def iteration(it, par, k_a, k_b, k_c):
        tk_a, _, _, s_a, _ = kind_cfg[k_a]
        _, bias_b, rc_b, s_b, p_b = kind_cfg[k_b]
        tk_c, _, _, _, p_c = kind_cfg[k_c]
        scores(it + 2, s_a[par], tk_a)
        softmax(it + 1, s_b[1 - par], a_refs[1 - par], p_b[1 - par], bias_b, rc_b)
        values(it, a_refs[par], p_c[par], tk_c)

    pw0_ref[...] = jnp.zeros(pw0_ref.shape, BF16)
    a0_ref[...] = jnp.ones(a0_ref.shape, F32)
    scores(1, sw1_ref, 2 * tq)

    n_it = len(kinds) - 2
    it = 0
    while it < n_it:
        trio = (kinds[it + 2], kinds[it + 1], kinds[it])
        run = 1
        while it + run < n_it and (kinds[it + run + 2], kinds[it + run + 1], kinds[it + run]) == trio:
            run += 1
        start, stop = it, it + run
        if start % 2 and start < stop:
            iteration(start, 1, *trio)
            start += 1
        n_pairs = (stop - start) // 2
        if n_pairs > 0:
            def two_iterations(i2, c, start=start, trio=trio):
                iteration(start + 2 * i2, 0, *trio)
                iteration(start + 2 * i2 + 1, 1, *trio)
                return c

            lax.fori_loop(0, n_pairs, two_iterations, 0)
        if (stop - start) % 2:
            iteration(stop - 1, (stop - 1) % 2, *trio)
        it = stop

    lq = lq_ref[...]
    lam = (jnp.exp(jnp.sum(lq[0:1] * lq[1:2], axis=1, keepdims=True))
           - jnp.exp(jnp.sum(lq[2:3] * lq[3:4], axis=1, keepdims=True)) + lam_init)

    def finalize(i, c):
        acc = acc_ref[i]
        o_all = acc[:, 0:LANES] / acc[:, LANES:2 * LANES]
        o = o_all[0:tq] - lam * o_all[tq:2 * tq]
        ms = jnp.mean(o * o, axis=-1, keepdims=True)
        o = (o * lax.rsqrt(ms + RMS_EPS) * sg_ref[...]) * (1.0 - lam_init)
        o_ref[pl.ds(pl.multiple_of(i * tq, tq), tq), :] = o.astype(o_ref.dtype)
        return c

    lax.fori_loop(0, nq, finalize, 0, unroll=4)


def _diff_attention(qn, kn, vb, bias_tabs, lambda_qk, subln_g, lam_init, batch):
    n, d = qn.shape
    l = n // batch
    tq = ATTN_TQ
    nq = l // tq
    nh = d // LANES
    tables, kinds = _attn_pair_tables(nq)
    row_blk = pl.BlockSpec((l, LANES), lambda b, h, *_: (b, h))
    grid_spec = pltpu.PrefetchScalarGridSpec(
        num_scalar_prefetch=4,
        grid=(batch, nh),
        in_specs=[row_blk, row_blk, row_blk,
                  pl.BlockSpec((None, 2, tq, 2 * tq), lambda b, h, *_: (h, 0, 0, 0)),
                  pl.BlockSpec((4, HEAD_DIM), lambda b, h, *_: (0, 0)),
                  pl.BlockSpec((1, LANES), lambda b, h, *_: (0, 0))],
        out_specs=row_blk,
        scratch_shapes=[pltpu.VMEM((nq + 1, 2 * tq, LANES), F32),
                        pltpu.VMEM((nq + 1, 2 * tq, 2 * LANES), F32),
                        pltpu.VMEM((l, 2 * LANES), BF16),
                        pltpu.VMEM((2 * tq, tq), F32), pltpu.VMEM((2 * tq, tq), F32),
                        pltpu.VMEM((2 * tq, tq), BF16), pltpu.VMEM((2 * tq, tq), BF16),
                        pltpu.VMEM((2 * tq, LANES), F32), pltpu.VMEM((2 * tq, LANES), F32),
                        pltpu.VMEM((2 * tq, 2 * tq), F32), pltpu.VMEM((2 * tq, 2 * tq), F32),
                        pltpu.VMEM((2 * tq, 2 * tq), BF16), pltpu.VMEM((2 * tq, 2 * tq), BF16)],
    )
    return pl.pallas_call(
        functools.partial(_attn_body, tq=tq, nq=nq, kinds=kinds, lam_init=lam_init),
        grid_spec=grid_spec,
        out_shape=jax.ShapeDtypeStruct((n, d), BF16),
        compiler_params=_cparams(("arbitrary", "arbitrary")),
        name="diff_attention",
    )(*tables, qn, kn, vb, bias_tabs, lambda_qk.astype(F32),
      subln_g.reshape(1, LANES).astype(F32))


ROUTER_E0 = N_EXPERT_GROUPS
META_COLS = 8


def _router_body(h_ref, g_ref, wr_ref, br_ref, xn_ref, meta_ref, cnt_ref, base_ref):
    tm = h_ref.shape[0]

    @pl.when(pl.program_id(0) == 0)
    def _():
        base_ref[...] = jnp.zeros(base_ref.shape, F32)

    x = h_ref[...]
    ms = jnp.mean(x * x, axis=-1, keepdims=True)
    xn = x * lax.rsqrt(ms + RMS_EPS) * g_ref[...]
    xn_ref[...] = xn.astype(xn_ref.dtype)
    x_hi = xn.astype(BF16)
    x_lo = (xn - x_hi.astype(F32)).astype(BF16)
    logits = (jnp.dot(x_hi, wr_ref[0], preferred_element_type=F32)
              + (jnp.dot(x_hi, wr_ref[1], preferred_element_type=F32)
                 + jnp.dot(x_lo, wr_ref[0], preferred_element_type=F32))) + br_ref[...]
    lane = lax.broadcasted_iota(jnp.int32, (tm, LANES), 1)
    neg_inf = -jnp.inf

    def first_argmax(vals):
        vmax = jnp.max(vals, axis=-1, keepdims=True)
        idx = jnp.min(jnp.where(vals == vmax, lane, LANES), axis=-1, keepdims=True)
        return vmax, idx

    is_g = lane < N_EXPERT_GROUPS
    gmax, g_idx = first_argmax(jnp.where(is_g, logits, neg_inf))
    g_w = 1.0 / jnp.sum(jnp.where(is_g, jnp.exp(logits - gmax), 0.0), axis=-1, keepdims=True)
    e_lo = ROUTER_E0 + EXPERTS_PER_GROUP * g_idx
    elog = jnp.where((lane >= e_lo) & (lane < e_lo + EXPERTS_PER_GROUP), logits, neg_inf)
    v0, i0 = first_argmax(elog)
    v1, i1 = first_argmax(jnp.where(lane == i0, neg_inf, elog))
    t = jnp.exp(v1 - v0)
    w0 = g_w / (1.0 + t)
    w1 = g_w * t / (1.0 + t)

    sel0 = lane == i0
    sel1 = lane == i1
    onehot = jnp.where(sel0 | sel1, 1.0, 0.0)
    r_i = lax.broadcasted_iota(jnp.int32, (tm, tm), 0)
    c_i = lax.broadcasted_iota(jnp.int32, (tm, tm), 1)
    tri = jnp.where(c_i < r_i, 1.0, 0.0).astype(BF16)
    before = jnp.dot(tri, onehot.astype(BF16), preferred_element_type=F32) + base_ref[...]
    rank0 = jnp.sum(jnp.where(sel0, before, 0.0), axis=-1, keepdims=True)
    rank1 = jnp.sum(jnp.where(sel1, before, 0.0), axis=-1, keepdims=True)
    base_ref[...] = base_ref[...] + jnp.sum(onehot, axis=0, keepdims=True)
    cnt_ref[...] = base_ref[...]

    eid0 = (i0 - ROUTER_E0).astype(F32)
    eid1 = (i1 - ROUTER_E0).astype(F32)
    meta = jnp.where(lane == 0, eid0, jnp.where(lane == 1, eid1, jnp.where(
        lane == 2, rank0, jnp.where(lane == 3, rank1, jnp.where(
            lane == 4, w0, jnp.where(lane == 5, w1, 0.0))))))
    meta_ref[...] = meta[:, 0:META_COLS]


def _router(h, g, wg1, bg1, wg2, bg2, tm=256):
    n, d = h.shape
    wr = jnp.concatenate([wg1.astype(F32),
                          wg2.astype(F32).transpose(1, 0, 2).reshape(d, N_EXPERTS)], axis=1)
    wr = jnp.pad(wr, ((0, 0), (0, LANES - wr.shape[1])))
    wr_hi = wr.astype(BF16)
    wr = jnp.stack([wr_hi, (wr - wr_hi.astype(F32)).astype(BF16)])
    br = jnp.pad(jnp.concatenate([bg1.astype(F32), bg2.astype(F32).reshape(-1)]),
                 (0, LANES - N_EXPERT_GROUPS - N_EXPERTS)).reshape(1, LANES)
    return pl.pallas_call(
        _router_body,
        grid=(n // tm,),
        in_specs=[pl.BlockSpec((tm, d), lambda i: (i, 0)),
                  pl.BlockSpec((1, d), lambda i: (0, 0)),
                  pl.BlockSpec((2, d, LANES), lambda i: (0, 0, 0)),
                  pl.BlockSpec((1, LANES), lambda i: (0, 0))],
        out_specs=[pl.BlockSpec((tm, d), lambda i: (i, 0)),
                   pl.BlockSpec((tm, META_COLS), lambda i: (i, 0)),
                   pl.BlockSpec((1, LANES), lambda i: (0, 0))],
        out_shape=[jax.ShapeDtypeStruct((n, d), F32),
                   jax.ShapeDtypeStruct((n, META_COLS), F32),
                   jax.ShapeDtypeStruct((1, LANES), F32)],
        scratch_shapes=[pltpu.VMEM((1, LANES), F32)],
        compiler_params=_cparams(("arbitrary",)),
        name="moe_router",
    )(h, g.reshape(1, d).astype(F32), wr, br)


def _invert_body(dest_ref, rt_ref, *, ts):
    t = pl.program_id(0)

    def body(r, c):
        tok = t * ts + r
        rt_ref[dest_ref[0, 2 * r]] = tok
        rt_ref[dest_ref[0, 2 * r + 1]] = tok
        return c

    lax.fori_loop(0, ts, body, 0, unroll=True)


def _invert(dest, ts=512):
    n = dest.shape[0]
    dest3 = dest.reshape(n // ts, 1, 2 * ts)
    return pl.pallas_call(
        functools.partial(_invert_body, ts=ts),
        grid=(n // ts,),
        in_specs=[pl.BlockSpec((None, 1, 2 * ts), lambda t: (t, 0, 0), memory_space=pltpu.SMEM)],
        out_specs=pl.BlockSpec(memory_space=pltpu.SMEM),
        out_shape=jax.ShapeDtypeStruct((2 * n,), jnp.int32),
        compiler_params=_cparams(("arbitrary",)),
        name="moe_invert",
    )(dest3)


def _combine_body(d0_ref, dn_ref, h_ref, w_ref, y_hbm, *rest, ts, nt, with_norm):
    if with_norm:
        g_ref, o_ref, xn_ref, y0_ref, y1_ref, sem = rest
    else:
        o_ref, y0_ref, y1_ref, sem = rest
    t = pl.program_id(0)
    slot = t % 2

    def gather_start(dest_ref, dst):
        def issue(r, c):
            pltpu.make_async_copy(y_hbm.at[pl.ds(dest_ref[0, 2 * r], 1), :],
                                  y0_ref.at[dst, pl.ds(r, 1), :], sem.at[dst, 0]).start()
            pltpu.make_async_copy(y_hbm.at[pl.ds(dest_ref[0, 2 * r + 1], 1), :],
                                  y1_ref.at[dst, pl.ds(r, 1), :], sem.at[dst, 1]).start(priority=1)
            return c

        lax.fori_loop(0, ts, issue, 0, unroll=True)

    @pl.when(t == 0)
    def _():
        gather_start(d0_ref, 0)

    @pl.when(t + 1 < nt)
    def _():
        gather_start(dn_ref, 1 - slot)

    pltpu.make_async_copy(y_hbm.at[pl.ds(0, ts), :], y0_ref.at[slot], sem.at[slot, 0]).wait()
    pltpu.make_async_copy(y_hbm.at[pl.ds(0, ts), :], y1_ref.at[slot], sem.at[slot, 1]).wait()
    w = w_ref[...]
    out = h_ref[...] + w[:, 4:5] * y0_ref[slot] + w[:, 5:6] * y1_ref[slot]
    o_ref[...] = out
    if with_norm:
        ms = jnp.mean(out * out, axis=-1, keepdims=True)
        xn_ref[...] = (out * lax.rsqrt(ms + RMS_EPS) * g_ref[...]).astype(xn_ref.dtype)


def _combine(h, meta, y_sorted, dest, next_norm_g=None, ts=256):
    n, d = h.shape
    nt = n // ts
    dest3 = dest.reshape(nt, 1, 2 * ts)
    with_norm = next_norm_g is not None
    row_blk = pl.BlockSpec((ts, d), lambda t: (t, 0))
    in_specs = [pl.BlockSpec((None, 1, 2 * ts), lambda t: (0, 0, 0), memory_space=pltpu.SMEM),
                pl.BlockSpec((None, 1, 2 * ts), lambda t: (jnp.minimum(t + 1, nt - 1), 0, 0),
                             memory_space=pltpu.SMEM),
                row_blk,
                pl.BlockSpec((ts, META_COLS), lambda t: (t, 0)),
                pl.BlockSpec(memory_space=pl.ANY)]
    args = [dest3, dest3, h, meta, y_sorted]
    out_specs, out_shape = row_blk, jax.ShapeDtypeStruct((n, d), F32)
    if with_norm:
        in_specs.append(pl.BlockSpec((1, d), lambda t: (0, 0)))
        args.append(next_norm_g.reshape(1, d).astype(F32))
        out_specs = [row_blk, row_blk]
        out_shape = [out_shape, jax.ShapeDtypeStruct((n, d), BF16)]
    return pl.pallas_call(
        functools.partial(_combine_body, ts=ts, nt=nt, with_norm=with_norm),
        grid=(nt,),
        in_specs=in_specs,
        out_specs=out_specs,
        out_shape=out_shape,
        scratch_shapes=[pltpu.VMEM((2, ts, d), F32), pltpu.VMEM((2, ts, d), F32),
                        pltpu.SemaphoreType.DMA((2, 2))],
        compiler_params=_cparams(("arbitrary",)),
        name="moe_combine",
    )(*args)


EXPERT_TM = 256
EXPERT_VMEM_LIMIT = 56 * 1024 * 1024


def _expert_body(ie_ref, it_ref, lo_ref, hi_ref, first_ref, rfirst_ref, nxt_ref,
                 rt0_ref, rtn_ref, xn_hbm, wg_hbm, wu_hbm, wd_hbm, o_ref,
                 xbuf, stg_g, stg_u, stg_d, wgb, wub, wdb, sem_x, sem_w, *, tm, layer, n_tiles):
    i = pl.program_id(0)
    e = ie_ref[i]
    t = it_ref[i]
    lo = lo_ref[i]
    hi = hi_ref[i]
    slot = t % 2

    def weight_copies(ex):
        return (pltpu.make_async_copy(wg_hbm.at[layer, ex], stg_g, sem_w.at[0]),
                pltpu.make_async_copy(wu_hbm.at[layer, ex], stg_u, sem_w.at[1]),
                pltpu.make_async_copy(wd_hbm.at[layer, ex], stg_d, sem_w.at[2]))

    def gather_start(rt_ref, dst_slot):
        def issue(r2, c):
            for par in range(2):
                r = 2 * r2 + par
                pltpu.make_async_copy(xn_hbm.at[pl.ds(rt_ref[0, r], 1), :],
                                      xbuf.at[dst_slot, pl.ds(r, 1), :],
                                      sem_x.at[dst_slot]).start(priority=par)
            return c

        lax.fori_loop(0, tm // 2, issue, 0, unroll=True)

    @pl.when(i == 0)
    def _():
        for cp in weight_copies(e):
            cp.start()
        gather_start(rt0_ref, 0)

    @pl.when(rfirst_ref[i] == 1)
    def _():
        for cp in weight_copies(e):
            cp.wait()
        rows = 256

        def cast_rows(r, c):
            r0 = pl.multiple_of(r * rows, rows)
            wgb[pl.ds(r0, rows), :] = stg_g[pl.ds(r0, rows), :].astype(BF16)
            wub[pl.ds(r0, rows), :] = stg_u[pl.ds(r0, rows), :].astype(BF16)
            return c

        lax.fori_loop(0, stg_g.shape[0] // rows, cast_rows, 0)

        def cast_rows_d(r, c):
            r0 = pl.multiple_of(r * rows, rows)
            wdb[pl.ds(r0, rows), :] = stg_d[pl.ds(r0, rows), :].astype(BF16)
            return c

        lax.fori_loop(0, stg_d.shape[0] // rows, cast_rows_d, 0)

        @pl.when(nxt_ref[i] >= 0)
        def _():
            for cp in weight_copies(nxt_ref[i]):
                cp.start()

    @pl.when(first_ref[i] == 1)
    def _():
        @pl.when(t + 1 < n_tiles)
        def _():
            gather_start(rtn_ref, 1 - slot)

        pltpu.make_async_copy(xn_hbm.at[pl.ds(0, tm), :], xbuf.at[slot], sem_x.at[slot]).wait()

    @pl.when(hi > lo)
    def _():
        xb = xbuf[slot].astype(BF16)
        g = jnp.dot(xb, wgb[...], preferred_element_type=F32)
        u = jnp.dot(xb, wub[...], preferred_element_type=F32)
        hdn = (g * (1.0 / (1.0 + jnp.exp(-g)))) * u
        row = t * tm + lax.broadcasted_iota(jnp.int32, (tm, 1), 0)
        hdn = jnp.where((row >= lo) & (row < hi), hdn, 0.0)
        y = jnp.dot(hdn.astype(BF16), wdb[...], preferred_element_type=F32)

        @pl.when(first_ref[i] == 1)
        def _():
            o_ref[...] = y

        @pl.when(first_ref[i] != 1)
        def _():
            o_ref[...] = o_ref[...] + y


def _experts(xn, row_token, items, w_gate, w_up, w_down, layer, tm):
    _, d = xn.shape
    dh = w_gate.shape[-1]
    r_total = row_token.shape[0]
    n_tiles = r_total // tm
    n_items = items[0].shape[0]
    rt3 = row_token.reshape(n_tiles, 1, tm)
    grid_spec = pltpu.PrefetchScalarGridSpec(
        num_scalar_prefetch=7,
        grid=(n_items,),
        in_specs=[
            pl.BlockSpec((None, 1, tm), lambda i, *_: (0, 0, 0), memory_space=pltpu.SMEM),
            pl.BlockSpec((None, 1, tm), lambda i, ie, it, *_: (jnp.minimum(it[i] + 1, n_tiles - 1), 0, 0),
                         memory_space=pltpu.SMEM),
            pl.BlockSpec(memory_space=pl.ANY), pl.BlockSpec(memory_space=pl.ANY),
            pl.BlockSpec(memory_space=pl.ANY), pl.BlockSpec(memory_space=pl.ANY),
        ],
        out_specs=pl.BlockSpec((tm, d), lambda i, ie, it, *_: (it[i], 0)),
        scratch_shapes=[pltpu.VMEM((2, tm, d), F32),
                        pltpu.VMEM((d, dh), F32), pltpu.VMEM((d, dh), F32), pltpu.VMEM((dh, d), F32),
                        pltpu.VMEM((d, dh), BF16), pltpu.VMEM((d, dh), BF16), pltpu.VMEM((dh, d), BF16),
                        pltpu.SemaphoreType.DMA((2,)), pltpu.SemaphoreType.DMA((3,))],
    )
    return pl.pallas_call(
        functools.partial(_expert_body, tm=tm, layer=layer, n_tiles=n_tiles),
        grid_spec=grid_spec,
        out_shape=jax.ShapeDtypeStruct((r_total, d), F32),
        compiler_params=pltpu.CompilerParams(dimension_semantics=("arbitrary",),
                                             vmem_limit_bytes=EXPERT_VMEM_LIMIT),
        name="moe_experts",
    )(*items, rt3, rt3, xn, w_gate, w_up, w_down)


def _moe_plan(meta, counts_row, n_rows, tm):
    counts = counts_row[0, ROUTER_E0:ROUTER_E0 + N_EXPERTS].astype(jnp.int32)
    ends = jnp.cumsum(counts)
    starts = ends - counts
    eid = meta[:, 0:2].astype(jnp.int32)
    rank = meta[:, 2:4].astype(jnp.int32)
    dest = starts[eid] + rank

    first_tile = starts // tm
    last_tile = jnp.maximum(ends - 1, 0) // tm
    ntile_e = jnp.where(counts > 0, last_tile - first_tile + 1, 0)
    item_end = jnp.cumsum(ntile_e)
    item_start = item_end - ntile_e
    total = item_end[-1]
    n_items = n_rows // tm + N_EXPERTS - 1
    ii = jnp.arange(n_items, dtype=jnp.int32)
    valid = ii < total
    ii_c = jnp.minimum(ii, total - 1)
    e_i = jnp.searchsorted(item_end, ii_c, side='right').astype(jnp.int32)
    t_i = (first_tile[e_i] + (ii_c - item_start[e_i])).astype(jnp.int32)
    lo = jnp.where(valid, jnp.maximum(starts[e_i], t_i * tm), 0).astype(jnp.int32)
    hi = jnp.where(valid, jnp.minimum(ends[e_i], (t_i + 1) * tm), 0).astype(jnp.int32)
    minus1 = jnp.full((1,), -1, jnp.int32)
    first = (valid & (t_i != jnp.concatenate([minus1, t_i[:-1]]))).astype(jnp.int32)
    run_first = (valid & (e_i != jnp.concatenate([minus1, e_i[:-1]]))).astype(jnp.int32)
    ar = jnp.arange(N_EXPERTS, dtype=jnp.int32)
    later = (ar[None, :] > ar[:, None]) & (counts > 0)[None, :]
    nxt_e = jnp.min(jnp.where(later, ar[None, :], N_EXPERTS), axis=1)
    nxt_e = jnp.where(nxt_e == N_EXPERTS, -1, nxt_e).astype(jnp.int32)
    items = (e_i, t_i, lo, hi, first, run_first, nxt_e[e_i])
    return dest.astype(jnp.int32), items


def _hier_moe(h, g, wg1, bg1, wg2, bg2, w_gate, w_up, w_down, layer, next_norm_g=None,
              tm=EXPERT_TM):
    n, _ = h.shape
    xn, meta, counts_row = _router(h, g, wg1, bg1, wg2, bg2)
    dest, items = _moe_plan(meta, counts_row, 2 * n, tm)
    y_sorted = _experts(xn, _invert(dest), items, w_gate, w_up, w_down, layer, tm)
    return _combine(h, meta, y_sorted, dest, next_norm_g)


def _time_permute(h, batch, inverse=False):
    n, d = h.shape
    l = n // batch
    shape = (batch, l // S5_SEGMENTS, S5_SEGMENTS, d) if inverse else (batch, S5_SEGMENTS, l // S5_SEGMENTS, d)
    return h.reshape(shape).transpose(0, 2, 1, 3).reshape(n, d)


def kernel(x, norm_mix_g, norm_ffn_g, ssm_lam_re, ssm_lam_im, ssm_b_re, ssm_b_im, ssm_c_re, ssm_c_im, ssm_d, ssm_log_step, ssm_w_glu1, ssm_w_glu2, kv_norm_g, w_kv, k_norm_g, w_q, q_norm_g, lambda_qk, subln_g, w_o, rel_bias, router_group_w, router_group_b, router_expert_w, router_expert_b, w_gate, w_up, w_down):
    batch, l, d = x.shape
    n = batch * l
    h = _time_permute(x.astype(F32).reshape(n, d), batch)
    kn = vb = bias_tabs = None
    xn = _rmsnorm(h, norm_mix_g[0], BF16)
    for layer in range(DEPTH):
        if layer < N_A_LAYERS:
            b_blk, c_blk, lam_v = _s5_params(ssm_lam_re[layer], ssm_lam_im[layer], ssm_b_re[layer],
                                             ssm_b_im[layer], ssm_c_re[layer], ssm_c_im[layer],
                                             ssm_log_step[layer])
            z = _s5_scan(xn, b_blk, c_blk, lam_v, ssm_d[layer], batch)
            h = _matmul(z, [ssm_w_glu1, ssm_w_glu2], layer, F32, res=h)
        else:
            j = layer - N_A_LAYERS
            lam_init = 0.8 - 0.6 * math.exp(-0.3 * layer)
            qn = _matmul(xn, [w_q], j, BF16, head_norm=(q_norm_g[j], HEAD_DIM ** -0.5 * LOG2E),
                         tn=PROJ_TN)
            o = _diff_attention(qn, kn, vb, bias_tabs, lambda_qk[j], subln_g[j], lam_init, batch)
            h = _matmul(o, [w_o], j, F32, res=h, tn=PROJ_TN)
        fuse_next = layer + 1 < DEPTH and layer != N_A_LAYERS - 1
        moe_out = _hier_moe(h, norm_ffn_g[layer], router_group_w[layer], router_group_b[layer],
                            router_expert_w[layer], router_expert_b[layer], w_gate, w_up, w_down,
                            layer, norm_mix_g[layer + 1] if fuse_next else None)
        h, xn = moe_out if fuse_next else (moe_out, None)
        if layer == N_A_LAYERS - 1:
            h = _time_permute(h, batch, inverse=True)
            xkv = _rmsnorm(h, kv_norm_g, BF16)
            w_kv3 = w_kv.reshape(1, d, 2 * d)
            kn = _matmul(xkv, [w_kv3], 0, BF16, n_out=d, head_norm=(k_norm_g, 1.0), tn=PROJ_TN)
            vb = _matmul(xkv, [w_kv3], 0, BF16, col_off=d // PROJ_TN, n_out=d, tn=PROJ_TN)
            bias_tabs = _attn_bias_tables(rel_bias, ATTN_TQ)
            xn = _rmsnorm(h, norm_mix_g[layer + 1], BF16)
    return h.reshape(batch, l, d)
```

```python
import functools
import math

import jax
import jax.numpy as jnp
from jax import lax
from jax.experimental import pallas as pl
from jax.experimental.pallas import tpu as pltpu

F32 = jnp.float32
BF16 = jnp.bfloat16

DEPTH = 4
N_A_LAYERS = DEPTH // 2
SSM_GROUP = 16
SSM_STATE = 64
HEAD_DIM = 64
REL_BUCKETS = 32
REL_MAX_EXACT = REL_BUCKETS // 2
REL_MAX_DISTANCE = 128
N_EXPERT_GROUPS = 4
EXPERTS_PER_GROUP = 4
N_EXPERTS = N_EXPERT_GROUPS * EXPERTS_PER_GROUP
RMS_EPS = 1e-6

LANES = 128
SUBLANES = 8
VMEM_LIMIT = 48 * 1024 * 1024
NEG_BIG = -1e30

S5_SEGMENTS = SUBLANES
S5_GROUPS_PER_BLOCK = 16
S5_TJ = 32
PROJ_TN = 1024


def _cparams(sem):
    return pltpu.CompilerParams(dimension_semantics=sem, vmem_limit_bytes=VMEM_LIMIT)


def _rmsnorm_body(h_ref, g_ref, o_ref):
    x = h_ref[...]
    ms = jnp.mean(x * x, axis=-1, keepdims=True)
    o_ref[...] = (x * lax.rsqrt(ms + RMS_EPS) * g_ref[...]).astype(o_ref.dtype)


def _rmsnorm(h, g, out_dtype, tm=512):
    n, d = h.shape
    return pl.pallas_call(
        _rmsnorm_body,
        grid=(n // tm,),
        in_specs=[pl.BlockSpec((tm, d), lambda i: (i, 0)),
                  pl.BlockSpec((1, d), lambda i: (0, 0))],
        out_specs=pl.BlockSpec((tm, d), lambda i: (i, 0)),
        out_shape=jax.ShapeDtypeStruct((n, d), out_dtype),
        compiler_params=_cparams(("arbitrary",)),
        name="rmsnorm",
    )(h, g.reshape(1, d))


def _mm_body(*refs, n_w, has_res, hn_scale):
    a_ref = refs[0]
    w_refs = refs[1:1 + n_w]
    pos = 1 + n_w
    res_ref = refs[pos] if has_res else None
    pos += int(has_res)
    g_ref = refs[pos] if hn_scale is not None else None
    pos += int(hn_scale is not None)
    o_ref = refs[pos]
    wb_refs = refs[pos + 1:]

    @pl.when(pl.program_id(1) == 0)
    def _():
        for w_ref, wb_ref in zip(w_refs, wb_refs):
            wb_ref[...] = w_ref[...].astype(BF16)

    a = a_ref[...]
    y = jnp.dot(a, wb_refs[0][...], preferred_element_type=F32)
    if n_w == 2:
        y2 = jnp.dot(a, wb_refs[1][...], preferred_element_type=F32)
        y = y * (1.0 / (1.0 + jnp.exp(-y2)))
    if has_res:
        y = y + res_ref[...]
    if hn_scale is None:
        o_ref[...] = y.astype(o_ref.dtype)
    else:
        tm, tn = y.shape
        lane = lax.broadcasted_iota(jnp.int32, (tm, LANES), 1)
        lo = lane < HEAD_DIM
        g = g_ref[...]
        for hb in range(tn // LANES):
            x = y[:, hb * LANES:(hb + 1) * LANES]
            x2 = x * x
            s_lo = jnp.sum(jnp.where(lo, x2, 0.0), axis=-1, keepdims=True)
            s_hi = jnp.sum(jnp.where(lo, 0.0, x2), axis=-1, keepdims=True)
            r = jnp.where(lo, lax.rsqrt(s_lo * (1.0 / HEAD_DIM) + RMS_EPS),
                          lax.rsqrt(s_hi * (1.0 / HEAD_DIM) + RMS_EPS))
            o_ref[:, hb * LANES:(hb + 1) * LANES] = ((x * r * g) * hn_scale).astype(o_ref.dtype)


def _matmul(a, ws, layer, out_dtype, res=None, col_off=0, n_out=None, head_norm=None,
            tm=512, tn=512):
    m, k = a.shape
    n_out = ws[0].shape[-1] if n_out is None else n_out
    n_w = len(ws)
    in_specs = [pl.BlockSpec((tm, k), lambda j, i: (i, 0))]
    for _ in ws:
        in_specs.append(pl.BlockSpec((None, k, tn), lambda j, i: (layer, 0, j + col_off)))
    args = [a, *ws]
    if res is not None:
        in_specs.append(pl.BlockSpec((tm, tn), lambda j, i: (i, j)))
        args.append(res)
    hn_scale = None
    if head_norm is not None:
        gain, hn_scale = head_norm
        in_specs.append(pl.BlockSpec((1, LANES), lambda j, i: (0, 0)))
        args.append(jnp.concatenate([gain, gain]).reshape(1, LANES).astype(F32))
    return pl.pallas_call(
        functools.partial(_mm_body, n_w=n_w, has_res=res is not None, hn_scale=hn_scale),
        grid=(n_out // tn, m // tm),
        in_specs=in_specs,
        out_specs=pl.BlockSpec((tm, tn), lambda j, i: (i, j)),
        out_shape=jax.ShapeDtypeStruct((m, n_out), out_dtype),
        scratch_shapes=[pltpu.VMEM((k, tn), BF16) for _ in ws],
        compiler_params=_cparams(("arbitrary", "arbitrary")),
        name="dense_matmul",
    )(*args)


def _gelu_tanh(y):
    c = math.sqrt(2.0 / math.pi)
    return y * (0.5 * (1.0 + jnp.tanh(c * (y + 0.044715 * (y * y * y)))))


def _s5_body(x_ref, b_ref, c_ref, lam_ref, d_ref, z_ref, bu0_ref, bu1_ref, st0_ref, st1_ref,
             e_ref, init_ref, *, seg_len):
    ch = lam_ref.shape[-1]
    rows = S5_TJ * S5_SEGMENTS
    n_tiles = seg_len // S5_TJ
    lr = jnp.broadcast_to(lam_ref[0:1, :], (S5_SEGMENTS, ch))
    li = jnp.broadcast_to(lam_ref[1:2, :], (S5_SEGMENTS, ch))

    def row0(t):
        return pl.multiple_of(jnp.clip(t, 0, n_tiles - 1) * rows, rows)

    def bu_tile(t, bu_ref):
        bu_ref[...] = jnp.dot(x_ref[pl.ds(row0(t), rows), :], b_ref[...],
                              preferred_element_type=F32)

    def scan_tile(bu_ref, s_re, s_im, st_ref):
        for j in range(S5_TJ):
            sl = slice(S5_SEGMENTS * j, S5_SEGMENTS * (j + 1))
            n_re = lr * s_re - li * s_im + bu_ref[sl, 0:ch]
            n_im = lr * s_im + li * s_re + bu_ref[sl, ch:2 * ch]
            s_re, s_im = n_re, n_im
            if st_ref is not None:
                st_ref[sl, 0:ch] = s_re
                st_ref[sl, ch:2 * ch] = s_im
        return s_re, s_im

    def project(t, st_ref):
        r0 = row0(t)
        y = jnp.dot(st_ref[...].astype(BF16), c_ref[...], preferred_element_type=F32)
        y = y + d_ref[...] * x_ref[pl.ds(r0, rows), :].astype(F32)
        z_ref[pl.ds(r0, rows), :] = _gelu_tanh(y).astype(z_ref.dtype)

    def pass1(i2, c):
        t = 2 * i2
        bu_tile(t + 1, bu1_ref)
        c = scan_tile(bu0_ref, c[0], c[1], None)
        bu_tile(t + 2, bu0_ref)
        return scan_tile(bu1_ref, c[0], c[1], None)

    zeros = jnp.zeros((S5_SEGMENTS, ch), F32)
    bu_tile(0, bu0_ref)
    e_re, e_im = lax.fori_loop(0, n_tiles // 2, pass1, (zeros, zeros))
    e_ref[:, 0:ch] = e_re
    e_ref[:, ch:2 * ch] = e_im

    pr, pi = lam_ref[0:1, :], lam_ref[1:2, :]
    for _ in range(seg_len.bit_length() - 1):
        pr, pi = pr * pr - pi * pi, 2.0 * pr * pi
    cr = jnp.zeros((1, ch), F32)
    ci = jnp.zeros((1, ch), F32)
    init_ref[0:1, :] = jnp.zeros((1, 2 * ch), F32)
    for k in range(S5_SEGMENTS - 1):
        er = e_ref[k:k + 1, 0:ch]
        ei = e_ref[k:k + 1, ch:2 * ch]
        cr, ci = pr * cr - pi * ci + er, pr * ci + pi * cr + ei
        init_ref[k + 1:k + 2, 0:ch] = cr
        init_ref[k + 1:k + 2, ch:2 * ch] = ci

    def pass2(i2, c):
        t = 2 * i2
        bu_tile(t + 1, bu1_ref)
        c = scan_tile(bu0_ref, c[0], c[1], st0_ref)
        project(t - 1, st1_ref)
        bu_tile(t + 2, bu0_ref)
        c = scan_tile(bu1_ref, c[0], c[1], st1_ref)
        project(t, st0_ref)
        return c

    st1_ref[...] = jnp.zeros(st1_ref.shape, F32)
    bu_tile(0, bu0_ref)
    lax.fori_loop(0, n_tiles // 2, pass2, (init_ref[:, 0:ch], init_ref[:, ch:2 * ch]))
    project(n_tiles - 1, st1_ref)


def _s5_params(lam_re, lam_im, b_re, b_im, c_re, c_im, log_step):
    g, p = lam_re.shape
    gb = S5_GROUPS_PER_BLOCK
    nb = g // gb
    lam = lax.complex(lam_re.astype(F32), lam_im.astype(F32))
    delta = jnp.exp(log_step.astype(F32))[:, None]
    lam_bar = jnp.exp(lam * delta)
    b_bar = ((lam_bar - 1.0) / lam)[..., None] * lax.complex(b_re.astype(F32), b_im.astype(F32))
    eye = jnp.eye(gb, dtype=F32)

    def blk_b(part):
        part = part.reshape(nb, gb, p, SSM_GROUP)
        return jnp.einsum('cgph,gk->cghkp', part, eye).reshape(nb, gb * SSM_GROUP, gb * p)

    def blk_c(part):
        part = part.reshape(nb, gb, SSM_GROUP, p)
        return jnp.einsum('cghp,gk->cgpkh', part, eye).reshape(nb, gb * p, gb * SSM_GROUP)

    b_blk = jnp.concatenate([blk_b(jnp.real(b_bar)), blk_b(jnp.imag(b_bar))], axis=-1)
    c_blk = jnp.concatenate([blk_c(c_re.astype(F32)), blk_c(-c_im.astype(F32))], axis=1)
    lam_v = jnp.stack([jnp.real(lam_bar).reshape(nb, gb * p),
                       jnp.imag(lam_bar).reshape(nb, gb * p)], axis=1)
    return b_blk.astype(BF16), c_blk.astype(BF16), lam_v


def _s5_scan(xn, b_blk, c_blk, lam_v, d_skip, batch):
    n, d = xn.shape
    l = n // batch
    seg_len = l // S5_SEGMENTS
    nb, fb, ch2 = b_blk.shape
    rows = S5_TJ * S5_SEGMENTS
    return pl.pallas_call(
        functools.partial(_s5_body, seg_len=seg_len),
        grid=(batch, nb),
        in_specs=[pl.BlockSpec((l, fb), lambda b, c: (b, c)),
                  pl.BlockSpec((None, fb, ch2), lambda b, c: (c, 0, 0)),
                  pl.BlockSpec((None, ch2, fb), lambda b, c: (c, 0, 0)),
                  pl.BlockSpec((None, 2, ch2 // 2), lambda b, c: (c, 0, 0)),
                  pl.BlockSpec((1, fb), lambda b, c: (0, c))],
        out_specs=pl.BlockSpec((l, fb), lambda b, c: (b, c)),
        out_shape=jax.ShapeDtypeStruct((n, d), BF16),
        scratch_shapes=[pltpu.VMEM((rows, ch2), F32), pltpu.VMEM((rows, ch2), F32),
                        pltpu.VMEM((rows, ch2), F32), pltpu.VMEM((rows, ch2), F32),
                        pltpu.VMEM((S5_SEGMENTS, ch2), F32), pltpu.VMEM((S5_SEGMENTS, ch2), F32)],
        compiler_params=_cparams(("arbitrary", "arbitrary")),
        name="s5_scan",
    )(xn, b_blk, c_blk, lam_v, d_skip.reshape(1, d).astype(F32))


def _t5_bucket(n):
    n_safe = jnp.maximum(n, 1).astype(F32)
    large = REL_MAX_EXACT + (jnp.log(n_safe / REL_MAX_EXACT)
                             / math.log(REL_MAX_DISTANCE / REL_MAX_EXACT)
                             * (REL_BUCKETS - REL_MAX_EXACT)).astype(jnp.int32)
    large = jnp.minimum(large, REL_BUCKETS - 1)
    return jnp.where(n < REL_MAX_EXACT, n, large)


ATTN_TQ = 256
M_INIT = -1e29
LOG2E = 1.4426950408889634


def _attn_bias_tables(rel_bias, tq):
    qi = jnp.arange(tq, dtype=jnp.int32)[:, None]
    ki = jnp.arange(tq, dtype=jnp.int32)[None, :]
    d_diag = qi - ki
    rb = rel_bias.astype(F32)
    rb = (rb - rb[REL_BUCKETS - 1][None, :]) * LOG2E

    def lookup(dist):
        onehot = jax.nn.one_hot(_t5_bucket(dist), REL_BUCKETS, dtype=F32)
        return jnp.einsum('qkb,bh->hqk', onehot, rb, precision=lax.Precision.HIGHEST)

    b_diag = jnp.where((d_diag >= 0)[None], lookup(jnp.maximum(d_diag, 0)), NEG_BIG)
    b_prev = lookup(d_diag + tq)
    masked = jnp.full_like(b_diag, NEG_BIG)
    return jnp.stack([jnp.concatenate([b_prev, b_diag], axis=-1),
                      jnp.concatenate([b_diag, masked], axis=-1)], axis=1)


def _attn_pair_tables(nq):
    idle = (0, nq, 0, 0)
    wide = [(i, i, 2 * j, 0) for i in range(nq) for j in range((i - 1) // 2)]
    odd = [(i, i, i - 2, 0) for i in range(2, nq) if (i - 1) % 2]
    near = [(i, i, max(i - 1, 0), 0 if i else 1) for i in range(nq)]
    rows = [idle, idle] + wide + odd + near + [idle, idle]
    kinds = 'WW' + 'W' * len(wide) + 'O' * len(odd) + 'N' * len(near) + 'NN'
    cols = list(zip(*rows))
    return [jnp.asarray(c, jnp.int32) for c in cols], kinds


def _attn_body(qrow_ref, slot_ref, kt_ref, bidx_ref, q_ref, k_ref, v_ref, bias_ref, lq_ref, sg_ref,
               o_ref, m_ref, acc_ref, vaug_ref, s0_ref, s1_ref, p0_ref, p1_ref, a0_ref, a1_ref,
               sw0_ref, sw1_ref, pw0_ref, pw1_ref, *, tq, nq, kinds, lam_init):
    m_ref[...] = jnp.full(m_ref.shape, M_INIT, F32)
    acc_ref[...] = jnp.zeros(acc_ref.shape, F32)
    vaug_ref[:, 0:LANES] = v_ref[...]
    vaug_ref[:, LANES:2 * LANES] = jnp.ones(v_ref.shape, BF16)
    lane = lax.broadcasted_iota(jnp.int32, (tq, LANES), 1)

    def scores(e, s_ref, tk):
        q = q_ref[pl.ds(pl.multiple_of(qrow_ref[e] * tq, tq), tq), :]
        zero = jnp.zeros_like(q)
        qq = jnp.concatenate([jnp.where(lane < HEAD_DIM, q, zero),
                              jnp.where(lane < HEAD_DIM, zero, q)], axis=0)
        k = k_ref[pl.ds(pl.multiple_of(kt_ref[e] * tq, tq), tk), :]
        s_ref[...] = lax.dot_general(qq, k, (((1,), (1,)), ((), ())), preferred_element_type=F32)

    def softmax(e, s_ref, a_ref, p_ref, with_bias, rc):
        m_view = m_ref.at[slot_ref[e]]
        b_view = bias_ref.at[bidx_ref[e]]
        for c in range(tq // rc):
            b = b_view[c * rc:(c + 1) * rc, :] if with_bias else None
            for half in range(2):
                rows = slice(half * tq + c * rc, half * tq + (c + 1) * rc)
                s = s_ref[rows, :]
                if with_bias:
                    s = s + b
                m_prev = m_view[rows, :]
                m_next = jnp.maximum(m_prev, jnp.max(s, axis=1, keepdims=True))
                a_ref[rows, :] = jnp.exp2(m_prev - m_next)
                p_ref[rows, :] = jnp.exp2(s - m_next[:, 0:1]).astype(BF16)
                m_view[rows, :] = m_next

    def values(e, a_ref, p_ref, tk):
        acc = acc_ref.at[slot_ref[e]]
        v = vaug_ref[pl.ds(pl.multiple_of(kt_ref[e] * tq, tq), tk), :]
        pv = jnp.dot(p_ref[...], v, preferred_element_type=F32)
        a = a_ref[...]
        acc[:, 0:LANES] = acc[:, 0:LANES] * a + pv[:, 0:LANES]
        acc[:, LANES:2 * LANES] = acc[:, LANES:2 * LANES] * a + pv[:, LANES:2 * LANES]

    kind_cfg = {'W': (2 * tq, False, 32, (sw0_ref, sw1_ref), (pw0_ref, pw1_ref)),
                'O': (tq, False, 64, (s0_ref, s1_ref), (p0_ref, p1_ref)),
                'N': (2 * tq, True, 32, (sw0_ref, sw1_ref), (pw0_ref, pw1_ref))}
    a_refs = (a0_ref, a1_ref)

    def iteration(it, par, k_a, k_b, k_c):
        tk_a, _, _, s_a, _ = kind_cfg[k_a]
        _, bias_b, rc_b, s_b, p_b = kind_cfg[k_b]
        tk_c, _, _, _, p_c = kind_cfg[k_c]
        scores(it + 2, s_a[par], tk_a)
        softmax(it + 1, s_b[1 - par], a_refs[1 - par], p_b[1 - par], bias_b, rc_b)
        values(it, a_refs[par], p_c[par], tk_c)

    pw0_ref[...] = jnp.zeros(pw0_ref.shape, BF16)
    a0_ref[...] = jnp.ones(a0_ref.shape, F32)
    scores(1, sw1_ref, 2 * tq)

    n_it = len(kinds) - 2
    it = 0
    while it < n_it:
        trio = (kinds[it + 2], kinds[it + 1], kinds[it])
        run = 1
        while it + run < n_it and (kinds[it + run + 2], kinds[it + run + 1], kinds[it + run]) == trio:
            run += 1
        start, stop = it, it + run
        if start % 2 and start < stop:
            iteration(start, 1, *trio)
            start += 1
        n_pairs = (stop - start) // 2
        if n_pairs > 0:
            def two_iterations(i2, c, start=start, trio=trio):
                iteration(start + 2 * i2, 0, *trio)
                iteration(start + 2 * i2 + 1, 1, *trio)
                return c

            lax.fori_loop(0, n_pairs, two_iterations, 0)
        if (stop - start) % 2:
            iteration(stop - 1, (stop - 1) % 2, *trio)
        it = stop

    lq = lq_ref[...]
    lam = (jnp.exp(jnp.sum(lq[0:1] * lq[1:2], axis=1, keepdims=True))
           - jnp.exp(jnp.sum(lq[2:3] * lq[3:4], axis=1, keepdims=True)) + lam_init)

    def finalize(i, c):
        acc = acc_ref[i]
        o_all = acc[:, 0:LANES] / acc[:, LANES:2 * LANES]
        o = o_all[0:tq] - lam * o_all[tq:2 * tq]
        ms = jnp.mean(o * o, axis=-1, keepdims=True)
        o = (o * lax.rsqrt(ms + RMS_EPS) * sg_ref[...]) * (1.0 - lam_init)
        o_ref[pl.ds(pl.multiple_of(i * tq, tq), tq), :] = o.astype(o_ref.dtype)
        return c

    lax.fori_loop(0, nq, finalize, 0, unroll=4)


def _diff_attention(qn, kn, vb, bias_tabs, lambda_qk, subln_g, lam_init, batch):
    n, d = qn.shape
    l = n // batch
    tq = ATTN_TQ
    nq = l // tq
    nh = d // LANES
    tables, kinds = _attn_pair_tables(nq)
    row_blk = pl.BlockSpec((l, LANES), lambda b, h, *_: (b, h))
    grid_spec = pltpu.PrefetchScalarGridSpec(
        num_scalar_prefetch=4,
        grid=(batch, nh),
        in_specs=[row_blk, row_blk, row_blk,
                  pl.BlockSpec((None, 2, tq, 2 * tq), lambda b, h, *_: (h, 0, 0, 0)),
                  pl.BlockSpec((4, HEAD_DIM), lambda b, h, *_: (0, 0)),
                  pl.BlockSpec((1, LANES), lambda b, h, *_: (0, 0))],
        out_specs=row_blk,
        scratch_shapes=[pltpu.VMEM((nq + 1, 2 * tq, LANES), F32),
                        pltpu.VMEM((nq + 1, 2 * tq, 2 * LANES), F32),
                        pltpu.VMEM((l, 2 * LANES), BF16),
                        pltpu.VMEM((2 * tq, tq), F32), pltpu.VMEM((2 * tq, tq), F32),
                        pltpu.VMEM((2 * tq, tq), BF16), pltpu.VMEM((2 * tq, tq), BF16),
                        pltpu.VMEM((2 * tq, LANES), F32), pltpu.VMEM((2 * tq, LANES), F32),
                        pltpu.VMEM((2 * tq, 2 * tq), F32), pltpu.VMEM((2 * tq, 2 * tq), F32),
                        pltpu.VMEM((2 * tq, 2 * tq), BF16), pltpu.VMEM((2 * tq, 2 * tq), BF16)],
    )
    return pl.pallas_call(
        functools.partial(_attn_body, tq=tq, nq=nq, kinds=kinds, lam_init=lam_init),
        grid_spec=grid_spec,
        out_shape=jax.ShapeDtypeStruct((n, d), BF16),
        compiler_params=_cparams(("arbitrary", "arbitrary")),
        name="diff_attention",
    )(*tables, qn, kn, vb, bias_tabs, lambda_qk.astype(F32),
      subln_g.reshape(1, LANES).astype(F32))


ROUTER_E0 = N_EXPERT_GROUPS
META_COLS = 8


def _router_body(h_ref, g_ref, wr_ref, br_ref, xn_ref, meta_ref, cnt_ref, base_ref):
    tm = h_ref.shape[0]

    @pl.when(pl.program_id(0) == 0)
    def _():
        base_ref[...] = jnp.zeros(base_ref.shape, F32)

    x = h_ref[...]
    ms = jnp.mean(x * x, axis=-1, keepdims=True)
    xn = x * lax.rsqrt(ms + RMS_EPS) * g_ref[...]
    xn_ref[...] = xn.astype(xn_ref.dtype)
    x_hi = xn.astype(BF16)
    x_lo = (xn - x_hi.astype(F32)).astype(BF16)
    logits = (jnp.dot(x_hi, wr_ref[0], preferred_element_type=F32)
              + (jnp.dot(x_hi, wr_ref[1], preferred_element_type=F32)
                 + jnp.dot(x_lo, wr_ref[0], preferred_element_type=F32))) + br_ref[...]
    lane = lax.broadcasted_iota(jnp.int32, (tm, LANES), 1)
    neg_inf = -jnp.inf

    def first_argmax(vals):
        vmax = jnp.max(vals, axis=-1, keepdims=True)
        idx = jnp.min(jnp.where(vals == vmax, lane, LANES), axis=-1, keepdims=True)
        return vmax, idx

    is_g = lane < N_EXPERT_GROUPS
    gmax, g_idx = first_argmax(jnp.where(is_g, logits, neg_inf))
    g_w = 1.0 / jnp.sum(jnp.where(is_g, jnp.exp(logits - gmax), 0.0), axis=-1, keepdims=True)
    e_lo = ROUTER_E0 + EXPERTS_PER_GROUP * g_idx
    elog = jnp.where((lane >= e_lo) & (lane < e_lo + EXPERTS_PER_GROUP), logits, neg_inf)
    v0, i0 = first_argmax(elog)
    v1, i1 = first_argmax(jnp.where(lane == i0, neg_inf, elog))
    t = jnp.exp(v1 - v0)
    w0 = g_w / (1.0 + t)
    w1 = g_w * t / (1.0 + t)

    sel0 = lane == i0
    sel1 = lane == i1
    onehot = jnp.where(sel0 | sel1, 1.0, 0.0)
    r_i = lax.broadcasted_iota(jnp.int32, (tm, tm), 0)
    c_i = lax.broadcasted_iota(jnp.int32, (tm, tm), 1)
    tri = jnp.where(c_i < r_i, 1.0, 0.0).astype(BF16)
    before = jnp.dot(tri, onehot.astype(BF16), preferred_element_type=F32) + base_ref[...]
    rank0 = jnp.sum(jnp.where(sel0, before, 0.0), axis=-1, keepdims=True)
    rank1 = jnp.sum(jnp.where(sel1, before, 0.0), axis=-1, keepdims=True)
    base_ref[...] = base_ref[...] + jnp.sum(onehot, axis=0, keepdims=True)
    cnt_ref[...] = base_ref[...]

    eid0 = (i0 - ROUTER_E0).astype(F32)
    eid1 = (i1 - ROUTER_E0).astype(F32)
    meta = jnp.where(lane == 0, eid0, jnp.where(lane == 1, eid1, jnp.where(
        lane == 2, rank0, jnp.where(lane == 3, rank1, jnp.where(
            lane == 4, w0, jnp.where(lane == 5, w1, 0.0))))))
    meta_ref[...] = meta[:, 0:META_COLS]


def _router(h, g, wg1, bg1, wg2, bg2, tm=256):
    n, d = h.shape
    wr = jnp.concatenate([wg1.astype(F32),
                          wg2.astype(F32).transpose(1, 0, 2).reshape(d, N_EXPERTS)], axis=1)
    wr = jnp.pad(wr, ((0, 0), (0, LANES - wr.shape[1])))
    wr_hi = wr.astype(BF16)
    wr = jnp.stack([wr_hi, (wr - wr_hi.astype(F32)).astype(BF16)])
    br = jnp.pad(jnp.concatenate([bg1.astype(F32), bg2.astype(F32).reshape(-1)]),
                 (0, LANES - N_EXPERT_GROUPS - N_EXPERTS)).reshape(1, LANES)
    return pl.pallas_call(
        _router_body,
        grid=(n // tm,),
        in_specs=[pl.BlockSpec((tm, d), lambda i: (i, 0)),
                  pl.BlockSpec((1, d), lambda i: (0, 0)),
                  pl.BlockSpec((2, d, LANES), lambda i: (0, 0, 0)),
                  pl.BlockSpec((1, LANES), lambda i: (0, 0))],
        out_specs=[pl.BlockSpec((tm, d), lambda i: (i, 0)),
                   pl.BlockSpec((tm, META_COLS), lambda i: (i, 0)),
                   pl.BlockSpec((1, LANES), lambda i: (0, 0))],
        out_shape=[jax.ShapeDtypeStruct((n, d), F32),
                   jax.ShapeDtypeStruct((n, META_COLS), F32),
                   jax.ShapeDtypeStruct((1, LANES), F32)],
        scratch_shapes=[pltpu.VMEM((1, LANES), F32)],
        compiler_params=_cparams(("arbitrary",)),
        name="moe_router",
    )(h, g.reshape(1, d).astype(F32), wr, br)


def _invert_body(dest_ref, rt_ref, *, ts):
    t = pl.program_id(0)

    def body(r, c):
        tok = t * ts + r
        rt_ref[dest_ref[0, 2 * r]] = tok
        rt_ref[dest_ref[0, 2 * r + 1]] = tok
        return c

    lax.fori_loop(0, ts, body, 0, unroll=True)


def _invert(dest, ts=512):
    n = dest.shape[0]
    dest3 = dest.reshape(n // ts, 1, 2 * ts)
    return pl.pallas_call(
        functools.partial(_invert_body, ts=ts),
        grid=(n // ts,),
        in_specs=[pl.BlockSpec((None, 1, 2 * ts), lambda t: (t, 0, 0), memory_space=pltpu.SMEM)],
        out_specs=pl.BlockSpec(memory_space=pltpu.SMEM),
        out_shape=jax.ShapeDtypeStruct((2 * n,), jnp.int32),
        compiler_params=_cparams(("arbitrary",)),
        name="moe_invert",
    )(dest3)


def _combine_body(d0_ref, dn_ref, h_ref, w_ref, y_hbm, *rest, ts, nt, with_norm):
    if with_norm:
        g_ref, o_ref, xn_ref, y0_ref, y1_ref, sem = rest
    else:
        o_ref, y0_ref, y1_ref, sem = rest
    t = pl.program_id(0)
    slot = t % 2

    def gather_start(dest_ref, dst):
        def issue(r, c):
            pltpu.make_async_copy(y_hbm.at[pl.ds(dest_ref[0, 2 * r], 1), :],
                                  y0_ref.at[dst, pl.ds(r, 1), :], sem.at[dst, 0]).start()
            pltpu.make_async_copy(y_hbm.at[pl.ds(dest_ref[0, 2 * r + 1], 1), :],
                                  y1_ref.at[dst, pl.ds(r, 1), :], sem.at[dst, 1]).start(priority=1)
            return c

        lax.fori_loop(0, ts, issue, 0, unroll=True)

    @pl.when(t == 0)
    def _():
        gather_start(d0_ref, 0)

    @pl.when(t + 1 < nt)
    def _():
        gather_start(dn_ref, 1 - slot)

    pltpu.make_async_copy(y_hbm.at[pl.ds(0, ts), :], y0_ref.at[slot], sem.at[slot, 0]).wait()
    pltpu.make_async_copy(y_hbm.at[pl.ds(0, ts), :], y1_ref.at[slot], sem.at[slot, 1]).wait()
    w = w_ref[...]
    out = h_ref[...] + w[:, 4:5] * y0_ref[slot] + w[:, 5:6] * y1_ref[slot]
    o_ref[...] = out
    if with_norm:
        ms = jnp.mean(out * out, axis=-1, keepdims=True)
        xn_ref[...] = (out * lax.rsqrt(ms + RMS_EPS) * g_ref[...]).astype(xn_ref.dtype)


def _combine(h, meta, y_sorted, dest, next_norm_g=None, ts=256):
    n, d = h.shape
    nt = n // ts
    dest3 = dest.reshape(nt, 1, 2 * ts)
    with_norm = next_norm_g is not None
    row_blk = pl.BlockSpec((ts, d), lambda t: (t, 0))
    in_specs = [pl.BlockSpec((None, 1, 2 * ts), lambda t: (0, 0, 0), memory_space=pltpu.SMEM),
                pl.BlockSpec((None, 1, 2 * ts), lambda t: (jnp.minimum(t + 1, nt - 1), 0, 0),
                             memory_space=pltpu.SMEM),
                row_blk,
                pl.BlockSpec((ts, META_COLS), lambda t: (t, 0)),
                pl.BlockSpec(memory_space=pl.ANY)]
    args = [dest3, dest3, h, meta, y_sorted]
    out_specs, out_shape = row_blk, jax.ShapeDtypeStruct((n, d), F32)
    if with_norm:
        in_specs.append(pl.BlockSpec((1, d), lambda t: (0, 0)))
        args.append(next_norm_g.reshape(1, d).astype(F32))
        out_specs = [row_blk, row_blk]
        out_shape = [out_shape, jax.ShapeDtypeStruct((n, d), BF16)]
    return pl.pallas_call(
        functools.partial(_combine_body, ts=ts, nt=nt, with_norm=with_norm),
        grid=(nt,),
        in_specs=in_specs,
        out_specs=out_specs,
        out_shape=out_shape,
        scratch_shapes=[pltpu.VMEM((2, ts, d), F32), pltpu.VMEM((2, ts, d), F32),
                        pltpu.SemaphoreType.DMA((2, 2))],
        compiler_params=_cparams(("arbitrary",)),
        name="moe_combine",
    )(*args)


EXPERT_TM = 256
EXPERT_VMEM_LIMIT = 56 * 1024 * 1024


def _expert_body(ie_ref, it_ref, lo_ref, hi_ref, first_ref, rfirst_ref, nxt_ref,
                 rt0_ref, rtn_ref, xn_hbm, wg_hbm, wu_hbm, wd_hbm, o_ref,
                 xbuf, stg_g, stg_u, stg_d, wgb, wub, wdb, sem_x, sem_w, *, tm, layer, n_tiles):
    i = pl.program_id(0)
    e = ie_ref[i]
    t = it_ref[i]
    lo = lo_ref[i]
    hi = hi_ref[i]
    slot = t % 2

    def weight_copies(ex):
        return (pltpu.make_async_copy(wg_hbm.at[layer, ex], stg_g, sem_w.at[0]),
                pltpu.make_async_copy(wu_hbm.at[layer, ex], stg_u, sem_w.at[1]),
                pltpu.make_async_copy(wd_hbm.at[layer, ex], stg_d, sem_w.at[2]))

    def gather_start(rt_ref, dst_slot):
        def issue(r2, c):
            for par in range(2):
                r = 2 * r2 + par
                pltpu.make_async_copy(xn_hbm.at[pl.ds(rt_ref[0, r], 1), :],
                                      xbuf.at[dst_slot, pl.ds(r, 1), :],
                                      sem_x.at[dst_slot]).start(priority=par)
            return c

        lax.fori_loop(0, tm // 2, issue, 0, unroll=True)

    @pl.when(i == 0)
    def _():
        for cp in weight_copies(e):
            cp.start()
        gather_start(rt0_ref, 0)

    @pl.when(rfirst_ref[i] == 1)
    def _():
        for cp in weight_copies(e):
            cp.wait()
        rows = 256

        def cast_rows(r, c):
            r0 = pl.multiple_of(r * rows, rows)
            wgb[pl.ds(r0, rows), :] = stg_g[pl.ds(r0, rows), :].astype(BF16)
            wub[pl.ds(r0, rows), :] = stg_u[pl.ds(r0, rows), :].astype(BF16)
            return c

        lax.fori_loop(0, stg_g.shape[0] // rows, cast_rows, 0)

        def cast_rows_d(r, c):
            r0 = pl.multiple_of(r * rows, rows)
            wdb[pl.ds(r0, rows), :] = stg_d[pl.ds(r0, rows), :].astype(BF16)
            return c

        lax.fori_loop(0, stg_d.shape[0] // rows, cast_rows_d, 0)

        @pl.when(nxt_ref[i] >= 0)
        def _():
            for cp in weight_copies(nxt_ref[i]):
                cp.start()

    @pl.when(first_ref[i] == 1)
    def _():
        @pl.when(t + 1 < n_tiles)
        def _():
            gather_start(rtn_ref, 1 - slot)

        pltpu.make_async_copy(xn_hbm.at[pl.ds(0, tm), :], xbuf.at[slot], sem_x.at[slot]).wait()

    @pl.when(hi > lo)
    def _():
        xb = xbuf[slot].astype(BF16)
        g = jnp.dot(xb, wgb[...], preferred_element_type=F32)
        u = jnp.dot(xb, wub[...], preferred_element_type=F32)
        hdn = (g * (1.0 / (1.0 + jnp.exp(-g)))) * u
        row = t * tm + lax.broadcasted_iota(jnp.int32, (tm, 1), 0)
        hdn = jnp.where((row >= lo) & (row < hi), hdn, 0.0)
        y = jnp.dot(hdn.astype(BF16), wdb[...], preferred_element_type=F32)

        @pl.when(first_ref[i] == 1)
        def _():
            o_ref[...] = y

        @pl.when(first_ref[i] != 1)
        def _():
            o_ref[...] = o_ref[...] + y


def _experts(xn, row_token, items, w_gate, w_up, w_down, layer, tm):
    _, d = xn.shape
    dh = w_gate.shape[-1]
    r_total = row_token.shape[0]
    n_tiles = r_total // tm
    n_items = items[0].shape[0]
    rt3 = row_token.reshape(n_tiles, 1, tm)
    grid_spec = pltpu.PrefetchScalarGridSpec(
        num_scalar_prefetch=7,
        grid=(n_items,),
        in_specs=[
            pl.BlockSpec((None, 1, tm), lambda i, *_: (0, 0, 0), memory_space=pltpu.SMEM),
            pl.BlockSpec((None, 1, tm), lambda i, ie, it, *_: (jnp.minimum(it[i] + 1, n_tiles - 1), 0, 0),
                         memory_space=pltpu.SMEM),
            pl.BlockSpec(memory_space=pl.ANY), pl.BlockSpec(memory_space=pl.ANY),
            pl.BlockSpec(memory_space=pl.ANY), pl.BlockSpec(memory_space=pl.ANY),
        ],
        out_specs=pl.BlockSpec((tm, d), lambda i, ie, it, *_: (it[i], 0)),
        scratch_shapes=[pltpu.VMEM((2, tm, d), F32),
                        pltpu.VMEM((d, dh), F32), pltpu.VMEM((d, dh), F32), pltpu.VMEM((dh, d), F32),
                        pltpu.VMEM((d, dh), BF16), pltpu.VMEM((d, dh), BF16), pltpu.VMEM((dh, d), BF16),
                        pltpu.SemaphoreType.DMA((2,)), pltpu.SemaphoreType.DMA((3,))],
    )
    return pl.pallas_call(
        functools.partial(_expert_body, tm=tm, layer=layer, n_tiles=n_tiles),
        grid_spec=grid_spec,
        out_shape=jax.ShapeDtypeStruct((r_total, d), F32),
        compiler_params=pltpu.CompilerParams(dimension_semantics=("arbitrary",),
                                             vmem_limit_bytes=EXPERT_VMEM_LIMIT),
        name="moe_experts",
    )(*items, rt3, rt3, xn, w_gate, w_up, w_down)


def _moe_plan(meta, counts_row, n_rows, tm):
    counts = counts_row[0, ROUTER_E0:ROUTER_E0 + N_EXPERTS].astype(jnp.int32)
    ends = jnp.cumsum(counts)
    starts = ends - counts
    eid = meta[:, 0:2].astype(jnp.int32)
    rank = meta[:, 2:4].astype(jnp.int32)
    dest = starts[eid] + rank

    first_tile = starts // tm
    last_tile = jnp.maximum(ends - 1, 0) // tm
    ntile_e = jnp.where(counts > 0, last_tile - first_tile + 1, 0)
    item_end = jnp.cumsum(ntile_e)
    item_start = item_end - ntile_e
    total = item_end[-1]
    n_items = n_rows // tm + N_EXPERTS - 1
    ii = jnp.arange(n_items, dtype=jnp.int32)
    valid = ii < total
    ii_c = jnp.minimum(ii, total - 1)
    e_i = jnp.sum((ii_c[:, None] >= item_end[None, :]).astype(jnp.int32), axis=1)
    t_i = (first_tile[e_i] + (ii_c - item_start[e_i])).astype(jnp.int32)
    lo = jnp.where(valid, jnp.maximum(starts[e_i], t_i * tm), 0).astype(jnp.int32)
    hi = jnp.where(valid, jnp.minimum(ends[e_i], (t_i + 1) * tm), 0).astype(jnp.int32)
    minus1 = jnp.full((1,), -1, jnp.int32)
    first = (valid & (t_i != jnp.concatenate([minus1, t_i[:-1]]))).astype(jnp.int32)
    run_first = (valid & (e_i != jnp.concatenate([minus1, e_i[:-1]]))).astype(jnp.int32)
    ar = jnp.arange(N_EXPERTS, dtype=jnp.int32)
    later = (ar[None, :] > ar[:, None]) & (counts > 0)[None, :]
    nxt_e = jnp.min(jnp.where(later, ar[None, :], N_EXPERTS), axis=1)
    nxt_e = jnp.where(nxt_e == N_EXPERTS, -1, nxt_e).astype(jnp.int32)
    items = (e_i, t_i, lo, hi, first, run_first, nxt_e[e_i])
    return dest.astype(jnp.int32), items


def _hier_moe(h, g, wg1, bg1, wg2, bg2, w_gate, w_up, w_down, layer, next_norm_g=None,
              tm=EXPERT_TM):
    n, _ = h.shape
    xn, meta, counts_row = _router(h, g, wg1, bg1, wg2, bg2)
    dest, items = _moe_plan(meta, counts_row, 2 * n, tm)
    y_sorted = _experts(xn, _invert(dest), items, w_gate, w_up, w_down, layer, tm)
    return _combine(h, meta, y_sorted, dest, next_norm_g)


def _time_permute(h, batch, inverse=False):
    n, d = h.shape
    l = n // batch
    shape = (batch, l // S5_SEGMENTS, S5_SEGMENTS, d) if inverse else (batch, S5_SEGMENTS, l // S5_SEGMENTS, d)
    return h.reshape(shape).transpose(0, 2, 1, 3).reshape(n, d)


def kernel(x, norm_mix_g, norm_ffn_g, ssm_lam_re, ssm_lam_im, ssm_b_re, ssm_b_im, ssm_c_re, ssm_c_im, ssm_d, ssm_log_step, ssm_w_glu1, ssm_w_glu2, kv_norm_g, w_kv, k_norm_g, w_q, q_norm_g, lambda_qk, subln_g, w_o, rel_bias, router_group_w, router_group_b, router_expert_w, router_expert_b, w_gate, w_up, w_down):
    batch, l, d = x.shape
    n = batch * l
    h = _time_permute(x.astype(F32).reshape(n, d), batch)
    kn = vb = bias_tabs = None
    xn = _rmsnorm(h, norm_mix_g[0], BF16)
    for layer in range(DEPTH):
        if layer < N_A_LAYERS:
            b_blk, c_blk, lam_v = _s5_params(ssm_lam_re[layer], ssm_lam_im[layer], ssm_b_re[layer],
                                             ssm_b_im[layer], ssm_c_re[layer], ssm_c_im[layer],
                                             ssm_log_step[layer])
            z = _s5_scan(xn, b_blk, c_blk, lam_v, ssm_d[layer], batch)
            h = _matmul(z, [ssm_w_glu1, ssm_w_glu2], layer, F32, res=h)
        else:
            j = layer - N_A_LAYERS
            lam_init = 0.8 - 0.6 * math.exp(-0.3 * layer)
            qn = _matmul(xn, [w_q], j, BF16, head_norm=(q_norm_g[j], HEAD_DIM ** -0.5 * LOG2E),
                         tn=PROJ_TN)
            o = _diff_attention(qn, kn, vb, bias_tabs, lambda_qk[j], subln_g[j], lam_init, batch)
            h = _matmul(o, [w_o], j, F32, res=h, tn=PROJ_TN)
        fuse_next = layer + 1 < DEPTH and layer != N_A_LAYERS - 1
        moe_out = _hier_moe(h, norm_ffn_g[layer], router_group_w[layer], router_group_b[layer],
                            router_expert_w[layer], router_expert_b[layer], w_gate, w_up, w_down,
                            layer, norm_mix_g[layer + 1] if fuse_next else None)
        h, xn = moe_out if fuse_next else (moe_out, None)
        if layer == N_A_LAYERS - 1:
            h = _time_permute(h, batch, inverse=True)
            xkv = _rmsnorm(h, kv_norm_g, BF16)
            w_kv3 = w_kv.reshape(1, d, 2 * d)
            kn = _matmul(xkv, [w_kv3], 0, BF16, n_out=d, head_norm=(k_norm_g, 1.0), tn=PROJ_TN)
            vb = _matmul(xkv, [w_kv3], 0, BF16, col_off=d // PROJ_TN, n_out=d, tn=PROJ_TN)
            bias_tabs = _attn_bias_tables(rel_bias, ATTN_TQ)
            xn = _rmsnorm(h, norm_mix_g[layer + 1], BF16)
    return h.reshape(batch, l, d)
```

```python
import functools
import math

import jax
import jax.numpy as jnp
from jax import lax
from jax.experimental import pallas as pl
from jax.experimental.pallas import tpu as pltpu

F32 = jnp.float32
BF16 = jnp.bfloat16

DEPTH = 4
N_A_LAYERS = DEPTH // 2
SSM_GROUP = 16
SSM_STATE = 64
HEAD_DIM = 64
REL_BUCKETS = 32
REL_MAX_EXACT = REL_BUCKETS // 2
REL_MAX_DISTANCE = 128
N_EXPERT_GROUPS = 4
EXPERTS_PER_GROUP = 4
N_EXPERTS = N_EXPERT_GROUPS * EXPERTS_PER_GROUP
RMS_EPS = 1e-6

LANES = 128
SUBLANES = 8
VMEM_LIMIT = 48 * 1024 * 1024
NEG_BIG = -1e30

S5_SEGMENTS = SUBLANES
S5_GROUPS_PER_BLOCK = 16
S5_TJ = 32
PROJ_TN = 1024


def _cparams(sem):
    return pltpu.CompilerParams(dimension_semantics=sem, vmem_limit_bytes=VMEM_LIMIT)


def _rmsnorm_body(h_ref, g_ref, o_ref):
    x = h_ref[...]
    ms = jnp.mean(x * x, axis=-1, keepdims=True)
    o_ref[...] = (x * lax.rsqrt(ms + RMS_EPS) * g_ref[...]).astype(o_ref.dtype)


def _rmsnorm(h, g, out_dtype, tm=512):
    n, d = h.shape
    return pl.pallas_call(
        _rmsnorm_body,
        grid=(n // tm,),
        in_specs=[pl.BlockSpec((tm, d), lambda i: (i, 0)),
                  pl.BlockSpec((1, d), lambda i: (0, 0))],
        out_specs=pl.BlockSpec((tm, d), lambda i: (i, 0)),
        out_shape=jax.ShapeDtypeStruct((n, d), out_dtype),
        compiler_params=_cparams(("arbitrary",)),
        name="rmsnorm",
    )(h, g.reshape(1, d))


def _mm_body(*refs, n_w, has_res, hn_scale):
    a_ref = refs[0]
    w_refs = refs[1:1 + n_w]
    pos = 1 + n_w
    res_ref = refs[pos] if has_res else None
    pos += int(has_res)
    g_ref = refs[pos] if hn_scale is not None else None
    pos += int(hn_scale is not None)
    o_ref = refs[pos]
    wb_refs = refs[pos + 1:]

    @pl.when(pl.program_id(1) == 0)
    def _():
        for w_ref, wb_ref in zip(w_refs, wb_refs):
            wb_ref[...] = w_ref[...].astype(BF16)

    a = a_ref[...]
    y = jnp.dot(a, wb_refs[0][...], preferred_element_type=F32)
    if n_w == 2:
        y2 = jnp.dot(a, wb_refs[1][...], preferred_element_type=F32)
        y = y * (1.0 / (1.0 + jnp.exp(-y2)))
    if has_res:
        y = y + res_ref[...]
    if hn_scale is None:
        o_ref[...] = y.astype(o_ref.dtype)
    else:
        tm, tn = y.shape
        lane = lax.broadcasted_iota(jnp.int32, (tm, LANES), 1)
        lo = lane < HEAD_DIM
        g = g_ref[...]
        for hb in range(tn // LANES):
            x = y[:, hb * LANES:(hb + 1) * LANES]
            x2 = x * x
            s_lo = jnp.sum(jnp.where(lo, x2, 0.0), axis=-1, keepdims=True)
            s_hi = jnp.sum(jnp.where(lo, 0.0, x2), axis=-1, keepdims=True)
            r = jnp.where(lo, lax.rsqrt(s_lo * (1.0 / HEAD_DIM) + RMS_EPS),
                          lax.rsqrt(s_hi * (1.0 / HEAD_DIM) + RMS_EPS))
            o_ref[:, hb * LANES:(hb + 1) * LANES] = ((x * r * g) * hn_scale).astype(o_ref.dtype)


def _matmul(a, ws, layer, out_dtype, res=None, col_off=0, n_out=None, head_norm=None,
            tm=512, tn=512):
    m, k = a.shape
    n_out = ws[0].shape[-1] if n_out is None else n_out
    n_w = len(ws)
    in_specs = [pl.BlockSpec((tm, k), lambda j, i: (i, 0))]
    for _ in ws:
        in_specs.append(pl.BlockSpec((None, k, tn), lambda j, i: (layer, 0, j + col_off)))
    args = [a, *ws]
    if res is not None:
        in_specs.append(pl.BlockSpec((tm, tn), lambda j, i: (i, j)))
        args.append(res)
    hn_scale = None
    if head_norm is not None:
        gain, hn_scale = head_norm
        in_specs.append(pl.BlockSpec((1, LANES), lambda j, i: (0, 0)))
        args.append(jnp.concatenate([gain, gain]).reshape(1, LANES).astype(F32))
    return pl.pallas_call(
        functools.partial(_mm_body, n_w=n_w, has_res=res is not None, hn_scale=hn_scale),
        grid=(n_out // tn, m // tm),
        in_specs=in_specs,
        out_specs=pl.BlockSpec((tm, tn), lambda j, i: (i, j)),
        out_shape=jax.ShapeDtypeStruct((m, n_out), out_dtype),
        scratch_shapes=[pltpu.VMEM((k, tn), BF16) for _ in ws],
        compiler_params=_cparams(("arbitrary", "arbitrary")),
        name="dense_matmul",
    )(*args)


def _gelu_tanh(y):
    c = math.sqrt(2.0 / math.pi)
    return y * (0.5 * (1.0 + jnp.tanh(c * (y + 0.044715 * (y * y * y)))))


def _s5_body(x_ref, b_ref, c_ref, lam_ref, d_ref, z_ref, bu0_ref, bu1_ref, st0_ref, st1_ref,
             e_ref, init_ref, *, seg_len):
    ch = lam_ref.shape[-1]
    rows = S5_TJ * S5_SEGMENTS
    n_tiles = seg_len // S5_TJ
    lr = jnp.broadcast_to(lam_ref[0:1, :], (S5_SEGMENTS, ch))
    li = jnp.broadcast_to(lam_ref[1:2, :], (S5_SEGMENTS, ch))

    def row0(t):
        return pl.multiple_of(jnp.clip(t, 0, n_tiles - 1) * rows, rows)

    def bu_tile(t, bu_ref):
        bu_ref[...] = jnp.dot(x_ref[pl.ds(row0(t), rows), :], b_ref[...],
                              preferred_element_type=F32)

    def scan_tile(bu_ref, s_re, s_im, st_ref):
        for j in range(S5_TJ):
            sl = slice(S5_SEGMENTS * j, S5_SEGMENTS * (j + 1))
            n_re = lr * s_re - li * s_im + bu_ref[sl, 0:ch]
            n_im = lr * s_im + li * s_re + bu_ref[sl, ch:2 * ch]
            s_re, s_im = n_re, n_im
            if st_ref is not None:
                st_ref[sl, 0:ch] = s_re
                st_ref[sl, ch:2 * ch] = s_im
        return s_re, s_im

    def project(t, st_ref):
        r0 = row0(t)
        y = jnp.dot(st_ref[...].astype(BF16), c_ref[...], preferred_element_type=F32)
        y = y + d_ref[...] * x_ref[pl.ds(r0, rows), :].astype(F32)
        z_ref[pl.ds(r0, rows), :] = _gelu_tanh(y).astype(z_ref.dtype)

    def pass1(i2, c):
        t = 2 * i2
        bu_tile(t + 1, bu1_ref)
        c = scan_tile(bu0_ref, c[0], c[1], None)
        bu_tile(t + 2, bu0_ref)
        return scan_tile(bu1_ref, c[0], c[1], None)

    zeros = jnp.zeros((S5_SEGMENTS, ch), F32)
    bu_tile(0, bu0_ref)
    e_re, e_im = lax.fori_loop(0, n_tiles // 2, pass1, (zeros, zeros))
    e_ref[:, 0:ch] = e_re
    e_ref[:, ch:2 * ch] = e_im

    pr, pi = lam_ref[0:1, :], lam_ref[1:2, :]
    for _ in range(seg_len.bit_length() - 1):
        pr, pi = pr * pr - pi * pi, 2.0 * pr * pi
    cr = jnp.zeros((1, ch), F32)
    ci = jnp.zeros((1, ch), F32)
    init_ref[0:1, :] = jnp.zeros((1, 2 * ch), F32)
    for k in range(S5_SEGMENTS - 1):
        er = e_ref[k:k + 1, 0:ch]
        ei = e_ref[k:k + 1, ch:2 * ch]
        cr, ci = pr * cr - pi * ci + er, pr * ci + pi * cr + ei
        init_ref[k + 1:k + 2, 0:ch] = cr
        init_ref[k + 1:k + 2, ch:2 * ch] = ci

    def pass2(i2, c):
        t = 2 * i2
        bu_tile(t + 1, bu1_ref)
        c = scan_tile(bu0_ref, c[0], c[1], st0_ref)
        project(t - 1, st1_ref)
        bu_tile(t + 2, bu0_ref)
        c = scan_tile(bu1_ref, c[0], c[1], st1_ref)
        project(t, st0_ref)
        return c

    st1_ref[...] = jnp.zeros(st1_ref.shape, F32)
    bu_tile(0, bu0_ref)
    lax.fori_loop(0, n_tiles // 2, pass2, (init_ref[:, 0:ch], init_ref[:, ch:2 * ch]))
    project(n_tiles - 1, st1_ref)


def _s5_params(lam_re, lam_im, b_re, b_im, c_re, c_im, log_step):
    g, p = lam_re.shape
    gb = S5_GROUPS_PER_BLOCK
    nb = g // gb
    lam = lax.complex(lam_re.astype(F32), lam_im.astype(F32))
    delta = jnp.exp(log_step.astype(F32))[:, None]
    lam_bar = jnp.exp(lam * delta)
    b_bar = ((lam_bar - 1.0) / lam)[..., None] * lax.complex(b_re.astype(F32), b_im.astype(F32))
    eye = jnp.eye(gb, dtype=F32)

    def blk_b(part):
        part = part.reshape(nb, gb, p, SSM_GROUP)
        return jnp.einsum('cgph,gk->cghkp', part, eye).reshape(nb, gb * SSM_GROUP, gb * p)

    def blk_c(part):
        part = part.reshape(nb, gb, SSM_GROUP, p)
        return jnp.einsum('cghp,gk->cgpkh', part, eye).reshape(nb, gb * p, gb * SSM_GROUP)

    b_blk = jnp.concatenate([blk_b(jnp.real(b_bar)), blk_b(jnp.imag(b_bar))], axis=-1)
    c_blk = jnp.concatenate([blk_c(c_re.astype(F32)), blk_c(-c_im.astype(F32))], axis=1)
    lam_v = jnp.stack([jnp.real(lam_bar).reshape(nb, gb * p),
                       jnp.imag(lam_bar).reshape(nb, gb * p)], axis=1)
    return b_blk.astype(BF16), c_blk.astype(BF16), lam_v


def _s5_scan(xn, b_blk, c_blk, lam_v, d_skip, batch):
    n, d = xn.shape
    l = n // batch
    seg_len = l // S5_SEGMENTS
    nb, fb, ch2 = b_blk.shape
    rows = S5_TJ * S5_SEGMENTS
    return pl.pallas_call(
        functools.partial(_s5_body, seg_len=seg_len),
        grid=(batch, nb),
        in_specs=[pl.BlockSpec((l, fb), lambda b, c: (b, c)),
                  pl.BlockSpec((None, fb, ch2), lambda b, c: (c, 0, 0)),
                  pl.BlockSpec((None, ch2, fb), lambda b, c: (c, 0, 0)),
                  pl.BlockSpec((None, 2, ch2 // 2), lambda b, c: (c, 0, 0)),
                  pl.BlockSpec((1, fb), lambda b, c: (0, c))],
        out_specs=pl.BlockSpec((l, fb), lambda b, c: (b, c)),
        out_shape=jax.ShapeDtypeStruct((n, d), BF16),
        scratch_shapes=[pltpu.VMEM((rows, ch2), F32), pltpu.VMEM((rows, ch2), F32),
                        pltpu.VMEM((rows, ch2), F32), pltpu.VMEM((rows, ch2), F32),
                        pltpu.VMEM((S5_SEGMENTS, ch2), F32), pltpu.VMEM((S5_SEGMENTS, ch2), F32)],
        compiler_params=_cparams(("arbitrary", "arbitrary")),
        name="s5_scan",
    )(xn, b_blk, c_blk, lam_v, d_skip.reshape(1, d).astype(F32))


def _t5_bucket(n):
    n_safe = jnp.maximum(n, 1).astype(F32)
    large = REL_MAX_EXACT + (jnp.log(n_safe / REL_MAX_EXACT)
                             / math.log(REL_MAX_DISTANCE / REL_MAX_EXACT)
                             * (REL_BUCKETS - REL_MAX_EXACT)).astype(jnp.int32)
    large = jnp.minimum(large, REL_BUCKETS - 1)
    return jnp.where(n < REL_MAX_EXACT, n, large)


ATTN_TQ = 256
M_INIT = -1e29
LOG2E = 1.4426950408889634


def _attn_bias_tables(rel_bias, tq):
    qi = jnp.arange(tq, dtype=jnp.int32)[:, None]
    ki = jnp.arange(tq, dtype=jnp.int32)[None, :]
    d_diag = qi - ki
    rb = rel_bias.astype(F32)
    rb = (rb - rb[REL_BUCKETS - 1][None, :]) * LOG2E

    def lookup(dist):
        onehot = jax.nn.one_hot(_t5_bucket(dist), REL_BUCKETS, dtype=F32)
        return jnp.einsum('qkb,bh->hqk', onehot, rb, precision=lax.Precision.HIGHEST)

    b_diag = jnp.where((d_diag >= 0)[None], lookup(jnp.maximum(d_diag, 0)), NEG_BIG)
    b_prev = lookup(d_diag + tq)
    masked = jnp.full_like(b_diag, NEG_BIG)
    return jnp.stack([jnp.concatenate([b_prev, b_diag], axis=-1),
                      jnp.concatenate([b_diag, masked], axis=-1)], axis=1)


def _attn_pair_tables(nq):
    idle = (0, nq, 0, 0)
    wide = [(i, i, 2 * j, 0) for i in range(nq) for j in range((i - 1) // 2)]
    odd = [(i, i, i - 2, 0) for i in range(2, nq) if (i - 1) % 2]
    near = [(i, i, max(i - 1, 0), 0 if i else 1) for i in range(nq)]
    rows = [idle, idle] + wide + odd + near + [idle, idle]
    kinds = 'WW' + 'W' * len(wide) + 'O' * len(odd) + 'N' * len(near) + 'NN'
    cols = list(zip(*rows))
    return [jnp.asarray(c, jnp.int32) for c in cols], kinds


def _attn_body(qrow_ref, slot_ref, kt_ref, bidx_ref, q_ref, k_ref, v_ref, bias_ref, lq_ref, sg_ref,
               o_ref, m_ref, acc_ref, vaug_ref, s0_ref, s1_ref, p0_ref, p1_ref, a0_ref, a1_ref,
               sw0_ref, sw1_ref, pw0_ref, pw1_ref, *, tq, nq, kinds, lam_init):
    m_ref[...] = jnp.full(m_ref.shape, M_INIT, F32)
    acc_ref[...] = jnp.zeros(acc_ref.shape, F32)
    vaug_ref[:, 0:LANES] = v_ref[...]
    vaug_ref[:, LANES:2 * LANES] = jnp.ones(v_ref.shape, BF16)
    lane = lax.broadcasted_iota(jnp.int32, (tq, LANES), 1)

    def scores(e, s_ref, tk):
        q = q_ref[pl.ds(pl.multiple_of(qrow_ref[e] * tq, tq), tq), :]
        zero = jnp.zeros_like(q)
        qq = jnp.concatenate([jnp.where(lane < HEAD_DIM, q, zero),
                              jnp.where(lane < HEAD_DIM, zero, q)], axis=0)
        k = k_ref[pl.ds(pl.multiple_of(kt_ref[e] * tq, tq), tk), :]
        s_ref[...] = lax.dot_general(qq, k, (((1,), (1,)), ((), ())), preferred_element_type=F32)

    def softmax(e, s_ref, a_ref, p_ref, with_bias, rc):
        m_view = m_ref.at[slot_ref[e]]
        b_view = bias_ref.at[bidx_ref[e]]
        for c in range(tq // rc):
            b = b_view[c * rc:(c + 1) * rc, :] if with_bias else None
            for half in range(2):
                rows = slice(half * tq + c * rc, half * tq + (c + 1) * rc)
                s = s_ref[rows, :]
                if with_bias:
                    s = s + b
                m_prev = m_view[rows, :]
                m_next = jnp.maximum(m_prev, jnp.max(s, axis=1, keepdims=True))
                a_ref[rows, :] = jnp.exp2(m_prev - m_next)
                p_ref[rows, :] = jnp.exp2(s - m_next[:, 0:1]).astype(BF16)
                m_view[rows, :] = m_next

    def values(e, a_ref, p_ref, tk):
        acc = acc_ref.at[slot_ref[e]]
        v = vaug_ref[pl.ds(pl.multiple_of(kt_ref[e] * tq, tq), tk), :]
        pv = jnp.dot(p_ref[...], v, preferred_element_type=F32)
        a = a_ref[...]
        acc[:, 0:LANES] = acc[:, 0:LANES] * a + pv[:, 0:LANES]
        acc[:, LANES:2 * LANES] = acc[:, LANES:2 * LANES] * a + pv[:, LANES:2 * LANES]

    kind_cfg = {'W': (2 * tq, False, 32, (sw0_ref, sw1_ref), (pw0_ref, pw1_ref)),
                'O': (tq, False, 64, (s0_ref, s1_ref), (p0_ref, p1_ref)),
                'N': (2 * tq, True, 32, (sw0_ref, sw1_ref), (pw0_ref, pw1_ref))}
    a_refs = (a0_ref, a1_ref)

    def iteration(it, par, k_a, k_b, k_c):
        tk_a, _, _, s_a, _ = kind_cfg[k_a]
        _, bias_b, rc_b, s_b, p_b = kind_cfg[k_b]
        tk_c, _, _, _, p_c = kind_cfg[k_c]
        scores(it + 2, s_a[par], tk_a)
        softmax(it + 1, s_b[1 - par], a_refs[1 - par], p_b[1 - par], bias_b, rc_b)
        values(it, a_refs[par], p_c[par], tk_c)

    pw0_ref[...] = jnp.zeros(pw0_ref.shape, BF16)
    a0_ref[...] = jnp.ones(a0_ref.shape, F32)
    scores(1, sw1_ref, 2 * tq)

    n_it = len(kinds) - 2
    it = 0
    while it < n_it:
        trio = (kinds[it + 2], kinds[it + 1], kinds[it])
        run = 1
        while it + run < n_it and (kinds[it + run + 2], kinds[it + run + 1], kinds[it + run]) == trio:
            run += 1
        start, stop = it, it + run
        if start % 2 and start < stop:
            iteration(start, 1, *trio)
            start += 1
        n_pairs = (stop - start) // 2
        if n_pairs > 0:
            def two_iterations(i2, c, start=start, trio=trio):
                iteration(start + 2 * i2, 0, *trio)
                iteration(start + 2 * i2 + 1, 1, *trio)
                return c

            lax.fori_loop(0, n_pairs, two_iterations, 0)
        if (stop - start) % 2:
            iteration(stop - 1, (stop - 1) % 2, *trio)
        it = stop

    lq = lq_ref[...]
    lam = (jnp.exp(jnp.sum(lq[0:1] * lq[1:2], axis=1, keepdims=True))
           - jnp.exp(jnp.sum(lq[2:3] * lq[3:4], axis=1, keepdims=True)) + lam_init)

    def finalize(i, c):
        acc = acc_ref[i]
        o_all = acc[:, 0:LANES] / acc[:, LANES:2 * LANES]
        o = o_all[0:tq] - lam * o_all[tq:2 * tq]
        ms = jnp.mean(o * o, axis=-1, keepdims=True)
        o = (o * lax.rsqrt(ms + RMS_EPS) * sg_ref[...]) * (1.0 - lam_init)
        o_ref[pl.ds(pl.multiple_of(i * tq, tq), tq), :] = o.astype(o_ref.dtype)
        return c

    lax.fori_loop(0, nq, finalize, 0, unroll=4)


def _diff_attention(qn, kn, vb, bias_tabs, lambda_qk, subln_g, lam_init, batch):
    n, d = qn.shape
    l = n // batch
    tq = ATTN_TQ
    nq = l // tq
    nh = d // LANES
    tables, kinds = _attn_pair_tables(nq)
    row_blk = pl.BlockSpec((l, LANES), lambda b, h, *_: (b, h))
    grid_spec = pltpu.PrefetchScalarGridSpec(
        num_scalar_prefetch=4,
        grid=(batch, nh),
        in_specs=[row_blk, row_blk, row_blk,
                  pl.BlockSpec((None, 2, tq, 2 * tq), lambda b, h, *_: (h, 0, 0, 0)),
                  pl.BlockSpec((4, HEAD_DIM), lambda b, h, *_: (0, 0)),
                  pl.BlockSpec((1, LANES), lambda b, h, *_: (0, 0))],
        out_specs=row_blk,
        scratch_shapes=[pltpu.VMEM((nq + 1, 2 * tq, LANES), F32),
                        pltpu.VMEM((nq + 1, 2 * tq, 2 * LANES), F32),
                        pltpu.VMEM((l, 2 * LANES), BF16),
                        pltpu.VMEM((2 * tq, tq), F32), pltpu.VMEM((2 * tq, tq), F32),
                        pltpu.VMEM((2 * tq, tq), BF16), pltpu.VMEM((2 * tq, tq), BF16),
                        pltpu.VMEM((2 * tq, LANES), F32), pltpu.VMEM((2 * tq, LANES), F32),
                        pltpu.VMEM((2 * tq, 2 * tq), F32), pltpu.VMEM((2 * tq, 2 * tq), F32),
                        pltpu.VMEM((2 * tq, 2 * tq), BF16), pltpu.VMEM((2 * tq, 2 * tq), BF16)],
    )
    return pl.pallas_call(
        functools.partial(_attn_body, tq=tq, nq=nq, kinds=kinds, lam_init=lam_init),
        grid_spec=grid_spec,
        out_shape=jax.ShapeDtypeStruct((n, d), BF16),
        compiler_params=_cparams(("arbitrary", "arbitrary")),
        name="diff_attention",
    )(*tables, qn, kn, vb, bias_tabs, lambda_qk.astype(F32),
      subln_g.reshape(1, LANES).astype(F32))


ROUTER_E0 = N_EXPERT_GROUPS
META_COLS = 8


def _router_body(h_ref, g_ref, wr_ref, br_ref, xn_ref, meta_ref, cnt_ref, base_ref):
    tm = h_ref.shape[0]

    @pl.when(pl.program_id(0) == 0)
    def _():
        base_ref[...] = jnp.zeros(base_ref.shape, F32)

    x = h_ref[...]
    ms = jnp.mean(x * x, axis=-1, keepdims=True)
    xn = x * lax.rsqrt(ms + RMS_EPS) * g_ref[...]
    xn_ref[...] = xn.astype(xn_ref.dtype)
    x_hi = xn.astype(BF16)
    x_lo = (xn - x_hi.astype(F32)).astype(BF16)
    logits = (jnp.dot(x_hi, wr_ref[0], preferred_element_type=F32)
              + (jnp.dot(x_hi, wr_ref[1], preferred_element_type=F32)
                 + jnp.dot(x_lo, wr_ref[0], preferred_element_type=F32))) + br_ref[...]
    lane = lax.broadcasted_iota(jnp.int32, (tm, LANES), 1)
    neg_inf = -jnp.inf

    def first_argmax(vals):
        vmax = jnp.max(vals, axis=-1, keepdims=True)
        idx = jnp.min(jnp.where(vals == vmax, lane, LANES), axis=-1, keepdims=True)
        return vmax, idx

    is_g = lane < N_EXPERT_GROUPS
    gmax, g_idx = first_argmax(jnp.where(is_g, logits, neg_inf))
    g_w = 1.0 / jnp.sum(jnp.where(is_g, jnp.exp(logits - gmax), 0.0), axis=-1, keepdims=True)
    e_lo = ROUTER_E0 + EXPERTS_PER_GROUP * g_idx
    elog = jnp.where((lane >= e_lo) & (lane < e_lo + EXPERTS_PER_GROUP), logits, neg_inf)
    v0, i0 = first_argmax(elog)
    v1, i1 = first_argmax(jnp.where(lane == i0, neg_inf, elog))
    t = jnp.exp(v1 - v0)
    w0 = g_w / (1.0 + t)
    w1 = g_w * t / (1.0 + t)

    sel0 = lane == i0
    sel1 = lane == i1
    onehot = jnp.where(sel0 | sel1, 1.0, 0.0)
    r_i = lax.broadcasted_iota(jnp.int32, (tm, tm), 0)
    c_i = lax.broadcasted_iota(jnp.int32, (tm, tm), 1)
    tri = jnp.where(c_i < r_i, 1.0, 0.0).astype(BF16)
    before = jnp.dot(tri, onehot.astype(BF16), preferred_element_type=F32) + base_ref[...]
    rank0 = jnp.sum(jnp.where(sel0, before, 0.0), axis=-1, keepdims=True)
    rank1 = jnp.sum(jnp.where(sel1, before, 0.0), axis=-1, keepdims=True)
    base_ref[...] = base_ref[...] + jnp.sum(onehot, axis=0, keepdims=True)
    cnt_ref[...] = base_ref[...]

    eid0 = (i0 - ROUTER_E0).astype(F32)
    eid1 = (i1 - ROUTER_E0).astype(F32)
    meta = jnp.where(lane == 0, eid0, jnp.where(lane == 1, eid1, jnp.where(
        lane == 2, rank0, jnp.where(lane == 3, rank1, jnp.where(
            lane == 4, w0, jnp.where(lane == 5, w1, 0.0))))))
    meta_ref[...] = meta[:, 0:META_COLS]


def _router(h, g, wg1, bg1, wg2, bg2, tm=256):
    n, d = h.shape
    wr = jnp.concatenate([wg1.astype(F32),
                          wg2.astype(F32).transpose(1, 0, 2).reshape(d, N_EXPERTS)], axis=1)
    wr = jnp.pad(wr, ((0, 0), (0, LANES - wr.shape[1])))
    wr_hi = wr.astype(BF16)
    wr = jnp.stack([wr_hi, (wr - wr_hi.astype(F32)).astype(BF16)])
    br = jnp.pad(jnp.concatenate([bg1.astype(F32), bg2.astype(F32).reshape(-1)]),
                 (0, LANES - N_EXPERT_GROUPS - N_EXPERTS)).reshape(1, LANES)
    return pl.pallas_call(
        _router_body,
        grid=(n // tm,),
        in_specs=[pl.BlockSpec((tm, d), lambda i: (i, 0)),
                  pl.BlockSpec((1, d), lambda i: (0, 0)),
                  pl.BlockSpec((2, d, LANES), lambda i: (0, 0, 0)),
                  pl.BlockSpec((1, LANES), lambda i: (0, 0))],
        out_specs=[pl.BlockSpec((tm, d), lambda i: (i, 0)),
                   pl.BlockSpec((tm, META_COLS), lambda i: (i, 0)),
                   pl.BlockSpec((1, LANES), lambda i: (0, 0))],
        out_shape=[jax.ShapeDtypeStruct((n, d), F32),
                   jax.ShapeDtypeStruct((n, META_COLS), F32),
                   jax.ShapeDtypeStruct((1, LANES), F32)],
        scratch_shapes=[pltpu.VMEM((1, LANES), F32)],
        compiler_params=_cparams(("arbitrary",)),
        name="moe_router",
    )(h, g.reshape(1, d).astype(F32), wr, br)


def _invert_body(dest_ref, rt_ref, *, ts):
    t = pl.program_id(0)

    def body(r, c):
        tok = t * ts + r
        rt_ref[dest_ref[0, 2 * r]] = tok
        rt_ref[dest_ref[0, 2 * r + 1]] = tok
        return c

    lax.fori_loop(0, ts, body, 0, unroll=True)


def _invert(dest, ts=512):
    n = dest.shape[0]
    dest3 = dest.reshape(n // ts, 1, 2 * ts)
    return pl.pallas_call(
        functools.partial(_invert_body, ts=ts),
        grid=(n // ts,),
        in_specs=[pl.BlockSpec((None, 1, 2 * ts), lambda t: (t, 0, 0), memory_space=pltpu.SMEM)],
        out_specs=pl.BlockSpec(memory_space=pltpu.SMEM),
        out_shape=jax.ShapeDtypeStruct((2 * n,), jnp.int32),
        compiler_params=_cparams(("arbitrary",)),
        name="moe_invert",
    )(dest3)


def _combine_body(d0_ref, dn_ref, h_ref, w_ref, y_hbm, *rest, ts, nt, with_norm):
    if with_norm:
        g_ref, o_ref, xn_ref, y0_ref, y1_ref, sem = rest
    else:
        o_ref, y0_ref, y1_ref, sem = rest
    t = pl.program_id(0)
    slot = t % 2

    def gather_start(dest_ref, dst):
        def issue(r, c):
            pltpu.make_async_copy(y_hbm.at[pl.ds(dest_ref[0, 2 * r], 1), :],
                                  y0_ref.at[dst, pl.ds(r, 1), :], sem.at[dst, 0]).start()
            pltpu.make_async_copy(y_hbm.at[pl.ds(dest_ref[0, 2 * r + 1], 1), :],
                                  y1_ref.at[dst, pl.ds(r, 1), :], sem.at[dst, 1]).start(priority=1)
            return c

        lax.fori_loop(0, ts, issue, 0, unroll=True)

    @pl.when(t == 0)
    def _():
        gather_start(d0_ref, 0)

    @pl.when(t + 1 < nt)
    def _():
        gather_start(dn_ref, 1 - slot)

    pltpu.make_async_copy(y_hbm.at[pl.ds(0, ts), :], y0_ref.at[slot], sem.at[slot, 0]).wait()
    pltpu.make_async_copy(y_hbm.at[pl.ds(0, ts), :], y1_ref.at[slot], sem.at[slot, 1]).wait()
    w = w_ref[...]
    out = h_ref[...] + w[:, 4:5] * y0_ref[slot] + w[:, 5:6] * y1_ref[slot]
    o_ref[...] = out
    if with_norm:
        ms = jnp.mean(out * out, axis=-1, keepdims=True)
        xn_ref[...] = (out * lax.rsqrt(ms + RMS_EPS) * g_ref[...]).astype(xn_ref.dtype)


def _combine(h, meta, y_sorted, dest, next_norm_g=None, ts=256):
    n, d = h.shape
    nt = n // ts
    dest3 = dest.reshape(nt, 1, 2 * ts)
    with_norm = next_norm_g is not None
    row_blk = pl.BlockSpec((ts, d), lambda t: (t, 0))
    in_specs = [pl.BlockSpec((None, 1, 2 * ts), lambda t: (0, 0, 0), memory_space=pltpu.SMEM),
                pl.BlockSpec((None, 1, 2 * ts), lambda t: (jnp.minimum(t + 1, nt - 1), 0, 0),
                             memory_space=pltpu.SMEM),
                row_blk,
                pl.BlockSpec((ts, META_COLS), lambda t: (t, 0)),
                pl.BlockSpec(memory_space=pl.ANY)]
    args = [dest3, dest3, h, meta, y_sorted]
    out_specs, out_shape = row_blk, jax.ShapeDtypeStruct((n, d), F32)
    if with_norm:
        in_specs.append(pl.BlockSpec((1, d), lambda t: (0, 0)))
        args.append(next_norm_g.reshape(1, d).astype(F32))
        out_specs = [row_blk, row_blk]
        out_shape = [out_shape, jax.ShapeDtypeStruct((n, d), BF16)]
    return pl.pallas_call(
        functools.partial(_combine_body, ts=ts, nt=nt, with_norm=with_norm),
        grid=(nt,),
        in_specs=in_specs,
        out_specs=out_specs,
        out_shape=out_shape,
        scratch_shapes=[pltpu.VMEM((2, ts, d), F32), pltpu.VMEM((2, ts, d), F32),
                        pltpu.SemaphoreType.DMA((2, 2))],
        compiler_params=_cparams(("arbitrary",)),
        name="moe_combine",
    )(*args)


EXPERT_TM = 256
EXPERT_VMEM_LIMIT = 56 * 1024 * 1024


def _expert_body(ie_ref, it_ref, lo_ref, hi_ref, first_ref, rfirst_ref, nxt_ref,
                 rt0_ref, rtn_ref, xn_hbm, wg_hbm, wu_hbm, wd_hbm, o_ref,
                 xbuf, stg_g, stg_u, stg_d, wgb, wub, wdb, sem_x, sem_w, *, tm, layer, n_tiles):
    i = pl.program_id(0)
    e = ie_ref[i]
    t = it_ref[i]
    lo = lo_ref[i]
    hi = hi_ref[i]
    slot = t % 2

    def weight_copies(ex):
        return (pltpu.make_async_copy(wg_hbm.at[layer, ex], stg_g, sem_w.at[0]),
                pltpu.make_async_copy(wu_hbm.at[layer, ex], stg_u, sem_w.at[1]),
                pltpu.make_async_copy(wd_hbm.at[layer, ex], stg_d, sem_w.at[2]))

    def gather_start(rt_ref, dst_slot):
        def issue(r2, c):
            for par in range(2):
                r = 2 * r2 + par
                pltpu.make_async_copy(xn_hbm.at[pl.ds(rt_ref[0, r], 1), :],
                                      xbuf.at[dst_slot, pl.ds(r, 1), :],
                                      sem_x.at[dst_slot]).start(priority=par)
            return c

        lax.fori_loop(0, tm // 2, issue, 0, unroll=True)

    @pl.when(i == 0)
    def _():
        for cp in weight_copies(e):
            cp.start()
        gather_start(rt0_ref, 0)

    @pl.when(rfirst_ref[i] == 1)
    def _():
        for cp in weight_copies(e):
            cp.wait()
        rows = 256

        def cast_rows(r, c):
            r0 = pl.multiple_of(r * rows, rows)
            wgb[pl.ds(r0, rows), :] = stg_g[pl.ds(r0, rows), :].astype(BF16)
            wub[pl.ds(r0, rows), :] = stg_u[pl.ds(r0, rows), :].astype(BF16)
            return c

        lax.fori_loop(0, stg_g.shape[0] // rows, cast_rows, 0)

        def cast_rows_d(r, c):
            r0 = pl.multiple_of(r * rows, rows)
            wdb[pl.ds(r0, rows), :] = stg_d[pl.ds(r0, rows), :].astype(BF16)
            return c

        lax.fori_loop(0, stg_d.shape[0] // rows, cast_rows_d, 0)

        @pl.when(nxt_ref[i] >= 0)
        def _():
            for cp in weight_copies(nxt_ref[i]):
                cp.start()

    @pl.when(first_ref[i] == 1)
    def _():
        @pl.when(t + 1 < n_tiles)
        def _():
            gather_start(rtn_ref, 1 - slot)

        pltpu.make_async_copy(xn_hbm.at[pl.ds(0, tm), :], xbuf.at[slot], sem_x.at[slot]).wait()

    @pl.when(hi > lo)
    def _():
        xb = xbuf[slot].astype(BF16)
        g = jnp.dot(xb, wgb[...], preferred_element_type=F32)
        u = jnp.dot(xb, wub[...], preferred_element_type=F32)
        hdn = (g * (1.0 / (1.0 + jnp.exp(-g)))) * u
        row = t * tm + lax.broadcasted_iota(jnp.int32, (tm, 1), 0)
        hdn = jnp.where((row >= lo) & (row < hi), hdn, 0.0)
        y = jnp.dot(hdn.astype(BF16), wdb[...], preferred_element_type=F32)

        @pl.when(first_ref[i] == 1)
        def _():
            o_ref[...] = y

        @pl.when(first_ref[i] != 1)
        def _():
            o_ref[...] = o_ref[...] + y


def _experts(xn, row_token, items, w_gate, w_up, w_down, layer, tm):
    _, d = xn.shape
    dh = w_gate.shape[-1]
    r_total = row_token.shape[0]
    n_tiles = r_total // tm
    n_items = items[0].shape[0]
    rt3 = row_token.reshape(n_tiles, 1, tm)
    grid_spec = pltpu.PrefetchScalarGridSpec(
        num_scalar_prefetch=7,
        grid=(n_items,),
        in_specs=[
            pl.BlockSpec((None, 1, tm), lambda i, *_: (0, 0, 0), memory_space=pltpu.SMEM),
            pl.BlockSpec((None, 1, tm), lambda i, ie, it, *_: (jnp.minimum(it[i] + 1, n_tiles - 1), 0, 0),
                         memory_space=pltpu.SMEM),
            pl.BlockSpec(memory_space=pl.ANY), pl.BlockSpec(memory_space=pl.ANY),
            pl.BlockSpec(memory_space=pl.ANY), pl.BlockSpec(memory_space=pl.ANY),
        ],
        out_specs=pl.BlockSpec((tm, d), lambda i, ie, it, *_: (it[i], 0)),
        scratch_shapes=[pltpu.VMEM((2, tm, d), F32),
                        pltpu.VMEM((d, dh), F32), pltpu.VMEM((d, dh), F32), pltpu.VMEM((dh, d), F32),
                        pltpu.VMEM((d, dh), BF16), pltpu.VMEM((d, dh), BF16), pltpu.VMEM((dh, d), BF16),
                        pltpu.SemaphoreType.DMA((2,)), pltpu.SemaphoreType.DMA((3,))],
    )
    return pl.pallas_call(
        functools.partial(_expert_body, tm=tm, layer=layer, n_tiles=n_tiles),
        grid_spec=grid_spec,
        out_shape=jax.ShapeDtypeStruct((r_total, d), F32),
        compiler_params=pltpu.CompilerParams(dimension_semantics=("arbitrary",),
                                             vmem_limit_bytes=EXPERT_VMEM_LIMIT),
        name="moe_experts",
    )(*items, rt3, rt3, xn, w_gate, w_up, w_down)


def _moe_plan(meta, counts_row, n_rows, tm):
    counts = counts_row[0, ROUTER_E0:ROUTER_E0 + N_EXPERTS].astype(jnp.int32)
    ends = jnp.cumsum(counts)
    starts = ends - counts
    eid = meta[:, 0:2].astype(jnp.int32)
    rank = meta[:, 2:4].astype(jnp.int32)
    dest = starts[eid] + rank

    first_tile = starts // tm
    last_tile = jnp.maximum(ends - 1, 0) // tm
    ntile_e = jnp.where(counts > 0, last_tile - first_tile + 1, 0)
    item_end = jnp.cumsum(ntile_e)
    item_start = item_end - ntile_e
    total = item_end[-1]
    n_items = n_rows // tm + N_EXPERTS - 1
    ii = jnp.arange(n_items, dtype=jnp.int32)
    valid = ii < total
    ii_c = jnp.minimum(ii, total - 1)
    e_i = jnp.sum((ii_c[:, None] >= item_end[None, :]).astype(jnp.int32), axis=1)
    t_i = (first_tile[e_i] + (ii_c - item_start[e_i])).astype(jnp.int32)
    lo = jnp.where(valid, jnp.maximum(starts[e_i], t_i * tm), 0).astype(jnp.int32)
    hi = jnp.where(valid, jnp.minimum(ends[e_i], (t_i + 1) * tm), 0).astype(jnp.int32)
    minus1 = jnp.full((1,), -1, jnp.int32)
    first = (valid & (t_i != jnp.concatenate([minus1, t_i[:-1]]))).astype(jnp.int32)
    run_first = (valid & (e_i != jnp.concatenate([minus1, e_i[:-1]]))).astype(jnp.int32)
    ar = jnp.arange(N_EXPERTS, dtype=jnp.int32)
    later = (ar[None, :] > ar[:, None]) & (counts > 0)[None, :]
    nxt_e = jnp.min(jnp.where(later, ar[None, :], N_EXPERTS), axis=1)
    nxt_e = jnp.where(nxt_e == N_EXPERTS, -1, nxt_e).astype(jnp.int32)
    items = (e_i, t_i, lo, hi, first, run_first, nxt_e[e_i])
    return dest.astype(jnp.int32), items


def _hier_moe(h, g, wg1, bg1, wg2, bg2, w_gate, w_up, w_down, layer, next_norm_g=None,
              tm=EXPERT_TM):
    n, _ = h.shape
    xn, meta, counts_row = _router(h, g, wg1, bg1, wg2, bg2)
    dest, items = _moe_plan(meta, counts_row, 2 * n, tm)
    y_sorted = _experts(xn, _invert(dest), items, w_gate, w_up, w_down, layer, tm)
    return _combine(h, meta, y_sorted, dest, next_norm_g)


def _time_permute(h, batch, inverse=False):
    n, d = h.shape
    l = n // batch
    shape = (batch, l // S5_SEGMENTS, S5_SEGMENTS, d) if inverse else (batch, S5_SEGMENTS, l // S5_SEGMENTS, d)
    return h.reshape(shape).transpose(0, 2, 1, 3).reshape(n, d)


def kernel(x, norm_mix_g, norm_ffn_g, ssm_lam_re, ssm_lam_im, ssm_b_re, ssm_b_im, ssm_c_re, ssm_c_im, ssm_d, ssm_log_step, ssm_w_glu1, ssm_w_glu2, kv_norm_g, w_kv, k_norm_g, w_q, q_norm_g, lambda_qk, subln_g, w_o, rel_bias, router_group_w, router_group_b, router_expert_w, router_expert_b, w_gate, w_up, w_down):
    batch, l, d = x.shape
    n = batch * l
    h = _time_permute(x.astype(F32).reshape(n, d), batch)
    kn = vb = bias_tabs = None
    xn = _rmsnorm(h, norm_mix_g[0], BF16)
    for layer in range(DEPTH):
        if layer < N_A_LAYERS:
            b_blk, c_blk, lam_v = _s5_params(ssm_lam_re[layer], ssm_lam_im[layer], ssm_b_re[layer],
                                             ssm_b_im[layer], ssm_c_re[layer], ssm_c_im[layer],
                                             ssm_log_step[layer])
            z = _s5_scan(xn, b_blk, c_blk, lam_v, ssm_d[layer], batch)
            h = _matmul(z, [ssm_w_glu1, ssm_w_glu2], layer, F32, res=h, tm=1024)
        else:
            j = layer - N_A_LAYERS
            lam_init = 0.8 - 0.6 * math.exp(-0.3 * layer)
            qn = _matmul(xn, [w_q], j, BF16, head_norm=(q_norm_g[j], HEAD_DIM ** -0.5 * LOG2E),
                         tn=PROJ_TN)
            o = _diff_attention(qn, kn, vb, bias_tabs, lambda_qk[j], subln_g[j], lam_init, batch)
            h = _matmul(o, [w_o], j, F32, res=h, tn=PROJ_TN)
        fuse_next = layer + 1 < DEPTH and layer != N_A_LAYERS - 1
        moe_out = _hier_moe(h, norm_ffn_g[layer], router_group_w[layer], router_group_b[layer],
                            router_expert_w[layer], router_expert_b[layer], w_gate, w_up, w_down,
                            layer, norm_mix_g[layer + 1] if fuse_next else None)
        h, xn = moe_out if fuse_next else (moe_out, None)
        if layer == N_A_LAYERS - 1:
            h = _time_permute(h, batch, inverse=True)
            xkv = _rmsnorm(h, kv_norm_g, BF16)
            w_kv3 = w_kv.reshape(1, d, 2 * d)
            kn = _matmul(xkv, [w_kv3], 0, BF16, n_out=d, head_norm=(k_norm_g, 1.0), tn=PROJ_TN)
            vb = _matmul(xkv, [w_kv3], 0, BF16, col_off=d // PROJ_TN, n_out=d, tn=PROJ_TN)
            bias_tabs = _attn_bias_tables(rel_bias, ATTN_TQ)
            xn = _rmsnorm(h, norm_mix_g[layer + 1], BF16)
    return h.reshape(batch, l, d)
```

```python
import functools
import math

import jax
import jax.numpy as jnp
from jax import lax
from jax.experimental import pallas as pl
from jax.experimental.pallas import tpu as pltpu

F32 = jnp.float32
BF16 = jnp.bfloat16

DEPTH = 4
N_A_LAYERS = DEPTH // 2
SSM_GROUP = 16
SSM_STATE = 64
HEAD_DIM = 64
REL_BUCKETS = 32
REL_MAX_EXACT = REL_BUCKETS // 2
REL_MAX_DISTANCE = 128
N_EXPERT_GROUPS = 4
EXPERTS_PER_GROUP = 4
N_EXPERTS = N_EXPERT_GROUPS * EXPERTS_PER_GROUP
RMS_EPS = 1e-6

LANES = 128
SUBLANES = 8
VMEM_LIMIT = 48 * 1024 * 1024
NEG_BIG = -1e30

S5_SEGMENTS = SUBLANES
S5_GROUPS_PER_BLOCK = 16
S5_TJ = 32
PROJ_TN = 1024


def _cparams(sem):
    return pltpu.CompilerParams(dimension_semantics=sem, vmem_limit_bytes=VMEM_LIMIT)


def _rmsnorm_body(h_ref, g_ref, o_ref):
    x = h_ref[...]
    ms = jnp.mean(x * x, axis=-1, keepdims=True)
    o_ref[...] = (x * lax.rsqrt(ms + RMS_EPS) * g_ref[...]).astype(o_ref.dtype)


def _rmsnorm(h, g, out_dtype, tm=512):
    n, d = h.shape
    return pl.pallas_call(
        _rmsnorm_body,
        grid=(n // tm,),
        in_specs=[pl.BlockSpec((tm, d), lambda i: (i, 0)),
                  pl.BlockSpec((1, d), lambda i: (0, 0))],
        out_specs=pl.BlockSpec((tm, d), lambda i: (i, 0)),
        out_shape=jax.ShapeDtypeStruct((n, d), out_dtype),
        compiler_params=_cparams(("arbitrary",)),
        name="rmsnorm",
    )(h, g.reshape(1, d))


def _mm_body(*refs, n_w, has_res, hn_scale):
    a_ref = refs[0]
    w_refs = refs[1:1 + n_w]
    pos = 1 + n_w
    res_ref = refs[pos] if has_res else None
    pos += int(has_res)
    g_ref = refs[pos] if hn_scale is not None else None
    pos += int(hn_scale is not None)
    o_ref = refs[pos]
    wb_refs = refs[pos + 1:]

    @pl.when(pl.program_id(1) == 0)
    def _():
        for w_ref, wb_ref in zip(w_refs, wb_refs):
            wb_ref[...] = w_ref[...].astype(BF16)

    a = a_ref[...]
    y = jnp.dot(a, wb_refs[0][...], preferred_element_type=F32)
    if n_w == 2:
        y2 = jnp.dot(a, wb_refs[1][...], preferred_element_type=F32)
        y = y * (1.0 / (1.0 + jnp.exp(-y2)))
    if has_res:
        y = y + res_ref[...]
    if hn_scale is None:
        o_ref[...] = y.astype(o_ref.dtype)
    else:
        tm, tn = y.shape
        lane = lax.broadcasted_iota(jnp.int32, (tm, LANES), 1)
        lo = lane < HEAD_DIM
        g = g_ref[...]
        for hb in range(tn // LANES):
            x = y[:, hb * LANES:(hb + 1) * LANES]
            x2 = x * x
            s_lo = jnp.sum(jnp.where(lo, x2, 0.0), axis=-1, keepdims=True)
            s_hi = jnp.sum(jnp.where(lo, 0.0, x2), axis=-1, keepdims=True)
            r = jnp.where(lo, lax.rsqrt(s_lo * (1.0 / HEAD_DIM) + RMS_EPS),
                          lax.rsqrt(s_hi * (1.0 / HEAD_DIM) + RMS_EPS))
            o_ref[:, hb * LANES:(hb + 1) * LANES] = ((x * r * g) * hn_scale).astype(o_ref.dtype)


def _matmul(a, ws, layer, out_dtype, res=None, col_off=0, n_out=None, head_norm=None,
            tm=512, tn=512):
    m, k = a.shape
    n_out = ws[0].shape[-1] if n_out is None else n_out
    n_w = len(ws)
    in_specs = [pl.BlockSpec((tm, k), lambda j, i: (i, 0))]
    for _ in ws:
        in_specs.append(pl.BlockSpec((None, k, tn), lambda j, i: (layer, 0, j + col_off)))
    args = [a, *ws]
    if res is not None:
        in_specs.append(pl.BlockSpec((tm, tn), lambda j, i: (i, j)))
        args.append(res)
    hn_scale = None
    if head_norm is not None:
        gain, hn_scale = head_norm
        in_specs.append(pl.BlockSpec((1, LANES), lambda j, i: (0, 0)))
        args.append(jnp.concatenate([gain, gain]).reshape(1, LANES).astype(F32))
    return pl.pallas_call(
        functools.partial(_mm_body, n_w=n_w, has_res=res is not None, hn_scale=hn_scale),
        grid=(n_out // tn, m // tm),
        in_specs=in_specs,
        out_specs=pl.BlockSpec((tm, tn), lambda j, i: (i, j)),
        out_shape=jax.ShapeDtypeStruct((m, n_out), out_dtype),
        scratch_shapes=[pltpu.VMEM((k, tn), BF16) for _ in ws],
        compiler_params=_cparams(("arbitrary", "arbitrary")),
        name="dense_matmul",
    )(*args)


def _gelu_tanh(y):
    c = math.sqrt(2.0 / math.pi)
    return y * (0.5 * (1.0 + jnp.tanh(c * (y + 0.044715 * (y * y * y)))))


def _s5_body(x_ref, b_ref, c_ref, lam_ref, d_ref, z_ref, bu0_ref, bu1_ref, st0_ref, st1_ref,
             e_ref, init_ref, *, seg_len):
    ch = lam_ref.shape[-1]
    rows = S5_TJ * S5_SEGMENTS
    n_tiles = seg_len // S5_TJ
    lr = jnp.broadcast_to(lam_ref[0:1, :], (S5_SEGMENTS, ch))
    li = jnp.broadcast_to(lam_ref[1:2, :], (S5_SEGMENTS, ch))

    def row0(t):
        return pl.multiple_of(jnp.clip(t, 0, n_tiles - 1) * rows, rows)

    def bu_tile(t, bu_ref):
        bu_ref[...] = jnp.dot(x_ref[pl.ds(row0(t), rows), :], b_ref[...],
                              preferred_element_type=F32)

    def scan_tile(bu_ref, s_re, s_im, st_ref):
        for j in range(S5_TJ):
            sl = slice(S5_SEGMENTS * j, S5_SEGMENTS * (j + 1))
            n_re = lr * s_re - li * s_im + bu_ref[sl, 0:ch]
            n_im = lr * s_im + li * s_re + bu_ref[sl, ch:2 * ch]
            s_re, s_im = n_re, n_im
            if st_ref is not None:
                st_ref[sl, 0:ch] = s_re
                st_ref[sl, ch:2 * ch] = s_im
        return s_re, s_im

    def project(t, st_ref):
        r0 = row0(t)
        y = jnp.dot(st_ref[...].astype(BF16), c_ref[...], preferred_element_type=F32)
        y = y + d_ref[...] * x_ref[pl.ds(r0, rows), :].astype(F32)
        z_ref[pl.ds(r0, rows), :] = _gelu_tanh(y).astype(z_ref.dtype)

    def pass1(i2, c):
        t = 2 * i2
        bu_tile(t + 1, bu1_ref)
        c = scan_tile(bu0_ref, c[0], c[1], None)
        bu_tile(t + 2, bu0_ref)
        return scan_tile(bu1_ref, c[0], c[1], None)

    zeros = jnp.zeros((S5_SEGMENTS, ch), F32)
    bu_tile(0, bu0_ref)
    e_re, e_im = lax.fori_loop(0, n_tiles // 2, pass1, (zeros, zeros))
    e_ref[:, 0:ch] = e_re
    e_ref[:, ch:2 * ch] = e_im

    pr, pi = lam_ref[0:1, :], lam_ref[1:2, :]
    for _ in range(seg_len.bit_length() - 1):
        pr, pi = pr * pr - pi * pi, 2.0 * pr * pi
    cr = jnp.zeros((1, ch), F32)
    ci = jnp.zeros((1, ch), F32)
    init_ref[0:1, :] = jnp.zeros((1, 2 * ch), F32)
    for k in range(S5_SEGMENTS - 1):
        er = e_ref[k:k + 1, 0:ch]
        ei = e_ref[k:k + 1, ch:2 * ch]
        cr, ci = pr * cr - pi * ci + er, pr * ci + pi * cr + ei
        init_ref[k + 1:k + 2, 0:ch] = cr
        init_ref[k + 1:k + 2, ch:2 * ch] = ci

    def pass2(i2, c):
        t = 2 * i2
        bu_tile(t + 1, bu1_ref)
        c = scan_tile(bu0_ref, c[0], c[1], st0_ref)
        project(t - 1, st1_ref)
        bu_tile(t + 2, bu0_ref)
        c = scan_tile(bu1_ref, c[0], c[1], st1_ref)
        project(t, st0_ref)
        return c

    st1_ref[...] = jnp.zeros(st1_ref.shape, F32)
    bu_tile(0, bu0_ref)
    lax.fori_loop(0, n_tiles // 2, pass2, (init_ref[:, 0:ch], init_ref[:, ch:2 * ch]))
    project(n_tiles - 1, st1_ref)


def _s5_params(lam_re, lam_im, b_re, b_im, c_re, c_im, log_step):
    g, p = lam_re.shape
    gb = S5_GROUPS_PER_BLOCK
    nb = g // gb
    lam = lax.complex(lam_re.astype(F32), lam_im.astype(F32))
    delta = jnp.exp(log_step.astype(F32))[:, None]
    lam_bar = jnp.exp(lam * delta)
    b_bar = ((lam_bar - 1.0) / lam)[..., None] * lax.complex(b_re.astype(F32), b_im.astype(F32))
    eye = jnp.eye(gb, dtype=F32)

    def blk_b(part):
        part = part.reshape(nb, gb, p, SSM_GROUP)
        return jnp.einsum('cgph,gk->cghkp', part, eye).reshape(nb, gb * SSM_GROUP, gb * p)

    def blk_c(part):
        part = part.reshape(nb, gb, SSM_GROUP, p)
        return jnp.einsum('cghp,gk->cgpkh', part, eye).reshape(nb, gb * p, gb * SSM_GROUP)

    b_blk = jnp.concatenate([blk_b(jnp.real(b_bar)), blk_b(jnp.imag(b_bar))], axis=-1)
    c_blk = jnp.concatenate([blk_c(c_re.astype(F32)), blk_c(-c_im.astype(F32))], axis=1)
    lam_v = jnp.stack([jnp.real(lam_bar).reshape(nb, gb * p),
                       jnp.imag(lam_bar).reshape(nb, gb * p)], axis=1)
    return b_blk.astype(BF16), c_blk.astype(BF16), lam_v


def _s5_scan(xn, b_blk, c_blk, lam_v, d_skip, batch):
    n, d = xn.shape
    l = n // batch
    seg_len = l // S5_SEGMENTS
    nb, fb, ch2 = b_blk.shape
    rows = S5_TJ * S5_SEGMENTS
    return pl.pallas_call(
        functools.partial(_s5_body, seg_len=seg_len),
        grid=(batch, nb),
        in_specs=[pl.BlockSpec((l, fb), lambda b, c: (b, c)),
                  pl.BlockSpec((None, fb, ch2), lambda b, c: (c, 0, 0)),
                  pl.BlockSpec((None, ch2, fb), lambda b, c: (c, 0, 0)),
                  pl.BlockSpec((None, 2, ch2 // 2), lambda b, c: (c, 0, 0)),
                  pl.BlockSpec((1, fb), lambda b, c: (0, c))],
        out_specs=pl.BlockSpec((l, fb), lambda b, c: (b, c)),
        out_shape=jax.ShapeDtypeStruct((n, d), BF16),
        scratch_shapes=[pltpu.VMEM((rows, ch2), F32), pltpu.VMEM((rows, ch2), F32),
                        pltpu.VMEM((rows, ch2), F32), pltpu.VMEM((rows, ch2), F32),
                        pltpu.VMEM((S5_SEGMENTS, ch2), F32), pltpu.VMEM((S5_SEGMENTS, ch2), F32)],
        compiler_params=_cparams(("arbitrary", "arbitrary")),
        name="s5_scan",
    )(xn, b_blk, c_blk, lam_v, d_skip.reshape(1, d).astype(F32))


def _t5_bucket(n):
    n_safe = jnp.maximum(n, 1).astype(F32)
    large = REL_MAX_EXACT + (jnp.log(n_safe / REL_MAX_EXACT)
                             / math.log(REL_MAX_DISTANCE / REL_MAX_EXACT)
                             * (REL_BUCKETS - REL_MAX_EXACT)).astype(jnp.int32)
    large = jnp.minimum(large, REL_BUCKETS - 1)
    return jnp.where(n < REL_MAX_EXACT, n, large)


ATTN_TQ = 256
M_INIT = -1e29
LOG2E = 1.4426950408889634


def _attn_bias_tables(rel_bias, tq):
    qi = jnp.arange(tq, dtype=jnp.int32)[:, None]
    ki = jnp.arange(tq, dtype=jnp.int32)[None, :]
    d_diag = qi - ki
    rb = rel_bias.astype(F32)
    rb = (rb - rb[REL_BUCKETS - 1][None, :]) * LOG2E

    def lookup(dist):
        onehot = jax.nn.one_hot(_t5_bucket(dist), REL_BUCKETS, dtype=F32)
        return jnp.einsum('qkb,bh->hqk', onehot, rb, precision=lax.Precision.HIGHEST)

    b_diag = jnp.where((d_diag >= 0)[None], lookup(jnp.maximum(d_diag, 0)), NEG_BIG)
    b_prev = lookup(d_diag + tq)
    masked = jnp.full_like(b_diag, NEG_BIG)
    return jnp.stack([jnp.concatenate([b_prev, b_diag], axis=-1),
                      jnp.concatenate([b_diag, masked], axis=-1)], axis=1)


def _attn_pair_tables(nq):
    idle = (0, nq, 0, 0)
    wide = [(i, i, 2 * j, 0) for i in range(nq) for j in range((i - 1) // 2)]
    odd = [(i, i, i - 2, 0) for i in range(2, nq) if (i - 1) % 2]
    near = [(i, i, max(i - 1, 0), 0 if i else 1) for i in range(nq)]
    rows = [idle, idle] + wide + odd + near + [idle, idle]
    kinds = 'WW' + 'W' * len(wide) + 'O' * len(odd) + 'N' * len(near) + 'NN'
    cols = list(zip(*rows))
    return [jnp.asarray(c, jnp.int32) for c in cols], kinds


def _attn_body(qrow_ref, slot_ref, kt_ref, bidx_ref, q_ref, k_ref, v_ref, bias_ref, lq_ref, sg_ref,
               o_ref, m_ref, acc_ref, vaug_ref, s0_ref, s1_ref, p0_ref, p1_ref, a0_ref, a1_ref,
               sw0_ref, sw1_ref, pw0_ref, pw1_ref, *, tq, nq, kinds, lam_init):
    m_ref[...] = jnp.full(m_ref.shape, M_INIT, F32)
    acc_ref[...] = jnp.zeros(acc_ref.shape, F32)
    vaug_ref[:, 0:LANES] = v_ref[...]
    vaug_ref[:, LANES:2 * LANES] = jnp.ones(v_ref.shape, BF16)
    lane = lax.broadcasted_iota(jnp.int32, (tq, LANES), 1)

    def scores(e, s_ref, tk):
        q = q_ref[pl.ds(pl.multiple_of(qrow_ref[e] * tq, tq), tq), :]
        zero = jnp.zeros_like(q)
        qq = jnp.concatenate([jnp.where(lane < HEAD_DIM, q, zero),
                              jnp.where(lane < HEAD_DIM, zero, q)], axis=0)
        k = k_ref[pl.ds(pl.multiple_of(kt_ref[e] * tq, tq), tk), :]
        s_ref[...] = lax.dot_general(qq, k, (((1,), (1,)), ((), ())), preferred_element_type=F32)

    def softmax(e, s_ref, a_ref, p_ref, with_bias, rc):
        m_view = m_ref.at[slot_ref[e]]
        b_view = bias_ref.at[bidx_ref[e]]
        for c in range(tq // rc):
            b = b_view[c * rc:(c + 1) * rc, :] if with_bias else None
            for half in range(2):
                rows = slice(half * tq + c * rc, half * tq + (c + 1) * rc)
                s = s_ref[rows, :]
                if with_bias:
                    s = s + b
                m_prev = m_view[rows, :]
                m_next = jnp.maximum(m_prev, jnp.max(s, axis=1, keepdims=True))
                a_ref[rows, :] = jnp.exp2(m_prev - m_next)
                p_ref[rows, :] = jnp.exp2(s - m_next[:, 0:1]).astype(BF16)
                m_view[rows, :] = m_next

    def values(e, a_ref, p_ref, tk):
        acc = acc_ref.at[slot_ref[e]]
        v = vaug_ref[pl.ds(pl.multiple_of(kt_ref[e] * tq, tq), tk), :]
        pv = jnp.dot(p_ref[...], v, preferred_element_type=F32)
        a = a_ref[...]
        acc[:, 0:LANES] = acc[:, 0:LANES] * a + pv[:, 0:LANES]
        acc[:, LANES:2 * LANES] = acc[:, LANES:2 * LANES] * a + pv[:, LANES:2 * LANES]

    kind_cfg = {'W': (2 * tq, False, 32, (sw0_ref, sw1_ref), (pw0_ref, pw1_ref)),
                'O': (tq, False, 64, (s0_ref, s1_ref), (p0_ref, p1_ref)),
                'N': (2 * tq, True, 32, (sw0_ref, sw1_ref), (pw0_ref, pw1_ref))}
    a_refs = (a0_ref, a1_ref)

    def iteration(it, par, k_a, k_b, k_c):
        tk_a, _, _, s_a, _ = kind_cfg[k_a]
        _, bias_b, rc_b, s_b, p_b = kind_cfg[k_b]
        tk_c, _, _, _, p_c = kind_cfg[k_c]
        scores(it + 2, s_a[par], tk_a)
        softmax(it + 1, s_b[1 - par], a_refs[1 - par], p_b[1 - par], bias_b, rc_b)
        values(it, a_refs[par], p_c[par], tk_c)

    pw0_ref[...] = jnp.zeros(pw0_ref.shape, BF16)
    a0_ref[...] = jnp.ones(a0_ref.shape, F32)
    scores(1, sw1_ref, 2 * tq)

    n_it = len(kinds) - 2
    it = 0
    while it < n_it:
        trio = (kinds[it + 2], kinds[it + 1], kinds[it])
        run = 1
        while it + run < n_it and (kinds[it + run + 2], kinds[it + run + 1], kinds[it + run]) == trio:
            run += 1
        start, stop = it, it + run
        if start % 2 and start < stop:
            iteration(start, 1, *trio)
            start += 1
        n_pairs = (stop - start) // 2
        if n_pairs > 0:
            def two_iterations(i2, c, start=start, trio=trio):
                iteration(start + 2 * i2, 0, *trio)
                iteration(start + 2 * i2 + 1, 1, *trio)
                return c

            lax.fori_loop(0, n_pairs, two_iterations, 0)
        if (stop - start) % 2:
            iteration(stop - 1, (stop - 1) % 2, *trio)
        it = stop

    lq = lq_ref[...]
    lam = (jnp.exp(jnp.sum(lq[0:1] * lq[1:2], axis=1, keepdims=True))
           - jnp.exp(jnp.sum(lq[2:3] * lq[3:4], axis=1, keepdims=True)) + lam_init)

    def finalize(i, c):
        acc = acc_ref[i]
        o_all = acc[:, 0:LANES] / acc[:, LANES:2 * LANES]
        o = o_all[0:tq] - lam * o_all[tq:2 * tq]
        ms = jnp.mean(o * o, axis=-1, keepdims=True)
        o = (o * lax.rsqrt(ms + RMS_EPS) * sg_ref[...]) * (1.0 - lam_init)
        o_ref[pl.ds(pl.multiple_of(i * tq, tq), tq), :] = o.astype(o_ref.dtype)
        return c

    lax.fori_loop(0, nq, finalize, 0, unroll=4)


def _diff_attention(qn, kn, vb, bias_tabs, lambda_qk, subln_g, lam_init, batch):
    n, d = qn.shape
    l = n // batch
    tq = ATTN_TQ
    nq = l // tq
    nh = d // LANES
    tables, kinds = _attn_pair_tables(nq)
    row_blk = pl.BlockSpec((l, LANES), lambda b, h, *_: (b, h))
    grid_spec = pltpu.PrefetchScalarGridSpec(
        num_scalar_prefetch=4,
        grid=(batch, nh),
        in_specs=[row_blk, row_blk, row_blk,
                  pl.BlockSpec((None, 2, tq, 2 * tq), lambda b, h, *_: (h, 0, 0, 0)),
                  pl.BlockSpec((4, HEAD_DIM), lambda b, h, *_: (0, 0)),
                  pl.BlockSpec((1, LANES), lambda b, h, *_: (0, 0))],
        out_specs=row_blk,
        scratch_shapes=[pltpu.VMEM((nq + 1, 2 * tq, LANES), F32),
                        pltpu.VMEM((nq + 1, 2 * tq, 2 * LANES), F32),
                        pltpu.VMEM((l, 2 * LANES), BF16),
                        pltpu.VMEM((2 * tq, tq), F32), pltpu.VMEM((2 * tq, tq), F32),
                        pltpu.VMEM((2 * tq, tq), BF16), pltpu.VMEM((2 * tq, tq), BF16),
                        pltpu.VMEM((2 * tq, LANES), F32), pltpu.VMEM((2 * tq, LANES), F32),
                        pltpu.VMEM((2 * tq, 2 * tq), F32), pltpu.VMEM((2 * tq, 2 * tq), F32),
                        pltpu.VMEM((2 * tq, 2 * tq), BF16), pltpu.VMEM((2 * tq, 2 * tq), BF16)],
    )
    return pl.pallas_call(
        functools.partial(_attn_body, tq=tq, nq=nq, kinds=kinds, lam_init=lam_init),
        grid_spec=grid_spec,
        out_shape=jax.ShapeDtypeStruct((n, d), BF16),
        compiler_params=_cparams(("arbitrary", "arbitrary")),
        name="diff_attention",
    )(*tables, qn, kn, vb, bias_tabs, lambda_qk.astype(F32),
      subln_g.reshape(1, LANES).astype(F32))


ROUTER_E0 = N_EXPERT_GROUPS
META_COLS = 8


def _router_body(h_ref, g_ref, wr_ref, br_ref, xn_ref, meta_ref, cnt_ref, base_ref):
    tm = h_ref.shape[0]

    @pl.when(pl.program_id(0) == 0)
    def _():
        base_ref[...] = jnp.zeros(base_ref.shape, F32)

    x = h_ref[...]
    ms = jnp.mean(x * x, axis=-1, keepdims=True)
    xn = x * lax.rsqrt(ms + RMS_EPS) * g_ref[...]
    xn_ref[...] = xn.astype(xn_ref.dtype)
    x_hi = xn.astype(BF16)
    x_lo = (xn - x_hi.astype(F32)).astype(BF16)
    logits = (jnp.dot(x_hi, wr_ref[0], preferred_element_type=F32)
              + (jnp.dot(x_hi, wr_ref[1], preferred_element_type=F32)
                 + jnp.dot(x_lo, wr_ref[0], preferred_element_type=F32))) + br_ref[...]
    lane = lax.broadcasted_iota(jnp.int32, (tm, LANES), 1)
    neg_inf = -jnp.inf

    def first_argmax(vals):
        vmax = jnp.max(vals, axis=-1, keepdims=True)
        idx = jnp.min(jnp.where(vals == vmax, lane, LANES), axis=-1, keepdims=True)
        return vmax, idx

    is_g = lane < N_EXPERT_GROUPS
    gmax, g_idx = first_argmax(jnp.where(is_g, logits, neg_inf))
    g_w = 1.0 / jnp.sum(jnp.where(is_g, jnp.exp(logits - gmax), 0.0), axis=-1, keepdims=True)
    e_lo = ROUTER_E0 + EXPERTS_PER_GROUP * g_idx
    elog = jnp.where((lane >= e_lo) & (lane < e_lo + EXPERTS_PER_GROUP), logits, neg_inf)
    v0, i0 = first_argmax(elog)
    v1, i1 = first_argmax(jnp.where(lane == i0, neg_inf, elog))
    t = jnp.exp(v1 - v0)
    w0 = g_w / (1.0 + t)
    w1 = g_w * t / (1.0 + t)

    sel0 = lane == i0
    sel1 = lane == i1
    onehot = jnp.where(sel0 | sel1, 1.0, 0.0)
    r_i = lax.broadcasted_iota(jnp.int32, (tm, tm), 0)
    c_i = lax.broadcasted_iota(jnp.int32, (tm, tm), 1)
    tri = jnp.where(c_i < r_i, 1.0, 0.0).astype(BF16)
    before = jnp.dot(tri, onehot.astype(BF16), preferred_element_type=F32) + base_ref[...]
    rank0 = jnp.sum(jnp.where(sel0, before, 0.0), axis=-1, keepdims=True)
    rank1 = jnp.sum(jnp.where(sel1, before, 0.0), axis=-1, keepdims=True)
    base_ref[...] = base_ref[...] + jnp.sum(onehot, axis=0, keepdims=True)
    cnt_ref[...] = base_ref[...]

    eid0 = (i0 - ROUTER_E0).astype(F32)
    eid1 = (i1 - ROUTER_E0).astype(F32)
    meta = jnp.where(lane == 0, eid0, jnp.where(lane == 1, eid1, jnp.where(
        lane == 2, rank0, jnp.where(lane == 3, rank1, jnp.where(
            lane == 4, w0, jnp.where(lane == 5, w1, 0.0))))))
    meta_ref[...] = meta[:, 0:META_COLS]


def _router(h, g, wg1, bg1, wg2, bg2, tm=256):
    n, d = h.shape
    wr = jnp.concatenate([wg1.astype(F32),
                          wg2.astype(F32).transpose(1, 0, 2).reshape(d, N_EXPERTS)], axis=1)
    wr = jnp.pad(wr, ((0, 0), (0, LANES - wr.shape[1])))
    wr_hi = wr.astype(BF16)
    wr = jnp.stack([wr_hi, (wr - wr_hi.astype(F32)).astype(BF16)])
    br = jnp.pad(jnp.concatenate([bg1.astype(F32), bg2.astype(F32).reshape(-1)]),
                 (0, LANES - N_EXPERT_GROUPS - N_EXPERTS)).reshape(1, LANES)
    return pl.pallas_call(
        _router_body,
        grid=(n // tm,),
        in_specs=[pl.BlockSpec((tm, d), lambda i: (i, 0)),
                  pl.BlockSpec((1, d), lambda i: (0, 0)),
                  pl.BlockSpec((2, d, LANES), lambda i: (0, 0, 0)),
                  pl.BlockSpec((1, LANES), lambda i: (0, 0))],
        out_specs=[pl.BlockSpec((tm, d), lambda i: (i, 0)),
                   pl.BlockSpec((tm, META_COLS), lambda i: (i, 0)),
                   pl.BlockSpec((1, LANES), lambda i: (0, 0))],
        out_shape=[jax.ShapeDtypeStruct((n, d), F32),
                   jax.ShapeDtypeStruct((n, META_COLS), F32),
                   jax.ShapeDtypeStruct((1, LANES), F32)],
        scratch_shapes=[pltpu.VMEM((1, LANES), F32)],
        compiler_params=_cparams(("arbitrary",)),
        name="moe_router",
    )(h, g.reshape(1, d).astype(F32), wr, br)


def _invert_body(dest_ref, rt_ref, *, ts):
    t = pl.program_id(0)

    def body(r, c):
        tok = t * ts + r
        rt_ref[dest_ref[0, 2 * r]] = tok
        rt_ref[dest_ref[0, 2 * r + 1]] = tok
        return c

    lax.fori_loop(0, ts, body, 0, unroll=True)


def _invert(dest, ts=512):
    n = dest.shape[0]
    dest3 = dest.reshape(n // ts, 1, 2 * ts)
    return pl.pallas_call(
        functools.partial(_invert_body, ts=ts),
        grid=(n // ts,),
        in_specs=[pl.BlockSpec((None, 1, 2 * ts), lambda t: (t, 0, 0), memory_space=pltpu.SMEM)],
        out_specs=pl.BlockSpec(memory_space=pltpu.SMEM),
        out_shape=jax.ShapeDtypeStruct((2 * n,), jnp.int32),
        compiler_params=_cparams(("arbitrary",)),
        name="moe_invert",
    )(dest3)


def _combine_body(d0_ref, dn_ref, h_ref, w_ref, y_hbm, *rest, ts, nt, with_norm):
    if with_norm:
        g_ref, o_ref, xn_ref, y0_ref, y1_ref, sem = rest
    else:
        o_ref, y0_ref, y1_ref, sem = rest
    t = pl.program_id(0)
    slot = t % 2

    def gather_start(dest_ref, dst):
        def issue(r, c):
            pltpu.make_async_copy(y_hbm.at[pl.ds(dest_ref[0, 2 * r], 1), :],
                                  y0_ref.at[dst, pl.ds(r, 1), :], sem.at[dst, 0]).start()
            pltpu.make_async_copy(y_hbm.at[pl.ds(dest_ref[0, 2 * r + 1], 1), :],
                                  y1_ref.at[dst, pl.ds(r, 1), :], sem.at[dst, 1]).start(priority=1)
            return c

        lax.fori_loop(0, ts, issue, 0, unroll=True)

    @pl.when(t == 0)
    def _():
        gather_start(d0_ref, 0)

    @pl.when(t + 1 < nt)
    def _():
        gather_start(dn_ref, 1 - slot)

    pltpu.make_async_copy(y_hbm.at[pl.ds(0, ts), :], y0_ref.at[slot], sem.at[slot, 0]).wait()
    pltpu.make_async_copy(y_hbm.at[pl.ds(0, ts), :], y1_ref.at[slot], sem.at[slot, 1]).wait()
    w = w_ref[...]
    out = h_ref[...] + w[:, 4:5] * y0_ref[slot] + w[:, 5:6] * y1_ref[slot]
    o_ref[...] = out
    if with_norm:
        ms = jnp.mean(out * out, axis=-1, keepdims=True)
        xn_ref[...] = (out * lax.rsqrt(ms + RMS_EPS) * g_ref[...]).astype(xn_ref.dtype)


def _combine(h, meta, y_sorted, dest, next_norm_g=None, ts=256):
    n, d = h.shape
    nt = n // ts
    dest3 = dest.reshape(nt, 1, 2 * ts)
    with_norm = next_norm_g is not None
    row_blk = pl.BlockSpec((ts, d), lambda t: (t, 0))
    in_specs = [pl.BlockSpec((None, 1, 2 * ts), lambda t: (0, 0, 0), memory_space=pltpu.SMEM),
                pl.BlockSpec((None, 1, 2 * ts), lambda t: (jnp.minimum(t + 1, nt - 1), 0, 0),
                             memory_space=pltpu.SMEM),
                row_blk,
                pl.BlockSpec((ts, META_COLS), lambda t: (t, 0)),
                pl.BlockSpec(memory_space=pl.ANY)]
    args = [dest3, dest3, h, meta, y_sorted]
    out_specs, out_shape = row_blk, jax.ShapeDtypeStruct((n, d), F32)
    if with_norm:
        in_specs.append(pl.BlockSpec((1, d), lambda t: (0, 0)))
        args.append(next_norm_g.reshape(1, d).astype(F32))
        out_specs = [row_blk, row_blk]
        out_shape = [out_shape, jax.ShapeDtypeStruct((n, d), BF16)]
    return pl.pallas_call(
        functools.partial(_combine_body, ts=ts, nt=nt, with_norm=with_norm),
        grid=(nt,),
        in_specs=in_specs,
        out_specs=out_specs,
        out_shape=out_shape,
        scratch_shapes=[pltpu.VMEM((2, ts, d), F32), pltpu.VMEM((2, ts, d), F32),
                        pltpu.SemaphoreType.DMA((2, 2))],
        compiler_params=_cparams(("arbitrary",)),
        name="moe_combine",
    )(*args)


EXPERT_TM = 256
EXPERT_VMEM_LIMIT = 56 * 1024 * 1024


def _expert_body(ie_ref, it_ref, lo_ref, hi_ref, first_ref, rfirst_ref, nxt_ref,
                 rt0_ref, rtn_ref, xn_hbm, wg_hbm, wu_hbm, wd_hbm, o_ref,
                 xbuf, stg_g, stg_u, stg_d, wgb, wub, wdb, sem_x, sem_w, *, tm, layer, n_tiles):
    i = pl.program_id(0)
    e = ie_ref[i]
    t = it_ref[i]
    lo = lo_ref[i]
    hi = hi_ref[i]
    slot = t % 2

    def weight_copies(ex):
        return (pltpu.make_async_copy(wg_hbm.at[layer, ex], stg_g, sem_w.at[0]),
                pltpu.make_async_copy(wu_hbm.at[layer, ex], stg_u, sem_w.at[1]),
                pltpu.make_async_copy(wd_hbm.at[layer, ex], stg_d, sem_w.at[2]))

    def gather_start(rt_ref, dst_slot):
        def issue(r2, c):
            for par in range(2):
                r = 2 * r2 + par
                pltpu.make_async_copy(xn_hbm.at[pl.ds(rt_ref[0, r], 1), :],
                                      xbuf.at[dst_slot, pl.ds(r, 1), :],
                                      sem_x.at[dst_slot]).start(priority=par)
            return c

        lax.fori_loop(0, tm // 2, issue, 0, unroll=True)

    @pl.when(i == 0)
    def _():
        for cp in weight_copies(e):
            cp.start()
        gather_start(rt0_ref, 0)

    @pl.when(rfirst_ref[i] == 1)
    def _():
        for cp in weight_copies(e):
            cp.wait()
        rows = 256

        def cast_rows(r, c):
            r0 = pl.multiple_of(r * rows, rows)
            wgb[pl.ds(r0, rows), :] = stg_g[pl.ds(r0, rows), :].astype(BF16)
            wub[pl.ds(r0, rows), :] = stg_u[pl.ds(r0, rows), :].astype(BF16)
            return c

        lax.fori_loop(0, stg_g.shape[0] // rows, cast_rows, 0)

        def cast_rows_d(r, c):
            r0 = pl.multiple_of(r * rows, rows)
            wdb[pl.ds(r0, rows), :] = stg_d[pl.ds(r0, rows), :].astype(BF16)
            return c

        lax.fori_loop(0, stg_d.shape[0] // rows, cast_rows_d, 0)

        @pl.when(nxt_ref[i] >= 0)
        def _():
            for cp in weight_copies(nxt_ref[i]):
                cp.start()

    @pl.when(first_ref[i] == 1)
    def _():
        @pl.when(t + 1 < n_tiles)
        def _():
            gather_start(rtn_ref, 1 - slot)

        pltpu.make_async_copy(xn_hbm.at[pl.ds(0, tm), :], xbuf.at[slot], sem_x.at[slot]).wait()

    @pl.when(hi > lo)
    def _():
        xb = xbuf[slot].astype(BF16)
        g = jnp.dot(xb, wgb[...], preferred_element_type=F32)
        u = jnp.dot(xb, wub[...], preferred_element_type=F32)
        hdn = (g * (1.0 / (1.0 + jnp.exp(-g)))) * u
        row = t * tm + lax.broadcasted_iota(jnp.int32, (tm, 1), 0)
        hdn = jnp.where((row >= lo) & (row < hi), hdn, 0.0)
        y = jnp.dot(hdn.astype(BF16), wdb[...], preferred_element_type=F32)

        @pl.when(first_ref[i] == 1)
        def _():
            o_ref[...] = y

        @pl.when(first_ref[i] != 1)
        def _():
            o_ref[...] = o_ref[...] + y


def _experts(xn, row_token, items, w_gate, w_up, w_down, layer, tm):
    _, d = xn.shape
    dh = w_gate.shape[-1]
    r_total = row_token.shape[0]
    n_tiles = r_total // tm
    n_items = items[0].shape[0]
    rt3 = row_token.reshape(n_tiles, 1, tm)
    grid_spec = pltpu.PrefetchScalarGridSpec(
        num_scalar_prefetch=7,
        grid=(n_items,),
        in_specs=[
            pl.BlockSpec((None, 1, tm), lambda i, *_: (0, 0, 0), memory_space=pltpu.SMEM),
            pl.BlockSpec((None, 1, tm), lambda i, ie, it, *_: (jnp.minimum(it[i] + 1, n_tiles - 1), 0, 0),
                         memory_space=pltpu.SMEM),
            pl.BlockSpec(memory_space=pl.ANY), pl.BlockSpec(memory_space=pl.ANY),
            pl.BlockSpec(memory_space=pl.ANY), pl.BlockSpec(memory_space=pl.ANY),
        ],
        out_specs=pl.BlockSpec((tm, d), lambda i, ie, it, *_: (it[i], 0)),
        scratch_shapes=[pltpu.VMEM((2, tm, d), F32),
                        pltpu.VMEM((d, dh), F32), pltpu.VMEM((d, dh), F32), pltpu.VMEM((dh, d), F32),
                        pltpu.VMEM((d, dh), BF16), pltpu.VMEM((d, dh), BF16), pltpu.VMEM((dh, d), BF16),
                        pltpu.SemaphoreType.DMA((2,)), pltpu.SemaphoreType.DMA((3,))],
    )
    return pl.pallas_call(
        functools.partial(_expert_body, tm=tm, layer=layer, n_tiles=n_tiles),
        grid_spec=grid_spec,
        out_shape=jax.ShapeDtypeStruct((r_total, d), F32),
        compiler_params=pltpu.CompilerParams(dimension_semantics=("arbitrary",),
                                             vmem_limit_bytes=EXPERT_VMEM_LIMIT),
        name="moe_experts",
    )(*items, rt3, rt3, xn, w_gate, w_up, w_down)


def _moe_plan(meta, counts_row, n_rows, tm):
    counts = counts_row[0, ROUTER_E0:ROUTER_E0 + N_EXPERTS].astype(jnp.int32)
    ends = jnp.cumsum(counts)
    starts = ends - counts
    eid = meta[:, 0:2].astype(jnp.int32)
    rank = meta[:, 2:4].astype(jnp.int32)
    dest = starts[eid] + rank

    first_tile = starts // tm
    last_tile = jnp.maximum(ends - 1, 0) // tm
    ntile_e = jnp.where(counts > 0, last_tile - first_tile + 1, 0)
    item_end = jnp.cumsum(ntile_e)
    item_start = item_end - ntile_e
    total = item_end[-1]
    n_items = n_rows // tm + N_EXPERTS - 1
    ii = jnp.arange(n_items, dtype=jnp.int32)
    valid = ii < total
    ii_c = jnp.minimum(ii, total - 1)
    e_i = jnp.sum((ii_c[:, None] >= item_end[None, :]).astype(jnp.int32), axis=1)
    t_i = (first_tile[e_i] + (ii_c - item_start[e_i])).astype(jnp.int32)
    lo = jnp.where(valid, jnp.maximum(starts[e_i], t_i * tm), 0).astype(jnp.int32)
    hi = jnp.where(valid, jnp.minimum(ends[e_i], (t_i + 1) * tm), 0).astype(jnp.int32)
    minus1 = jnp.full((1,), -1, jnp.int32)
    first = (valid & (t_i != jnp.concatenate([minus1, t_i[:-1]]))).astype(jnp.int32)
    run_first = (valid & (e_i != jnp.concatenate([minus1, e_i[:-1]]))).astype(jnp.int32)
    ar = jnp.arange(N_EXPERTS, dtype=jnp.int32)
    later = (ar[None, :] > ar[:, None]) & (counts > 0)[None, :]
    nxt_e = jnp.min(jnp.where(later, ar[None, :], N_EXPERTS), axis=1)
    nxt_e = jnp.where(nxt_e == N_EXPERTS, -1, nxt_e).astype(jnp.int32)
    items = (e_i, t_i, lo, hi, first, run_first, nxt_e[e_i])
    return dest.astype(jnp.int32), items


def _hier_moe(h, g, wg1, bg1, wg2, bg2, w_gate, w_up, w_down, layer, next_norm_g=None,
              tm=EXPERT_TM):
    n, _ = h.shape
    xn, meta, counts_row = _router(h, g, wg1, bg1, wg2, bg2)
    dest, items = _moe_plan(meta, counts_row, 2 * n, tm)
    y_sorted = _experts(xn, _invert(dest), items, w_gate, w_up, w_down, layer, tm)
    return _combine(h, meta, y_sorted, dest, next_norm_g)


def _time_permute(h, batch, inverse=False):
    n, d = h.shape
    l = n // batch
    shape = (batch, l // S5_SEGMENTS, S5_SEGMENTS, d) if inverse else (batch, S5_SEGMENTS, l // S5_SEGMENTS, d)
    return h.reshape(shape).transpose(0, 2, 1, 3).reshape(n, d)


def kernel(x, norm_mix_g, norm_ffn_g, ssm_lam_re, ssm_lam_im, ssm_b_re, ssm_b_im, ssm_c_re, ssm_c_im, ssm_d, ssm_log_step, ssm_w_glu1, ssm_w_glu2, kv_norm_g, w_kv, k_norm_g, w_q, q_norm_g, lambda_qk, subln_g, w_o, rel_bias, router_group_w, router_group_b, router_expert_w, router_expert_b, w_gate, w_up, w_down):
    batch, l, d = x.shape
    n = batch * l
    h = _time_permute(x.astype(F32).reshape(n, d), batch)
    kn = vb = bias_tabs = None
    xn = _rmsnorm(h, norm_mix_g[0], BF16)
    for layer in range(DEPTH):
        if layer < N_A_LAYERS:
            b_blk, c_blk, lam_v = _s5_params(ssm_lam_re[layer], ssm_lam_im[layer], ssm_b_re[layer],
                                             ssm_b_im[layer], ssm_c_re[layer], ssm_c_im[layer],
                                             ssm_log_step[layer])
            z = _s5_scan(xn, b_blk, c_blk, lam_v, ssm_d[layer], batch)
            h = _matmul(z, [ssm_w_glu1, ssm_w_glu2], layer, F32, res=h, tm=1024)
        else:
            j = layer - N_A_LAYERS
            lam_init = 0.8 - 0.6 * math.exp(-0.3 * layer)
            qn = _matmul(xn, [w_q], j, BF16, head_norm=(q_norm_g[j], HEAD_DIM ** -0.5 * LOG2E),
                         tm=1024, tn=PROJ_TN)
            o = _diff_attention(qn, kn, vb, bias_tabs, lambda_qk[j], subln_g[j], lam_init, batch)
            h = _matmul(o, [w_o], j, F32, res=h, tn=PROJ_TN)
        fuse_next = layer + 1 < DEPTH and layer != N_A_LAYERS - 1
        moe_out = _hier_moe(h, norm_ffn_g[layer], router_group_w[layer], router_group_b[layer],
                            router_expert_w[layer], router_expert_b[layer], w_gate, w_up, w_down,
                            layer, norm_mix_g[layer + 1] if fuse_next else None)
        h, xn = moe_out if fuse_next else (moe_out, None)
        if layer == N_A_LAYERS - 1:
            h = _time_permute(h, batch, inverse=True)
            xkv = _rmsnorm(h, kv_norm_g, BF16)
            w_kv3 = w_kv.reshape(1, d, 2 * d)
            kn = _matmul(xkv, [w_kv3], 0, BF16, n_out=d, head_norm=(k_norm_g, 1.0), tm=1024,
                         tn=PROJ_TN)
            vb = _matmul(xkv, [w_kv3], 0, BF16, col_off=d // PROJ_TN, n_out=d, tm=1024, tn=PROJ_TN)
            bias_tabs = _attn_bias_tables(rel_bias, ATTN_TQ)
            xn = _rmsnorm(h, norm_mix_g[layer + 1], BF16)
    return h.reshape(batch, l, d)
```

```python
import functools
import math

import jax
import jax.numpy as jnp
from jax import lax
from jax.experimental import pallas as pl
from jax.experimental.pallas import tpu as pltpu

F32 = jnp.float32
BF16 = jnp.bfloat16

DEPTH = 4
N_A_LAYERS = DEPTH // 2
SSM_GROUP = 16
SSM_STATE = 64
HEAD_DIM = 64
REL_BUCKETS = 32
REL_MAX_EXACT = REL_BUCKETS // 2
REL_MAX_DISTANCE = 128
N_EXPERT_GROUPS = 4
EXPERTS_PER_GROUP = 4
N_EXPERTS = N_EXPERT_GROUPS * EXPERTS_PER_GROUP
RMS_EPS = 1e-6

LANES = 128
SUBLANES = 8
VMEM_LIMIT = 48 * 1024 * 1024
NEG_BIG = -1e30

S5_SEGMENTS = SUBLANES
S5_GROUPS_PER_BLOCK = 16
S5_TJ = 32
PROJ_TN = 1024


def _cparams(sem):
    return pltpu.CompilerParams(dimension_semantics=sem, vmem_limit_bytes=VMEM_LIMIT)


def _rmsnorm_body(h_ref, g_ref, o_ref):
    x = h_ref[...]
    ms = jnp.mean(x * x, axis=-1, keepdims=True)
    o_ref[...] = (x * lax.rsqrt(ms + RMS_EPS) * g_ref[...]).astype(o_ref.dtype)


def _rmsnorm(h, g, out_dtype, tm=512):
    n, d = h.shape
    return pl.pallas_call(
        _rmsnorm_body,
        grid=(n // tm,),
        in_specs=[pl.BlockSpec((tm, d), lambda i: (i, 0)),
                  pl.BlockSpec((1, d), lambda i: (0, 0))],
        out_specs=pl.BlockSpec((tm, d), lambda i: (i, 0)),
        out_shape=jax.ShapeDtypeStruct((n, d), out_dtype),
        compiler_params=_cparams(("arbitrary",)),
        name="rmsnorm",
    )(h, g.reshape(1, d))


def _mm_body(*refs, n_w, has_res, hn_scale):
    a_ref = refs[0]
    w_refs = refs[1:1 + n_w]
    pos = 1 + n_w
    res_ref = refs[pos] if has_res else None
    pos += int(has_res)
    g_ref = refs[pos] if hn_scale is not None else None
    pos += int(hn_scale is not None)
    o_ref = refs[pos]
    wb_refs = refs[pos + 1:]

    @pl.when(pl.program_id(1) == 0)
    def _():
        for w_ref, wb_ref in zip(w_refs, wb_refs):
            wb_ref[...] = w_ref[...].astype(BF16)

    a = a_ref[...]
    y = jnp.dot(a, wb_refs[0][...], preferred_element_type=F32)
    if n_w == 2:
        y2 = jnp.dot(a, wb_refs[1][...], preferred_element_type=F32)
        y = y * (1.0 / (1.0 + jnp.exp(-y2)))
    if has_res:
        y = y + res_ref[...]
    if hn_scale is None:
        o_ref[...] = y.astype(o_ref.dtype)
    else:
        tm, tn = y.shape
        lane = lax.broadcasted_iota(jnp.int32, (tm, LANES), 1)
        lo = lane < HEAD_DIM
        g = g_ref[...]
        for hb in range(tn // LANES):
            x = y[:, hb * LANES:(hb + 1) * LANES]
            x2 = x * x
            s_lo = jnp.sum(jnp.where(lo, x2, 0.0), axis=-1, keepdims=True)
            s_hi = jnp.sum(jnp.where(lo, 0.0, x2), axis=-1, keepdims=True)
            r = jnp.where(lo, lax.rsqrt(s_lo * (1.0 / HEAD_DIM) + RMS_EPS),
                          lax.rsqrt(s_hi * (1.0 / HEAD_DIM) + RMS_EPS))
            o_ref[:, hb * LANES:(hb + 1) * LANES] = ((x * r * g) * hn_scale).astype(o_ref.dtype)


def _matmul(a, ws, layer, out_dtype, res=None, col_off=0, n_out=None, head_norm=None,
            tm=512, tn=512):
    m, k = a.shape
    n_out = ws[0].shape[-1] if n_out is None else n_out
    n_w = len(ws)
    in_specs = [pl.BlockSpec((tm, k), lambda j, i: (i, 0))]
    for _ in ws:
        in_specs.append(pl.BlockSpec((None, k, tn), lambda j, i: (layer, 0, j + col_off)))
    args = [a, *ws]
    if res is not None:
        in_specs.append(pl.BlockSpec((tm, tn), lambda j, i: (i, j)))
        args.append(res)
    hn_scale = None
    if head_norm is not None:
        gain, hn_scale = head_norm
        in_specs.append(pl.BlockSpec((1, LANES), lambda j, i: (0, 0)))
        args.append(jnp.concatenate([gain, gain]).reshape(1, LANES).astype(F32))
    return pl.pallas_call(
        functools.partial(_mm_body, n_w=n_w, has_res=res is not None, hn_scale=hn_scale),
        grid=(n_out // tn, m // tm),
        in_specs=in_specs,
        out_specs=pl.BlockSpec((tm, tn), lambda j, i: (i, j)),
        out_shape=jax.ShapeDtypeStruct((m, n_out), out_dtype),
        scratch_shapes=[pltpu.VMEM((k, tn), BF16) for _ in ws],
        compiler_params=_cparams(("arbitrary", "arbitrary")),
        name="dense_matmul",
    )(*args)


def _gelu_tanh(y):
    c = math.sqrt(2.0 / math.pi)
    return y * (0.5 * (1.0 + jnp.tanh(c * (y + 0.044715 * (y * y * y)))))


def _s5_body(x_ref, b_ref, c_ref, lam_ref, d_ref, z_ref, bu0_ref, bu1_ref, st0_ref, st1_ref,
             e_ref, init_ref, *, seg_len):
    ch = lam_ref.shape[-1]
    rows = S5_TJ * S5_SEGMENTS
    n_tiles = seg_len // S5_TJ
    lr = jnp.broadcast_to(lam_ref[0:1, :], (S5_SEGMENTS, ch))
    li = jnp.broadcast_to(lam_ref[1:2, :], (S5_SEGMENTS, ch))

    def row0(t):
        return pl.multiple_of(jnp.clip(t, 0, n_tiles - 1) * rows, rows)

    def bu_tile(t, bu_ref):
        bu_ref[...] = jnp.dot(x_ref[pl.ds(row0(t), rows), :], b_ref[...],
                              preferred_element_type=F32)

    def scan_tile(bu_ref, s_re, s_im, st_ref):
        for j in range(S5_TJ):
            sl = slice(S5_SEGMENTS * j, S5_SEGMENTS * (j + 1))
            n_re = lr * s_re - li * s_im + bu_ref[sl, 0:ch]
            n_im = lr * s_im + li * s_re + bu_ref[sl, ch:2 * ch]
            s_re, s_im = n_re, n_im
            if st_ref is not None:
                st_ref[sl, 0:ch] = s_re
                st_ref[sl, ch:2 * ch] = s_im
        return s_re, s_im

    def project(t, st_ref):
        r0 = row0(t)
        y = jnp.dot(st_ref[...].astype(BF16), c_ref[...], preferred_element_type=F32)
        y = y + d_ref[...] * x_ref[pl.ds(r0, rows), :].astype(F32)
        z_ref[pl.ds(r0, rows), :] = _gelu_tanh(y).astype(z_ref.dtype)

    def pass1(i2, c):
        t = 2 * i2
        bu_tile(t + 1, bu1_ref)
        c = scan_tile(bu0_ref, c[0], c[1], None)
        bu_tile(t + 2, bu0_ref)
        return scan_tile(bu1_ref, c[0], c[1], None)

    zeros = jnp.zeros((S5_SEGMENTS, ch), F32)
    bu_tile(0, bu0_ref)
    e_re, e_im = lax.fori_loop(0, n_tiles // 2, pass1, (zeros, zeros))
    e_ref[:, 0:ch] = e_re
    e_ref[:, ch:2 * ch] = e_im

    pr, pi = lam_ref[0:1, :], lam_ref[1:2, :]
    for _ in range(seg_len.bit_length() - 1):
        pr, pi = pr * pr - pi * pi, 2.0 * pr * pi
    cr = jnp.zeros((1, ch), F32)
    ci = jnp.zeros((1, ch), F32)
    init_ref[0:1, :] = jnp.zeros((1, 2 * ch), F32)
    for k in range(S5_SEGMENTS - 1):
        er = e_ref[k:k + 1, 0:ch]
        ei = e_ref[k:k + 1, ch:2 * ch]
        cr, ci = pr * cr - pi * ci + er, pr * ci + pi * cr + ei
        init_ref[k + 1:k + 2, 0:ch] = cr
        init_ref[k + 1:k + 2, ch:2 * ch] = ci

    def pass2(i2, c):
        t = 2 * i2
        bu_tile(t + 1, bu1_ref)
        c = scan_tile(bu0_ref, c[0], c[1], st0_ref)
        project(t - 1, st1_ref)
        bu_tile(t + 2, bu0_ref)
        c = scan_tile(bu1_ref, c[0], c[1], st1_ref)
        project(t, st0_ref)
        return c

    st1_ref[...] = jnp.zeros(st1_ref.shape, F32)
    bu_tile(0, bu0_ref)
    lax.fori_loop(0, n_tiles // 2, pass2, (init_ref[:, 0:ch], init_ref[:, ch:2 * ch]))
    project(n_tiles - 1, st1_ref)


def _s5_params(lam_re, lam_im, b_re, b_im, c_re, c_im, log_step):
    g, p = lam_re.shape
    gb = S5_GROUPS_PER_BLOCK
    nb = g // gb
    lam = lax.complex(lam_re.astype(F32), lam_im.astype(F32))
    delta = jnp.exp(log_step.astype(F32))[:, None]
    lam_bar = jnp.exp(lam * delta)
    b_bar = ((lam_bar - 1.0) / lam)[..., None] * lax.complex(b_re.astype(F32), b_im.astype(F32))
    eye = jnp.eye(gb, dtype=F32)

    def blk_b(part):
        part = part.reshape(nb, gb, p, SSM_GROUP)
        return jnp.einsum('cgph,gk->cghkp', part, eye).reshape(nb, gb * SSM_GROUP, gb * p)

    def blk_c(part):
        part = part.reshape(nb, gb, SSM_GROUP, p)
        return jnp.einsum('cghp,gk->cgpkh', part, eye).reshape(nb, gb * p, gb * SSM_GROUP)

    b_blk = jnp.concatenate([blk_b(jnp.real(b_bar)), blk_b(jnp.imag(b_bar))], axis=-1)
    c_blk = jnp.concatenate([blk_c(c_re.astype(F32)), blk_c(-c_im.astype(F32))], axis=1)
    lam_v = jnp.stack([jnp.real(lam_bar).reshape(nb, gb * p),
                       jnp.imag(lam_bar).reshape(nb, gb * p)], axis=1)
    return b_blk.astype(BF16), c_blk.astype(BF16), lam_v


def _s5_scan(xn, b_blk, c_blk, lam_v, d_skip, batch):
    n, d = xn.shape
    l = n // batch
    seg_len = l // S5_SEGMENTS
    nb, fb, ch2 = b_blk.shape
    rows = S5_TJ * S5_SEGMENTS
    return pl.pallas_call(
        functools.partial(_s5_body, seg_len=seg_len),
        grid=(batch, nb),
        in_specs=[pl.BlockSpec((l, fb), lambda b, c: (b, c)),
                  pl.BlockSpec((None, fb, ch2), lambda b, c: (c, 0, 0)),
                  pl.BlockSpec((None, ch2, fb), lambda b, c: (c, 0, 0)),
                  pl.BlockSpec((None, 2, ch2 // 2), lambda b, c: (c, 0, 0)),
                  pl.BlockSpec((1, fb), lambda b, c: (0, c))],
        out_specs=pl.BlockSpec((l, fb), lambda b, c: (b, c)),
        out_shape=jax.ShapeDtypeStruct((n, d), BF16),
        scratch_shapes=[pltpu.VMEM((rows, ch2), F32), pltpu.VMEM((rows, ch2), F32),
                        pltpu.VMEM((rows, ch2), F32), pltpu.VMEM((rows, ch2), F32),
                        pltpu.VMEM((S5_SEGMENTS, ch2), F32), pltpu.VMEM((S5_SEGMENTS, ch2), F32)],
        compiler_params=_cparams(("arbitrary", "arbitrary")),
        name="s5_scan",
    )(xn, b_blk, c_blk, lam_v, d_skip.reshape(1, d).astype(F32))


def _t5_bucket(n):
    n_safe = jnp.maximum(n, 1).astype(F32)
    large = REL_MAX_EXACT + (jnp.log(n_safe / REL_MAX_EXACT)
                             / math.log(REL_MAX_DISTANCE / REL_MAX_EXACT)
                             * (REL_BUCKETS - REL_MAX_EXACT)).astype(jnp.int32)
    large = jnp.minimum(large, REL_BUCKETS - 1)
    return jnp.where(n < REL_MAX_EXACT, n, large)


ATTN_TQ = 256
M_INIT = -1e29
LOG2E = 1.4426950408889634


def _attn_bias_tables(rel_bias, tq):
    qi = jnp.arange(tq, dtype=jnp.int32)[:, None]
    ki = jnp.arange(tq, dtype=jnp.int32)[None, :]
    d_diag = qi - ki
    rb = rel_bias.astype(F32)
    rb = (rb - rb[REL_BUCKETS - 1][None, :]) * LOG2E

    def lookup(dist):
        onehot = jax.nn.one_hot(_t5_bucket(dist), REL_BUCKETS, dtype=F32)
        return jnp.einsum('qkb,bh->hqk', onehot, rb, precision=lax.Precision.HIGHEST)

    b_diag = jnp.where((d_diag >= 0)[None], lookup(jnp.maximum(d_diag, 0)), NEG_BIG)
    b_prev = lookup(d_diag + tq)
    masked = jnp.full_like(b_diag, NEG_BIG)
    return jnp.stack([jnp.concatenate([b_prev, b_diag], axis=-1),
                      jnp.concatenate([b_diag, masked], axis=-1)], axis=1)


def _attn_pair_tables(nq):
    idle = (0, nq, 0, 0)
    wide = [(i, i, 2 * j, 0) for i in range(nq) for j in range((i - 1) // 2)]
    odd = [(i, i, i - 2, 0) for i in range(2, nq) if (i - 1) % 2]
    near = [(i, i, max(i - 1, 0), 0 if i else 1) for i in range(nq)]
    rows = [idle, idle] + wide + odd + near + [idle, idle]
    kinds = 'WW' + 'W' * len(wide) + 'O' * len(odd) + 'N' * len(near) + 'NN'
    cols = list(zip(*rows))
    return [jnp.asarray(c, jnp.int32) for c in cols], kinds


def _attn_body(qrow_ref, slot_ref, kt_ref, bidx_ref, q_ref, k_ref, v_ref, bias_ref, lq_ref, sg_ref,
               o_ref, m_ref, acc_ref, vaug_ref, s0_ref, s1_ref, p0_ref, p1_ref, a0_ref, a1_ref,
               sw0_ref, sw1_ref, pw0_ref, pw1_ref, *, tq, nq, kinds, lam_init):
    m_ref[...] = jnp.full(m_ref.shape, M_INIT, F32)
    acc_ref[...] = jnp.zeros(acc_ref.shape, F32)
    vaug_ref[:, 0:LANES] = v_ref[...]
    vaug_ref[:, LANES:2 * LANES] = jnp.ones(v_ref.shape, BF16)
    lane = lax.broadcasted_iota(jnp.int32, (tq, LANES), 1)

    def scores(e, s_ref, tk):
        q = q_ref[pl.ds(pl.multiple_of(qrow_ref[e] * tq, tq), tq), :]
        zero = jnp.zeros_like(q)
        qq = jnp.concatenate([jnp.where(lane < HEAD_DIM, q, zero),
                              jnp.where(lane < HEAD_DIM, zero, q)], axis=0)
        k = k_ref[pl.ds(pl.multiple_of(kt_ref[e] * tq, tq), tk), :]
        s_ref[...] = lax.dot_general(qq, k, (((1,), (1,)), ((), ())), preferred_element_type=F32)

    def softmax(e, s_ref, a_ref, p_ref, with_bias, rc):
        m_view = m_ref.at[slot_ref[e]]
        b_view = bias_ref.at[bidx_ref[e]]
        for c in range(tq // rc):
            b = b_view[c * rc:(c + 1) * rc, :] if with_bias else None
            for half in range(2):
                rows = slice(half * tq + c * rc, half * tq + (c + 1) * rc)
                s = s_ref[rows, :]
                if with_bias:
                    s = s + b
                m_prev = m_view[rows, :]
                m_next = jnp.maximum(m_prev, jnp.max(s, axis=1, keepdims=True))
                a_ref[rows, :] = jnp.exp2(m_prev - m_next)
                p_ref[rows, :] = jnp.exp2(s - m_next[:, 0:1]).astype(BF16)
                m_view[rows, :] = m_next

    def values(e, a_ref, p_ref, tk):
        acc = acc_ref.at[slot_ref[e]]
        v = vaug_ref[pl.ds(pl.multiple_of(kt_ref[e] * tq, tq), tk), :]
        pv = jnp.dot(p_ref[...], v, preferred_element_type=F32)
        a = a_ref[...]
        acc[:, 0:LANES] = acc[:, 0:LANES] * a + pv[:, 0:LANES]
        acc[:, LANES:2 * LANES] = acc[:, LANES:2 * LANES] * a + pv[:, LANES:2 * LANES]

    kind_cfg = {'W': (2 * tq, False, 32, (sw0_ref, sw1_ref), (pw0_ref, pw1_ref)),
                'O': (tq, False, 64, (s0_ref, s1_ref), (p0_ref, p1_ref)),
                'N': (2 * tq, True, 32, (sw0_ref, sw1_ref), (pw0_ref, pw1_ref))}
    a_refs = (a0_ref, a1_ref)

    def iteration(it, par, k_a, k_b, k_c):
        tk_a, _, _, s_a, _ = kind_cfg[k_a]
        _, bias_b, rc_b, s_b, p_b = kind_cfg[k_b]
        tk_c, _, _, _, p_c = kind_cfg[k_c]
        scores(it + 2, s_a[par], tk_a)
        softmax(it + 1, s_b[1 - par], a_refs[1 - par], p_b[1 - par], bias_b, rc_b)
        values(it, a_refs[par], p_c[par], tk_c)

    pw0_ref[...] = jnp.zeros(pw0_ref.shape, BF16)
    a0_ref[...] = jnp.ones(a0_ref.shape, F32)
    scores(1, sw1_ref, 2 * tq)

    n_it = len(kinds) - 2
    it = 0
    while it < n_it:
        trio = (kinds[it + 2], kinds[it + 1], kinds[it])
        run = 1
        while it + run < n_it and (kinds[it + run + 2], kinds[it + run + 1], kinds[it + run]) == trio:
            run += 1
        start, stop = it, it + run
        if start % 2 and start < stop:
            iteration(start, 1, *trio)
            start += 1
        n_pairs = (stop - start) // 2
        if n_pairs > 0:
            def two_iterations(i2, c, start=start, trio=trio):
                iteration(start + 2 * i2, 0, *trio)
                iteration(start + 2 * i2 + 1, 1, *trio)
                return c

            lax.fori_loop(0, n_pairs, two_iterations, 0)
        if (stop - start) % 2:
            iteration(stop - 1, (stop - 1) % 2, *trio)
        it = stop

    lq = lq_ref[...]
    lam = (jnp.exp(jnp.sum(lq[0:1] * lq[1:2], axis=1, keepdims=True))
           - jnp.exp(jnp.sum(lq[2:3] * lq[3:4], axis=1, keepdims=True)) + lam_init)

    def finalize(i, c):
        acc = acc_ref[i]
        o_all = acc[:, 0:LANES] / acc[:, LANES:2 * LANES]
        o = o_all[0:tq] - lam * o_all[tq:2 * tq]
        ms = jnp.mean(o * o, axis=-1, keepdims=True)
        o = (o * lax.rsqrt(ms + RMS_EPS) * sg_ref[...]) * (1.0 - lam_init)
        o_ref[pl.ds(pl.multiple_of(i * tq, tq), tq), :] = o.astype(o_ref.dtype)
        return c

    lax.fori_loop(0, nq, finalize, 0, unroll=4)


def _diff_attention(qn, kn, vb, bias_tabs, lambda_qk, subln_g, lam_init, batch):
    n, d = qn.shape
    l = n // batch
    tq = ATTN_TQ
    nq = l // tq
    nh = d // LANES
    tables, kinds = _attn_pair_tables(nq)
    row_blk = pl.BlockSpec((l, LANES), lambda b, h, *_: (b, h))
    grid_spec = pltpu.PrefetchScalarGridSpec(
        num_scalar_prefetch=4,
        grid=(batch, nh),
        in_specs=[row_blk, row_blk, row_blk,
                  pl.BlockSpec((None, 2, tq, 2 * tq), lambda b, h, *_: (h, 0, 0, 0)),
                  pl.BlockSpec((4, HEAD_DIM), lambda b, h, *_: (0, 0)),
                  pl.BlockSpec((1, LANES), lambda b, h, *_: (0, 0))],
        out_specs=row_blk,
        scratch_shapes=[pltpu.VMEM((nq + 1, 2 * tq, LANES), F32),
                        pltpu.VMEM((nq + 1, 2 * tq, 2 * LANES), F32),
                        pltpu.VMEM((l, 2 * LANES), BF16),
                        pltpu.VMEM((2 * tq, tq), F32), pltpu.VMEM((2 * tq, tq), F32),
                        pltpu.VMEM((2 * tq, tq), BF16), pltpu.VMEM((2 * tq, tq), BF16),
                        pltpu.VMEM((2 * tq, LANES), F32), pltpu.VMEM((2 * tq, LANES), F32),
                        pltpu.VMEM((2 * tq, 2 * tq), F32), pltpu.VMEM((2 * tq, 2 * tq), F32),
                        pltpu.VMEM((2 * tq, 2 * tq), BF16), pltpu.VMEM((2 * tq, 2 * tq), BF16)],
    )
    return pl.pallas_call(
        functools.partial(_attn_body, tq=tq, nq=nq, kinds=kinds, lam_init=lam_init),
        grid_spec=grid_spec,
        out_shape=jax.ShapeDtypeStruct((n, d), BF16),
        compiler_params=_cparams(("arbitrary", "arbitrary")),
        name="diff_attention",
    )(*tables, qn, kn, vb, bias_tabs, lambda_qk.astype(F32),
      subln_g.reshape(1, LANES).astype(F32))


ROUTER_E0 = N_EXPERT_GROUPS
META_COLS = 8


def _router_body(h_ref, g_ref, wr_ref, br_ref, xn_ref, meta_ref, cnt_ref, base_ref):
    tm = h_ref.shape[0]

    @pl.when(pl.program_id(0) == 0)
    def _():
        base_ref[...] = jnp.zeros(base_ref.shape, F32)

    x = h_ref[...]
    ms = jnp.mean(x * x, axis=-1, keepdims=True)
    xn = x * lax.rsqrt(ms + RMS_EPS) * g_ref[...]
    xn_ref[...] = xn.astype(xn_ref.dtype)
    x_hi = xn.astype(BF16)
    x_lo = (xn - x_hi.astype(F32)).astype(BF16)
    logits = (jnp.dot(x_hi, wr_ref[0], preferred_element_type=F32)
              + (jnp.dot(x_hi, wr_ref[1], preferred_element_type=F32)
                 + jnp.dot(x_lo, wr_ref[0], preferred_element_type=F32))) + br_ref[...]
    lane = lax.broadcasted_iota(jnp.int32, (tm, LANES), 1)
    neg_inf = -jnp.inf

    def first_argmax(vals):
        vmax = jnp.max(vals, axis=-1, keepdims=True)
        idx = jnp.min(jnp.where(vals == vmax, lane, LANES), axis=-1, keepdims=True)
        return vmax, idx

    is_g = lane < N_EXPERT_GROUPS
    gmax, g_idx = first_argmax(jnp.where(is_g, logits, neg_inf))
    g_w = 1.0 / jnp.sum(jnp.where(is_g, jnp.exp(logits - gmax), 0.0), axis=-1, keepdims=True)
    e_lo = ROUTER_E0 + EXPERTS_PER_GROUP * g_idx
    elog = jnp.where((lane >= e_lo) & (lane < e_lo + EXPERTS_PER_GROUP), logits, neg_inf)
    v0, i0 = first_argmax(elog)
    v1, i1 = first_argmax(jnp.where(lane == i0, neg_inf, elog))
    t = jnp.exp(v1 - v0)
    w0 = g_w / (1.0 + t)
    w1 = g_w * t / (1.0 + t)

    sel0 = lane == i0
    sel1 = lane == i1
    onehot = jnp.where(sel0 | sel1, 1.0, 0.0)
    r_i = lax.broadcasted_iota(jnp.int32, (tm, tm), 0)
    c_i = lax.broadcasted_iota(jnp.int32, (tm, tm), 1)
    tri = jnp.where(c_i < r_i, 1.0, 0.0).astype(BF16)
    before = jnp.dot(tri, onehot.astype(BF16), preferred_element_type=F32) + base_ref[...]
    rank0 = jnp.sum(jnp.where(sel0, before, 0.0), axis=-1, keepdims=True)
    rank1 = jnp.sum(jnp.where(sel1, before, 0.0), axis=-1, keepdims=True)
    base_ref[...] = base_ref[...] + jnp.sum(onehot, axis=0, keepdims=True)
    cnt_ref[...] = base_ref[...]

    eid0 = (i0 - ROUTER_E0).astype(F32)
    eid1 = (i1 - ROUTER_E0).astype(F32)
    meta = jnp.where(lane == 0, eid0, jnp.where(lane == 1, eid1, jnp.where(
        lane == 2, rank0, jnp.where(lane == 3, rank1, jnp.where(
            lane == 4, w0, jnp.where(lane == 5, w1, 0.0))))))
    meta_ref[...] = meta[:, 0:META_COLS]


def _router(h, g, wg1, bg1, wg2, bg2, tm=256):
    n, d = h.shape
    wr = jnp.concatenate([wg1.astype(F32),
                          wg2.astype(F32).transpose(1, 0, 2).reshape(d, N_EXPERTS)], axis=1)
    wr = jnp.pad(wr, ((0, 0), (0, LANES - wr.shape[1])))
    wr_hi = wr.astype(BF16)
    wr = jnp.stack([wr_hi, (wr - wr_hi.astype(F32)).astype(BF16)])
    br = jnp.pad(jnp.concatenate([bg1.astype(F32), bg2.astype(F32).reshape(-1)]),
                 (0, LANES - N_EXPERT_GROUPS - N_EXPERTS)).reshape(1, LANES)
    return pl.pallas_call(
        _router_body,
        grid=(n // tm,),
        in_specs=[pl.BlockSpec((tm, d), lambda i: (i, 0)),
                  pl.BlockSpec((1, d), lambda i: (0, 0)),
                  pl.BlockSpec((2, d, LANES), lambda i: (0, 0, 0)),
                  pl.BlockSpec((1, LANES), lambda i: (0, 0))],
        out_specs=[pl.BlockSpec((tm, d), lambda i: (i, 0)),
                   pl.BlockSpec((tm, META_COLS), lambda i: (i, 0)),
                   pl.BlockSpec((1, LANES), lambda i: (0, 0))],
        out_shape=[jax.ShapeDtypeStruct((n, d), F32),
                   jax.ShapeDtypeStruct((n, META_COLS), F32),
                   jax.ShapeDtypeStruct((1, LANES), F32)],
        scratch_shapes=[pltpu.VMEM((1, LANES), F32)],
        compiler_params=_cparams(("arbitrary",)),
        name="moe_router",
    )(h, g.reshape(1, d).astype(F32), wr, br)


def _invert_body(dest_ref, rt_ref, *, ts):
    t = pl.program_id(0)

    def body(r, c):
        tok = t * ts + r
        rt_ref[dest_ref[0, 2 * r]] = tok
        rt_ref[dest_ref[0, 2 * r + 1]] = tok
        return c

    lax.fori_loop(0, ts, body, 0, unroll=True)


def _invert(dest, ts=512):
    n = dest.shape[0]
    dest3 = dest.reshape(n // ts, 1, 2 * ts)
    return pl.pallas_call(
        functools.partial(_invert_body, ts=ts),
        grid=(n // ts,),
        in_specs=[pl.BlockSpec((None, 1, 2 * ts), lambda t: (t, 0, 0), memory_space=pltpu.SMEM)],
        out_specs=pl.BlockSpec(memory_space=pltpu.SMEM),
        out_shape=jax.ShapeDtypeStruct((2 * n,), jnp.int32),
        compiler_params=_cparams(("arbitrary",)),
        name="moe_invert",
    )(dest3)


def _combine_body(d0_ref, dn_ref, h_ref, w_ref, y_hbm, *rest, ts, nt, with_norm):
    if with_norm:
        g_ref, o_ref, xn_ref, y0_ref, y1_ref, sem = rest
    else:
        o_ref, y0_ref, y1_ref, sem = rest
    t = pl.program_id(0)
    slot = t % 2

    def gather_start(dest_ref, dst):
        def issue(r, c):
            pltpu.make_async_copy(y_hbm.at[pl.ds(dest_ref[0, 2 * r], 1), :],
                                  y0_ref.at[dst, pl.ds(r, 1), :], sem.at[dst, 0]).start()
            pltpu.make_async_copy(y_hbm.at[pl.ds(dest_ref[0, 2 * r + 1], 1), :],
                                  y1_ref.at[dst, pl.ds(r, 1), :], sem.at[dst, 1]).start(priority=1)
            return c

        lax.fori_loop(0, ts, issue, 0, unroll=True)

    @pl.when(t == 0)
    def _():
        gather_start(d0_ref, 0)

    @pl.when(t + 1 < nt)
    def _():
        gather_start(dn_ref, 1 - slot)

    pltpu.make_async_copy(y_hbm.at[pl.ds(0, ts), :], y0_ref.at[slot], sem.at[slot, 0]).wait()
    pltpu.make_async_copy(y_hbm.at[pl.ds(0, ts), :], y1_ref.at[slot], sem.at[slot, 1]).wait()
    w = w_ref[...]
    out = h_ref[...] + w[:, 4:5] * y0_ref[slot] + w[:, 5:6] * y1_ref[slot]
    o_ref[...] = out
    if with_norm:
        ms = jnp.mean(out * out, axis=-1, keepdims=True)
        xn_ref[...] = (out * lax.rsqrt(ms + RMS_EPS) * g_ref[...]).astype(xn_ref.dtype)


def _combine(h, meta, y_sorted, dest, next_norm_g=None, ts=256):
    n, d = h.shape
    nt = n // ts
    dest3 = dest.reshape(nt, 1, 2 * ts)
    with_norm = next_norm_g is not None
    row_blk = pl.BlockSpec((ts, d), lambda t: (t, 0))
    in_specs = [pl.BlockSpec((None, 1, 2 * ts), lambda t: (0, 0, 0), memory_space=pltpu.SMEM),
                pl.BlockSpec((None, 1, 2 * ts), lambda t: (jnp.minimum(t + 1, nt - 1), 0, 0),
                             memory_space=pltpu.SMEM),
                row_blk,
                pl.BlockSpec((ts, META_COLS), lambda t: (t, 0)),
                pl.BlockSpec(memory_space=pl.ANY)]
    args = [dest3, dest3, h, meta, y_sorted]
    out_specs, out_shape = row_blk, jax.ShapeDtypeStruct((n, d), F32)
    if with_norm:
        in_specs.append(pl.BlockSpec((1, d), lambda t: (0, 0)))
        args.append(next_norm_g.reshape(1, d).astype(F32))
        out_specs = [row_blk, row_blk]
        out_shape = [out_shape, jax.ShapeDtypeStruct((n, d), BF16)]
    return pl.pallas_call(
        functools.partial(_combine_body, ts=ts, nt=nt, with_norm=with_norm),
        grid=(nt,),
        in_specs=in_specs,
        out_specs=out_specs,
        out_shape=out_shape,
        scratch_shapes=[pltpu.VMEM((2, ts, d), F32), pltpu.VMEM((2, ts, d), F32),
                        pltpu.SemaphoreType.DMA((2, 2))],
        compiler_params=_cparams(("arbitrary",)),
        name="moe_combine",
    )(*args)


EXPERT_TM = 256
EXPERT_VMEM_LIMIT = 56 * 1024 * 1024


def _expert_body(ie_ref, it_ref, lo_ref, hi_ref, first_ref, rfirst_ref, nxt_ref,
                 rt0_ref, rtn_ref, xn_hbm, wg_hbm, wu_hbm, wd_hbm, o_ref,
                 xbuf, stg_g, stg_u, stg_d, wgb, wub, wdb, sem_x, sem_w, *, tm, layer, n_tiles):
    i = pl.program_id(0)
    e = ie_ref[i]
    t = it_ref[i]
    lo = lo_ref[i]
    hi = hi_ref[i]
    slot = t % 2

    def weight_copies(ex):
        return (pltpu.make_async_copy(wg_hbm.at[layer, ex], stg_g, sem_w.at[0]),
                pltpu.make_async_copy(wu_hbm.at[layer, ex], stg_u, sem_w.at[1]),
                pltpu.make_async_copy(wd_hbm.at[layer, ex], stg_d, sem_w.at[2]))

    def gather_start(rt_ref, dst_slot):
        def issue(r2, c):
            for par in range(2):
                r = 2 * r2 + par
                pltpu.make_async_copy(xn_hbm.at[pl.ds(rt_ref[0, r], 1), :],
                                      xbuf.at[dst_slot, pl.ds(r, 1), :],
                                      sem_x.at[dst_slot]).start(priority=par)
            return c

        lax.fori_loop(0, tm // 2, issue, 0, unroll=True)

    @pl.when(i == 0)
    def _():
        for cp in weight_copies(e):
            cp.start()
        gather_start(rt0_ref, 0)

    @pl.when(rfirst_ref[i] == 1)
    def _():
        for cp in weight_copies(e):
            cp.wait()
        rows = 256

        def cast_rows(r, c):
            r0 = pl.multiple_of(r * rows, rows)
            wgb[pl.ds(r0, rows), :] = stg_g[pl.ds(r0, rows), :].astype(BF16)
            wub[pl.ds(r0, rows), :] = stg_u[pl.ds(r0, rows), :].astype(BF16)
            return c

        lax.fori_loop(0, stg_g.shape[0] // rows, cast_rows, 0)

        def cast_rows_d(r, c):
            r0 = pl.multiple_of(r * rows, rows)
            wdb[pl.ds(r0, rows), :] = stg_d[pl.ds(r0, rows), :].astype(BF16)
            return c

        lax.fori_loop(0, stg_d.shape[0] // rows, cast_rows_d, 0)

        @pl.when(nxt_ref[i] >= 0)
        def _():
            for cp in weight_copies(nxt_ref[i]):
                cp.start()

    @pl.when(first_ref[i] == 1)
    def _():
        @pl.when(t + 1 < n_tiles)
        def _():
            gather_start(rtn_ref, 1 - slot)

        pltpu.make_async_copy(xn_hbm.at[pl.ds(0, tm), :], xbuf.at[slot], sem_x.at[slot]).wait()

    @pl.when(hi > lo)
    def _():
        xb = xbuf[slot].astype(BF16)
        g = jnp.dot(xb, wgb[...], preferred_element_type=F32)
        u = jnp.dot(xb, wub[...], preferred_element_type=F32)
        hdn = (g * (1.0 / (1.0 + jnp.exp(-g)))) * u
        row = t * tm + lax.broadcasted_iota(jnp.int32, (tm, 1), 0)
        hdn = jnp.where((row >= lo) & (row < hi), hdn, 0.0)
        y = jnp.dot(hdn.astype(BF16), wdb[...], preferred_element_type=F32)

        @pl.when(first_ref[i] == 1)
        def _():
            o_ref[...] = y

        @pl.when(first_ref[i] != 1)
        def _():
            o_ref[...] = o_ref[...] + y


def _experts(xn, row_token, items, w_gate, w_up, w_down, layer, tm):
    _, d = xn.shape
    dh = w_gate.shape[-1]
    r_total = row_token.shape[0]
    n_tiles = r_total // tm
    n_items = items[0].shape[0]
    rt3 = row_token.reshape(n_tiles, 1, tm)
    grid_spec = pltpu.PrefetchScalarGridSpec(
        num_scalar_prefetch=7,
        grid=(n_items,),
        in_specs=[
            pl.BlockSpec((None, 1, tm), lambda i, *_: (0, 0, 0), memory_space=pltpu.SMEM),
            pl.BlockSpec((None, 1, tm), lambda i, ie, it, *_: (jnp.minimum(it[i] + 1, n_tiles - 1), 0, 0),
                         memory_space=pltpu.SMEM),
            pl.BlockSpec(memory_space=pl.ANY), pl.BlockSpec(memory_space=pl.ANY),
            pl.BlockSpec(memory_space=pl.ANY), pl.BlockSpec(memory_space=pl.ANY),
        ],
        out_specs=pl.BlockSpec((tm, d), lambda i, ie, it, *_: (it[i], 0)),
        scratch_shapes=[pltpu.VMEM((2, tm, d), F32),
                        pltpu.VMEM((d, dh), F32), pltpu.VMEM((d, dh), F32), pltpu.VMEM((dh, d), F32),
                        pltpu.VMEM((d, dh), BF16), pltpu.VMEM((d, dh), BF16), pltpu.VMEM((dh, d), BF16),
                        pltpu.SemaphoreType.DMA((2,)), pltpu.SemaphoreType.DMA((3,))],
    )
    return pl.pallas_call(
        functools.partial(_expert_body, tm=tm, layer=layer, n_tiles=n_tiles),
        grid_spec=grid_spec,
        out_shape=jax.ShapeDtypeStruct((r_total, d), F32),
        compiler_params=pltpu.CompilerParams(dimension_semantics=("arbitrary",),
                                             vmem_limit_bytes=EXPERT_VMEM_LIMIT),
        name="moe_experts",
    )(*items, rt3, rt3, xn, w_gate, w_up, w_down)


def _moe_plan(meta, counts_row, n_rows, tm):
    counts = counts_row[0, ROUTER_E0:ROUTER_E0 + N_EXPERTS].astype(jnp.int32)
    ends = jnp.cumsum(counts)
    starts = ends - counts
    eid = meta[:, 0:2].astype(jnp.int32)
    rank = meta[:, 2:4].astype(jnp.int32)
    dest = starts[eid] + rank

    first_tile = starts // tm
    last_tile = jnp.maximum(ends - 1, 0) // tm
    ntile_e = jnp.where(counts > 0, last_tile - first_tile + 1, 0)
    item_end = jnp.cumsum(ntile_e)
    item_start = item_end - ntile_e
    total = item_end[-1]
    n_items = n_rows // tm + N_EXPERTS - 1
    ii = jnp.arange(n_items, dtype=jnp.int32)
    valid = ii < total
    ii_c = jnp.minimum(ii, total - 1)
    e_i = jnp.sum((ii_c[:, None] >= item_end[None, :]).astype(jnp.int32), axis=1)
    t_i = (first_tile[e_i] + (ii_c - item_start[e_i])).astype(jnp.int32)
    lo = jnp.where(valid, jnp.maximum(starts[e_i], t_i * tm), 0).astype(jnp.int32)
    hi = jnp.where(valid, jnp.minimum(ends[e_i], (t_i + 1) * tm), 0).astype(jnp.int32)
    minus1 = jnp.full((1,), -1, jnp.int32)
    first = (valid & (t_i != jnp.concatenate([minus1, t_i[:-1]]))).astype(jnp.int32)
    run_first = (valid & (e_i != jnp.concatenate([minus1, e_i[:-1]]))).astype(jnp.int32)
    ar = jnp.arange(N_EXPERTS, dtype=jnp.int32)
    later = (ar[None, :] > ar[:, None]) & (counts > 0)[None, :]
    nxt_e = jnp.min(jnp.where(later, ar[None, :], N_EXPERTS), axis=1)
    nxt_e = jnp.where(nxt_e == N_EXPERTS, -1, nxt_e).astype(jnp.int32)
    items = (e_i, t_i, lo, hi, first, run_first, nxt_e[e_i])
    return dest.astype(jnp.int32), items


def _hier_moe(h, g, wg1, bg1, wg2, bg2, w_gate, w_up, w_down, layer, next_norm_g=None,
              tm=EXPERT_TM):
    n, _ = h.shape
    xn, meta, counts_row = _router(h, g, wg1, bg1, wg2, bg2)
    dest, items = _moe_plan(meta, counts_row, 2 * n, tm)
    y_sorted = _experts(xn, _invert(dest), items, w_gate, w_up, w_down, layer, tm)
    return _combine(h, meta, y_sorted, dest, next_norm_g)


def _time_permute(h, batch, inverse=False):
    n, d = h.shape
    l = n // batch
    shape = (batch, l // S5_SEGMENTS, S5_SEGMENTS, d) if inverse else (batch, S5_SEGMENTS, l // S5_SEGMENTS, d)
    return h.reshape(shape).transpose(0, 2, 1, 3).reshape(n, d)


def kernel(x, norm_mix_g, norm_ffn_g, ssm_lam_re, ssm_lam_im, ssm_b_re, ssm_b_im, ssm_c_re, ssm_c_im, ssm_d, ssm_log_step, ssm_w_glu1, ssm_w_glu2, kv_norm_g, w_kv, k_norm_g, w_q, q_norm_g, lambda_qk, subln_g, w_o, rel_bias, router_group_w, router_group_b, router_expert_w, router_expert_b, w_gate, w_up, w_down):
    batch, l, d = x.shape
    n = batch * l
    h = _time_permute(x.astype(F32).reshape(n, d), batch)
    kn = vb = bias_tabs = None
    xn = _rmsnorm(h, norm_mix_g[0], BF16)
    for layer in range(DEPTH):
        if layer < N_A_LAYERS:
            b_blk, c_blk, lam_v = _s5_params(ssm_lam_re[layer], ssm_lam_im[layer], ssm_b_re[layer],
                                             ssm_b_im[layer], ssm_c_re[layer], ssm_c_im[layer],
                                             ssm_log_step[layer])
            z = _s5_scan(xn, b_blk, c_blk, lam_v, ssm_d[layer], batch)
            h = _matmul(z, [ssm_w_glu1, ssm_w_glu2], layer, F32, res=h, tm=1024)
        else:
            j = layer - N_A_LAYERS
            lam_init = 0.8 - 0.6 * math.exp(-0.3 * layer)
            qn = _matmul(xn, [w_q], j, BF16, head_norm=(q_norm_g[j], HEAD_DIM ** -0.5 * LOG2E),
                         tm=1024, tn=PROJ_TN)
            o = _diff_attention(qn, kn, vb, bias_tabs, lambda_qk[j], subln_g[j], lam_init, batch)
            h = _matmul(o, [w_o], j, F32, res=h, tm=1024, tn=PROJ_TN)
        fuse_next = layer + 1 < DEPTH and layer != N_A_LAYERS - 1
        moe_out = _hier_moe(h, norm_ffn_g[layer], router_group_w[layer], router_group_b[layer],
                            router_expert_w[layer], router_expert_b[layer], w_gate, w_up, w_down,
                            layer, norm_mix_g[layer + 1] if fuse_next else None)
        h, xn = moe_out if fuse_next else (moe_out, None)
        if layer == N_A_LAYERS - 1:
            h = _time_permute(h, batch, inverse=True)
            xkv = _rmsnorm(h, kv_norm_g, BF16)
            w_kv3 = w_kv.reshape(1, d, 2 * d)
            kn = _matmul(xkv, [w_kv3], 0, BF16, n_out=d, head_norm=(k_norm_g, 1.0), tm=1024,
                         tn=PROJ_TN)
            vb = _matmul(xkv, [w_kv3], 0, BF16, col_off=d // PROJ_TN, n_out=d, tm=1024, tn=PROJ_TN)
            bias_tabs = _attn_bias_tables(rel_bias, ATTN_TQ)
            xn = _rmsnorm(h, norm_mix_g[layer + 1], BF16)
    return h.reshape(batch, l, d)
```
